```python
import math
import jax, jax.numpy as jnp
from jax import lax
import numpy as np

D_MODEL = 4096
BATCH = 2
SEQ = 8192
DEPTH = 2

RMS_EPS = 1e-6
MIX_WIDTH = D_MODEL
POOL_GROUPS = 4
POOL_WINDOWS = (2, 4, 8, 16)
POOL_WIDTH = MIX_WIDTH // 4
POOL_GROUP_DIM = POOL_WIDTH // POOL_GROUPS
NSA_HEAD_DIM = 128
NSA_HEADS = (MIX_WIDTH - POOL_WIDTH) // NSA_HEAD_DIM
NSA_KV_GROUPS = 4
NSA_HEADS_PER_GROUP = NSA_HEADS // NSA_KV_GROUPS
NSA_WIDTH = NSA_HEADS * NSA_HEAD_DIM
NSA_KV_WIDTH = NSA_KV_GROUPS * NSA_HEAD_DIM
N_BRANCH = 3
CMP_BLOCK = 32
CMP_STRIDE = 16
CMP_HIDDEN = 256
SLC_BLOCK = 64
SLC_TOPN = 16
SWA_WINDOW = 512
NSA_Q_BLOCK = 64
FORCE_BONUS = 1e4
NEG = -1e30
AB_IN_WIDTH = POOL_WIDTH + NSA_WIDTH + 2 * N_BRANCH * NSA_KV_WIDTH + N_BRANCH * NSA_HEADS
MLSTM_HEADS = 8
MLSTM_V_DIM = MIX_WIDTH // MLSTM_HEADS
MLSTM_QK_DIM = MLSTM_V_DIM // 2
MLSTM_CHUNK = 64
C_IN_WIDTH = 2 * MLSTM_HEADS * MLSTM_QK_DIM + 2 * MIX_WIDTH + 2 * MLSTM_HEADS
D_FF = 10944
CONV_WIDTH = 3

N_EVEN = (DEPTH + 1) // 2
N_ODD = DEPTH // 2

kernel_name = 'hybrid_pool_nsa_mlstm_convffn'


def _rmsnorm(x, gain):
    xf = x.astype(jnp.float32)
    y = xf * lax.rsqrt(jnp.mean(xf * xf, axis=-1, keepdims=True) + RMS_EPS)
    return (y * gain.astype(jnp.float32)).astype(x.dtype)


def _masked_softmax(s, mask):
    return jax.nn.softmax(jnp.where(mask, s, NEG), axis=-1)


def _pool_mixer(u, w_pool, scale):
    B_, S_, _ = u.shape
    uf = u.astype(jnp.float32).reshape(B_, S_, POOL_GROUPS, POOL_GROUP_DIM)
    cs = jnp.pad(jnp.cumsum(uf, axis=1), ((0, 0), (1, 0), (0, 0), (0, 0)))
    t = jnp.arange(S_)
    groups = []
    for g, w in enumerate(POOL_WINDOWS):
        lo = jnp.maximum(t + 1 - w, 0)
        win_sum = cs[:, 1:, g] - jnp.take(cs[:, :, g], lo, axis=1)
        count = jnp.minimum(t + 1, w).astype(jnp.float32)[None, :, None]
        groups.append(win_sum / count - uf[:, :, g])
    pooled = jnp.stack(groups, axis=2)
    y = jnp.einsum('bsgc,gcd->bsgd', pooled, w_pool.astype(jnp.float32))
    return y.reshape(B_, S_, POOL_WIDTH) * scale.astype(jnp.float32)


def _compress(kv, pos, w1, w2):
    B_, S_ = kv.shape[:2]
    n_cmp = (S_ - CMP_BLOCK) // CMP_STRIDE + 1
    idx = jnp.arange(n_cmp)[:, None] * CMP_STRIDE + jnp.arange(CMP_BLOCK)[None, :]
    blocks = kv[:, idx] + pos.astype(jnp.float32)[None, None, :, None, :]
    flat = jnp.transpose(blocks, (0, 3, 1, 2, 4)).reshape(B_, NSA_KV_GROUPS, n_cmp, CMP_BLOCK * NSA_HEAD_DIM)
    return jax.nn.gelu(flat @ w1.astype(jnp.float32)) @ w2.astype(jnp.float32)


def _nsa(q, k_cmp, v_cmp, k_slc, v_slc, k_swa, v_swa, gate_logits, cmp_pos, ck_w1, ck_w2, cv_w1, cv_w2):
    B_, S_ = q.shape[:2]
    G, HG, DH, QB = NSA_KV_GROUPS, NSA_HEADS_PER_GROUP, NSA_HEAD_DIM, NSA_Q_BLOCK
    n_cmp = (S_ - CMP_BLOCK) // CMP_STRIDE + 1
    n_slc = S_ // SLC_BLOCK
    top_n = min(SLC_TOPN, n_slc)
    qg = jnp.transpose(q.reshape(B_, S_, G, HG, DH), (0, 2, 3, 1, 4)) * (DH ** -0.5)
    gates = jnp.transpose(jax.nn.sigmoid(gate_logits).reshape(B_, S_, G, HG, N_BRANCH), (0, 2, 3, 1, 4))
    kc = _compress(k_cmp, cmp_pos, ck_w1, ck_w2)
    vc = _compress(v_cmp, cmp_pos, cv_w1, cv_w2)
    cmp_start = jnp.arange(n_cmp) * CMP_STRIDE
    cmp_end = cmp_start + CMP_BLOCK - 1
    slc_start = jnp.arange(n_slc) * SLC_BLOCK
    overlap = jnp.clip(jnp.minimum(cmp_start[:, None] + CMP_BLOCK, slc_start[None, :] + SLC_BLOCK)
                       - jnp.maximum(cmp_start[:, None], slc_start[None, :]), 0).astype(jnp.float32) / CMP_STRIDE
    ks_blocks = jnp.transpose(k_slc.reshape(B_, n_slc, SLC_BLOCK, G, DH), (0, 3, 1, 2, 4))
    vs_blocks = jnp.transpose(v_slc.reshape(B_, n_slc, SLC_BLOCK, G, DH), (0, 3, 1, 2, 4))
    pad = ((0, 0), (0, 0), (SWA_WINDOW, 0), (0, 0))
    kw_pad = jnp.pad(jnp.transpose(k_swa, (0, 2, 1, 3)), pad)
    vw_pad = jnp.pad(jnp.transpose(v_swa, (0, 2, 1, 3)), pad)
    gather_blocks = jax.vmap(jax.vmap(lambda blk, idx: blk[idx]))
    blk_ids = jnp.arange(n_slc)

    def query_block(qb):
        t0 = qb * QB
        t = t0 + jnp.arange(QB)
        qblk = lax.dynamic_slice_in_dim(qg, t0, QB, axis=3)
        gblk = lax.dynamic_slice_in_dim(gates, t0, QB, axis=3)
        valid_c = cmp_end[None, :] <= t[:, None]
        s_c = jnp.einsum('bghqd,bgnd->bghqn', qblk, kc)
        p_c = _masked_softmax(s_c, valid_c) * jnp.any(valid_c, axis=-1)[:, None].astype(jnp.float32)
        o_c = jnp.einsum('bghqn,bgnd->bghqd', p_c, vc)
        imp = jnp.einsum('bghqn,nj->bgqj', p_c, overlap)
        cur = t // SLC_BLOCK
        forced = (blk_ids[None, :] == 0) | (blk_ids[None, :] == cur[:, None]) | (blk_ids[None, :] == cur[:, None] - 1)
        future = blk_ids[None, :] * SLC_BLOCK > t[:, None]
        imp = jnp.where(forced, imp + FORCE_BONUS, jnp.where(future, -FORCE_BONUS, imp))
        _, sel = lax.top_k(imp, top_n)
        k_sel = gather_blocks(ks_blocks, sel).reshape(B_, G, QB, top_n * SLC_BLOCK, DH)
        v_sel = gather_blocks(vs_blocks, sel).reshape(B_, G, QB, top_n * SLC_BLOCK, DH)
        pos_sel = (sel[..., None] * SLC_BLOCK + jnp.arange(SLC_BLOCK)).reshape(B_, G, QB, top_n * SLC_BLOCK)
        s_s = jnp.einsum('bghqd,bgqkd->bghqk', qblk, k_sel)
        p_s = _masked_softmax(s_s, (pos_sel <= t[:, None])[:, :, None])
        o_s = jnp.einsum('bghqk,bgqkd->bghqd', p_s, v_sel)
        k_w = lax.dynamic_slice_in_dim(kw_pad, t0, QB + SWA_WINDOW, axis=2)
        v_w = lax.dynamic_slice_in_dim(vw_pad, t0, QB + SWA_WINDOW, axis=2)
        pos_w = t0 - SWA_WINDOW + jnp.arange(QB + SWA_WINDOW)
        dist = t[:, None] - pos_w[None, :]
        valid_w = (dist >= 0) & (dist < SWA_WINDOW) & (pos_w[None, :] >= 0)
        s_w = jnp.einsum('bghqd,bgkd->bghqk', qblk, k_w)
        o_w = jnp.einsum('bghqk,bgkd->bghqd', _masked_softmax(s_w, valid_w), v_w)
        return gblk[..., 0:1] * o_c + gblk[..., 1:2] * o_s + gblk[..., 2:3] * o_w

    out = lax.map(query_block, jnp.arange(S_ // QB))
    return jnp.transpose(out, (1, 0, 4, 2, 3, 5)).reshape(B_, S_, NSA_WIDTH)


def _ab_mixer(u, w_in, pool_w, pool_scale, cmp_pos, ck_w1, ck_w2, cv_w1, cv_w2, w_out):
    B_, S_, _ = u.shape
    proj = u @ w_in
    sizes = [POOL_WIDTH, NSA_WIDTH] + [NSA_KV_WIDTH] * (2 * N_BRANCH)
    cuts = []
    acc = 0
    for sz in sizes:
        acc += sz
        cuts.append(acc)
    p_in, q, kc, vc, ks, vs, kw, vw, g = jnp.split(proj, cuts, axis=-1)
    kvs = [a.astype(jnp.float32).reshape(B_, S_, NSA_KV_GROUPS, NSA_HEAD_DIM) for a in (kc, vc, ks, vs, kw, vw)]
    y_a = _pool_mixer(p_in, pool_w, pool_scale)
    y_b = _nsa(q.astype(jnp.float32).reshape(B_, S_, NSA_HEADS, NSA_HEAD_DIM), *kvs,
               g.astype(jnp.float32).reshape(B_, S_, NSA_HEADS, N_BRANCH), cmp_pos, ck_w1, ck_w2, cv_w1, cv_w2)
    return jnp.concatenate([y_a, y_b], axis=-1).astype(u.dtype) @ w_out


def _mlstm(q, k, v, ig, fg):
    B_, S_, H = q.shape[:3]
    L = MLSTM_CHUNK
    nc = S_ // L

    def to_chunks(a):
        a = a.astype(jnp.float32).reshape((B_, nc, L, H) + a.shape[3:])
        return jnp.moveaxis(a, (1, 3), (0, 2))

    qc = to_chunks(q) * (MLSTM_QK_DIM ** -0.5)
    kc, vc, ic = to_chunks(k), to_chunks(v), to_chunks(ig)
    lfc = jax.nn.log_sigmoid(to_chunks(fg))
    causal = jnp.tril(jnp.ones((L, L), dtype=bool))

    def step(carry, xs):
        C, n, m = carry
        qq, kk, vv, ii, lf = xs
        b = jnp.cumsum(lf, axis=-1)
        dmat = jnp.where(causal, b[..., :, None] - b[..., None, :] + ii[..., None, :], -jnp.inf)
        a = b + m[..., None]
        m_t = jnp.maximum(a, jnp.max(dmat, axis=-1))
        wq = jnp.einsum('bhtd,bhsd->bhts', qq, kk) * jnp.exp(dmat - m_t[..., None])
        inter = jnp.exp(a - m_t)
        num = inter[..., None] * jnp.einsum('bhtd,bhde->bhte', qq, C) + jnp.einsum('bhts,bhse->bhte', wq, vv)
        den = inter * jnp.einsum('bhtd,bhd->bht', qq, n) + jnp.sum(wq, axis=-1)
        h = num / jnp.maximum(jnp.abs(den), jnp.exp(-m_t))[..., None]
        b_last = b[..., -1]
        g = b_last[..., None] - b + ii
        m_new = jnp.maximum(b_last + m, jnp.max(g, axis=-1))
        decay = jnp.exp(b_last + m - m_new)
        kw = kk * jnp.exp(g - m_new[..., None])[..., None]
        C_new = decay[..., None, None] * C + jnp.einsum('bhsd,bhse->bhde', kw, vv)
        n_new = decay[..., None] * n + jnp.sum(kw, axis=2)
        return (C_new, n_new, m_new), h

    init = (jnp.zeros((B_, H, MLSTM_QK_DIM, MLSTM_V_DIM), jnp.float32),
            jnp.zeros((B_, H, MLSTM_QK_DIM), jnp.float32),
            jnp.zeros((B_, H), jnp.float32))
    _, hs = lax.scan(step, init, (qc, kc, vc, ic, lfc))
    return jnp.moveaxis(hs, (0, 2), (1, 3)).reshape(B_, S_, H, MLSTM_V_DIM)


def _c_mixer(u, w_in, b_if, head_norm, w_out):
    B_, S_, _ = u.shape
    H = MLSTM_HEADS
    qk_w = H * MLSTM_QK_DIM
    proj = u @ w_in
    q, k, v, o, gif = jnp.split(proj, [qk_w, 2 * qk_w, 2 * qk_w + MIX_WIDTH, 2 * qk_w + 2 * MIX_WIDTH], axis=-1)
    gif = gif.astype(jnp.float32) + b_if.astype(jnp.float32)
    h = _mlstm(q.reshape(B_, S_, H, MLSTM_QK_DIM), k.reshape(B_, S_, H, MLSTM_QK_DIM),
               v.reshape(B_, S_, H, MLSTM_V_DIM), gif[..., :H], gif[..., H:])
    h = h * lax.rsqrt(jnp.mean(h * h, axis=-1, keepdims=True) + RMS_EPS) * head_norm.astype(jnp.float32)
    y = jax.nn.sigmoid(o.astype(jnp.float32)) * h.reshape(B_, S_, MIX_WIDTH)
    return y.astype(u.dtype) @ w_out


def _conv_ffn(u, w_in, conv_w, conv_b, w_out):
    S_ = u.shape[1]
    hid = u @ w_in
    gate, up = hid[..., :D_FF], hid[..., D_FF:]
    gp = jnp.pad(gate, ((0, 0), (CONV_WIDTH - 1, 0), (0, 0)))
    gate = (conv_b + conv_w[0] * gp[:, 2:2 + S_] + conv_w[1] * gp[:, 1:1 + S_] + conv_w[2] * gp[:, 0:S_])
    return (jax.nn.silu(gate) * up) @ w_out


def setup_inputs(seed: int = 0) -> dict:
    key = jax.random.key(seed)
    ks = jax.random.split(key, 24)
    f32 = jnp.float32

    def nrm(k, shape, scale):
        return jax.random.normal(k, shape, f32) * scale

    def gain(k, shape, s=0.02):
        return 1.0 + s * jax.random.normal(k, shape, f32)

    H = MLSTM_HEADS
    b_if = jnp.concatenate([nrm(ks[12], (N_ODD, H), 0.1),
                            jnp.linspace(3.0, 6.0, H, dtype=f32)[None, :] + nrm(ks[13], (N_ODD, H), 0.1)], axis=-1)
    cmp_in = CMP_BLOCK * NSA_HEAD_DIM
    return {
        'x': nrm(ks[0], (BATCH, SEQ, D_MODEL), 1.0),
        'ln_pre': gain(ks[1], (DEPTH, D_MODEL)),
        'ln_post': gain(ks[2], (DEPTH, D_MODEL)),
        'w_in_ab': nrm(ks[3], (N_EVEN, D_MODEL, AB_IN_WIDTH), D_MODEL ** -0.5),
        'pool_w': nrm(ks[4], (N_EVEN, POOL_GROUPS, POOL_GROUP_DIM, POOL_GROUP_DIM), POOL_GROUP_DIM ** -0.5),
        'pool_scale': gain(ks[5], (N_EVEN, POOL_WIDTH), 0.1),
        'cmp_pos': nrm(ks[6], (N_EVEN, CMP_BLOCK, NSA_HEAD_DIM), 0.1),
        'cmp_k_w1': nrm(ks[7], (N_EVEN, cmp_in, CMP_HIDDEN), cmp_in ** -0.5),
        'cmp_k_w2': nrm(ks[8], (N_EVEN, CMP_HIDDEN, NSA_HEAD_DIM), CMP_HIDDEN ** -0.5),
        'cmp_v_w1': nrm(ks[9], (N_EVEN, cmp_in, CMP_HIDDEN), cmp_in ** -0.5),
        'cmp_v_w2': nrm(ks[10], (N_EVEN, CMP_HIDDEN, NSA_HEAD_DIM), CMP_HIDDEN ** -0.5),
        'w_out_ab': nrm(ks[11], (N_EVEN, MIX_WIDTH, D_MODEL), MIX_WIDTH ** -0.5),
        'w_in_c': nrm(ks[14], (N_ODD, D_MODEL, C_IN_WIDTH), D_MODEL ** -0.5),
        'b_if_c': b_if,
        'head_norm_c': gain(ks[15], (N_ODD, H, MLSTM_V_DIM)),
        'w_out_c': nrm(ks[16], (N_ODD, MIX_WIDTH, D_MODEL), MIX_WIDTH ** -0.5),
        'ffn_ln_pre': gain(ks[17], (DEPTH, D_MODEL)),
        'ffn_ln_post': gain(ks[18], (DEPTH, D_MODEL)),
        'ffn_w_in': nrm(ks[19], (DEPTH, D_MODEL, 2 * D_FF), D_MODEL ** -0.5),
        'ffn_conv_w': nrm(ks[20], (DEPTH, CONV_WIDTH, D_FF), CONV_WIDTH ** -0.5),
        'ffn_conv_b': nrm(ks[21], (DEPTH, D_FF), 0.02),
        'ffn_w_out': nrm(ks[22], (DEPTH, D_FF, D_MODEL), D_FF ** -0.5),
    }


def reference(x, ln_pre, ln_post, w_in_ab, pool_w, pool_scale, cmp_pos, cmp_k_w1, cmp_k_w2, cmp_v_w1, cmp_v_w2,
              w_out_ab, w_in_c, b_if_c, head_norm_c, w_out_c, ffn_ln_pre, ffn_ln_post, ffn_w_in, ffn_conv_w,
              ffn_conv_b, ffn_w_out):
    h = x
    for layer in range(DEPTH):
        i = layer // 2
        u = _rmsnorm(h, ln_pre[layer])
        if layer % 2 == 0:
            y = _ab_mixer(u, w_in_ab[i], pool_w[i], pool_scale[i], cmp_pos[i], cmp_k_w1[i], cmp_k_w2[i],
                          cmp_v_w1[i], cmp_v_w2[i], w_out_ab[i])
        else:
            y = _c_mixer(u, w_in_c[i], b_if_c[i], head_norm_c[i], w_out_c[i])
        h = h + _rmsnorm(y, ln_post[layer])
        u = _rmsnorm(h, ffn_ln_pre[layer])
        f = _conv_ffn(u, ffn_w_in[layer], ffn_conv_w[layer], ffn_conv_b[layer], ffn_w_out[layer])
        h = h + _rmsnorm(f, ffn_ln_post[layer])
    return h
```

```python
import functools
import math

import jax
import jax.numpy as jnp
from jax import lax
from jax.experimental import pallas as pl
from jax.experimental.pallas import tpu as pltpu

F32 = jnp.float32
BF16 = jnp.bfloat16

RMS_EPS = 1e-6
POOL_GROUPS = 4
POOL_WINDOWS = (2, 4, 8, 16)
POOL_HALO = 16
NSA_HEAD_DIM = 128
NSA_KV_GROUPS = 4
N_BRANCH = 3
CMP_BLOCK = 32
CMP_STRIDE = 16
SLC_BLOCK = 64
SLC_TOPN = 16
SWA_WINDOW = 512
NSA_Q_BLOCK = 64
FORCE_BONUS = 1e4
NEG = -1e30
BIG = 1e30
MLSTM_HEADS = 8
CONV_WIDTH = 3
CONV_HALO = 16

LANES = 128
SEL_TILE = 512
SWA_SPAN = SWA_WINDOW + 2 * NSA_Q_BLOCK
MLSTM_CHUNK = 256
VMEM_LIMIT = 56 * 1024 * 1024


def _cparams(*sem):
    return pltpu.CompilerParams(dimension_semantics=sem, vmem_limit_bytes=VMEM_LIMIT)


def _round_up(n, m):
    return (n + m - 1) // m * m


def _pad_cols(w, n):
    return jnp.pad(w, ((0, 0), (0, n - w.shape[1])))


def _rmsnorm_kernel(x_ref, g_ref, o_ref):
    x = x_ref[...]
    ms = jnp.mean(x * x, axis=-1, keepdims=True)
    o_ref[...] = (x * lax.rsqrt(ms + RMS_EPS) * g_ref[...]).astype(o_ref.dtype)


def _rmsnorm(x, gain, tm=512):
    T, D = x.shape
    return pl.pallas_call(
        _rmsnorm_kernel,
        grid=(T // tm,),
        in_specs=[pl.BlockSpec((tm, D), lambda i: (i, 0)), pl.BlockSpec((1, D), lambda i: (0, 0))],
        out_specs=pl.BlockSpec((tm, D), lambda i: (i, 0)),
        out_shape=jax.ShapeDtypeStruct((T, D), BF16),
        compiler_params=_cparams("parallel"),
        name="rmsnorm",
    )(x, gain.reshape(1, D))


def _mm_kernel(a_ref, b_ref, bias_ref, o_ref, *, scale):
    acc = jnp.dot(a_ref[...], b_ref[...], preferred_element_type=F32)
    o_ref[...] = ((acc + bias_ref[...]) * scale).astype(o_ref.dtype)


def _mm(a, w, out_dtype, *, bias=None, scale=1.0, tm=1024, tn=512):
    M, K = a.shape
    N = w.shape[1]
    tm = min(tm, M)
    tn = min(tn, N)
    assert M % tm == 0 and N % tn == 0
    if bias is None:
        bias = jnp.zeros((1, N), F32)
    return pl.pallas_call(
        functools.partial(_mm_kernel, scale=scale),
        grid=(M // tm, N // tn),
        in_specs=[
            pl.BlockSpec((tm, K), lambda i, j: (i, 0)),
            pl.BlockSpec((K, tn), lambda i, j: (0, j)),
            pl.BlockSpec((1, tn), lambda i, j: (0, j)),
        ],
        out_specs=pl.BlockSpec((tm, tn), lambda i, j: (i, j)),
        out_shape=jax.ShapeDtypeStruct((M, N), out_dtype),
        compiler_params=_cparams("parallel", "arbitrary"),
        name="proj",
    )(a, w, bias)


def _mm_norm_res_kernel(a_ref, w_ref, h_ref, g_ref, o_ref, *, nk):
    k = pl.program_id(1)

    @pl.when(k == 0)
    def _():
        o_ref[...] = jnp.zeros_like(o_ref)

    o_ref[...] += jnp.dot(a_ref[...], w_ref[...], preferred_element_type=F32)

    @pl.when(k == nk - 1)
    def _():
        y = o_ref[...]
        ms = jnp.mean(y * y, axis=-1, keepdims=True)
        o_ref[...] = h_ref[...] + y * lax.rsqrt(ms + RMS_EPS) * g_ref[...]


def _mm_norm_res(a, w, h, gain, tm=512, tk=512):
    M, K = a.shape
    N = w.shape[1]
    assert M % tm == 0 and K % tk == 0
    nk = K // tk
    return pl.pallas_call(
        functools.partial(_mm_norm_res_kernel, nk=nk),
        grid=(M // tm, nk),
        in_specs=[
            pl.BlockSpec((tm, tk), lambda i, k: (i, k)),
            pl.BlockSpec((tk, N), lambda i, k: (k, 0)),
            pl.BlockSpec((tm, N), lambda i, k: (i, 0)),
            pl.BlockSpec((1, N), lambda i, k: (0, 0)),
        ],
        out_specs=pl.BlockSpec((tm, N), lambda i, k: (i, 0)),
        out_shape=jax.ShapeDtypeStruct((M, N), F32),
        compiler_params=_cparams("parallel", "arbitrary"),
        name="out_proj_norm_res",
    )(a, w, h, gain.reshape(1, N))


def _pool_kernel(cur_ref, prev_ref, w_ref, scale_ref, o_ref, ext_ref, *, ts, gd):
    i = pl.program_id(1)
    ext_ref[0:POOL_HALO, :] = jnp.where(i > 0, prev_ref[0], 0.0)
    ext_ref[POOL_HALO:POOL_HALO + ts, :] = cur_ref[0]
    t = i * ts + lax.broadcasted_iota(jnp.int32, (ts, 1), 0)
    for g, win in enumerate(POOL_WINDOWS):
        cols = slice(g * gd, (g + 1) * gd)
        tok = ext_ref[POOL_HALO:POOL_HALO + ts, cols]
        acc = tok
        for back in range(1, win):
            acc = acc + ext_ref[POOL_HALO - back:POOL_HALO - back + ts, cols]
        count = jnp.minimum(t + 1, win).astype(F32)
        pooled = acc / count - tok
        y = jnp.dot(pooled.astype(BF16), w_ref[g], preferred_element_type=F32)
        o_ref[0, :, cols] = (y * scale_ref[:, cols]).astype(o_ref.dtype)


def _pool_mixer(p_in, w_pool, scale, ts=512):
    B, S, PW = p_in.shape
    gd = PW // POOL_GROUPS
    hb = ts // POOL_HALO
    return pl.pallas_call(
        functools.partial(_pool_kernel, ts=ts, gd=gd),
        grid=(B, S // ts),
        in_specs=[
            pl.BlockSpec((1, ts, PW), lambda b, i: (b, i, 0)),
            pl.BlockSpec((1, POOL_HALO, PW), lambda b, i: (b, jnp.maximum(i * hb - 1, 0), 0)),
            pl.BlockSpec((POOL_GROUPS, gd, gd), lambda b, i: (0, 0, 0)),
            pl.BlockSpec((1, PW), lambda b, i: (0, 0)),
        ],
        out_specs=pl.BlockSpec((1, ts, PW), lambda b, i: (b, i, 0)),
        out_shape=jax.ShapeDtypeStruct((B, S, PW), BF16),
        scratch_shapes=[pltpu.VMEM((POOL_HALO + ts, PW), F32)],
        compiler_params=_cparams("parallel", "arbitrary"),
        name="pool_mixer",
    )(p_in, p_in, w_pool.astype(BF16), scale.reshape(1, PW))


def _gelu_tanh(x):
    c = math.sqrt(2.0 / math.pi)
    return x * (0.5 * (1.0 + jnp.tanh(c * (x + 0.044715 * (x * x * x)))))


def _compress_kernel(x_ref, plo_ref, phi_ref, w1a_ref, w1b_ref, w2_ref, o_ref, tmp_ref, *, nch):
    x = x_ref[0, 0]
    a = jnp.dot((x + plo_ref[...]).astype(BF16), w1a_ref[0], preferred_element_type=F32)
    b = jnp.dot((x + phi_ref[...]).astype(BF16), w1b_ref[0], preferred_element_type=F32)
    tmp_ref[0:nch, :] = b
    tmp_ref[nch:nch + 8, :] = jnp.zeros((8, b.shape[1]), F32)
    pre = a + tmp_ref[1:nch + 1, :]
    out = jnp.dot(_gelu_tanh(pre).astype(BF16), w2_ref[0], preferred_element_type=F32)
    row = lax.broadcasted_iota(jnp.int32, out.shape, 0)
    o_ref[0, 0] = jnp.where(row < nch - 1, out, 0.0).astype(o_ref.dtype)


def _compress(x, pos, w1, w2):
    two, BG, nch, cw = x.shape
    hid = w1.shape[-1]
    dh = w2.shape[-1]
    half = CMP_BLOCK // 2
    plo = pos[:half].reshape(1, cw)
    phi = pos[half:].reshape(1, cw)
    w1 = w1.astype(BF16)
    return pl.pallas_call(
        functools.partial(_compress_kernel, nch=nch),
        grid=(two, BG),
        in_specs=[
            pl.BlockSpec((1, 1, nch, cw), lambda s, b: (s, b, 0, 0)),
            pl.BlockSpec((1, cw), lambda s, b: (0, 0)),
            pl.BlockSpec((1, cw), lambda s, b: (0, 0)),
            pl.BlockSpec((1, cw, hid), lambda s, b: (s, 0, 0)),
            pl.BlockSpec((1, cw, hid), lambda s, b: (s, 1, 0)),
            pl.BlockSpec((1, hid, dh), lambda s, b: (s, 0, 0)),
        ],
        out_specs=pl.BlockSpec((1, 1, nch, dh), lambda s, b: (s, b, 0, 0)),
        out_shape=jax.ShapeDtypeStruct((two, BG, nch, dh), BF16),
        scratch_shapes=[pltpu.VMEM((nch + 8, hid), F32)],
        compiler_params=_cparams("parallel", "arbitrary"),
        name="nsa_compress",
    )(x, plo, phi, w1, w1, w2.astype(BF16))


def _nt_dot(a, b):
    return lax.dot_general(a, b, (((1,), (1,)), ((), ())), preferred_element_type=F32)


def _split_dot(x, w):
    hi = x.astype(BF16)
    r1 = x - hi.astype(F32)
    mid = r1.astype(BF16)
    lo = (r1 - mid.astype(F32)).astype(BF16)
    return (jnp.dot(hi, w, preferred_element_type=F32) + jnp.dot(mid, w, preferred_element_type=F32)
            + jnp.dot(lo, w, preferred_element_type=F32))


def _nsa_kernel(q_ref, gl_ref, kc_ref, vc_ref, ks_ref, vs_ref, kw_ref, vw_ref, ov_ref, e_ref, o_ref,
                m_ref, l_ref, acc_ref, *, hg, n_cmp, n_slc, top_n):
    QB, DH = NSA_Q_BLOCK, NSA_HEAD_DIM
    qb = pl.program_id(1)
    t0 = qb * QB
    q = jnp.concatenate([q_ref[:, h * DH:(h + 1) * DH] for h in range(hg)], axis=0)
    t = t0 + lax.broadcasted_iota(jnp.int32, (QB, 1), 0)

    def tile_heads(x):
        return jnp.concatenate([x] * hg, axis=0)

    ncp = kc_ref.shape[2]
    ncol = lax.broadcasted_iota(jnp.int32, (QB, ncp), 1)
    valid_c = (ncol * CMP_STRIDE + (CMP_BLOCK - 1) <= t) & (ncol < n_cmp)
    s_c = jnp.minimum(_nt_dot(q, kc_ref[0, 0]), tile_heads(jnp.where(valid_c, BIG, NEG)))
    m_c = jnp.max(s_c, axis=-1, keepdims=True)
    p_c = jnp.exp(s_c - m_c) * tile_heads(jnp.where(valid_c, 1.0, 0.0))
    l_c = jnp.sum(p_c, axis=-1, keepdims=True)
    p_c = p_c / jnp.where(l_c > 0.0, l_c, 1.0)
    o_c = jnp.dot(p_c.astype(BF16), vc_ref[0, 0], preferred_element_type=F32)

    p_sum = p_c[0:QB]
    for h in range(1, hg):
        p_sum = p_sum + p_c[h * QB:(h + 1) * QB]
    imp = _split_dot(p_sum, ov_ref[...])
    blk = lax.broadcasted_iota(jnp.int32, (QB, LANES), 1)
    forced = (blk == 0) | (blk == qb) | (blk == qb - 1)
    val = jnp.where(forced, imp + FORCE_BONUS, jnp.where(blk > qb, -FORCE_BONUS, imp))
    if n_slc < LANES:
        val = jnp.where(blk < n_slc, val, -jnp.inf)
    rank = jnp.zeros((QB, LANES), F32)
    for other in range(n_slc):
        c = val[:, other:other + 1]
        ge = jnp.where(c >= val, 1.0, 0.0)
        gt = jnp.where(c > val, 1.0, 0.0)
        rank = rank + jnp.where(blk > other, ge, gt)
    sel = jnp.where(rank < top_n, 1.0, 0.0).astype(BF16)

    m_ref[...] = jnp.full(m_ref.shape, NEG, F32)
    l_ref[...] = jnp.zeros(l_ref.shape, F32)
    acc_ref[...] = jnp.zeros(acc_ref.shape, F32)
    col = lax.broadcasted_iota(jnp.int32, (QB, SEL_TILE), 1)

    def sel_step(kb, carry):
        start = pl.multiple_of(kb * SEL_TILE, SEL_TILE)
        k = ks_ref[0, pl.ds(start, SEL_TILE), :]
        v = vs_ref[0, pl.ds(start, SEL_TILE), :]
        chosen = jnp.dot(sel, e_ref[kb], preferred_element_type=F32)
        ok = (chosen > 0.5) & (start + col <= t)
        s = jnp.minimum(_nt_dot(q, k), tile_heads(jnp.where(ok, BIG, NEG)))
        m_prev = m_ref[...]
        m_new = jnp.maximum(m_prev, jnp.max(s, axis=-1, keepdims=True))
        alpha = jnp.exp(m_prev - m_new)
        p = jnp.exp(s - m_new)
        l_ref[...] = alpha * l_ref[...] + jnp.sum(p, axis=-1, keepdims=True)
        acc_ref[...] = alpha * acc_ref[...] + jnp.dot(p.astype(BF16), v, preferred_element_type=F32)
        m_ref[...] = m_new
        return carry

    lax.fori_loop(0, t0 // SEL_TILE + 1, sel_step, 0)
    o_s = acc_ref[...] / l_ref[...]

    ws = pl.multiple_of(jnp.maximum(t0 + QB - SWA_SPAN, 0), QB)
    kwin = kw_ref[0, pl.ds(ws, SWA_SPAN), :]
    vwin = vw_ref[0, pl.ds(ws, SWA_SPAN), :]
    dist = t - (ws + lax.broadcasted_iota(jnp.int32, (QB, SWA_SPAN), 1))
    valid_w = (dist >= 0) & (dist < SWA_WINDOW)
    s_w = jnp.minimum(_nt_dot(q, kwin), tile_heads(jnp.where(valid_w, BIG, NEG)))
    p_w = jnp.exp(s_w - jnp.max(s_w, axis=-1, keepdims=True))
    l_w = jnp.sum(p_w, axis=-1, keepdims=True)
    o_w = jnp.dot(p_w.astype(BF16), vwin, preferred_element_type=F32) / l_w

    gate = jax.nn.sigmoid(gl_ref[...])
    for h in range(hg):
        rows = slice(h * QB, (h + 1) * QB)
        g_c = gate[:, N_BRANCH * h:N_BRANCH * h + 1]
        g_s = gate[:, N_BRANCH * h + 1:N_BRANCH * h + 2]
        g_w = gate[:, N_BRANCH * h + 2:N_BRANCH * h + 3]
        out = g_c * o_c[rows] + g_s * o_s[rows] + g_w * o_w[rows]
        o_ref[:, h * DH:(h + 1) * DH] = out.astype(o_ref.dtype)


def _nsa(q, gates, kvc, kv, B, S):
    T, HW = q.shape
    G, DH, QB = NSA_KV_GROUPS, NSA_HEAD_DIM, NSA_Q_BLOCK
    hg = HW // DH // G
    n_cmp = (S - CMP_BLOCK) // CMP_STRIDE + 1
    ncp = kvc.shape[2]
    n_slc = S // SLC_BLOCK
    top_n = min(SLC_TOPN, n_slc)
    nqb = S // QB
    assert n_slc <= LANES and S % SEL_TILE == 0 and S >= SWA_SPAN

    cs = jnp.arange(ncp)[:, None] * CMP_STRIDE
    ss = jnp.arange(LANES)[None, :] * SLC_BLOCK
    overlap = jnp.clip(jnp.minimum(cs + CMP_BLOCK, ss + SLC_BLOCK) - jnp.maximum(cs, ss), 0) // CMP_STRIDE
    overlap = jnp.where((jnp.arange(ncp)[:, None] < n_cmp) & (jnp.arange(LANES)[None, :] < n_slc), overlap, 0)
    overlap = overlap.astype(BF16)
    key_blk = (jnp.arange(S) // SLC_BLOCK).reshape(S // SEL_TILE, 1, SEL_TILE)
    expand = (key_blk == jnp.arange(LANES)[None, :, None]).astype(BF16)

    def kv_spec(which):
        return pl.BlockSpec((1, S, DH), lambda bg, i: (bg // G, 0, which * G + bg % G))

    return pl.pallas_call(
        functools.partial(_nsa_kernel, hg=hg, n_cmp=n_cmp, n_slc=n_slc, top_n=top_n),
        grid=(B * G, nqb),
        in_specs=[
            pl.BlockSpec((QB, hg * DH), lambda bg, i: ((bg // G) * nqb + i, bg % G)),
            pl.BlockSpec((QB, LANES), lambda bg, i: ((bg // G) * nqb + i, bg % G)),
            pl.BlockSpec((1, 1, ncp, DH), lambda bg, i: (0, bg, 0, 0)),
            pl.BlockSpec((1, 1, ncp, DH), lambda bg, i: (1, bg, 0, 0)),
            kv_spec(0), kv_spec(1), kv_spec(2), kv_spec(3),
            pl.BlockSpec((ncp, LANES), lambda bg, i: (0, 0)),
            pl.BlockSpec((S // SEL_TILE, LANES, SEL_TILE), lambda bg, i: (0, 0, 0)),
        ],
        out_specs=pl.BlockSpec((QB, hg * DH), lambda bg, i: ((bg // G) * nqb + i, bg % G)),
        out_shape=jax.ShapeDtypeStruct((T, HW), BF16),
        scratch_shapes=[pltpu.VMEM((hg * QB, 1), F32), pltpu.VMEM((hg * QB, 1), F32),
                        pltpu.VMEM((hg * QB, DH), F32)],
        compiler_params=_cparams("parallel", "arbitrary"),
        name="nsa_attention",
    )(q, gates, kvc, kvc, kv, kv, kv, kv, overlap, expand)


def _mlstm_kernel(q_ref, kt_ref, v_ref, o_ref, gr_ref, gc_ref, hn_ref, y_ref, c_ref, n_ref, m_ref, *, L):
    ci = pl.program_id(1)

    @pl.when(ci == 0)
    def _():
        c_ref[...] = jnp.zeros_like(c_ref)
        n_ref[...] = jnp.zeros_like(n_ref)
        m_ref[...] = jnp.zeros_like(m_ref)

    q = q_ref[...]
    kt = kt_ref[0]
    v = v_ref[...]
    ig_row = gr_ref[0, 0, 0:1, :]
    lf_row = jax.nn.log_sigmoid(gr_ref[0, 0, 1:2, :])
    lf_col = jax.nn.log_sigmoid(gc_ref[0, 0, :, 1:2])
    m_prev = m_ref[0:1, 0:1]

    ti = lax.broadcasted_iota(jnp.int32, (L, L), 0)
    si = lax.broadcasted_iota(jnp.int32, (L, L), 1)
    causal = si <= ti
    b_col = jnp.sum(jnp.where(causal, lf_row, 0.0), axis=1, keepdims=True)
    b_row = jnp.sum(jnp.where(ti <= si, lf_col, 0.0), axis=0, keepdims=True)
    b_last = b_col[L - 1:L, :]

    dmat = jnp.where(causal, b_col - b_row + ig_row, -jnp.inf)
    a_col = b_col + m_prev
    m_t = jnp.maximum(a_col, jnp.max(dmat, axis=1, keepdims=True))
    wq = jnp.dot(q, kt, preferred_element_type=F32) * jnp.exp(dmat - m_t)
    inter = jnp.exp(a_col - m_t)
    q_c = jnp.dot(q, c_ref[...].astype(BF16), preferred_element_type=F32)
    q_n = jnp.dot(q, n_ref[...].astype(BF16), preferred_element_type=F32)[:, 0:1]
    num = inter * q_c + jnp.dot(wq.astype(BF16), v, preferred_element_type=F32)
    den = inter * q_n + jnp.sum(wq, axis=1, keepdims=True)
    h = num / jnp.maximum(jnp.abs(den), jnp.exp(-m_t))
    h = h * lax.rsqrt(jnp.mean(h * h, axis=-1, keepdims=True) + RMS_EPS) * hn_ref[0]
    y_ref[...] = (jax.nn.sigmoid(o_ref[...]) * h).astype(y_ref.dtype)

    g_row = b_last - b_row + ig_row
    m_new = jnp.maximum(b_last + m_prev, jnp.max(g_row, axis=1, keepdims=True))
    decay = jnp.exp(b_last + m_prev - m_new)
    kw_t = kt.astype(F32) * jnp.exp(g_row - m_new)
    c_ref[...] = decay * c_ref[...] + jnp.dot(kw_t.astype(BF16), v, preferred_element_type=F32)
    n_ref[...] = decay * n_ref[...] + jnp.sum(kw_t, axis=1, keepdims=True)
    m_ref[...] = jnp.broadcast_to(m_new, m_ref.shape)


def _mlstm(q, kt, v, o, gates, head_norm, B, S):
    T = q.shape[0]
    H = MLSTM_HEADS
    dk = q.shape[1] // H
    dv = v.shape[1] // H
    L = min(MLSTM_CHUNK, S)
    nc = S // L
    g = gates[:, :2 * H].reshape(B, nc, L, 2, H)
    g_row = jnp.transpose(g, (0, 4, 1, 3, 2)).reshape(B * H, nc, 2, L)
    g_col = jnp.transpose(g, (0, 4, 1, 2, 3)).reshape(B * H, nc, L, 2)
    return pl.pallas_call(
        functools.partial(_mlstm_kernel, L=L),
        grid=(B * H, nc),
        in_specs=[
            pl.BlockSpec((L, dk), lambda bh, c: ((bh // H) * nc + c, bh % H)),
            pl.BlockSpec((1, dk, L), lambda bh, c: (bh // H, bh % H, c)),
            pl.BlockSpec((L, dv), lambda bh, c: ((bh // H) * nc + c, bh % H)),
            pl.BlockSpec((L, dv), lambda bh, c: ((bh // H) * nc + c, bh % H)),
            pl.BlockSpec((1, 1, 2, L), lambda bh, c: (bh, c, 0, 0)),
            pl.BlockSpec((1, 1, L, 2), lambda bh, c: (bh, c, 0, 0)),
            pl.BlockSpec((1, 1, dv), lambda bh, c: (bh % H, 0, 0)),
        ],
        out_specs=pl.BlockSpec((L, dv), lambda bh, c: ((bh // H) * nc + c, bh % H)),
        out_shape=jax.ShapeDtypeStruct((T, H * dv), BF16),
        scratch_shapes=[pltpu.VMEM((dk, dv), F32), pltpu.VMEM((dk, LANES), F32), pltpu.VMEM((8, LANES), F32)],
        compiler_params=_cparams("parallel", "arbitrary"),
        name="mlstm",
    )(q, kt, v, o, g_row, g_col, head_norm.reshape(H, 1, dv))


def _ffn_in_kernel(u_ref, halo_ref, wg_ref, wu_ref, cw_ref, cb_ref, o_ref, lhs_ref, g_ref, *, tm, tiles_per_seq):
    i = pl.program_id(0)
    j = pl.program_id(1)

    @pl.when(j == 0)
    def _():
        halo = halo_ref[...]
        lhs_ref[0:CONV_HALO, :] = jnp.where(i % tiles_per_seq != 0, halo, jnp.zeros_like(halo))
        lhs_ref[CONV_HALO:CONV_HALO + tm, :] = u_ref[...]

    g_ref[...] = jnp.dot(lhs_ref[...], wg_ref[...], preferred_element_type=F32)
    up = jnp.dot(u_ref[...], wu_ref[...], preferred_element_type=F32)
    gate = cb_ref[...] + cw_ref[0:1, :] * g_ref[CONV_HALO:CONV_HALO + tm, :]
    for back in range(1, CONV_WIDTH):
        gate = gate + cw_ref[back:back + 1, :] * g_ref[CONV_HALO - back:CONV_HALO - back + tm, :]
    o_ref[...] = (gate * jax.nn.sigmoid(gate) * up).astype(o_ref.dtype)


def _ffn_in(u, wg, wu, conv_w, conv_b, S, tm=1024, tn=512):
    T, D = u.shape
    FP = wg.shape[1]
    tm = min(tm, S)
    assert T % tm == 0 and S % tm == 0 and FP % tn == 0
    hb = tm // CONV_HALO
    return pl.pallas_call(
        functools.partial(_ffn_in_kernel, tm=tm, tiles_per_seq=S // tm),
        grid=(T // tm, FP // tn),
        in_specs=[
            pl.BlockSpec((tm, D), lambda i, j: (i, 0)),
            pl.BlockSpec((CONV_HALO, D), lambda i, j: (jnp.maximum(i * hb - 1, 0), 0)),
            pl.BlockSpec((D, tn), lambda i, j: (0, j)),
            pl.BlockSpec((D, tn), lambda i, j: (0, j)),
            pl.BlockSpec((CONV_WIDTH, tn), lambda i, j: (0, j)),
            pl.BlockSpec((1, tn), lambda i, j: (0, j)),
        ],
        out_specs=pl.BlockSpec((tm, tn), lambda i, j: (i, j)),
        out_shape=jax.ShapeDtypeStruct((T, FP), BF16),
        scratch_shapes=[pltpu.VMEM((CONV_HALO + tm, D), BF16), pltpu.VMEM((CONV_HALO + tm, tn), F32)],
        compiler_params=_cparams("parallel", "arbitrary"),
        name="ffn_in_conv_act",
    )(u, u, wg, wu, conv_w, conv_b)


def _conv_ffn(h, ln_pre, ln_post, w_in, conv_w, conv_b, w_out, S):
    ff = conv_w.shape[-1]
    fp = _round_up(ff, 1024)
    u = _rmsnorm(h, ln_pre)
    wg = _pad_cols(w_in[:, :ff].astype(BF16), fp)
    wu = _pad_cols(w_in[:, ff:].astype(BF16), fp)
    act = _ffn_in(u, wg, wu, _pad_cols(conv_w, fp), _pad_cols(conv_b.reshape(1, ff), fp), S)
    w_out = jnp.pad(w_out.astype(BF16), ((0, fp - ff), (0, 0)))
    return _mm_norm_res(act, w_out, h, ln_post)


def _ab_layer(h, ln_pre, ln_post, w_in, pool_w, pool_scale, cmp_pos, ck_w1, ck_w2, cv_w1, cv_w2, w_out, B, S):
    T, D = h.shape
    G, DH = NSA_KV_GROUPS, NSA_HEAD_DIM
    pw = D // 4
    hw = D - pw
    hg = hw // DH // G
    kvw = G * DH
    u = _rmsnorm(h, ln_pre)
    w = w_in.astype(BF16)
    c0 = pw + hw
    p_in = _mm(u, w[:, :pw], F32)
    q = _mm(u, w[:, pw:c0], BF16, scale=DH ** -0.5)
    kv_cmp = _mm(u, w[:, c0:c0 + 2 * kvw], F32)
    kv = _mm(u, w[:, c0 + 2 * kvw:c0 + 6 * kvw], BF16)
    wgate = w[:, c0 + 6 * kvw:].reshape(D, G, hg * N_BRANCH)
    wgate = jnp.pad(wgate, ((0, 0), (0, 0), (0, LANES - hg * N_BRANCH))).reshape(D, G * LANES)
    gates = _mm(u, wgate, F32)

    y_a = _pool_mixer(p_in.reshape(B, S, pw), pool_w, pool_scale).reshape(T, pw)
    chunks = kv_cmp.reshape(B, S, 2, G, DH).transpose(2, 0, 3, 1, 4)
    chunks = chunks.reshape(2, B * G, S // CMP_STRIDE, CMP_STRIDE * DH)
    kvc = _compress(chunks, cmp_pos, jnp.stack([ck_w1, cv_w1]), jnp.stack([ck_w2, cv_w2]))
    y_b = _nsa(q, gates, kvc, kv.reshape(B, S, 4 * kvw), B, S)
    y = jnp.concatenate([y_a, y_b], axis=-1)
    return _mm_norm_res(y, w_out.astype(BF16), h, ln_post)


def _c_layer(h, ln_pre, ln_post, w_in, b_if, head_norm, w_out, B, S):
    T, D = h.shape
    H = MLSTM_HEADS
    dv = D // H
    dk = dv // 2
    qk = H * dk
    u = _rmsnorm(h, ln_pre)
    w = w_in.astype(BF16)
    q = _mm(u, w[:, :qk], BF16, scale=dk ** -0.5)
    k = _mm(u, w[:, qk:2 * qk], BF16)
    v = _mm(u, w[:, 2 * qk:2 * qk + D], BF16)
    o = _mm(u, w[:, 2 * qk + D:2 * qk + 2 * D], F32)
    gates = _mm(u, _pad_cols(w[:, 2 * qk + 2 * D:], LANES), F32, bias=_pad_cols(b_if.reshape(1, 2 * H), LANES))
    kt = k.reshape(B, S, qk).transpose(0, 2, 1)
    y = _mlstm(q, kt, v, o, gates, head_norm, B, S)
    return _mm_norm_res(y, w_out.astype(BF16), h, ln_post)


def kernel(x, ln_pre, ln_post, w_in_ab, pool_w, pool_scale, cmp_pos, cmp_k_w1, cmp_k_w2, cmp_v_w1, cmp_v_w2,
           w_out_ab, w_in_c, b_if_c, head_norm_c, w_out_c, ffn_ln_pre, ffn_ln_post, ffn_w_in, ffn_conv_w,
           ffn_conv_b, ffn_w_out):
    B, S, D = x.shape
    h = x.reshape(B * S, D)
    for layer in range(ln_pre.shape[0]):
        i = layer // 2
        if layer % 2 == 0:
            h = _ab_layer(h, ln_pre[layer], ln_post[layer], w_in_ab[i], pool_w[i], pool_scale[i], cmp_pos[i],
                          cmp_k_w1[i], cmp_k_w2[i], cmp_v_w1[i], cmp_v_w2[i], w_out_ab[i], B, S)
        else:
            h = _c_layer(h, ln_pre[layer], ln_post[layer], w_in_c[i], b_if_c[i], head_norm_c[i], w_out_c[i], B, S)
        h = _conv_ffn(h, ffn_ln_pre[layer], ffn_ln_post[layer], ffn_w_in[layer], ffn_conv_w[layer],
                      ffn_conv_b[layer], ffn_w_out[layer], S)
    return h.reshape(B, S, D)
```

```python
import functools
import math

import jax
import jax.numpy as jnp
from jax import lax
from jax.experimental import pallas as pl
from jax.experimental.pallas import tpu as pltpu

F32 = jnp.float32
BF16 = jnp.bfloat16

RMS_EPS = 1e-6
POOL_GROUPS = 4
POOL_WINDOWS = (2, 4, 8, 16)
POOL_HALO = 16
NSA_HEAD_DIM = 128
NSA_KV_GROUPS = 4
N_BRANCH = 3
CMP_BLOCK = 32
CMP_STRIDE = 16
SLC_BLOCK = 64
SLC_TOPN = 16
SWA_WINDOW = 512
FORCE_BONUS = 1e4
NEG = -1e30
BIG = 1e30
MLSTM_HEADS = 8
CONV_WIDTH = 3
CONV_HALO = 16

LANES = 128
NSA_Q_TILE = LANES
SEL_TILE = 512
SWA_SPAN = SWA_WINDOW + NSA_Q_TILE
MLSTM_CHUNK = 256
VMEM_LIMIT = 56 * 1024 * 1024


def _cparams(*sem):
    return pltpu.CompilerParams(dimension_semantics=sem, vmem_limit_bytes=VMEM_LIMIT)


def _round_up(n, m):
    return (n + m - 1) // m * m


def _pad_cols(w, n):
    return jnp.pad(w, ((0, 0), (0, n - w.shape[1])))


def _rmsnorm_kernel(x_ref, g_ref, o_ref):
    x = x_ref[...]
    ms = jnp.mean(x * x, axis=-1, keepdims=True)
    o_ref[...] = (x * lax.rsqrt(ms + RMS_EPS) * g_ref[...]).astype(o_ref.dtype)


def _rmsnorm(x, gain, tm=512):
    T, D = x.shape
    return pl.pallas_call(
        _rmsnorm_kernel,
        grid=(T // tm,),
        in_specs=[pl.BlockSpec((tm, D), lambda i: (i, 0)), pl.BlockSpec((1, D), lambda i: (0, 0))],
        out_specs=pl.BlockSpec((tm, D), lambda i: (i, 0)),
        out_shape=jax.ShapeDtypeStruct((T, D), BF16),
        compiler_params=_cparams("parallel"),
        name="rmsnorm",
    )(x, gain.reshape(1, D))


def _mm_kernel(a_ref, b_ref, bias_ref, o_ref, *, scale):
    acc = jnp.dot(a_ref[...], b_ref[...], preferred_element_type=F32)
    o_ref[...] = ((acc + bias_ref[...]) * scale).astype(o_ref.dtype)


def _mm(a, w, out_dtype, *, bias=None, scale=1.0, tm=1024, tn=512):
    M, K = a.shape
    N = w.shape[1]
    tm = min(tm, M)
    tn = min(tn, N)
    assert M % tm == 0 and N % tn == 0
    if bias is None:
        bias = jnp.zeros((1, N), F32)
    return pl.pallas_call(
        functools.partial(_mm_kernel, scale=scale),
        grid=(M // tm, N // tn),
        in_specs=[
            pl.BlockSpec((tm, K), lambda i, j: (i, 0)),
            pl.BlockSpec((K, tn), lambda i, j: (0, j)),
            pl.BlockSpec((1, tn), lambda i, j: (0, j)),
        ],
        out_specs=pl.BlockSpec((tm, tn), lambda i, j: (i, j)),
        out_shape=jax.ShapeDtypeStruct((M, N), out_dtype),
        compiler_params=_cparams("parallel", "arbitrary"),
        name="proj",
    )(a, w, bias)


def _mm_norm_res_kernel(a_ref, w_ref, h_ref, g_ref, o_ref, *, nk):
    k = pl.program_id(1)

    @pl.when(k == 0)
    def _():
        o_ref[...] = jnp.zeros_like(o_ref)

    o_ref[...] += jnp.dot(a_ref[...], w_ref[...], preferred_element_type=F32)

    @pl.when(k == nk - 1)
    def _():
        y = o_ref[...]
        ms = jnp.mean(y * y, axis=-1, keepdims=True)
        o_ref[...] = h_ref[...] + y * lax.rsqrt(ms + RMS_EPS) * g_ref[...]


def _mm_norm_res(a, w, h, gain, tm=512, tk=512):
    M, K = a.shape
    N = w.shape[1]
    assert M % tm == 0 and K % tk == 0
    nk = K // tk
    return pl.pallas_call(
        functools.partial(_mm_norm_res_kernel, nk=nk),
        grid=(M // tm, nk),
        in_specs=[
            pl.BlockSpec((tm, tk), lambda i, k: (i, k)),
            pl.BlockSpec((tk, N), lambda i, k: (k, 0)),
            pl.BlockSpec((tm, N), lambda i, k: (i, 0)),
            pl.BlockSpec((1, N), lambda i, k: (0, 0)),
        ],
        out_specs=pl.BlockSpec((tm, N), lambda i, k: (i, 0)),
        out_shape=jax.ShapeDtypeStruct((M, N), F32),
        compiler_params=_cparams("parallel", "arbitrary"),
        name="out_proj_norm_res",
    )(a, w, h, gain.reshape(1, N))


def _pool_kernel(cur_ref, prev_ref, w_ref, scale_ref, o_ref, ext_ref, *, ts, gd):
    i = pl.program_id(1)
    ext_ref[0:POOL_HALO, :] = jnp.where(i > 0, prev_ref[0], 0.0)
    ext_ref[POOL_HALO:POOL_HALO + ts, :] = cur_ref[0]
    t = i * ts + lax.broadcasted_iota(jnp.int32, (ts, 1), 0)
    for g, win in enumerate(POOL_WINDOWS):
        cols = slice(g * gd, (g + 1) * gd)
        tok = ext_ref[POOL_HALO:POOL_HALO + ts, cols]
        acc = tok
        for back in range(1, win):
            acc = acc + ext_ref[POOL_HALO - back:POOL_HALO - back + ts, cols]
        count = jnp.minimum(t + 1, win).astype(F32)
        pooled = acc / count - tok
        y = jnp.dot(pooled.astype(BF16), w_ref[g], preferred_element_type=F32)
        o_ref[0, :, cols] = (y * scale_ref[:, cols]).astype(o_ref.dtype)


def _pool_mixer(p_in, w_pool, scale, ts=512):
    B, S, PW = p_in.shape
    gd = PW // POOL_GROUPS
    hb = ts // POOL_HALO
    return pl.pallas_call(
        functools.partial(_pool_kernel, ts=ts, gd=gd),
        grid=(B, S // ts),
        in_specs=[
            pl.BlockSpec((1, ts, PW), lambda b, i: (b, i, 0)),
            pl.BlockSpec((1, POOL_HALO, PW), lambda b, i: (b, jnp.maximum(i * hb - 1, 0), 0)),
            pl.BlockSpec((POOL_GROUPS, gd, gd), lambda b, i: (0, 0, 0)),
            pl.BlockSpec((1, PW), lambda b, i: (0, 0)),
        ],
        out_specs=pl.BlockSpec((1, ts, PW), lambda b, i: (b, i, 0)),
        out_shape=jax.ShapeDtypeStruct((B, S, PW), BF16),
        scratch_shapes=[pltpu.VMEM((POOL_HALO + ts, PW), F32)],
        compiler_params=_cparams("parallel", "arbitrary"),
        name="pool_mixer",
    )(p_in, p_in, w_pool.astype(BF16), scale.reshape(1, PW))


def _gelu_tanh(x):
    c = math.sqrt(2.0 / math.pi)
    return x * (0.5 * (1.0 + jnp.tanh(c * (x + 0.044715 * (x * x * x)))))


def _compress_kernel(x_ref, plo_ref, phi_ref, w1a_ref, w1b_ref, w2_ref, o_ref, tmp_ref, *, nch):
    x = x_ref[0, 0]
    a = jnp.dot((x + plo_ref[...]).astype(BF16), w1a_ref[0], preferred_element_type=F32)
    b = jnp.dot((x + phi_ref[...]).astype(BF16), w1b_ref[0], preferred_element_type=F32)
    tmp_ref[0:nch, :] = b
    tmp_ref[nch:nch + 8, :] = jnp.zeros((8, b.shape[1]), F32)
    pre = a + tmp_ref[1:nch + 1, :]
    out = jnp.dot(_gelu_tanh(pre).astype(BF16), w2_ref[0], preferred_element_type=F32)
    row = lax.broadcasted_iota(jnp.int32, out.shape, 0)
    o_ref[0, 0] = jnp.where(row < nch - 1, out, 0.0).astype(o_ref.dtype)


def _compress(x, pos, w1, w2):
    two, BG, nch, cw = x.shape
    hid = w1.shape[-1]
    dh = w2.shape[-1]
    half = CMP_BLOCK // 2
    plo = pos[:half].reshape(1, cw)
    phi = pos[half:].reshape(1, cw)
    w1 = w1.astype(BF16)
    return pl.pallas_call(
        functools.partial(_compress_kernel, nch=nch),
        grid=(two, BG),
        in_specs=[
            pl.BlockSpec((1, 1, nch, cw), lambda s, b: (s, b, 0, 0)),
            pl.BlockSpec((1, cw), lambda s, b: (0, 0)),
            pl.BlockSpec((1, cw), lambda s, b: (0, 0)),
            pl.BlockSpec((1, cw, hid), lambda s, b: (s, 0, 0)),
            pl.BlockSpec((1, cw, hid), lambda s, b: (s, 1, 0)),
            pl.BlockSpec((1, hid, dh), lambda s, b: (s, 0, 0)),
        ],
        out_specs=pl.BlockSpec((1, 1, nch, dh), lambda s, b: (s, b, 0, 0)),
        out_shape=jax.ShapeDtypeStruct((two, BG, nch, dh), BF16),
        scratch_shapes=[pltpu.VMEM((nch + 8, hid), F32)],
        compiler_params=_cparams("parallel", "arbitrary"),
        name="nsa_compress",
    )(x, plo, phi, w1, w1, w2.astype(BF16))


def _nt_dot(a, b):
    return lax.dot_general(a, b, (((1,), (1,)), ((), ())), preferred_element_type=F32)


def _split_dot(w, x):
    hi = x.astype(BF16)
    r1 = x - hi.astype(F32)
    mid = r1.astype(BF16)
    lo = (r1 - mid.astype(F32)).astype(BF16)
    return (jnp.dot(w, hi, preferred_element_type=F32) + jnp.dot(w, mid, preferred_element_type=F32)
            + jnp.dot(w, lo, preferred_element_type=F32))


def _nsa_kernel(q_ref, gl_ref, kc_ref, vct_ref, ks_ref, vst_ref, kw_ref, vwt_ref, ovt_ref, et_ref, o_ref,
                m_ref, l_ref, acc_ref, *, hg, n_cmp, n_slc, top_n):
    QT, DH = NSA_Q_TILE, NSA_HEAD_DIM
    t0 = pl.program_id(1) * QT
    q = jnp.concatenate([q_ref[:, h * DH:(h + 1) * DH] for h in range(hg)], axis=0)
    lane = lax.broadcasted_iota(jnp.int32, (1, QT), 1)
    t = t0 + lane

    def capped(s, cap):
        return jnp.concatenate([jnp.minimum(s[:, h * QT:(h + 1) * QT], cap) for h in range(hg)], axis=1)

    def softmax_cols(s):
        p = jnp.exp(s - jnp.max(s, axis=0, keepdims=True))
        return p, jnp.sum(p, axis=0, keepdims=True)

    ncp = kc_ref.shape[2]
    nrow = lax.broadcasted_iota(jnp.int32, (ncp, QT), 0)
    valid_c = (nrow * CMP_STRIDE + (CMP_BLOCK - 1) <= t) & (nrow < n_cmp)
    p_c, _ = softmax_cols(capped(_nt_dot(kc_ref[0, 0], q), jnp.where(valid_c, BIG, NEG)))
    p_c = p_c * jnp.concatenate([jnp.where(valid_c, 1.0, 0.0)] * hg, axis=1)
    l_c = jnp.sum(p_c, axis=0, keepdims=True)
    p_c = p_c / jnp.where(l_c > 0.0, l_c, 1.0)
    o_c = jnp.dot(vct_ref[0], p_c.astype(BF16), preferred_element_type=F32)

    p_sum = p_c[:, 0:QT]
    for h in range(1, hg):
        p_sum = p_sum + p_c[:, h * QT:(h + 1) * QT]
    imp = _split_dot(ovt_ref[...], p_sum)
    blk = lax.broadcasted_iota(jnp.int32, (LANES, QT), 0)
    cur = t0 // SLC_BLOCK + jnp.zeros((1, QT), jnp.int32)
    for k in range(1, QT // SLC_BLOCK):
        cur = cur + jnp.where(lane >= k * SLC_BLOCK, 1, 0)
    forced = (blk == 0) | (blk == cur) | (blk == cur - 1)
    val = jnp.where(forced, imp + FORCE_BONUS, jnp.where(blk > cur, -FORCE_BONUS, imp))
    if n_slc < LANES:
        val = jnp.where(blk < n_slc, val, -jnp.inf)
    nslab = (n_slc + 7) // 8
    slabs = [val[8 * r:8 * r + 8, :] for r in range(nslab)]
    ranks = [jnp.zeros((8, QT), F32) for _ in range(nslab)]
    sub = lax.broadcasted_iota(jnp.int32, (8, QT), 0)
    for other in range(n_slc):
        c = val[other:other + 1, :]
        for r in range(nslab):
            if 8 * r > other:
                beat = c >= slabs[r]
            elif 8 * r + 7 < other:
                beat = c > slabs[r]
            else:
                beat = (c > slabs[r]) | ((c == slabs[r]) & (sub + 8 * r > other))
            ranks[r] = ranks[r] + jnp.where(beat, 1.0, 0.0)
    sel = jnp.concatenate([jnp.where(rk < top_n, 1.0, 0.0) for rk in ranks], axis=0)
    if nslab * 8 < LANES:
        sel = jnp.concatenate([sel, jnp.zeros((LANES - nslab * 8, QT), F32)], axis=0)
    sel = sel.astype(BF16)

    m_ref[...] = jnp.full(m_ref.shape, NEG, F32)
    l_ref[...] = jnp.zeros(l_ref.shape, F32)
    acc_ref[...] = jnp.zeros(acc_ref.shape, F32)
    key = lax.broadcasted_iota(jnp.int32, (SEL_TILE, QT), 0)

    def sel_step(kb, carry):
        start = pl.multiple_of(kb * SEL_TILE, SEL_TILE)
        chosen = jnp.dot(et_ref[kb], sel, preferred_element_type=F32)
        ok = (chosen > 0.5) & (start + key <= t)
        s = capped(_nt_dot(ks_ref[0, pl.ds(start, SEL_TILE), :], q), jnp.where(ok, BIG, NEG))
        m_prev = m_ref[...]
        m_new = jnp.maximum(m_prev, jnp.max(s, axis=0, keepdims=True))
        alpha = jnp.exp(m_prev - m_new)
        p = jnp.exp(s - m_new)
        l_ref[...] = alpha * l_ref[...] + jnp.sum(p, axis=0, keepdims=True)
        acc_ref[...] = alpha * acc_ref[...] + jnp.dot(vst_ref[0, kb], p.astype(BF16), preferred_element_type=F32)
        m_ref[...] = m_new
        return carry

    lax.fori_loop(0, t0 // SEL_TILE + 1, sel_step, 0)
    o_s = acc_ref[...] / l_ref[...]

    ws = pl.multiple_of(jnp.maximum(t0 + QT - SWA_SPAN, 0), LANES)
    wb = ws // LANES
    vwin = jnp.concatenate([vwt_ref[0, wb + j] for j in range(SWA_SPAN // LANES)], axis=1)
    dist = t - (ws + lax.broadcasted_iota(jnp.int32, (SWA_SPAN, QT), 0))
    valid_w = (dist >= 0) & (dist < SWA_WINDOW)
    p_w, l_w = softmax_cols(capped(_nt_dot(kw_ref[0, pl.ds(ws, SWA_SPAN), :], q), jnp.where(valid_w, BIG, NEG)))
    o_w = jnp.dot(vwin, p_w.astype(BF16), preferred_element_type=F32) / l_w

    gate = jax.nn.sigmoid(gl_ref[0, 0])
    for h in range(hg):
        cols = slice(h * QT, (h + 1) * QT)
        r = N_BRANCH * h
        out = gate[r:r + 1] * o_c[:, cols] + gate[r + 1:r + 2] * o_s[:, cols] + gate[r + 2:r + 3] * o_w[:, cols]
        o_ref[:, h * DH:(h + 1) * DH] = out.T.astype(o_ref.dtype)


def _nsa(q, gates, kvc, kv, B, S):
    T, HW = q.shape
    G, DH, QT = NSA_KV_GROUPS, NSA_HEAD_DIM, NSA_Q_TILE
    hg = HW // DH // G
    n_cmp = (S - CMP_BLOCK) // CMP_STRIDE + 1
    ncp = kvc.shape[2]
    n_slc = S // SLC_BLOCK
    top_n = min(SLC_TOPN, n_slc)
    nqt = S // QT
    kvw = G * DH
    assert n_slc <= LANES and S % SEL_TILE == 0 and S >= SWA_SPAN and QT == LANES

    cs = jnp.arange(ncp)[None, :] * CMP_STRIDE
    ss = jnp.arange(LANES)[:, None] * SLC_BLOCK
    overlap = jnp.clip(jnp.minimum(cs + CMP_BLOCK, ss + SLC_BLOCK) - jnp.maximum(cs, ss), 0) // CMP_STRIDE
    overlap = jnp.where((jnp.arange(ncp)[None, :] < n_cmp) & (jnp.arange(LANES)[:, None] < n_slc), overlap, 0)
    overlap = overlap.astype(BF16)
    key_blk = (jnp.arange(S) // SLC_BLOCK).reshape(S // SEL_TILE, SEL_TILE, 1)
    expand = (key_blk == jnp.arange(LANES)[None, None, :]).astype(BF16)

    gr = _round_up(N_BRANCH * hg, 8)
    gl = gates[:, :G * hg * N_BRANCH].reshape(B, nqt, QT, G, hg * N_BRANCH)
    gl = jnp.pad(gl, ((0, 0),) * 4 + ((0, gr - hg * N_BRANCH),)).transpose(0, 3, 1, 4, 2).reshape(B * G, nqt, gr, QT)

    def v_tiles(which, tile):
        v = kv[:, :, which * kvw:(which + 1) * kvw].reshape(B, S // tile, tile, G, DH)
        return v.transpose(0, 3, 1, 4, 2).reshape(B * G, S // tile, DH, tile)

    def k_spec(which):
        return pl.BlockSpec((1, S, DH), lambda bg, i: (bg // G, 0, which * G + bg % G))

    return pl.pallas_call(
        functools.partial(_nsa_kernel, hg=hg, n_cmp=n_cmp, n_slc=n_slc, top_n=top_n),
        grid=(B * G, nqt),
        in_specs=[
            pl.BlockSpec((QT, hg * DH), lambda bg, i: ((bg // G) * nqt + i, bg % G)),
            pl.BlockSpec((1, 1, gr, QT), lambda bg, i: (bg, i, 0, 0)),
            pl.BlockSpec((1, 1, ncp, DH), lambda bg, i: (0, bg, 0, 0)),
            pl.BlockSpec((1, DH, ncp), lambda bg, i: (bg, 0, 0)),
            k_spec(0),
            pl.BlockSpec((1, S // SEL_TILE, DH, SEL_TILE), lambda bg, i: (bg, 0, 0, 0)),
            k_spec(2),
            pl.BlockSpec((1, S // LANES, DH, LANES), lambda bg, i: (bg, 0, 0, 0)),
            pl.BlockSpec((LANES, ncp), lambda bg, i: (0, 0)),
            pl.BlockSpec((S // SEL_TILE, SEL_TILE, LANES), lambda bg, i: (0, 0, 0)),
        ],
        out_specs=pl.BlockSpec((QT, hg * DH), lambda bg, i: ((bg // G) * nqt + i, bg % G)),
        out_shape=jax.ShapeDtypeStruct((T, HW), BF16),
        scratch_shapes=[pltpu.VMEM((1, hg * QT), F32), pltpu.VMEM((1, hg * QT), F32),
                        pltpu.VMEM((DH, hg * QT), F32)],
        compiler_params=_cparams("parallel", "arbitrary"),
        name="nsa_attention",
    )(q, gl, kvc, jnp.swapaxes(kvc[1], 1, 2), kv, v_tiles(1, SEL_TILE), kv, v_tiles(3, LANES), overlap, expand)


def _mlstm_kernel(q_ref, kt_ref, v_ref, o_ref, gr_ref, gc_ref, hn_ref, y_ref, c_ref, n_ref, m_ref, *, L):
    ci = pl.program_id(1)

    @pl.when(ci == 0)
    def _():
        c_ref[...] = jnp.zeros_like(c_ref)
        n_ref[...] = jnp.zeros_like(n_ref)
        m_ref[...] = jnp.zeros_like(m_ref)

    q = q_ref[...]
    kt = kt_ref[0]
    v = v_ref[...]
    ig_row = gr_ref[0, 0, 0:1, :]
    lf_row = jax.nn.log_sigmoid(gr_ref[0, 0, 1:2, :])
    lf_col = jax.nn.log_sigmoid(gc_ref[0, 0, :, 1:2])
    m_prev = m_ref[0:1, 0:1]

    ti = lax.broadcasted_iota(jnp.int32, (L, L), 0)
    si = lax.broadcasted_iota(jnp.int32, (L, L), 1)
    causal = si <= ti
    b_col = jnp.sum(jnp.where(causal, lf_row, 0.0), axis=1, keepdims=True)
    b_row = jnp.sum(jnp.where(ti <= si, lf_col, 0.0), axis=0, keepdims=True)
    b_last = b_col[L - 1:L, :]

    dmat = jnp.where(causal, b_col - b_row + ig_row, -jnp.inf)
    a_col = b_col + m_prev
    m_t = jnp.maximum(a_col, jnp.max(dmat, axis=1, keepdims=True))
    wq = jnp.dot(q, kt, preferred_element_type=F32) * jnp.exp(dmat - m_t)
    inter = jnp.exp(a_col - m_t)
    q_c = jnp.dot(q, c_ref[...].astype(BF16), preferred_element_type=F32)
    q_n = jnp.dot(q, n_ref[...].astype(BF16), preferred_element_type=F32)[:, 0:1]
    num = inter * q_c + jnp.dot(wq.astype(BF16), v, preferred_element_type=F32)
    den = inter * q_n + jnp.sum(wq, axis=1, keepdims=True)
    h = num / jnp.maximum(jnp.abs(den), jnp.exp(-m_t))
    h = h * lax.rsqrt(jnp.mean(h * h, axis=-1, keepdims=True) + RMS_EPS) * hn_ref[0]
    y_ref[...] = (jax.nn.sigmoid(o_ref[...]) * h).astype(y_ref.dtype)

    g_row = b_last - b_row + ig_row
    m_new = jnp.maximum(b_last + m_prev, jnp.max(g_row, axis=1, keepdims=True))
    decay = jnp.exp(b_last + m_prev - m_new)
    kw_t = kt.astype(F32) * jnp.exp(g_row - m_new)
    c_ref[...] = decay * c_ref[...] + jnp.dot(kw_t.astype(BF16), v, preferred_element_type=F32)
    n_ref[...] = decay * n_ref[...] + jnp.sum(kw_t, axis=1, keepdims=True)
    m_ref[...] = jnp.broadcast_to(m_new, m_ref.shape)


def _mlstm(q, kt, v, o, gates, head_norm, B, S):
    T = q.shape[0]
    H = MLSTM_HEADS
    dk = q.shape[1] // H
    dv = v.shape[1] // H
    L = min(MLSTM_CHUNK, S)
    nc = S // L
    g = gates[:, :2 * H].reshape(B, nc, L, 2, H)
    g_row = jnp.transpose(g, (0, 4, 1, 3, 2)).reshape(B * H, nc, 2, L)
    g_col = jnp.transpose(g, (0, 4, 1, 2, 3)).reshape(B * H, nc, L, 2)
    return pl.pallas_call(
        functools.partial(_mlstm_kernel, L=L),
        grid=(B * H, nc),
        in_specs=[
            pl.BlockSpec((L, dk), lambda bh, c: ((bh // H) * nc + c, bh % H)),
            pl.BlockSpec((1, dk, L), lambda bh, c: (bh // H, bh % H, c)),
            pl.BlockSpec((L, dv), lambda bh, c: ((bh // H) * nc + c, bh % H)),
            pl.BlockSpec((L, dv), lambda bh, c: ((bh // H) * nc + c, bh % H)),
            pl.BlockSpec((1, 1, 2, L), lambda bh, c: (bh, c, 0, 0)),
            pl.BlockSpec((1, 1, L, 2), lambda bh, c: (bh, c, 0, 0)),
            pl.BlockSpec((1, 1, dv), lambda bh, c: (bh % H, 0, 0)),
        ],
        out_specs=pl.BlockSpec((L, dv), lambda bh, c: ((bh // H) * nc + c, bh % H)),
        out_shape=jax.ShapeDtypeStruct((T, H * dv), BF16),
        scratch_shapes=[pltpu.VMEM((dk, dv), F32), pltpu.VMEM((dk, LANES), F32), pltpu.VMEM((8, LANES), F32)],
        compiler_params=_cparams("parallel", "arbitrary"),
        name="mlstm",
    )(q, kt, v, o, g_row, g_col, head_norm.reshape(H, 1, dv))


def _ffn_in_kernel(u_ref, halo_ref, wg_ref, wu_ref, cw_ref, cb_ref, o_ref, lhs_ref, g_ref, *, tm, tiles_per_seq):
    i = pl.program_id(0)
    j = pl.program_id(1)

    @pl.when(j == 0)
    def _():
        halo = halo_ref[...]
        lhs_ref[0:CONV_HALO, :] = jnp.where(i % tiles_per_seq != 0, halo, jnp.zeros_like(halo))
        lhs_ref[CONV_HALO:CONV_HALO + tm, :] = u_ref[...]

    g_ref[...] = jnp.dot(lhs_ref[...], wg_ref[...], preferred_element_type=F32)
    up = jnp.dot(u_ref[...], wu_ref[...], preferred_element_type=F32)
    gate = cb_ref[...] + cw_ref[0:1, :] * g_ref[CONV_HALO:CONV_HALO + tm, :]
    for back in range(1, CONV_WIDTH):
        gate = gate + cw_ref[back:back + 1, :] * g_ref[CONV_HALO - back:CONV_HALO - back + tm, :]
    o_ref[...] = (gate * jax.nn.sigmoid(gate) * up).astype(o_ref.dtype)


def _ffn_in(u, wg, wu, conv_w, conv_b, S, tm=1024, tn=512):
    T, D = u.shape
    FP = wg.shape[1]
    tm = min(tm, S)
    assert T % tm == 0 and S % tm == 0 and FP % tn == 0
    hb = tm // CONV_HALO
    return pl.pallas_call(
        functools.partial(_ffn_in_kernel, tm=tm, tiles_per_seq=S // tm),
        grid=(T // tm, FP // tn),
        in_specs=[
            pl.BlockSpec((tm, D), lambda i, j: (i, 0)),
            pl.BlockSpec((CONV_HALO, D), lambda i, j: (jnp.maximum(i * hb - 1, 0), 0)),
            pl.BlockSpec((D, tn), lambda i, j: (0, j)),
            pl.BlockSpec((D, tn), lambda i, j: (0, j)),
            pl.BlockSpec((CONV_WIDTH, tn), lambda i, j: (0, j)),
            pl.BlockSpec((1, tn), lambda i, j: (0, j)),
        ],
        out_specs=pl.BlockSpec((tm, tn), lambda i, j: (i, j)),
        out_shape=jax.ShapeDtypeStruct((T, FP), BF16),
        scratch_shapes=[pltpu.VMEM((CONV_HALO + tm, D), BF16), pltpu.VMEM((CONV_HALO + tm, tn), F32)],
        compiler_params=_cparams("parallel", "arbitrary"),
        name="ffn_in_conv_act",
    )(u, u, wg, wu, conv_w, conv_b)


def _conv_ffn(h, ln_pre, ln_post, w_in, conv_w, conv_b, w_out, S):
    ff = conv_w.shape[-1]
    fp = _round_up(ff, 1024)
    u = _rmsnorm(h, ln_pre)
    wg = _pad_cols(w_in[:, :ff].astype(BF16), fp)
    wu = _pad_cols(w_in[:, ff:].astype(BF16), fp)
    act = _ffn_in(u, wg, wu, _pad_cols(conv_w, fp), _pad_cols(conv_b.reshape(1, ff), fp), S)
    w_out = jnp.pad(w_out.astype(BF16), ((0, fp - ff), (0, 0)))
    return _mm_norm_res(act, w_out, h, ln_post)


def _ab_layer(h, ln_pre, ln_post, w_in, pool_w, pool_scale, cmp_pos, ck_w1, ck_w2, cv_w1, cv_w2, w_out, B, S):
    T, D = h.shape
    G, DH = NSA_KV_GROUPS, NSA_HEAD_DIM
    pw = D // 4
    hw = D - pw
    hg = hw // DH // G
    kvw = G * DH
    u = _rmsnorm(h, ln_pre)
    w = w_in.astype(BF16)
    c0 = pw + hw
    p_in = _mm(u, w[:, :pw], F32)
    q = _mm(u, w[:, pw:c0], BF16, scale=DH ** -0.5)
    kv_cmp = _mm(u, w[:, c0:c0 + 2 * kvw], F32)
    kv = _mm(u, w[:, c0 + 2 * kvw:c0 + 6 * kvw], BF16)
    gates = _mm(u, _pad_cols(w[:, c0 + 6 * kvw:], LANES), F32)

    y_a = _pool_mixer(p_in.reshape(B, S, pw), pool_w, pool_scale).reshape(T, pw)
    chunks = kv_cmp.reshape(B, S, 2, G, DH).transpose(2, 0, 3, 1, 4)
    chunks = chunks.reshape(2, B * G, S // CMP_STRIDE, CMP_STRIDE * DH)
    kvc = _compress(chunks, cmp_pos, jnp.stack([ck_w1, cv_w1]), jnp.stack([ck_w2, cv_w2]))
    y_b = _nsa(q, gates, kvc, kv.reshape(B, S, 4 * kvw), B, S)
    y = jnp.concatenate([y_a, y_b], axis=-1)
    return _mm_norm_res(y, w_out.astype(BF16), h, ln_post)


def _c_layer(h, ln_pre, ln_post, w_in, b_if, head_norm, w_out, B, S):
    T, D = h.shape
    H = MLSTM_HEADS
    dv = D // H
    dk = dv // 2
    qk = H * dk
    u = _rmsnorm(h, ln_pre)
    w = w_in.astype(BF16)
    q = _mm(u, w[:, :qk], BF16, scale=dk ** -0.5)
    k = _mm(u, w[:, qk:2 * qk], BF16)
    v = _mm(u, w[:, 2 * qk:2 * qk + D], BF16)
    o = _mm(u, w[:, 2 * qk + D:2 * qk + 2 * D], F32)
    gates = _mm(u, _pad_cols(w[:, 2 * qk + 2 * D:], LANES), F32, bias=_pad_cols(b_if.reshape(1, 2 * H), LANES))
    kt = k.reshape(B, S, qk).transpose(0, 2, 1)
    y = _mlstm(q, kt, v, o, gates, head_norm, B, S)
    return _mm_norm_res(y, w_out.astype(BF16), h, ln_post)


def kernel(x, ln_pre, ln_post, w_in_ab, pool_w, pool_scale, cmp_pos, cmp_k_w1, cmp_k_w2, cmp_v_w1, cmp_v_w2,
           w_out_ab, w_in_c, b_if_c, head_norm_c, w_out_c, ffn_ln_pre, ffn_ln_post, ffn_w_in, ffn_conv_w,
           ffn_conv_b, ffn_w_out):
    B, S, D = x.shape
    h = x.reshape(B * S, D)
    for layer in range(ln_pre.shape[0]):
        i = layer // 2
        if layer % 2 == 0:
            h = _ab_layer(h, ln_pre[layer], ln_post[layer], w_in_ab[i], pool_w[i], pool_scale[i], cmp_pos[i],
                          cmp_k_w1[i], cmp_k_w2[i], cmp_v_w1[i], cmp_v_w2[i], w_out_ab[i], B, S)
        else:
            h = _c_layer(h, ln_pre[layer], ln_post[layer], w_in_c[i], b_if_c[i], head_norm_c[i], w_out_c[i], B, S)
        h = _conv_ffn(h, ffn_ln_pre[layer], ffn_ln_post[layer], ffn_w_in[layer], ffn_conv_w[layer],
                      ffn_conv_b[layer], ffn_w_out[layer], S)
    return h.reshape(B, S, D)
```

```python
import functools
import math

import jax
import jax.numpy as jnp
from jax import lax
from jax.experimental import pallas as pl
from jax.experimental.pallas import tpu as pltpu

F32 = jnp.float32
BF16 = jnp.bfloat16

RMS_EPS = 1e-6
POOL_GROUPS = 4
POOL_WINDOWS = (2, 4, 8, 16)
POOL_HALO = 16
NSA_HEAD_DIM = 128
NSA_KV_GROUPS = 4
N_BRANCH = 3
CMP_BLOCK = 32
CMP_STRIDE = 16
SLC_BLOCK = 64
SLC_TOPN = 16
SWA_WINDOW = 512
FORCE_BONUS = 1e4
NEG = -1e30
BIG = 1e30
MLSTM_HEADS = 8
CONV_WIDTH = 3
CONV_HALO = 16

LANES = 128
LOG2E = math.log2(math.e)
NSA_Q_TILE = LANES
SEL_TILE = 512
RANK_SIZES = 4
SWA_SPAN = SWA_WINDOW + NSA_Q_TILE
MLSTM_CHUNK = 256
VMEM_LIMIT = 56 * 1024 * 1024


def _cparams(*sem):
    return pltpu.CompilerParams(dimension_semantics=sem, vmem_limit_bytes=VMEM_LIMIT)


def _round_up(n, m):
    return (n + m - 1) // m * m


def _pad_cols(w, n):
    return jnp.pad(w, ((0, 0), (0, n - w.shape[1])))


def _rmsnorm_kernel(x_ref, g_ref, o_ref):
    x = x_ref[...]
    ms = jnp.mean(x * x, axis=-1, keepdims=True)
    o_ref[...] = (x * lax.rsqrt(ms + RMS_EPS) * g_ref[...]).astype(o_ref.dtype)


def _rmsnorm(x, gain, tm=512):
    T, D = x.shape
    return pl.pallas_call(
        _rmsnorm_kernel,
        grid=(T // tm,),
        in_specs=[pl.BlockSpec((tm, D), lambda i: (i, 0)), pl.BlockSpec((1, D), lambda i: (0, 0))],
        out_specs=pl.BlockSpec((tm, D), lambda i: (i, 0)),
        out_shape=jax.ShapeDtypeStruct((T, D), BF16),
        compiler_params=_cparams("parallel"),
        name="rmsnorm",
    )(x, gain.reshape(1, D))


def _mm_kernel(a_ref, b_ref, bias_ref, o_ref, *, scale):
    acc = jnp.dot(a_ref[...], b_ref[...], preferred_element_type=F32)
    o_ref[...] = ((acc + bias_ref[...]) * scale).astype(o_ref.dtype)


def _mm(a, w, out_dtype, col0, ncols, *, bias=None, scale=1.0, tm=1024, tn=512):
    M, K = a.shape
    tm = min(tm, M)
    tn = min(tn, ncols)
    assert M % tm == 0 and ncols % tn == 0 and col0 % tn == 0
    jb = col0 // tn
    if bias is None:
        bias = jnp.zeros((1, ncols), F32)
    return pl.pallas_call(
        functools.partial(_mm_kernel, scale=scale),
        grid=(M // tm, ncols // tn),
        in_specs=[
            pl.BlockSpec((tm, K), lambda i, j: (i, 0)),
            pl.BlockSpec((K, tn), lambda i, j: (0, j + jb)),
            pl.BlockSpec((1, tn), lambda i, j: (0, j)),
        ],
        out_specs=pl.BlockSpec((tm, tn), lambda i, j: (i, j)),
        out_shape=jax.ShapeDtypeStruct((M, ncols), out_dtype),
        compiler_params=_cparams("parallel", "arbitrary"),
        name="proj",
    )(a, w, bias)


def _mm_norm_res_kernel(*refs, nk, nk1, nj, tn, n_a, emit_u):
    a_refs = refs[:n_a]
    w_ref, h_ref, g_ref = refs[n_a:n_a + 3]
    rest = refs[n_a + 3:]
    ng_ref = rest[0] if emit_u else None
    o_ref = rest[1] if emit_u else rest[0]
    u_ref = rest[2] if emit_u else None
    j = pl.program_id(1)
    k = pl.program_id(2)

    @pl.when((j == 0) & (k == 0))
    def _():
        o_ref[...] = jnp.zeros_like(o_ref)

    def accumulate(a_ref):
        part = jnp.dot(a_ref[...], w_ref[...], preferred_element_type=F32)
        for jj in range(nj):
            @pl.when(j == jj)
            def _(jj=jj):
                o_ref[:, jj * tn:(jj + 1) * tn] += part

    if n_a == 1:
        accumulate(a_refs[0])
    else:
        @pl.when(k < nk1)
        def _():
            accumulate(a_refs[0])

        @pl.when(k >= nk1)
        def _():
            accumulate(a_refs[1])

    @pl.when((j == nj - 1) & (k == nk - 1))
    def _():
        y = o_ref[...]
        ms = jnp.mean(y * y, axis=-1, keepdims=True)
        h_new = h_ref[...] + y * lax.rsqrt(ms + RMS_EPS) * g_ref[...]
        o_ref[...] = h_new
        if emit_u:
            ms2 = jnp.mean(h_new * h_new, axis=-1, keepdims=True)
            u_ref[...] = (h_new * lax.rsqrt(ms2 + RMS_EPS) * ng_ref[...]).astype(u_ref.dtype)


def _mm_norm_res(a_list, w, h, gain, next_gain=None, tm=512, tk=1024, tn=2048):
    M = h.shape[0]
    N = w.shape[1]
    n_a = len(a_list)
    tn = min(tn, N)
    tk = math.gcd(tk, *[a.shape[1] for a in a_list])
    assert n_a in (1, 2) and M % tm == 0 and N % tn == 0
    nk1 = a_list[0].shape[1] // tk
    nk = sum(a.shape[1] for a in a_list) // tk
    nj = N // tn
    emit_u = next_gain is not None
    a_specs = [pl.BlockSpec((tm, tk), lambda i, j, k: (i, jnp.minimum(k, nk1 - 1)))]
    if n_a == 2:
        a_specs.append(pl.BlockSpec((tm, tk), lambda i, j, k: (i, jnp.maximum(k - nk1, 0))))
    row_spec = pl.BlockSpec((tm, N), lambda i, j, k: (i, 0))
    vec_spec = pl.BlockSpec((1, N), lambda i, j, k: (0, 0))
    h_spec = pl.BlockSpec((tm, N), lambda i, j, k: (i, 0), pipeline_mode=pl.Buffered(1))
    in_specs = a_specs + [pl.BlockSpec((tk, tn), lambda i, j, k: (k, j)), h_spec, vec_spec]
    args = list(a_list) + [w, h, gain.reshape(1, N)]
    out_specs, out_shape = row_spec, jax.ShapeDtypeStruct((M, N), F32)
    if emit_u:
        in_specs.append(vec_spec)
        args.append(next_gain.reshape(1, N))
        out_specs = [row_spec, row_spec]
        out_shape = [out_shape, jax.ShapeDtypeStruct((M, N), BF16)]
    out = pl.pallas_call(
        functools.partial(_mm_norm_res_kernel, nk=nk, nk1=nk1, nj=nj, tn=tn, n_a=n_a, emit_u=emit_u),
        grid=(M // tm, nj, nk),
        in_specs=in_specs,
        out_specs=out_specs,
        out_shape=out_shape,
        compiler_params=_cparams("parallel", "arbitrary", "arbitrary"),
        name="out_proj_norm_res",
    )(*args)
    return out if emit_u else (out, None)


def _pool_kernel(cur_ref, prev_ref, w_ref, scale_ref, o_ref, ext_ref, *, ts, gd):
    i = pl.program_id(1)
    ext_ref[0:POOL_HALO, :] = jnp.where(i > 0, prev_ref[0], 0.0)
    ext_ref[POOL_HALO:POOL_HALO + ts, :] = cur_ref[0]
    t = i * ts + lax.broadcasted_iota(jnp.int32, (ts, 1), 0)
    for g, win in enumerate(POOL_WINDOWS):
        cols = slice(g * gd, (g + 1) * gd)
        tok = ext_ref[POOL_HALO:POOL_HALO + ts, cols]
        acc = tok
        for back in range(1, win):
            acc = acc + ext_ref[POOL_HALO - back:POOL_HALO - back + ts, cols]
        count = jnp.minimum(t + 1, win).astype(F32)
        pooled = acc / count - tok
        y = jnp.dot(pooled.astype(BF16), w_ref[g], preferred_element_type=F32)
        o_ref[0, :, cols] = (y * scale_ref[:, cols]).astype(o_ref.dtype)


def _pool_mixer(p_in, w_pool, scale, ts=512):
    B, S, PW = p_in.shape
    gd = PW // POOL_GROUPS
    hb = ts // POOL_HALO
    return pl.pallas_call(
        functools.partial(_pool_kernel, ts=ts, gd=gd),
        grid=(B, S // ts),
        in_specs=[
            pl.BlockSpec((1, ts, PW), lambda b, i: (b, i, 0)),
            pl.BlockSpec((1, POOL_HALO, PW), lambda b, i: (b, jnp.maximum(i * hb - 1, 0), 0)),
            pl.BlockSpec((POOL_GROUPS, gd, gd), lambda b, i: (0, 0, 0)),
            pl.BlockSpec((1, PW), lambda b, i: (0, 0)),
        ],
        out_specs=pl.BlockSpec((1, ts, PW), lambda b, i: (b, i, 0)),
        out_shape=jax.ShapeDtypeStruct((B, S, PW), BF16),
        scratch_shapes=[pltpu.VMEM((POOL_HALO + ts, PW), F32)],
        compiler_params=_cparams("parallel", "arbitrary"),
        name="pool_mixer",
    )(p_in, p_in, w_pool.astype(BF16), scale.reshape(1, PW))


def _gelu_tanh(x):
    c = math.sqrt(2.0 / math.pi)
    return x * (0.5 * (1.0 + jnp.tanh(c * (x + 0.044715 * (x * x * x)))))


def _compress_kernel(x_ref, plo_ref, phi_ref, w1a_ref, w1b_ref, w2_ref, o_ref, tmp_ref, *, nch):
    x = x_ref[0, 0]
    a = jnp.dot((x + plo_ref[...]).astype(BF16), w1a_ref[0], preferred_element_type=F32)
    b = jnp.dot((x + phi_ref[...]).astype(BF16), w1b_ref[0], preferred_element_type=F32)
    tmp_ref[0:nch, :] = b
    tmp_ref[nch:nch + 8, :] = jnp.zeros((8, b.shape[1]), F32)
    pre = a + tmp_ref[1:nch + 1, :]
    out = jnp.dot(_gelu_tanh(pre).astype(BF16), w2_ref[0], preferred_element_type=F32)
    row = lax.broadcasted_iota(jnp.int32, out.shape, 0)
    o_ref[0, 0] = jnp.where(row < nch - 1, out, 0.0).astype(o_ref.dtype)


def _compress(x, pos, w1, w2):
    two, BG, nch, cw = x.shape
    hid = w1.shape[-1]
    dh = w2.shape[-1]
    half = CMP_BLOCK // 2
    plo = pos[:half].reshape(1, cw)
    phi = pos[half:].reshape(1, cw)
    w1 = w1.astype(BF16)
    return pl.pallas_call(
        functools.partial(_compress_kernel, nch=nch),
        grid=(two, BG),
        in_specs=[
            pl.BlockSpec((1, 1, nch, cw), lambda s, b: (s, b, 0, 0)),
            pl.BlockSpec((1, cw), lambda s, b: (0, 0)),
            pl.BlockSpec((1, cw), lambda s, b: (0, 0)),
            pl.BlockSpec((1, cw, hid), lambda s, b: (s, 0, 0)),
            pl.BlockSpec((1, cw, hid), lambda s, b: (s, 1, 0)),
            pl.BlockSpec((1, hid, dh), lambda s, b: (s, 0, 0)),
        ],
        out_specs=pl.BlockSpec((1, 1, nch, dh), lambda s, b: (s, b, 0, 0)),
        out_shape=jax.ShapeDtypeStruct((two, BG, nch, dh), BF16),
        scratch_shapes=[pltpu.VMEM((nch + 8, hid), F32)],
        compiler_params=_cparams("parallel", "arbitrary"),
        name="nsa_compress",
    )(x, plo, phi, w1, w1, w2.astype(BF16))


def _nt_dot(a, b):
    return lax.dot_general(a, b, (((1,), (1,)), ((), ())), preferred_element_type=F32)


def _split_dot(w, x):
    hi = x.astype(BF16)
    r1 = x - hi.astype(F32)
    mid = r1.astype(BF16)
    lo = (r1 - mid.astype(F32)).astype(BF16)
    return (jnp.dot(w, hi, preferred_element_type=F32) + jnp.dot(w, mid, preferred_element_type=F32)
            + jnp.dot(w, lo, preferred_element_type=F32))


def _nsa_kernel(q_ref, gl_ref, kc_ref, vct_ref, ks_ref, vst_ref, kw_ref, vwt_ref, ovt_ref, et_ref, o_ref,
                m_ref, l_ref, acc_ref, s_ref, bias_ref, *, hg, n_cmp, n_slc, top_n):
    QT, DH = NSA_Q_TILE, NSA_HEAD_DIM
    t0 = pl.program_id(1) * QT
    q = jnp.concatenate([q_ref[:, h * DH:(h + 1) * DH] for h in range(hg)], axis=0)
    lane = lax.broadcasted_iota(jnp.int32, (1, QT), 1)
    t = t0 + lane

    def capped(s, cap):
        return jnp.concatenate([jnp.minimum(s[:, h * QT:(h + 1) * QT], cap) for h in range(hg)], axis=1)

    def softmax_cols(s):
        p = jnp.exp2(s - jnp.max(s, axis=0, keepdims=True))
        return p, jnp.sum(p, axis=0, keepdims=True)

    ncp = kc_ref.shape[2]
    nrow = lax.broadcasted_iota(jnp.int32, (ncp, QT), 0)
    valid_c = (nrow * CMP_STRIDE + (CMP_BLOCK - 1) <= t) & (nrow < n_cmp)
    p_c, _ = softmax_cols(capped(_nt_dot(kc_ref[0, 0], q), jnp.where(valid_c, BIG, NEG)))
    p_c = p_c * jnp.concatenate([jnp.where(valid_c, 1.0, 0.0)] * hg, axis=1)
    l_c = jnp.sum(p_c, axis=0, keepdims=True)
    p_c = p_c / jnp.where(l_c > 0.0, l_c, 1.0)
    o_c = jnp.dot(vct_ref[0], p_c.astype(BF16), preferred_element_type=F32)

    ws = pl.multiple_of(jnp.maximum(t0 + QT - SWA_SPAN, 0), LANES)
    wb = ws // LANES
    vwin = jnp.concatenate([vwt_ref[0, wb + j] for j in range(SWA_SPAN // LANES)], axis=1)
    dist = t - (ws + lax.broadcasted_iota(jnp.int32, (SWA_SPAN, QT), 0))
    valid_w = (dist >= 0) & (dist < SWA_WINDOW)
    p_w, l_w = softmax_cols(capped(_nt_dot(kw_ref[0, pl.ds(ws, SWA_SPAN), :], q), jnp.where(valid_w, BIG, NEG)))
    o_w = jnp.dot(vwin, p_w.astype(BF16), preferred_element_type=F32) / l_w

    p_sum = p_c[:, 0:QT]
    for h in range(1, hg):
        p_sum = p_sum + p_c[:, h * QT:(h + 1) * QT]
    imp = _split_dot(ovt_ref[...], p_sum)
    blk = lax.broadcasted_iota(jnp.int32, (LANES, QT), 0)
    cur = t0 // SLC_BLOCK + jnp.zeros((1, QT), jnp.int32)
    for k in range(1, QT // SLC_BLOCK):
        cur = cur + jnp.where(lane >= k * SLC_BLOCK, 1, 0)
    forced = (blk == 0) | (blk == cur) | (blk == cur - 1)
    val = jnp.where(forced, imp + FORCE_BONUS, jnp.where(blk > cur, -FORCE_BONUS, imp))
    if n_slc < LANES:
        val = jnp.where(blk < n_slc, val, -jnp.inf)

    def rank_bias(nb):
        nslab = nb // 8
        slabs = [val[8 * r:8 * r + 8, :] for r in range(nslab)]
        ranks = [jnp.zeros((8, QT), F32) for _ in range(nslab)]
        sub = lax.broadcasted_iota(jnp.int32, (8, QT), 0)
        for other in range(nb):
            c = val[other:other + 1, :]
            for r in range(nslab):
                if 8 * r > other:
                    beat = c >= slabs[r]
                elif 8 * r + 7 < other:
                    beat = c > slabs[r]
                else:
                    beat = (c > slabs[r]) | ((c == slabs[r]) & (sub + 8 * r > other))
                ranks[r] = ranks[r] + jnp.where(beat, 1.0, 0.0)
        for r in range(nslab):
            chosen = (ranks[r] < top_n) & (blk[8 * r:8 * r + 8] <= cur)
            bias_ref[8 * r:8 * r + 8, :] = jnp.where(chosen, 0.0, NEG)
        if nb < LANES:
            bias_ref[nb:LANES, :] = jnp.full((LANES - nb, QT), NEG, F32)

    live = (t0 + QT - 1) // SLC_BLOCK + 1
    sizes = sorted({min(_round_up(-(-n_slc * k // RANK_SIZES), 8), LANES) for k in range(1, RANK_SIZES + 1)})
    for lo, nb in zip([0] + sizes[:-1], sizes):
        @pl.when((live > lo) & (live <= nb))
        def _(nb=nb):
            rank_bias(nb)

    bias_q = bias_ref[...].T.astype(BF16)

    q_sel = jnp.concatenate([q, jnp.concatenate([bias_q] * hg, axis=0)], axis=1)
    m_ref[...] = jnp.full(m_ref.shape, NEG, F32)
    l_ref[...] = jnp.zeros(l_ref.shape, F32)
    acc_ref[...] = jnp.zeros(acc_ref.shape, F32)

    def sel_tiles(tiles):
        for slot, (kb, _) in enumerate(tiles):
            start = pl.multiple_of(kb * SEL_TILE, SEL_TILE)
            keys = jnp.concatenate([ks_ref[0, pl.ds(start, SEL_TILE), :], et_ref[kb]], axis=1)
            s_ref[slot] = _nt_dot(keys, q_sel)
        for slot, (kb, diagonal) in enumerate(tiles):
            s = s_ref[slot]
            if diagonal:
                key = kb * SEL_TILE + lax.broadcasted_iota(jnp.int32, (SEL_TILE, QT), 0)
                s = capped(s, jnp.where(key <= t, BIG, NEG))
            m_prev = m_ref[...]
            m_new = jnp.maximum(m_prev, jnp.max(s, axis=0, keepdims=True))
            alpha = jnp.exp2(m_prev - m_new)
            p = jnp.exp2(s - m_new)
            l_ref[...] = alpha * l_ref[...] + jnp.sum(p, axis=0, keepdims=True)
            acc_ref[...] = alpha * acc_ref[...] + jnp.dot(vst_ref[0, kb], p.astype(BF16),
                                                          preferred_element_type=F32)
            m_ref[...] = m_new

    def past_pair(i, carry):
        sel_tiles([(2 * i, False), (2 * i + 1, False)])
        return carry

    last = t0 // SEL_TILE
    lax.fori_loop(0, last // 2, past_pair, 0)

    @pl.when(last % 2 == 1)
    def _():
        sel_tiles([(last - 1, False), (last, True)])

    @pl.when(last % 2 == 0)
    def _():
        sel_tiles([(last, True)])

    o_s = acc_ref[...] / l_ref[...]

    gate = jax.nn.sigmoid(gl_ref[0, 0])
    for h in range(hg):
        cols = slice(h * QT, (h + 1) * QT)
        r = N_BRANCH * h
        out = gate[r:r + 1] * o_c[:, cols] + gate[r + 1:r + 2] * o_s[:, cols] + gate[r + 2:r + 3] * o_w[:, cols]
        o_ref[:, h * DH:(h + 1) * DH] = out.T.astype(o_ref.dtype)


def _nsa(q, gates, kvc, kv, B, S):
    T, HW = q.shape
    G, DH, QT = NSA_KV_GROUPS, NSA_HEAD_DIM, NSA_Q_TILE
    hg = HW // DH // G
    n_cmp = (S - CMP_BLOCK) // CMP_STRIDE + 1
    ncp = kvc.shape[2]
    n_slc = S // SLC_BLOCK
    top_n = min(SLC_TOPN, n_slc)
    nqt = S // QT
    kvw = G * DH
    assert n_slc <= LANES and S % SEL_TILE == 0 and S >= SWA_SPAN and QT == LANES

    cs = jnp.arange(ncp)[None, :] * CMP_STRIDE
    ss = jnp.arange(LANES)[:, None] * SLC_BLOCK
    overlap = jnp.clip(jnp.minimum(cs + CMP_BLOCK, ss + SLC_BLOCK) - jnp.maximum(cs, ss), 0) // CMP_STRIDE
    overlap = jnp.where((jnp.arange(ncp)[None, :] < n_cmp) & (jnp.arange(LANES)[:, None] < n_slc), overlap, 0)
    overlap = overlap.astype(BF16)
    key_blk = (jnp.arange(S) // SLC_BLOCK).reshape(S // SEL_TILE, SEL_TILE, 1)
    expand = (key_blk == jnp.arange(LANES)[None, None, :]).astype(BF16)

    gr = _round_up(N_BRANCH * hg, 8)
    gl = gates[:, :G * hg * N_BRANCH].reshape(B, nqt, QT, G, hg * N_BRANCH)
    gl = jnp.pad(gl, ((0, 0),) * 4 + ((0, gr - hg * N_BRANCH),)).transpose(0, 3, 1, 4, 2).reshape(B * G, nqt, gr, QT)

    def v_tiles(which, tile):
        v = kv[:, :, which * kvw:(which + 1) * kvw].reshape(B, S // tile, tile, G, DH)
        return v.transpose(0, 3, 1, 4, 2).reshape(B * G, S // tile, DH, tile)

    def k_spec(which):
        return pl.BlockSpec((1, S, DH), lambda bg, i: (bg // G, 0, which * G + bg % G))

    return pl.pallas_call(
        functools.partial(_nsa_kernel, hg=hg, n_cmp=n_cmp, n_slc=n_slc, top_n=top_n),
        grid=(B * G, nqt),
        in_specs=[
            pl.BlockSpec((QT, hg * DH), lambda bg, i: ((bg // G) * nqt + i, bg % G)),
            pl.BlockSpec((1, 1, gr, QT), lambda bg, i: (bg, i, 0, 0)),
            pl.BlockSpec((1, 1, ncp, DH), lambda bg, i: (0, bg, 0, 0)),
            pl.BlockSpec((1, DH, ncp), lambda bg, i: (bg, 0, 0)),
            k_spec(0),
            pl.BlockSpec((1, S // SEL_TILE, DH, SEL_TILE), lambda bg, i: (bg, 0, 0, 0)),
            k_spec(2),
            pl.BlockSpec((1, S // LANES, DH, LANES), lambda bg, i: (bg, 0, 0, 0)),
            pl.BlockSpec((LANES, ncp), lambda bg, i: (0, 0)),
            pl.BlockSpec((S // SEL_TILE, SEL_TILE, LANES), lambda bg, i: (0, 0, 0)),
        ],
        out_specs=pl.BlockSpec((QT, hg * DH), lambda bg, i: ((bg // G) * nqt + i, bg % G)),
        out_shape=jax.ShapeDtypeStruct((T, HW), BF16),
        scratch_shapes=[pltpu.VMEM((1, hg * QT), F32), pltpu.VMEM((1, hg * QT), F32),
                        pltpu.VMEM((DH, hg * QT), F32), pltpu.VMEM((2, SEL_TILE, hg * QT), F32),
                        pltpu.VMEM((LANES, QT), F32)],
        compiler_params=_cparams("parallel", "arbitrary"),
        name="nsa_attention",
    )(q, gl, kvc, jnp.swapaxes(kvc[1], 1, 2), kv, v_tiles(1, SEL_TILE), kv, v_tiles(3, LANES), overlap, expand)


def _mlstm_kernel(q_ref, kt_ref, v_ref, o_ref, gr_ref, gc_ref, hn_ref, y_ref, c_ref, n_ref, m_ref, *, L):
    ci = pl.program_id(1)

    @pl.when(ci == 0)
    def _():
        c_ref[...] = jnp.zeros_like(c_ref)
        n_ref[...] = jnp.zeros_like(n_ref)
        m_ref[...] = jnp.zeros_like(m_ref)

    q = q_ref[...]
    kt = kt_ref[0]
    v = v_ref[...]
    ig_row = gr_ref[0, 0, 0:1, :]
    lf_row = jax.nn.log_sigmoid(gr_ref[0, 0, 1:2, :])
    lf_col = jax.nn.log_sigmoid(gc_ref[0, 0, :, 1:2])
    m_prev = m_ref[0:1, 0:1]

    ti = lax.broadcasted_iota(jnp.int32, (L, L), 0)
    si = lax.broadcasted_iota(jnp.int32, (L, L), 1)
    causal = si <= ti
    b_col = jnp.sum(jnp.where(causal, lf_row, 0.0), axis=1, keepdims=True)
    b_row = jnp.sum(jnp.where(ti <= si, lf_col, 0.0), axis=0, keepdims=True)
    b_last = b_col[L - 1:L, :]

    dmat = jnp.where(causal, b_col - b_row + ig_row, -jnp.inf)
    a_col = b_col + m_prev
    m_t = jnp.maximum(a_col, jnp.max(dmat, axis=1, keepdims=True))
    wq = jnp.dot(q, kt, preferred_element_type=F32) * jnp.exp(dmat - m_t)
    inter = jnp.exp(a_col - m_t)
    q_c = jnp.dot(q, c_ref[...].astype(BF16), preferred_element_type=F32)
    q_n = jnp.dot(q, n_ref[...].astype(BF16), preferred_element_type=F32)[:, 0:1]
    num = inter * q_c + jnp.dot(wq.astype(BF16), v, preferred_element_type=F32)
    den = inter * q_n + jnp.sum(wq, axis=1, keepdims=True)
    h = num / jnp.maximum(jnp.abs(den), jnp.exp(-m_t))
    h = h * lax.rsqrt(jnp.mean(h * h, axis=-1, keepdims=True) + RMS_EPS) * hn_ref[0]
    y_ref[...] = (jax.nn.sigmoid(o_ref[...]) * h).astype(y_ref.dtype)

    g_row = b_last - b_row + ig_row
    m_new = jnp.maximum(b_last + m_prev, jnp.max(g_row, axis=1, keepdims=True))
    decay = jnp.exp(b_last + m_prev - m_new)
    kw_t = kt.astype(F32) * jnp.exp(g_row - m_new)
    c_ref[...] = decay * c_ref[...] + jnp.dot(kw_t.astype(BF16), v, preferred_element_type=F32)
    n_ref[...] = decay * n_ref[...] + jnp.sum(kw_t, axis=1, keepdims=True)
    m_ref[...] = jnp.broadcast_to(m_new, m_ref.shape)


def _mlstm(q, kt, v, o, gates, head_norm, B, S):
    T = q.shape[0]
    H = MLSTM_HEADS
    dk = q.shape[1] // H
    dv = v.shape[1] // H
    L = min(MLSTM_CHUNK, S)
    nc = S // L
    g = gates[:, :2 * H].reshape(B, nc, L, 2, H)
    g_row = jnp.transpose(g, (0, 4, 1, 3, 2)).reshape(B * H, nc, 2, L)
    g_col = jnp.transpose(g, (0, 4, 1, 2, 3)).reshape(B * H, nc, L, 2)
    return pl.pallas_call(
        functools.partial(_mlstm_kernel, L=L),
        grid=(B * H, nc),
        in_specs=[
            pl.BlockSpec((L, dk), lambda bh, c: ((bh // H) * nc + c, bh % H)),
            pl.BlockSpec((1, dk, L), lambda bh, c: (bh // H, bh % H, c)),
            pl.BlockSpec((L, dv), lambda bh, c: ((bh // H) * nc + c, bh % H)),
            pl.BlockSpec((L, dv), lambda bh, c: ((bh // H) * nc + c, bh % H)),
            pl.BlockSpec((1, 1, 2, L), lambda bh, c: (bh, c, 0, 0)),
            pl.BlockSpec((1, 1, L, 2), lambda bh, c: (bh, c, 0, 0)),
            pl.BlockSpec((1, 1, dv), lambda bh, c: (bh % H, 0, 0)),
        ],
        out_specs=pl.BlockSpec((L, dv), lambda bh, c: ((bh // H) * nc + c, bh % H)),
        out_shape=jax.ShapeDtypeStruct((T, H * dv), BF16),
        scratch_shapes=[pltpu.VMEM((dk, dv), F32), pltpu.VMEM((dk, LANES), F32), pltpu.VMEM((8, LANES), F32)],
        compiler_params=_cparams("parallel", "arbitrary"),
        name="mlstm",
    )(q, kt, v, o, g_row, g_col, head_norm.reshape(H, 1, dv))


def _ffn_in_kernel(u_ref, halo_ref, wg_ref, wu_ref, cw_ref, cb_ref, o_ref, lhs_ref, g_ref, *, tm, tiles_per_seq):
    i = pl.program_id(0)
    j = pl.program_id(1)

    @pl.when(j == 0)
    def _():
        halo = halo_ref[...]
        lhs_ref[0:CONV_HALO, :] = jnp.where(i % tiles_per_seq != 0, halo, jnp.zeros_like(halo))
        lhs_ref[CONV_HALO:CONV_HALO + tm, :] = u_ref[...]

    g_ref[...] = jnp.dot(lhs_ref[...], wg_ref[...], preferred_element_type=F32)
    up = jnp.dot(u_ref[...], wu_ref[...], preferred_element_type=F32)
    gate = cb_ref[...] + cw_ref[0:1, :] * g_ref[CONV_HALO:CONV_HALO + tm, :]
    for back in range(1, CONV_WIDTH):
        gate = gate + cw_ref[back:back + 1, :] * g_ref[CONV_HALO - back:CONV_HALO - back + tm, :]
    o_ref[...] = (gate * jax.nn.sigmoid(gate) * up).astype(o_ref.dtype)


def _ffn_in(u, wg, wu, conv_w, conv_b, S, tm=1024, tn=512):
    T, D = u.shape
    FP = wg.shape[1]
    tm = min(tm, S)
    assert T % tm == 0 and S % tm == 0 and FP % tn == 0
    hb = tm // CONV_HALO
    return pl.pallas_call(
        functools.partial(_ffn_in_kernel, tm=tm, tiles_per_seq=S // tm),
        grid=(T // tm, FP // tn),
        in_specs=[
            pl.BlockSpec((tm, D), lambda i, j: (i, 0)),
            pl.BlockSpec((CONV_HALO, D), lambda i, j: (jnp.maximum(i * hb - 1, 0), 0)),
            pl.BlockSpec((D, tn), lambda i, j: (0, j)),
            pl.BlockSpec((D, tn), lambda i, j: (0, j)),
            pl.BlockSpec((CONV_WIDTH, tn), lambda i, j: (0, j)),
            pl.BlockSpec((1, tn), lambda i, j: (0, j)),
        ],
        out_specs=pl.BlockSpec((tm, tn), lambda i, j: (i, j)),
        out_shape=jax.ShapeDtypeStruct((T, FP), BF16),
        scratch_shapes=[pltpu.VMEM((CONV_HALO + tm, D), BF16), pltpu.VMEM((CONV_HALO + tm, tn), F32)],
        compiler_params=_cparams("parallel", "arbitrary"),
        name="ffn_in_conv_act",
    )(u, u, wg, wu, conv_w, conv_b)


def _conv_ffn(h, u, ln_post, next_gain, w_in, conv_w, conv_b, w_out, S):
    ff = conv_w.shape[-1]
    fp = _round_up(ff, 1024)
    wg = _pad_cols(w_in[:, :ff].astype(BF16), fp)
    wu = _pad_cols(w_in[:, ff:].astype(BF16), fp)
    act = _ffn_in(u, wg, wu, _pad_cols(conv_w, fp), _pad_cols(conv_b.reshape(1, ff), fp), S)
    w_out = jnp.pad(w_out.astype(BF16), ((0, fp - ff), (0, 0)))
    return _mm_norm_res([act], w_out, h, ln_post, next_gain)


def _ab_layer(h, u, ln_post, next_gain, w_in, pool_w, pool_scale, cmp_pos, ck_w1, ck_w2, cv_w1, cv_w2, w_out, B, S):
    T, D = h.shape
    G, DH = NSA_KV_GROUPS, NSA_HEAD_DIM
    pw = D // 4
    hw = D - pw
    kvw = G * DH
    w = w_in.astype(BF16)
    c0 = pw + hw
    p_in = _mm(u, w, F32, 0, pw)
    q = _mm(u, w, BF16, pw, hw, scale=DH ** -0.5 * LOG2E)
    kv_cmp = _mm(u, w, F32, c0, 2 * kvw)
    kv = _mm(u, w, BF16, c0 + 2 * kvw, 4 * kvw)
    gates = _mm(u, w, F32, c0 + 6 * kvw, LANES)

    y_a = _pool_mixer(p_in.reshape(B, S, pw), pool_w, pool_scale).reshape(T, pw)
    chunks = kv_cmp.reshape(B, S, 2, G, DH).transpose(2, 0, 3, 1, 4)
    chunks = chunks.reshape(2, B * G, S // CMP_STRIDE, CMP_STRIDE * DH)
    kvc = _compress(chunks, cmp_pos, jnp.stack([ck_w1, cv_w1]), jnp.stack([ck_w2, cv_w2]))
    y_b = _nsa(q, gates, kvc, kv.reshape(B, S, 4 * kvw), B, S)
    return _mm_norm_res([y_a, y_b], w_out.astype(BF16), h, ln_post, next_gain)


def _c_layer(h, u, ln_post, next_gain, w_in, b_if, head_norm, w_out, B, S):
    T, D = h.shape
    H = MLSTM_HEADS
    dv = D // H
    dk = dv // 2
    qk = H * dk
    w = w_in.astype(BF16)
    q = _mm(u, w, BF16, 0, qk, scale=dk ** -0.5)
    k = _mm(u, w, BF16, qk, qk)
    v = _mm(u, w, BF16, 2 * qk, D)
    o = _mm(u, w, F32, 2 * qk + D, D)
    gates = _mm(u, w, F32, 2 * qk + 2 * D, LANES, bias=_pad_cols(b_if.reshape(1, 2 * H), LANES))
    kt = k.reshape(B, S, qk).transpose(0, 2, 1)
    y = _mlstm(q, kt, v, o, gates, head_norm, B, S)
    return _mm_norm_res([y], w_out.astype(BF16), h, ln_post, next_gain)


def kernel(x, ln_pre, ln_post, w_in_ab, pool_w, pool_scale, cmp_pos, cmp_k_w1, cmp_k_w2, cmp_v_w1, cmp_v_w2,
           w_out_ab, w_in_c, b_if_c, head_norm_c, w_out_c, ffn_ln_pre, ffn_ln_post, ffn_w_in, ffn_conv_w,
           ffn_conv_b, ffn_w_out):
    B, S, D = x.shape
    depth = ln_pre.shape[0]
    h = x.reshape(B * S, D)
    u = _rmsnorm(h, ln_pre[0])
    for layer in range(depth):
        i = layer // 2
        if layer % 2 == 0:
            h, u = _ab_layer(h, u, ln_post[layer], ffn_ln_pre[layer], w_in_ab[i], pool_w[i], pool_scale[i],
                             cmp_pos[i], cmp_k_w1[i], cmp_k_w2[i], cmp_v_w1[i], cmp_v_w2[i], w_out_ab[i], B, S)
        else:
            h, u = _c_layer(h, u, ln_post[layer], ffn_ln_pre[layer], w_in_c[i], b_if_c[i], head_norm_c[i],
                            w_out_c[i], B, S)
        next_gain = ln_pre[layer + 1] if layer + 1 < depth else None
        h, u = _conv_ffn(h, u, ffn_ln_post[layer], next_gain, ffn_w_in[layer], ffn_conv_w[layer],
                         ffn_conv_b[layer], ffn_w_out[layer], S)
    return h.reshape(B, S, D)
```

```python
import functools
import math

import jax
import jax.numpy as jnp
from jax import lax
from jax.experimental import pallas as pl
from jax.experimental.pallas import tpu as pltpu

F32 = jnp.float32
BF16 = jnp.bfloat16

RMS_EPS = 1e-6
POOL_GROUPS = 4
POOL_WINDOWS = (2, 4, 8, 16)
POOL_HALO = 16
NSA_HEAD_DIM = 128
NSA_KV_GROUPS = 4
N_BRANCH = 3
CMP_BLOCK = 32
CMP_STRIDE = 16
SLC_BLOCK = 64
SLC_TOPN = 16
SWA_WINDOW = 512
FORCE_BONUS = 1e4
NEG = -1e30
BIG = 1e30
MLSTM_HEADS = 8
CONV_WIDTH = 3
CONV_HALO = 16

LANES = 128
LOG2E = math.log2(math.e)
NSA_Q_TILE = LANES
SEL_TILE = 512
RANK_SIZES = 4
SWA_SPAN = SWA_WINDOW + NSA_Q_TILE
MLSTM_CHUNK = 256
FFN_COL_TILE = 512
VMEM_LIMIT = 56 * 1024 * 1024


def _cparams(*sem):
    return pltpu.CompilerParams(dimension_semantics=sem, vmem_limit_bytes=VMEM_LIMIT)


def _round_up(n, m):
    return (n + m - 1) // m * m


def _pad_cols(w, n):
    return jnp.pad(w, ((0, 0), (0, n - w.shape[1])))


def _rmsnorm_kernel(x_ref, g_ref, o_ref):
    x = x_ref[...]
    ms = jnp.mean(x * x, axis=-1, keepdims=True)
    o_ref[...] = (x * lax.rsqrt(ms + RMS_EPS) * g_ref[...]).astype(o_ref.dtype)


def _rmsnorm(x, gain, tm=512):
    T, D = x.shape
    return pl.pallas_call(
        _rmsnorm_kernel,
        grid=(T // tm,),
        in_specs=[pl.BlockSpec((tm, D), lambda i: (i, 0)), pl.BlockSpec((1, D), lambda i: (0, 0))],
        out_specs=pl.BlockSpec((tm, D), lambda i: (i, 0)),
        out_shape=jax.ShapeDtypeStruct((T, D), BF16),
        compiler_params=_cparams("parallel"),
        name="rmsnorm",
    )(x, gain.reshape(1, D))


def _mm_kernel(a_ref, b_ref, bias_ref, o_ref, *, scale):
    acc = jnp.dot(a_ref[...], b_ref[...], preferred_element_type=F32)
    o_ref[...] = ((acc + bias_ref[...]) * scale).astype(o_ref.dtype)


def _mm(a, w, out_dtype, col0, ncols, *, bias=None, scale=1.0, tm=1024, tn=512):
    M, K = a.shape
    tm = min(tm, M)
    tn = min(tn, ncols)
    assert M % tm == 0 and ncols % tn == 0 and col0 % tn == 0
    jb = col0 // tn
    if bias is None:
        bias = jnp.zeros((1, ncols), F32)
    return pl.pallas_call(
        functools.partial(_mm_kernel, scale=scale),
        grid=(M // tm, ncols // tn),
        in_specs=[
            pl.BlockSpec((tm, K), lambda i, j: (i, 0)),
            pl.BlockSpec((K, tn), lambda i, j: (0, j + jb)),
            pl.BlockSpec((1, tn), lambda i, j: (0, j)),
        ],
        out_specs=pl.BlockSpec((tm, tn), lambda i, j: (i, j)),
        out_shape=jax.ShapeDtypeStruct((M, ncols), out_dtype),
        compiler_params=_cparams("parallel", "arbitrary"),
        name="proj",
    )(a, w, bias)


def _mm_norm_res_kernel(*refs, nk, nk1, nj, tn, n_a, emit_u):
    a_refs = refs[:n_a]
    w_ref, h_ref, g_ref = refs[n_a:n_a + 3]
    rest = refs[n_a + 3:]
    ng_ref = rest[0] if emit_u else None
    o_ref = rest[1] if emit_u else rest[0]
    u_ref = rest[2] if emit_u else None
    j = pl.program_id(1)
    k = pl.program_id(2)

    @pl.when((j == 0) & (k == 0))
    def _():
        o_ref[...] = jnp.zeros_like(o_ref)

    def accumulate(a_ref):
        for jj in range(nj):
            @pl.when(j == jj)
            def _(jj=jj):
                o_ref[:, jj * tn:(jj + 1) * tn] += jnp.dot(a_ref[...], w_ref[...], preferred_element_type=F32)

    if n_a == 1:
        accumulate(a_refs[0])
    else:
        @pl.when(k < nk1)
        def _():
            accumulate(a_refs[0])

        @pl.when(k >= nk1)
        def _():
            accumulate(a_refs[1])

    @pl.when((j == nj - 1) & (k == nk - 1))
    def _():
        y = o_ref[...]
        ms = jnp.mean(y * y, axis=-1, keepdims=True)
        h_new = h_ref[...] + y * lax.rsqrt(ms + RMS_EPS) * g_ref[...]
        o_ref[...] = h_new
        if emit_u:
            ms2 = jnp.mean(h_new * h_new, axis=-1, keepdims=True)
            u_ref[...] = (h_new * lax.rsqrt(ms2 + RMS_EPS) * ng_ref[...]).astype(u_ref.dtype)


def _mm_norm_res(a_list, w, h, gain, next_gain=None, w_lead=None, tm=512, tk=1024, tn=2048):
    M = h.shape[0]
    N = w.shape[-1]
    n_a = len(a_list)
    tn = min(tn, N)
    tk = math.gcd(tk, *[a.shape[1] for a in a_list])
    assert n_a in (1, 2) and M % tm == 0 and N % tn == 0
    nk1 = a_list[0].shape[1] // tk
    nk = sum(a.shape[1] for a in a_list) // tk
    nj = N // tn
    emit_u = next_gain is not None
    a_specs = [pl.BlockSpec((tm, tk), lambda i, j, k: (i, jnp.minimum(k, nk1 - 1)))]
    if n_a == 2:
        a_specs.append(pl.BlockSpec((tm, tk), lambda i, j, k: (i, jnp.maximum(k - nk1, 0))))
    row_spec = pl.BlockSpec((tm, N), lambda i, j, k: (i, 0))
    vec_spec = pl.BlockSpec((1, N), lambda i, j, k: (0, 0))
    h_spec = pl.BlockSpec((tm, N), lambda i, j, k: (i, 0), pipeline_mode=pl.Buffered(1))
    if w_lead is None:
        w_spec = pl.BlockSpec((tk, tn), lambda i, j, k: (k, j))
    else:
        w_spec = pl.BlockSpec((None, tk, tn), lambda i, j, k: (w_lead, k, j))
    in_specs = a_specs + [w_spec, h_spec, vec_spec]
    args = list(a_list) + [w, h, gain.reshape(1, N)]
    out_specs, out_shape = row_spec, jax.ShapeDtypeStruct((M, N), F32)
    if emit_u:
        in_specs.append(vec_spec)
        args.append(next_gain.reshape(1, N))
        out_specs = [row_spec, row_spec]
        out_shape = [out_shape, jax.ShapeDtypeStruct((M, N), BF16)]
    out = pl.pallas_call(
        functools.partial(_mm_norm_res_kernel, nk=nk, nk1=nk1, nj=nj, tn=tn, n_a=n_a, emit_u=emit_u),
        grid=(M // tm, nj, nk),
        in_specs=in_specs,
        out_specs=out_specs,
        out_shape=out_shape,
        compiler_params=_cparams("parallel", "arbitrary", "arbitrary"),
        name="out_proj_norm_res",
    )(*args)
    return out if emit_u else (out, None)


def _pool_kernel(cur_ref, prev_ref, w_ref, scale_ref, o_ref, ext_ref, *, ts, gd):
    i = pl.program_id(1)
    ext_ref[0:POOL_HALO, :] = jnp.where(i > 0, prev_ref[0], 0.0)
    ext_ref[POOL_HALO:POOL_HALO + ts, :] = cur_ref[0]
    t = i * ts + lax.broadcasted_iota(jnp.int32, (ts, 1), 0)
    for g, win in enumerate(POOL_WINDOWS):
        cols = slice(g * gd, (g + 1) * gd)
        tok = ext_ref[POOL_HALO:POOL_HALO + ts, cols]
        acc = tok
        for back in range(1, win):
            acc = acc + ext_ref[POOL_HALO - back:POOL_HALO - back + ts, cols]
        count = jnp.minimum(t + 1, win).astype(F32)
        pooled = acc / count - tok
        y = jnp.dot(pooled.astype(BF16), w_ref[g], preferred_element_type=F32)
        o_ref[0, :, cols] = (y * scale_ref[:, cols]).astype(o_ref.dtype)


def _pool_mixer(p_in, w_pool, scale, ts=512):
    B, S, PW = p_in.shape
    gd = PW // POOL_GROUPS
    hb = ts // POOL_HALO
    return pl.pallas_call(
        functools.partial(_pool_kernel, ts=ts, gd=gd),
        grid=(B, S // ts),
        in_specs=[
            pl.BlockSpec((1, ts, PW), lambda b, i: (b, i, 0)),
            pl.BlockSpec((1, POOL_HALO, PW), lambda b, i: (b, jnp.maximum(i * hb - 1, 0), 0)),
            pl.BlockSpec((POOL_GROUPS, gd, gd), lambda b, i: (0, 0, 0)),
            pl.BlockSpec((1, PW), lambda b, i: (0, 0)),
        ],
        out_specs=pl.BlockSpec((1, ts, PW), lambda b, i: (b, i, 0)),
        out_shape=jax.ShapeDtypeStruct((B, S, PW), BF16),
        scratch_shapes=[pltpu.VMEM((POOL_HALO + ts, PW), F32)],
        compiler_params=_cparams("parallel", "arbitrary"),
        name="pool_mixer",
    )(p_in, p_in, w_pool.astype(BF16), scale.reshape(1, PW))


def _gelu_tanh(x):
    c = math.sqrt(2.0 / math.pi)
    return x * (0.5 * (1.0 + jnp.tanh(c * (x + 0.044715 * (x * x * x)))))


def _compress_kernel(x_ref, plo_ref, phi_ref, w1a_ref, w1b_ref, w2_ref, o_ref, tmp_ref, *, nch):
    x = x_ref[0, 0]
    a = jnp.dot((x + plo_ref[...]).astype(BF16), w1a_ref[0], preferred_element_type=F32)
    b = jnp.dot((x + phi_ref[...]).astype(BF16), w1b_ref[0], preferred_element_type=F32)
    tmp_ref[0:nch, :] = b
    tmp_ref[nch:nch + 8, :] = jnp.zeros((8, b.shape[1]), F32)
    pre = a + tmp_ref[1:nch + 1, :]
    out = jnp.dot(_gelu_tanh(pre).astype(BF16), w2_ref[0], preferred_element_type=F32)
    row = lax.broadcasted_iota(jnp.int32, out.shape, 0)
    o_ref[0, 0] = jnp.where(row < nch - 1, out, 0.0).astype(o_ref.dtype)


def _compress(x, pos, w1, w2):
    two, BG, nch, cw = x.shape
    hid = w1.shape[-1]
    dh = w2.shape[-1]
    half = CMP_BLOCK // 2
    plo = pos[:half].reshape(1, cw)
    phi = pos[half:].reshape(1, cw)
    w1 = w1.astype(BF16)
    return pl.pallas_call(
        functools.partial(_compress_kernel, nch=nch),
        grid=(two, BG),
        in_specs=[
            pl.BlockSpec((1, 1, nch, cw), lambda s, b: (s, b, 0, 0)),
            pl.BlockSpec((1, cw), lambda s, b: (0, 0)),
            pl.BlockSpec((1, cw), lambda s, b: (0, 0)),
            pl.BlockSpec((1, cw, hid), lambda s, b: (s, 0, 0)),
            pl.BlockSpec((1, cw, hid), lambda s, b: (s, 1, 0)),
            pl.BlockSpec((1, hid, dh), lambda s, b: (s, 0, 0)),
        ],
        out_specs=pl.BlockSpec((1, 1, nch, dh), lambda s, b: (s, b, 0, 0)),
        out_shape=jax.ShapeDtypeStruct((two, BG, nch, dh), BF16),
        scratch_shapes=[pltpu.VMEM((nch + 8, hid), F32)],
        compiler_params=_cparams("parallel", "arbitrary"),
        name="nsa_compress",
    )(x, plo, phi, w1, w1, w2.astype(BF16))


def _nt_dot(a, b):
    return lax.dot_general(a, b, (((1,), (1,)), ((), ())), preferred_element_type=F32)


def _split_dot(w, x):
    hi = x.astype(BF16)
    r1 = x - hi.astype(F32)
    mid = r1.astype(BF16)
    lo = (r1 - mid.astype(F32)).astype(BF16)
    return (jnp.dot(w, hi, preferred_element_type=F32) + jnp.dot(w, mid, preferred_element_type=F32)
            + jnp.dot(w, lo, preferred_element_type=F32))


def _nsa_kernel(q_ref, gl_ref, kc_ref, vct_ref, ks_ref, vst_ref, kw_ref, vwt_ref, ovt_ref, et_ref, o_ref,
                m_ref, l_ref, acc_ref, s_ref, bias_ref, *, hg, n_cmp, n_slc, top_n):
    QT, DH = NSA_Q_TILE, NSA_HEAD_DIM
    t0 = pl.program_id(1) * QT
    q = jnp.concatenate([q_ref[:, h * DH:(h + 1) * DH] for h in range(hg)], axis=0)
    lane = lax.broadcasted_iota(jnp.int32, (1, QT), 1)
    t = t0 + lane

    def capped(s, cap):
        return jnp.concatenate([jnp.minimum(s[:, h * QT:(h + 1) * QT], cap) for h in range(hg)], axis=1)

    def softmax_cols(s):
        p = jnp.exp2(s - jnp.max(s, axis=0, keepdims=True))
        return p, jnp.sum(p, axis=0, keepdims=True)

    ncp = kc_ref.shape[2]
    nrow = lax.broadcasted_iota(jnp.int32, (ncp, QT), 0)
    valid_c = (nrow * CMP_STRIDE + (CMP_BLOCK - 1) <= t) & (nrow < n_cmp)
    p_c, _ = softmax_cols(capped(_nt_dot(kc_ref[0, 0], q), jnp.where(valid_c, BIG, NEG)))
    p_c = p_c * jnp.concatenate([jnp.where(valid_c, 1.0, 0.0)] * hg, axis=1)
    l_c = jnp.sum(p_c, axis=0, keepdims=True)
    p_c = p_c / jnp.where(l_c > 0.0, l_c, 1.0)
    o_c = jnp.dot(vct_ref[0], p_c.astype(BF16), preferred_element_type=F32)

    ws = pl.multiple_of(jnp.maximum(t0 + QT - SWA_SPAN, 0), LANES)
    wb = ws // LANES
    vwin = jnp.concatenate([vwt_ref[0, wb + j] for j in range(SWA_SPAN // LANES)], axis=1)
    dist = t - (ws + lax.broadcasted_iota(jnp.int32, (SWA_SPAN, QT), 0))
    valid_w = (dist >= 0) & (dist < SWA_WINDOW)
    p_w, l_w = softmax_cols(capped(_nt_dot(kw_ref[0, pl.ds(ws, SWA_SPAN), :], q), jnp.where(valid_w, BIG, NEG)))
    o_w = jnp.dot(vwin, p_w.astype(BF16), preferred_element_type=F32) / l_w

    p_sum = p_c[:, 0:QT]
    for h in range(1, hg):
        p_sum = p_sum + p_c[:, h * QT:(h + 1) * QT]
    imp = _split_dot(ovt_ref[...], p_sum)
    blk = lax.broadcasted_iota(jnp.int32, (LANES, QT), 0)
    cur = t0 // SLC_BLOCK + jnp.zeros((1, QT), jnp.int32)
    for k in range(1, QT // SLC_BLOCK):
        cur = cur + jnp.where(lane >= k * SLC_BLOCK, 1, 0)
    forced = (blk == 0) | (blk == cur) | (blk == cur - 1)
    val = jnp.where(forced, imp + FORCE_BONUS, jnp.where(blk > cur, -FORCE_BONUS, imp))
    if n_slc < LANES:
        val = jnp.where(blk < n_slc, val, -jnp.inf)

    def rank_bias(nb):
        nslab = nb // 8
        slabs = [val[8 * r:8 * r + 8, :] for r in range(nslab)]
        ranks = [jnp.zeros((8, QT), F32) for _ in range(nslab)]
        sub = lax.broadcasted_iota(jnp.int32, (8, QT), 0)
        for other in range(nb):
            c = val[other:other + 1, :]
            for r in range(nslab):
                if 8 * r > other:
                    beat = c >= slabs[r]
                elif 8 * r + 7 < other:
                    beat = c > slabs[r]
                else:
                    beat = (c > slabs[r]) | ((c == slabs[r]) & (sub + 8 * r > other))
                ranks[r] = ranks[r] + jnp.where(beat, 1.0, 0.0)
        for r in range(nslab):
            chosen = (ranks[r] < top_n) & (blk[8 * r:8 * r + 8] <= cur)
            bias_ref[8 * r:8 * r + 8, :] = jnp.where(chosen, 0.0, NEG)
        if nb < LANES:
            bias_ref[nb:LANES, :] = jnp.full((LANES - nb, QT), NEG, F32)

    live = (t0 + QT - 1) // SLC_BLOCK + 1
    sizes = sorted({min(_round_up(-(-n_slc * k // RANK_SIZES), 8), LANES) for k in range(1, RANK_SIZES + 1)})
    for lo, nb in zip([0] + sizes[:-1], sizes):
        @pl.when((live > lo) & (live <= nb))
        def _(nb=nb):
            rank_bias(nb)

    bias_q = bias_ref[...].T.astype(BF16)

    q_sel = jnp.concatenate([q, jnp.concatenate([bias_q] * hg, axis=0)], axis=1)
    m_ref[...] = jnp.full(m_ref.shape, NEG, F32)
    l_ref[...] = jnp.zeros(l_ref.shape, F32)
    acc_ref[...] = jnp.zeros(acc_ref.shape, F32)

    def sel_tiles(tiles):
        for slot, (kb, _) in enumerate(tiles):
            start = pl.multiple_of(kb * SEL_TILE, SEL_TILE)
            keys = jnp.concatenate([ks_ref[0, pl.ds(start, SEL_TILE), :], et_ref[kb]], axis=1)
            s_ref[slot] = _nt_dot(keys, q_sel)
        for slot, (kb, diagonal) in enumerate(tiles):
            s = s_ref[slot]
            if diagonal:
                key = kb * SEL_TILE + lax.broadcasted_iota(jnp.int32, (SEL_TILE, QT), 0)
                s = capped(s, jnp.where(key <= t, BIG, NEG))
            m_prev = m_ref[...]
            m_new = jnp.maximum(m_prev, jnp.max(s, axis=0, keepdims=True))
            alpha = jnp.exp2(m_prev - m_new)
            p = jnp.exp2(s - m_new)
            l_ref[...] = alpha * l_ref[...] + jnp.sum(p, axis=0, keepdims=True)
            acc_ref[...] = alpha * acc_ref[...] + jnp.dot(vst_ref[0, kb], p.astype(BF16),
                                                          preferred_element_type=F32)
            m_ref[...] = m_new

    def past_pair(i, carry):
        sel_tiles([(2 * i, False), (2 * i + 1, False)])
        return carry

    last = t0 // SEL_TILE
    lax.fori_loop(0, last // 2, past_pair, 0)

    @pl.when(last % 2 == 1)
    def _():
        sel_tiles([(last - 1, False), (last, True)])

    @pl.when(last % 2 == 0)
    def _():
        sel_tiles([(last, True)])

    o_s = acc_ref[...] / l_ref[...]

    gate = jax.nn.sigmoid(gl_ref[0, 0])
    for h in range(hg):
        cols = slice(h * QT, (h + 1) * QT)
        r = N_BRANCH * h
        out = gate[r:r + 1] * o_c[:, cols] + gate[r + 1:r + 2] * o_s[:, cols] + gate[r + 2:r + 3] * o_w[:, cols]
        o_ref[:, h * DH:(h + 1) * DH] = out.T.astype(o_ref.dtype)


def _nsa(q, gates, kvc, kv, B, S):
    T, HW = q.shape
    G, DH, QT = NSA_KV_GROUPS, NSA_HEAD_DIM, NSA_Q_TILE
    hg = HW // DH // G
    n_cmp = (S - CMP_BLOCK) // CMP_STRIDE + 1
    ncp = kvc.shape[2]
    n_slc = S // SLC_BLOCK
    top_n = min(SLC_TOPN, n_slc)
    nqt = S // QT
    kvw = G * DH
    assert n_slc <= LANES and S % SEL_TILE == 0 and S >= SWA_SPAN and QT == LANES

    cs = jnp.arange(ncp)[None, :] * CMP_STRIDE
    ss = jnp.arange(LANES)[:, None] * SLC_BLOCK
    overlap = jnp.clip(jnp.minimum(cs + CMP_BLOCK, ss + SLC_BLOCK) - jnp.maximum(cs, ss), 0) // CMP_STRIDE
    overlap = jnp.where((jnp.arange(ncp)[None, :] < n_cmp) & (jnp.arange(LANES)[:, None] < n_slc), overlap, 0)
    overlap = overlap.astype(BF16)
    key_blk = (jnp.arange(S) // SLC_BLOCK).reshape(S // SEL_TILE, SEL_TILE, 1)
    expand = (key_blk == jnp.arange(LANES)[None, None, :]).astype(BF16)

    gr = _round_up(N_BRANCH * hg, 8)
    gl = gates[:, :G * hg * N_BRANCH].reshape(B, nqt, QT, G, hg * N_BRANCH)
    gl = jnp.pad(gl, ((0, 0),) * 4 + ((0, gr - hg * N_BRANCH),)).transpose(0, 3, 1, 4, 2).reshape(B * G, nqt, gr, QT)

    def v_tiles(which, tile):
        v = kv[:, :, which * kvw:(which + 1) * kvw].reshape(B, S // tile, tile, G, DH)
        return v.transpose(0, 3, 1, 4, 2).reshape(B * G, S // tile, DH, tile)

    def k_spec(which):
        return pl.BlockSpec((1, S, DH), lambda bg, i: (bg // G, 0, which * G + bg % G))

    return pl.pallas_call(
        functools.partial(_nsa_kernel, hg=hg, n_cmp=n_cmp, n_slc=n_slc, top_n=top_n),
        grid=(B * G, nqt),
        in_specs=[
            pl.BlockSpec((QT, hg * DH), lambda bg, i: ((bg // G) * nqt + i, bg % G)),
            pl.BlockSpec((1, 1, gr, QT), lambda bg, i: (bg, i, 0, 0)),
            pl.BlockSpec((1, 1, ncp, DH), lambda bg, i: (0, bg, 0, 0)),
            pl.BlockSpec((1, DH, ncp), lambda bg, i: (bg, 0, 0)),
            k_spec(0),
            pl.BlockSpec((1, S // SEL_TILE, DH, SEL_TILE), lambda bg, i: (bg, 0, 0, 0)),
            k_spec(2),
            pl.BlockSpec((1, S // LANES, DH, LANES), lambda bg, i: (bg, 0, 0, 0)),
            pl.BlockSpec((LANES, ncp), lambda bg, i: (0, 0)),
            pl.BlockSpec((S // SEL_TILE, SEL_TILE, LANES), lambda bg, i: (0, 0, 0)),
        ],
        out_specs=pl.BlockSpec((QT, hg * DH), lambda bg, i: ((bg // G) * nqt + i, bg % G)),
        out_shape=jax.ShapeDtypeStruct((T, HW), BF16),
        scratch_shapes=[pltpu.VMEM((1, hg * QT), F32), pltpu.VMEM((1, hg * QT), F32),
                        pltpu.VMEM((DH, hg * QT), F32), pltpu.VMEM((2, SEL_TILE, hg * QT), F32),
                        pltpu.VMEM((LANES, QT), F32)],
        compiler_params=_cparams("parallel", "arbitrary"),
        name="nsa_attention",
    )(q, gl, kvc, jnp.swapaxes(kvc[1], 1, 2), kv, v_tiles(1, SEL_TILE), kv, v_tiles(3, LANES), overlap, expand)


def _mlstm_kernel(q_ref, kt_ref, v_ref, o_ref, gr_ref, gc_ref, hn_ref, y_ref, c_ref, n_ref, m_ref, *, L):
    ci = pl.program_id(1)

    @pl.when(ci == 0)
    def _():
        c_ref[...] = jnp.zeros_like(c_ref)
        n_ref[...] = jnp.zeros_like(n_ref)
        m_ref[...] = jnp.zeros_like(m_ref)

    q = q_ref[...]
    kt = kt_ref[0]
    v = v_ref[...]
    ig_row = gr_ref[0, 0, 0:1, :]
    lf_row = jax.nn.log_sigmoid(gr_ref[0, 0, 1:2, :])
    lf_col = jax.nn.log_sigmoid(gc_ref[0, 0, :, 1:2])
    m_prev = m_ref[0:1, 0:1]

    ti = lax.broadcasted_iota(jnp.int32, (L, L), 0)
    si = lax.broadcasted_iota(jnp.int32, (L, L), 1)
    causal = si <= ti
    b_col = jnp.sum(jnp.where(causal, lf_row, 0.0), axis=1, keepdims=True)
    b_row = jnp.sum(jnp.where(ti <= si, lf_col, 0.0), axis=0, keepdims=True)
    b_last = b_col[L - 1:L, :]

    dmat = jnp.where(causal, b_col - b_row + ig_row, -jnp.inf)
    a_col = b_col + m_prev
    m_t = jnp.maximum(a_col, jnp.max(dmat, axis=1, keepdims=True))
    wq = jnp.dot(q, kt, preferred_element_type=F32) * jnp.exp(dmat - m_t)
    inter = jnp.exp(a_col - m_t)
    q_c = jnp.dot(q, c_ref[...].astype(BF16), preferred_element_type=F32)
    q_n = jnp.dot(q, n_ref[...].astype(BF16), preferred_element_type=F32)[:, 0:1]
    num = inter * q_c + jnp.dot(wq.astype(BF16), v, preferred_element_type=F32)
    den = inter * q_n + jnp.sum(wq, axis=1, keepdims=True)
    h = num / jnp.maximum(jnp.abs(den), jnp.exp(-m_t))
    h = h * lax.rsqrt(jnp.mean(h * h, axis=-1, keepdims=True) + RMS_EPS) * hn_ref[0]
    y_ref[...] = (jax.nn.sigmoid(o_ref[...]) * h).astype(y_ref.dtype)

    g_row = b_last - b_row + ig_row
    m_new = jnp.maximum(b_last + m_prev, jnp.max(g_row, axis=1, keepdims=True))
    decay = jnp.exp(b_last + m_prev - m_new)
    kw_t = kt.astype(F32) * jnp.exp(g_row - m_new)
    c_ref[...] = decay * c_ref[...] + jnp.dot(kw_t.astype(BF16), v, preferred_element_type=F32)
    n_ref[...] = decay * n_ref[...] + jnp.sum(kw_t, axis=1, keepdims=True)
    m_ref[...] = jnp.broadcast_to(m_new, m_ref.shape)


def _mlstm(q, kt, v, o, gates, head_norm, B, S):
    T = q.shape[0]
    H = MLSTM_HEADS
    dk = q.shape[1] // H
    dv = v.shape[1] // H
    L = min(MLSTM_CHUNK, S)
    nc = S // L
    g = gates[:, :2 * H].reshape(B, nc, L, 2, H)
    g_row = jnp.transpose(g, (0, 4, 1, 3, 2)).reshape(B * H, nc, 2, L)
    g_col = jnp.transpose(g, (0, 4, 1, 2, 3)).reshape(B * H, nc, L, 2)
    return pl.pallas_call(
        functools.partial(_mlstm_kernel, L=L),
        grid=(B * H, nc),
        in_specs=[
            pl.BlockSpec((L, dk), lambda bh, c: ((bh // H) * nc + c, bh % H)),
            pl.BlockSpec((1, dk, L), lambda bh, c: (bh // H, bh % H, c)),
            pl.BlockSpec((L, dv), lambda bh, c: ((bh // H) * nc + c, bh % H)),
            pl.BlockSpec((L, dv), lambda bh, c: ((bh // H) * nc + c, bh % H)),
            pl.BlockSpec((1, 1, 2, L), lambda bh, c: (bh, c, 0, 0)),
            pl.BlockSpec((1, 1, L, 2), lambda bh, c: (bh, c, 0, 0)),
            pl.BlockSpec((1, 1, dv), lambda bh, c: (bh % H, 0, 0)),
        ],
        out_specs=pl.BlockSpec((L, dv), lambda bh, c: ((bh // H) * nc + c, bh % H)),
        out_shape=jax.ShapeDtypeStruct((T, H * dv), BF16),
        scratch_shapes=[pltpu.VMEM((dk, dv), F32), pltpu.VMEM((dk, LANES), F32), pltpu.VMEM((8, LANES), F32)],
        compiler_params=_cparams("parallel", "arbitrary"),
        name="mlstm",
    )(q, kt, v, o, g_row, g_col, head_norm.reshape(H, 1, dv))


def _ffn_in_kernel(u_ref, halo_ref, wg_ref, wu_ref, cw_ref, cb_ref, o_ref, lhs_ref, g_ref, *, tm, tn, ff,
                   tiles_per_seq):
    i = pl.program_id(0)
    j = pl.program_id(1)

    @pl.when(j == 0)
    def _():
        halo = halo_ref[...]
        lhs_ref[0:CONV_HALO, :] = jnp.where(i % tiles_per_seq != 0, halo, jnp.zeros_like(halo))
        lhs_ref[CONV_HALO:CONV_HALO + tm, :] = u_ref[...]

    g_ref[...] = jnp.dot(lhs_ref[...], wg_ref[...], preferred_element_type=F32)
    up = jnp.dot(u_ref[...], wu_ref[...], preferred_element_type=F32)
    gate = cb_ref[...] + cw_ref[0:1, :] * g_ref[CONV_HALO:CONV_HALO + tm, :]
    for back in range(1, CONV_WIDTH):
        gate = gate + cw_ref[back:back + 1, :] * g_ref[CONV_HALO - back:CONV_HALO - back + tm, :]
    act = gate * jax.nn.sigmoid(gate) * up
    col = j * tn + lax.broadcasted_iota(jnp.int32, (1, tn), 1)
    o_ref[...] = jnp.where(col < ff, act, 0.0).astype(o_ref.dtype)


def _ffn_in(u, w2, conv_w, conv_b, layer, fp, S, tm=1024, tn=FFN_COL_TILE):
    T, D = u.shape
    ff = w2.shape[-1]
    tm = min(tm, S)
    assert T % tm == 0 and S % tm == 0 and fp % tn == 0 and fp - ff < tn
    hb = tm // CONV_HALO
    return pl.pallas_call(
        functools.partial(_ffn_in_kernel, tm=tm, tn=tn, ff=ff, tiles_per_seq=S // tm),
        grid=(T // tm, fp // tn),
        in_specs=[
            pl.BlockSpec((tm, D), lambda i, j: (i, 0)),
            pl.BlockSpec((CONV_HALO, D), lambda i, j: (jnp.maximum(i * hb - 1, 0), 0)),
            pl.BlockSpec((None, None, D, tn), lambda i, j: (layer, 0, 0, j)),
            pl.BlockSpec((None, None, D, tn), lambda i, j: (layer, 1, 0, j)),
            pl.BlockSpec((None, CONV_WIDTH, tn), lambda i, j: (layer, 0, j)),
            pl.BlockSpec((None, 1, tn), lambda i, j: (layer, 0, j)),
        ],
        out_specs=pl.BlockSpec((tm, tn), lambda i, j: (i, j)),
        out_shape=jax.ShapeDtypeStruct((T, fp), BF16),
        scratch_shapes=[pltpu.VMEM((CONV_HALO + tm, D), BF16), pltpu.VMEM((CONV_HALO + tm, tn), F32)],
        compiler_params=_cparams("parallel", "arbitrary"),
        name="ffn_in_conv_act",
    )(u, u, w2, w2, conv_w, conv_b)


def _ffn_weights(w_in, w_out):
    depth, D, ff2 = w_in.shape
    ff = ff2 // 2
    fp = _round_up(ff, FFN_COL_TILE)
    w2 = w_in.reshape(depth, D, 2, ff).transpose(0, 2, 1, 3).astype(BF16)
    w_out = jnp.pad(w_out.astype(BF16), ((0, 0), (0, fp - ff), (0, 0)))
    return w2, w_out, fp


def _conv_ffn(h, u, ln_post, next_gain, w2, w_out, fp, conv_w, conv_b, layer, S):
    act = _ffn_in(u, w2, conv_w, conv_b, layer, fp, S)
    return _mm_norm_res([act], w_out, h, ln_post, next_gain, w_lead=layer)


def _ab_layer(h, u, ln_post, next_gain, w_in, pool_w, pool_scale, cmp_pos, ck_w1, ck_w2, cv_w1, cv_w2, w_out, B, S):
    T, D = h.shape
    G, DH = NSA_KV_GROUPS, NSA_HEAD_DIM
    pw = D // 4
    hw = D - pw
    kvw = G * DH
    w = w_in.astype(BF16)
    c0 = pw + hw
    p_in = _mm(u, w, F32, 0, pw)
    q = _mm(u, w, BF16, pw, hw, scale=DH ** -0.5 * LOG2E)
    kv_cmp = _mm(u, w, F32, c0, 2 * kvw)
    kv = _mm(u, w, BF16, c0 + 2 * kvw, 4 * kvw)
    gates = _mm(u, w, F32, c0 + 6 * kvw, LANES)

    y_a = _pool_mixer(p_in.reshape(B, S, pw), pool_w, pool_scale).reshape(T, pw)
    chunks = kv_cmp.reshape(B, S, 2, G, DH).transpose(2, 0, 3, 1, 4)
    chunks = chunks.reshape(2, B * G, S // CMP_STRIDE, CMP_STRIDE * DH)
    kvc = _compress(chunks, cmp_pos, jnp.stack([ck_w1, cv_w1]), jnp.stack([ck_w2, cv_w2]))
    y_b = _nsa(q, gates, kvc, kv.reshape(B, S, 4 * kvw), B, S)
    return _mm_norm_res([y_a, y_b], w_out.astype(BF16), h, ln_post, next_gain)


def _c_layer(h, u, ln_post, next_gain, w_in, b_if, head_norm, w_out, B, S):
    T, D = h.shape
    H = MLSTM_HEADS
    dv = D // H
    dk = dv // 2
    qk = H * dk
    w = w_in.astype(BF16)
    q = _mm(u, w, BF16, 0, qk, scale=dk ** -0.5)
    k = _mm(u, w, BF16, qk, qk)
    v = _mm(u, w, BF16, 2 * qk, D)
    o = _mm(u, w, F32, 2 * qk + D, D)
    gates = _mm(u, w, F32, 2 * qk + 2 * D, LANES, bias=_pad_cols(b_if.reshape(1, 2 * H), LANES))
    kt = k.reshape(B, S, qk).transpose(0, 2, 1)
    y = _mlstm(q, kt, v, o, gates, head_norm, B, S)
    return _mm_norm_res([y], w_out.astype(BF16), h, ln_post, next_gain)


def kernel(x, ln_pre, ln_post, w_in_ab, pool_w, pool_scale, cmp_pos, cmp_k_w1, cmp_k_w2, cmp_v_w1, cmp_v_w2,
           w_out_ab, w_in_c, b_if_c, head_norm_c, w_out_c, ffn_ln_pre, ffn_ln_post, ffn_w_in, ffn_conv_w,
           ffn_conv_b, ffn_w_out):
    B, S, D = x.shape
    depth = ln_pre.shape[0]
    h = x.reshape(B * S, D)
    u = _rmsnorm(h, ln_pre[0])
    ffn_w2, ffn_wo, fp = _ffn_weights(ffn_w_in, ffn_w_out)
    ffn_cb = ffn_conv_b.reshape(depth, 1, -1)
    for layer in range(depth):
        i = layer // 2
        if layer % 2 == 0:
            h, u = _ab_layer(h, u, ln_post[layer], ffn_ln_pre[layer], w_in_ab[i], pool_w[i], pool_scale[i],
                             cmp_pos[i], cmp_k_w1[i], cmp_k_w2[i], cmp_v_w1[i], cmp_v_w2[i], w_out_ab[i], B, S)
        else:
            h, u = _c_layer(h, u, ln_post[layer], ffn_ln_pre[layer], w_in_c[i], b_if_c[i], head_norm_c[i],
                            w_out_c[i], B, S)
        next_gain = ln_pre[layer + 1] if layer + 1 < depth else None
        h, u = _conv_ffn(h, u, ffn_ln_post[layer], next_gain, ffn_w2, ffn_wo, fp, ffn_conv_w, ffn_cb, layer, S)
    return h.reshape(B, S, D)
```

```python
import functools
import math

import jax
import jax.numpy as jnp
from jax import lax
from jax.experimental import pallas as pl
from jax.experimental.pallas import tpu as pltpu

F32 = jnp.float32
BF16 = jnp.bfloat16

RMS_EPS = 1e-6
POOL_GROUPS = 4
POOL_WINDOWS = (2, 4, 8, 16)
POOL_HALO = 16
NSA_HEAD_DIM = 128
NSA_KV_GROUPS = 4
N_BRANCH = 3
CMP_BLOCK = 32
CMP_STRIDE = 16
SLC_BLOCK = 64
SLC_TOPN = 16
SWA_WINDOW = 512
FORCE_BONUS = 1e4
NEG = -1e30
BIG = 1e30
MLSTM_HEADS = 8
CONV_WIDTH = 3
CONV_HALO = 16

LANES = 128
LOG2E = math.log2(math.e)
NSA_Q_TILE = LANES
SEL_TILE = 512
RANK_SIZES = 4
SWA_SPAN = SWA_WINDOW + NSA_Q_TILE
MLSTM_CHUNK = 256
FFN_COL_TILE = 512
VMEM_LIMIT = 56 * 1024 * 1024


def _cparams(*sem):
    return pltpu.CompilerParams(dimension_semantics=sem, vmem_limit_bytes=VMEM_LIMIT)


def _round_up(n, m):
    return (n + m - 1) // m * m


def _pad_cols(w, n):
    return jnp.pad(w, ((0, 0), (0, n - w.shape[1])))


def _rmsnorm_kernel(x_ref, g_ref, o_ref):
    x = x_ref[...]
    ms = jnp.mean(x * x, axis=-1, keepdims=True)
    o_ref[...] = (x * lax.rsqrt(ms + RMS_EPS) * g_ref[...]).astype(o_ref.dtype)


def _rmsnorm(x, gain, tm=512):
    T, D = x.shape
    return pl.pallas_call(
        _rmsnorm_kernel,
        grid=(T // tm,),
        in_specs=[pl.BlockSpec((tm, D), lambda i: (i, 0)), pl.BlockSpec((1, D), lambda i: (0, 0))],
        out_specs=pl.BlockSpec((tm, D), lambda i: (i, 0)),
        out_shape=jax.ShapeDtypeStruct((T, D), BF16),
        compiler_params=_cparams("parallel"),
        name="rmsnorm",
    )(x, gain.reshape(1, D))


def _mm_kernel(a_ref, b_ref, bias_ref, o_ref, *, scale, live_cols):
    acc = jnp.dot(a_ref[...], b_ref[...], preferred_element_type=F32)
    out = (acc + bias_ref[...]) * scale
    if live_cols is not None:
        col = pl.program_id(1) * out.shape[1] + lax.broadcasted_iota(jnp.int32, (1, out.shape[1]), 1)
        out = jnp.where(col < live_cols, out, 0.0)
    o_ref[...] = out.astype(o_ref.dtype)


def _mm(a, w, out_dtype, col0, ncols, *, bias=None, scale=1.0, tm=1024, tn=512):
    M, K = a.shape
    tm = min(tm, M)
    tn = min(tn, ncols)
    assert M % tm == 0 and ncols % tn == 0 and col0 % tn == 0
    jb = col0 // tn
    live_cols = w.shape[1] - col0 if col0 + ncols > w.shape[1] else None
    if bias is None:
        bias = jnp.zeros((1, ncols), F32)
    return pl.pallas_call(
        functools.partial(_mm_kernel, scale=scale, live_cols=live_cols),
        grid=(M // tm, ncols // tn),
        in_specs=[
            pl.BlockSpec((tm, K), lambda i, j: (i, 0)),
            pl.BlockSpec((K, tn), lambda i, j: (0, j + jb)),
            pl.BlockSpec((1, tn), lambda i, j: (0, j)),
        ],
        out_specs=pl.BlockSpec((tm, tn), lambda i, j: (i, j)),
        out_shape=jax.ShapeDtypeStruct((M, ncols), out_dtype),
        compiler_params=_cparams("parallel", "arbitrary"),
        name="proj",
    )(a, w, bias)


def _mm_norm_res_kernel(*refs, nk, nk1, n_a, emit_u, tm, rows):
    a_refs = refs[:n_a]
    w_ref, h_hbm, g_ref = refs[n_a:n_a + 3]
    rest = refs[n_a + 3:]
    if emit_u:
        ng_ref, o_ref, u_ref, h_buf, h_sem = rest
    else:
        o_ref, h_buf, h_sem = rest
    i = pl.program_id(0)
    k = pl.program_id(1)

    def h_copy():
        return pltpu.make_async_copy(h_hbm.at[pl.ds(i * tm, tm), :], h_buf, h_sem)

    def product(a_ref):
        return jnp.dot(a_ref[...], w_ref[...], preferred_element_type=F32)

    @pl.when(k == 0)
    def _():
        h_copy().start()
        o_ref[...] = product(a_refs[0])

    if nk1 > 1 or n_a == 1:
        @pl.when((k > 0) & (k < nk1))
        def _():
            o_ref[...] += product(a_refs[0])

    if n_a == 2:
        @pl.when(k >= nk1)
        def _():
            o_ref[...] += product(a_refs[1])

    @pl.when(k == nk - 1)
    def _():
        h_copy().wait()

        def norm_rows(c, carry):
            r = pl.ds(pl.multiple_of(c * rows, rows), rows)
            y = o_ref[r, :]
            ms = jnp.mean(y * y, axis=-1, keepdims=True)
            h_new = h_buf[r, :] + y * lax.rsqrt(ms + RMS_EPS) * g_ref[...]
            o_ref[r, :] = h_new
            if emit_u:
                ms2 = jnp.mean(h_new * h_new, axis=-1, keepdims=True)
                u_ref[r, :] = (h_new * lax.rsqrt(ms2 + RMS_EPS) * ng_ref[...]).astype(u_ref.dtype)
            return carry

        lax.fori_loop(0, tm // rows, norm_rows, 0)


def _mm_norm_res(a_list, w, h, gain, next_gain=None, w_lead=None, tm=512, tk=1024, rows=64):
    M = h.shape[0]
    N = w.shape[-1]
    n_a = len(a_list)
    tk = math.gcd(tk, *[a.shape[1] for a in a_list])
    assert n_a in (1, 2) and M % tm == 0 and tm % rows == 0
    nk1 = a_list[0].shape[1] // tk
    nk = sum(a.shape[1] for a in a_list) // tk
    emit_u = next_gain is not None
    a_specs = [pl.BlockSpec((tm, tk), lambda i, k: (i, jnp.minimum(k, nk1 - 1)))]
    if n_a == 2:
        a_specs.append(pl.BlockSpec((tm, tk), lambda i, k: (i, jnp.maximum(k - nk1, 0))))
    row_spec = pl.BlockSpec((tm, N), lambda i, k: (i, 0))
    vec_spec = pl.BlockSpec((1, N), lambda i, k: (0, 0))
    if w_lead is None:
        w_spec = pl.BlockSpec((tk, N), lambda i, k: (k, 0))
    else:
        w_spec = pl.BlockSpec((None, tk, N), lambda i, k: (w_lead, k, 0))
    in_specs = a_specs + [w_spec, pl.BlockSpec(memory_space=pl.ANY), vec_spec]
    args = list(a_list) + [w, h, gain.reshape(1, N)]
    out_specs, out_shape = row_spec, jax.ShapeDtypeStruct((M, N), F32)
    if emit_u:
        in_specs.append(vec_spec)
        args.append(next_gain.reshape(1, N))
        out_specs = [row_spec, row_spec]
        out_shape = [out_shape, jax.ShapeDtypeStruct((M, N), BF16)]
    out = pl.pallas_call(
        functools.partial(_mm_norm_res_kernel, nk=nk, nk1=nk1, n_a=n_a, emit_u=emit_u, tm=tm, rows=rows),
        grid=(M // tm, nk),
        in_specs=in_specs,
        out_specs=out_specs,
        out_shape=out_shape,
        scratch_shapes=[pltpu.VMEM((tm, N), F32), pltpu.SemaphoreType.DMA(())],
        compiler_params=_cparams("parallel", "arbitrary"),
        name="out_proj_norm_res",
    )(*args)
    return out if emit_u else (out, None)


def _pool_kernel(cur_ref, prev_ref, w_ref, scale_ref, o_ref, ext_ref, *, ts, gd):
    i = pl.program_id(1)
    ext_ref[0:POOL_HALO, :] = jnp.where(i > 0, prev_ref[0], 0.0)
    ext_ref[POOL_HALO:POOL_HALO + ts, :] = cur_ref[0]
    t = i * ts + lax.broadcasted_iota(jnp.int32, (ts, 1), 0)
    for g, win in enumerate(POOL_WINDOWS):
        cols = slice(g * gd, (g + 1) * gd)
        tok = ext_ref[POOL_HALO:POOL_HALO + ts, cols]
        acc = tok
        for back in range(1, win):
            acc = acc + ext_ref[POOL_HALO - back:POOL_HALO - back + ts, cols]
        count = jnp.minimum(t + 1, win).astype(F32)
        pooled = acc / count - tok
        y = jnp.dot(pooled.astype(BF16), w_ref[g], preferred_element_type=F32)
        o_ref[0, :, cols] = (y * scale_ref[:, cols]).astype(o_ref.dtype)


def _pool_mixer(p_in, w_pool, scale, ts=512):
    B, S, PW = p_in.shape
    gd = PW // POOL_GROUPS
    hb = ts // POOL_HALO
    return pl.pallas_call(
        functools.partial(_pool_kernel, ts=ts, gd=gd),
        grid=(B, S // ts),
        in_specs=[
            pl.BlockSpec((1, ts, PW), lambda b, i: (b, i, 0)),
            pl.BlockSpec((1, POOL_HALO, PW), lambda b, i: (b, jnp.maximum(i * hb - 1, 0), 0)),
            pl.BlockSpec((POOL_GROUPS, gd, gd), lambda b, i: (0, 0, 0)),
            pl.BlockSpec((1, PW), lambda b, i: (0, 0)),
        ],
        out_specs=pl.BlockSpec((1, ts, PW), lambda b, i: (b, i, 0)),
        out_shape=jax.ShapeDtypeStruct((B, S, PW), BF16),
        scratch_shapes=[pltpu.VMEM((POOL_HALO + ts, PW), F32)],
        compiler_params=_cparams("parallel", "arbitrary"),
        name="pool_mixer",
    )(p_in, p_in, w_pool.astype(BF16), scale.reshape(1, PW))


def _gelu_tanh(x):
    c = math.sqrt(2.0 / math.pi)
    return x * (0.5 * (1.0 + jnp.tanh(c * (x + 0.044715 * (x * x * x)))))


def _compress_kernel(x_ref, plo_ref, phi_ref, w1a_ref, w1b_ref, w2_ref, o_ref, tmp_ref, *, nch):
    x = x_ref[0, 0]
    a = jnp.dot((x + plo_ref[...]).astype(BF16), w1a_ref[0], preferred_element_type=F32)
    b = jnp.dot((x + phi_ref[...]).astype(BF16), w1b_ref[0], preferred_element_type=F32)
    tmp_ref[0:nch, :] = b
    tmp_ref[nch:nch + 8, :] = jnp.zeros((8, b.shape[1]), F32)
    pre = a + tmp_ref[1:nch + 1, :]
    out = jnp.dot(_gelu_tanh(pre).astype(BF16), w2_ref[0], preferred_element_type=F32)
    row = lax.broadcasted_iota(jnp.int32, out.shape, 0)
    o_ref[0, 0] = jnp.where(row < nch - 1, out, 0.0).astype(o_ref.dtype)


def _compress(x, pos, w1, w2):
    two, BG, nch, cw = x.shape
    hid = w1.shape[-1]
    dh = w2.shape[-1]
    half = CMP_BLOCK // 2
    plo = pos[:half].reshape(1, cw)
    phi = pos[half:].reshape(1, cw)
    w1 = w1.astype(BF16)
    return pl.pallas_call(
        functools.partial(_compress_kernel, nch=nch),
        grid=(two, BG),
        in_specs=[
            pl.BlockSpec((1, 1, nch, cw), lambda s, b: (s, b, 0, 0)),
            pl.BlockSpec((1, cw), lambda s, b: (0, 0)),
            pl.BlockSpec((1, cw), lambda s, b: (0, 0)),
            pl.BlockSpec((1, cw, hid), lambda s, b: (s, 0, 0)),
            pl.BlockSpec((1, cw, hid), lambda s, b: (s, 1, 0)),
            pl.BlockSpec((1, hid, dh), lambda s, b: (s, 0, 0)),
        ],
        out_specs=pl.BlockSpec((1, 1, nch, dh), lambda s, b: (s, b, 0, 0)),
        out_shape=jax.ShapeDtypeStruct((two, BG, nch, dh), BF16),
        scratch_shapes=[pltpu.VMEM((nch + 8, hid), F32)],
        compiler_params=_cparams("parallel", "arbitrary"),
        name="nsa_compress",
    )(x, plo, phi, w1, w1, w2.astype(BF16))


def _nt_dot(a, b):
    return lax.dot_general(a, b, (((1,), (1,)), ((), ())), preferred_element_type=F32)


def _split_dot(w, x):
    hi = x.astype(BF16)
    r1 = x - hi.astype(F32)
    mid = r1.astype(BF16)
    lo = (r1 - mid.astype(F32)).astype(BF16)
    return (jnp.dot(w, hi, preferred_element_type=F32) + jnp.dot(w, mid, preferred_element_type=F32)
            + jnp.dot(w, lo, preferred_element_type=F32))


def _nsa_kernel(q_ref, gl_ref, kc_ref, vct_ref, ks_ref, vst_ref, kw_ref, vwt_ref, ovt_ref, et_ref, o_ref,
                m_ref, l_ref, acc_ref, s_ref, bias_ref, *, hg, n_cmp, n_slc, top_n):
    QT, DH = NSA_Q_TILE, NSA_HEAD_DIM
    t0 = pl.program_id(1) * QT
    q = jnp.concatenate([q_ref[:, h * DH:(h + 1) * DH] for h in range(hg)], axis=0)
    lane = lax.broadcasted_iota(jnp.int32, (1, QT), 1)
    t = t0 + lane

    def capped(s, cap):
        return jnp.concatenate([jnp.minimum(s[:, h * QT:(h + 1) * QT], cap) for h in range(hg)], axis=1)

    def softmax_cols(s):
        p = jnp.exp2(s - jnp.max(s, axis=0, keepdims=True))
        return p, jnp.sum(p, axis=0, keepdims=True)

    ncp = kc_ref.shape[2]
    nrow = lax.broadcasted_iota(jnp.int32, (ncp, QT), 0)
    valid_c = (nrow * CMP_STRIDE + (CMP_BLOCK - 1) <= t) & (nrow < n_cmp)
    p_c, _ = softmax_cols(capped(_nt_dot(kc_ref[0, 0], q), jnp.where(valid_c, BIG, NEG)))
    p_c = p_c * jnp.concatenate([jnp.where(valid_c, 1.0, 0.0)] * hg, axis=1)
    l_c = jnp.sum(p_c, axis=0, keepdims=True)
    p_c = p_c / jnp.where(l_c > 0.0, l_c, 1.0)
    o_c = jnp.dot(vct_ref[0], p_c.astype(BF16), preferred_element_type=F32)

    ws = pl.multiple_of(jnp.maximum(t0 + QT - SWA_SPAN, 0), LANES)
    wb = ws // LANES
    vwin = jnp.concatenate([vwt_ref[0, wb + j] for j in range(SWA_SPAN // LANES)], axis=1)
    dist = t - (ws + lax.broadcasted_iota(jnp.int32, (SWA_SPAN, QT), 0))
    valid_w = (dist >= 0) & (dist < SWA_WINDOW)
    p_w, l_w = softmax_cols(capped(_nt_dot(kw_ref[0, pl.ds(ws, SWA_SPAN), :], q), jnp.where(valid_w, BIG, NEG)))
    o_w = jnp.dot(vwin, p_w.astype(BF16), preferred_element_type=F32) / l_w

    p_sum = p_c[:, 0:QT]
    for h in range(1, hg):
        p_sum = p_sum + p_c[:, h * QT:(h + 1) * QT]
    imp = _split_dot(ovt_ref[...], p_sum)
    blk = lax.broadcasted_iota(jnp.int32, (LANES, QT), 0)
    cur = t0 // SLC_BLOCK + jnp.zeros((1, QT), jnp.int32)
    for k in range(1, QT // SLC_BLOCK):
        cur = cur + jnp.where(lane >= k * SLC_BLOCK, 1, 0)
    forced = (blk == 0) | (blk == cur) | (blk == cur - 1)
    val = jnp.where(forced, imp + FORCE_BONUS, jnp.where(blk > cur, -FORCE_BONUS, imp))
    if n_slc < LANES:
        val = jnp.where(blk < n_slc, val, -jnp.inf)

    def rank_bias(nb):
        nslab = nb // 8
        slabs = [val[8 * r:8 * r + 8, :] for r in range(nslab)]
        ranks = [jnp.zeros((8, QT), F32) for _ in range(nslab)]
        sub = lax.broadcasted_iota(jnp.int32, (8, QT), 0)
        for other in range(nb):
            c = val[other:other + 1, :]
            for r in range(nslab):
                if 8 * r > other:
                    beat = c >= slabs[r]
                elif 8 * r + 7 < other:
                    beat = c > slabs[r]
                else:
                    beat = (c > slabs[r]) | ((c == slabs[r]) & (sub + 8 * r > other))
                ranks[r] = ranks[r] + jnp.where(beat, 1.0, 0.0)
        for r in range(nslab):
            chosen = (ranks[r] < top_n) & (blk[8 * r:8 * r + 8] <= cur)
            bias_ref[8 * r:8 * r + 8, :] = jnp.where(chosen, 0.0, NEG)
        if nb < LANES:
            bias_ref[nb:LANES, :] = jnp.full((LANES - nb, QT), NEG, F32)

    live = (t0 + QT - 1) // SLC_BLOCK + 1
    sizes = sorted({min(_round_up(-(-n_slc * k // RANK_SIZES), 8), LANES) for k in range(1, RANK_SIZES + 1)})
    for lo, nb in zip([0] + sizes[:-1], sizes):
        @pl.when((live > lo) & (live <= nb))
        def _(nb=nb):
            rank_bias(nb)

    bias_q = bias_ref[...].T.astype(BF16)

    q_sel = jnp.concatenate([q, jnp.concatenate([bias_q] * hg, axis=0)], axis=1)
    m_ref[...] = jnp.full(m_ref.shape, NEG, F32)
    l_ref[...] = jnp.zeros(l_ref.shape, F32)
    acc_ref[...] = jnp.zeros(acc_ref.shape, F32)

    def sel_tiles(tiles):
        for slot, (kb, _) in enumerate(tiles):
            start = pl.multiple_of(kb * SEL_TILE, SEL_TILE)
            keys = jnp.concatenate([ks_ref[0, pl.ds(start, SEL_TILE), :], et_ref[kb]], axis=1)
            s_ref[slot] = _nt_dot(keys, q_sel)
        for slot, (kb, diagonal) in enumerate(tiles):
            s = s_ref[slot]
            if diagonal:
                key = kb * SEL_TILE + lax.broadcasted_iota(jnp.int32, (SEL_TILE, QT), 0)
                s = capped(s, jnp.where(key <= t, BIG, NEG))
            m_prev = m_ref[...]
            m_new = jnp.maximum(m_prev, jnp.max(s, axis=0, keepdims=True))
            alpha = jnp.exp2(m_prev - m_new)
            p = jnp.exp2(s - m_new)
            l_ref[...] = alpha * l_ref[...] + jnp.sum(p, axis=0, keepdims=True)
            acc_ref[...] = alpha * acc_ref[...] + jnp.dot(vst_ref[0, kb], p.astype(BF16),
                                                          preferred_element_type=F32)
            m_ref[...] = m_new

    def past_pair(i, carry):
        sel_tiles([(2 * i, False), (2 * i + 1, False)])
        return carry

    last = t0 // SEL_TILE
    lax.fori_loop(0, last // 2, past_pair, 0)

    @pl.when(last % 2 == 1)
    def _():
        sel_tiles([(last - 1, False), (last, True)])

    @pl.when(last % 2 == 0)
    def _():
        sel_tiles([(last, True)])

    o_s = acc_ref[...] / l_ref[...]

    gate = jax.nn.sigmoid(gl_ref[0, 0])
    for h in range(hg):
        cols = slice(h * QT, (h + 1) * QT)
        r = N_BRANCH * h
        out = gate[r:r + 1] * o_c[:, cols] + gate[r + 1:r + 2] * o_s[:, cols] + gate[r + 2:r + 3] * o_w[:, cols]
        o_ref[:, h * DH:(h + 1) * DH] = out.T.astype(o_ref.dtype)


def _nsa(q, gates, kvc, kv, B, S):
    T, HW = q.shape
    G, DH, QT = NSA_KV_GROUPS, NSA_HEAD_DIM, NSA_Q_TILE
    hg = HW // DH // G
    n_cmp = (S - CMP_BLOCK) // CMP_STRIDE + 1
    ncp = kvc.shape[2]
    n_slc = S // SLC_BLOCK
    top_n = min(SLC_TOPN, n_slc)
    nqt = S // QT
    kvw = G * DH
    assert n_slc <= LANES and S % SEL_TILE == 0 and S >= SWA_SPAN and QT == LANES

    cs = jnp.arange(ncp)[None, :] * CMP_STRIDE
    ss = jnp.arange(LANES)[:, None] * SLC_BLOCK
    overlap = jnp.clip(jnp.minimum(cs + CMP_BLOCK, ss + SLC_BLOCK) - jnp.maximum(cs, ss), 0) // CMP_STRIDE
    overlap = jnp.where((jnp.arange(ncp)[None, :] < n_cmp) & (jnp.arange(LANES)[:, None] < n_slc), overlap, 0)
    overlap = overlap.astype(BF16)
    key_blk = (jnp.arange(S) // SLC_BLOCK).reshape(S // SEL_TILE, SEL_TILE, 1)
    expand = (key_blk == jnp.arange(LANES)[None, None, :]).astype(BF16)

    gr = _round_up(N_BRANCH * hg, 8)
    gl = gates[:, :G * hg * N_BRANCH].reshape(B, nqt, QT, G, hg * N_BRANCH)
    gl = jnp.pad(gl, ((0, 0),) * 4 + ((0, gr - hg * N_BRANCH),)).transpose(0, 3, 1, 4, 2).reshape(B * G, nqt, gr, QT)

    def v_tiles(which, tile):
        v = kv[:, :, which * kvw:(which + 1) * kvw].reshape(B, S // tile, tile, G, DH)
        return v.transpose(0, 3, 1, 4, 2).reshape(B * G, S // tile, DH, tile)

    def k_spec(which):
        return pl.BlockSpec((1, S, DH), lambda bg, i: (bg // G, 0, which * G + bg % G))

    return pl.pallas_call(
        functools.partial(_nsa_kernel, hg=hg, n_cmp=n_cmp, n_slc=n_slc, top_n=top_n),
        grid=(B * G, nqt),
        in_specs=[
            pl.BlockSpec((QT, hg * DH), lambda bg, i: ((bg // G) * nqt + i, bg % G)),
            pl.BlockSpec((1, 1, gr, QT), lambda bg, i: (bg, i, 0, 0)),
            pl.BlockSpec((1, 1, ncp, DH), lambda bg, i: (0, bg, 0, 0)),
            pl.BlockSpec((1, DH, ncp), lambda bg, i: (bg, 0, 0)),
            k_spec(0),
            pl.BlockSpec((1, S // SEL_TILE, DH, SEL_TILE), lambda bg, i: (bg, 0, 0, 0)),
            k_spec(2),
            pl.BlockSpec((1, S // LANES, DH, LANES), lambda bg, i: (bg, 0, 0, 0)),
            pl.BlockSpec((LANES, ncp), lambda bg, i: (0, 0)),
            pl.BlockSpec((S // SEL_TILE, SEL_TILE, LANES), lambda bg, i: (0, 0, 0)),
        ],
        out_specs=pl.BlockSpec((QT, hg * DH), lambda bg, i: ((bg // G) * nqt + i, bg % G)),
        out_shape=jax.ShapeDtypeStruct((T, HW), BF16),
        scratch_shapes=[pltpu.VMEM((1, hg * QT), F32), pltpu.VMEM((1, hg * QT), F32),
                        pltpu.VMEM((DH, hg * QT), F32), pltpu.VMEM((2, SEL_TILE, hg * QT), F32),
                        pltpu.VMEM((LANES, QT), F32)],
        compiler_params=_cparams("parallel", "arbitrary"),
        name="nsa_attention",
    )(q, gl, kvc, jnp.swapaxes(kvc[1], 1, 2), kv, v_tiles(1, SEL_TILE), kv, v_tiles(3, LANES), overlap, expand)


def _mlstm_kernel(q_ref, kt_ref, v_ref, o_ref, gr_ref, gc_ref, hn_ref, y_ref, c_ref, n_ref, m_ref, *, L):
    ci = pl.program_id(1)

    @pl.when(ci == 0)
    def _():
        c_ref[...] = jnp.zeros_like(c_ref)
        n_ref[...] = jnp.zeros_like(n_ref)
        m_ref[...] = jnp.zeros_like(m_ref)

    q = q_ref[...]
    kt = kt_ref[0]
    v = v_ref[...]
    ig_row = gr_ref[0, 0, 0:1, :]
    lf_row = jax.nn.log_sigmoid(gr_ref[0, 0, 1:2, :])
    lf_col = jax.nn.log_sigmoid(gc_ref[0, 0, :, 1:2])
    m_prev = m_ref[0:1, 0:1]

    ti = lax.broadcasted_iota(jnp.int32, (L, L), 0)
    si = lax.broadcasted_iota(jnp.int32, (L, L), 1)
    causal = si <= ti
    b_col = jnp.sum(jnp.where(causal, lf_row, 0.0), axis=1, keepdims=True)
    b_row = jnp.sum(jnp.where(ti <= si, lf_col, 0.0), axis=0, keepdims=True)
    b_last = b_col[L - 1:L, :]

    dmat = jnp.where(causal, b_col - b_row + ig_row, -jnp.inf)
    a_col = b_col + m_prev
    m_t = jnp.maximum(a_col, jnp.max(dmat, axis=1, keepdims=True))
    wq = jnp.dot(q, kt, preferred_element_type=F32) * jnp.exp(dmat - m_t)
    inter = jnp.exp(a_col - m_t)
    q_c = jnp.dot(q, c_ref[...].astype(BF16), preferred_element_type=F32)
    q_n = jnp.dot(q, n_ref[...].astype(BF16), preferred_element_type=F32)[:, 0:1]
    num = inter * q_c + jnp.dot(wq.astype(BF16), v, preferred_element_type=F32)
    den = inter * q_n + jnp.sum(wq, axis=1, keepdims=True)
    h = num / jnp.maximum(jnp.abs(den), jnp.exp(-m_t))
    h = h * lax.rsqrt(jnp.mean(h * h, axis=-1, keepdims=True) + RMS_EPS) * hn_ref[0]
    y_ref[...] = (jax.nn.sigmoid(o_ref[...]) * h).astype(y_ref.dtype)

    g_row = b_last - b_row + ig_row
    m_new = jnp.maximum(b_last + m_prev, jnp.max(g_row, axis=1, keepdims=True))
    decay = jnp.exp(b_last + m_prev - m_new)
    kw_t = kt.astype(F32) * jnp.exp(g_row - m_new)
    c_ref[...] = decay * c_ref[...] + jnp.dot(kw_t.astype(BF16), v, preferred_element_type=F32)
    n_ref[...] = decay * n_ref[...] + jnp.sum(kw_t, axis=1, keepdims=True)
    m_ref[...] = jnp.broadcast_to(m_new, m_ref.shape)


def _mlstm(q, kt, v, o, gates, head_norm, B, S):
    T = q.shape[0]
    H = MLSTM_HEADS
    dk = q.shape[1] // H
    dv = v.shape[1] // H
    L = min(MLSTM_CHUNK, S)
    nc = S // L
    g = gates[:, :2 * H].reshape(B, nc, L, 2, H)
    g_row = jnp.transpose(g, (0, 4, 1, 3, 2)).reshape(B * H, nc, 2, L)
    g_col = jnp.transpose(g, (0, 4, 1, 2, 3)).reshape(B * H, nc, L, 2)
    return pl.pallas_call(
        functools.partial(_mlstm_kernel, L=L),
        grid=(B * H, nc),
        in_specs=[
            pl.BlockSpec((L, dk), lambda bh, c: ((bh // H) * nc + c, bh % H)),
            pl.BlockSpec((1, dk, L), lambda bh, c: (bh // H, bh % H, c)),
            pl.BlockSpec((L, dv), lambda bh, c: ((bh // H) * nc + c, bh % H)),
            pl.BlockSpec((L, dv), lambda bh, c: ((bh // H) * nc + c, bh % H)),
            pl.BlockSpec((1, 1, 2, L), lambda bh, c: (bh, c, 0, 0)),
            pl.BlockSpec((1, 1, L, 2), lambda bh, c: (bh, c, 0, 0)),
            pl.BlockSpec((1, 1, dv), lambda bh, c: (bh % H, 0, 0)),
        ],
        out_specs=pl.BlockSpec((L, dv), lambda bh, c: ((bh // H) * nc + c, bh % H)),
        out_shape=jax.ShapeDtypeStruct((T, H * dv), BF16),
        scratch_shapes=[pltpu.VMEM((dk, dv), F32), pltpu.VMEM((dk, LANES), F32), pltpu.VMEM((8, LANES), F32)],
        compiler_params=_cparams("parallel", "arbitrary"),
        name="mlstm",
    )(q, kt, v, o, g_row, g_col, head_norm.reshape(H, 1, dv))


def _ffn_in_kernel(u_ref, halo_ref, wg_ref, wu_ref, cw_ref, cb_ref, o_ref, lhs_ref, g_ref, *, tm, tn, ff,
                   tiles_per_seq):
    i = pl.program_id(0)
    j = pl.program_id(1)

    @pl.when(j == 0)
    def _():
        halo = halo_ref[...]
        lhs_ref[0:CONV_HALO, :] = jnp.where(i % tiles_per_seq != 0, halo, jnp.zeros_like(halo))
        lhs_ref[CONV_HALO:CONV_HALO + tm, :] = u_ref[...]

    g_ref[...] = jnp.dot(lhs_ref[...], wg_ref[...], preferred_element_type=F32)
    up = jnp.dot(u_ref[...], wu_ref[...], preferred_element_type=F32)
    gate = cb_ref[...] + cw_ref[0:1, :] * g_ref[CONV_HALO:CONV_HALO + tm, :]
    for back in range(1, CONV_WIDTH):
        gate = gate + cw_ref[back:back + 1, :] * g_ref[CONV_HALO - back:CONV_HALO - back + tm, :]
    act = gate * jax.nn.sigmoid(gate) * up
    col = j * tn + lax.broadcasted_iota(jnp.int32, (1, tn), 1)
    o_ref[...] = jnp.where(col < ff, act, 0.0).astype(o_ref.dtype)


def _ffn_in(u, w2, conv_w, conv_b, layer, fp, S, tm=1024, tn=FFN_COL_TILE):
    T, D = u.shape
    ff = w2.shape[-1]
    tm = min(tm, S)
    assert T % tm == 0 and S % tm == 0 and fp % tn == 0 and fp - ff < tn
    hb = tm // CONV_HALO
    return pl.pallas_call(
        functools.partial(_ffn_in_kernel, tm=tm, tn=tn, ff=ff, tiles_per_seq=S // tm),
        grid=(T // tm, fp // tn),
        in_specs=[
            pl.BlockSpec((tm, D), lambda i, j: (i, 0)),
            pl.BlockSpec((CONV_HALO, D), lambda i, j: (jnp.maximum(i * hb - 1, 0), 0)),
            pl.BlockSpec((None, None, D, tn), lambda i, j: (layer, 0, 0, j)),
            pl.BlockSpec((None, None, D, tn), lambda i, j: (layer, 1, 0, j)),
            pl.BlockSpec((None, CONV_WIDTH, tn), lambda i, j: (layer, 0, j)),
            pl.BlockSpec((None, 1, tn), lambda i, j: (layer, 0, j)),
        ],
        out_specs=pl.BlockSpec((tm, tn), lambda i, j: (i, j)),
        out_shape=jax.ShapeDtypeStruct((T, fp), BF16),
        scratch_shapes=[pltpu.VMEM((CONV_HALO + tm, D), BF16), pltpu.VMEM((CONV_HALO + tm, tn), F32)],
        compiler_params=_cparams("parallel", "arbitrary"),
        name="ffn_in_conv_act",
    )(u, u, w2, w2, conv_w, conv_b)


def _ffn_weights(w_in, w_out):
    depth, D, ff2 = w_in.shape
    ff = ff2 // 2
    fp = _round_up(ff, FFN_COL_TILE)
    w2 = w_in.reshape(depth, D, 2, ff).transpose(0, 2, 1, 3).astype(BF16)
    w_out = jnp.pad(w_out.astype(BF16), ((0, 0), (0, fp - ff), (0, 0)))
    return w2, w_out, fp


def _conv_ffn(h, u, ln_post, next_gain, w2, w_out, fp, conv_w, conv_b, layer, S):
    act = _ffn_in(u, w2, conv_w, conv_b, layer, fp, S)
    return _mm_norm_res([act], w_out, h, ln_post, next_gain, w_lead=layer)


def _ab_layer(h, u, ln_post, next_gain, w_in, pool_w, pool_scale, cmp_pos, ck_w1, ck_w2, cv_w1, cv_w2, w_out, B, S):
    T, D = h.shape
    G, DH = NSA_KV_GROUPS, NSA_HEAD_DIM
    pw = D // 4
    hw = D - pw
    kvw = G * DH
    w = w_in.astype(BF16)
    c0 = pw + hw
    p_in = _mm(u, w, F32, 0, pw)
    q = _mm(u, w, BF16, pw, hw, scale=DH ** -0.5 * LOG2E)
    kv_cmp = _mm(u, w, F32, c0, 2 * kvw)
    kv = _mm(u, w, BF16, c0 + 2 * kvw, 4 * kvw)
    gates = _mm(u, w, F32, c0 + 6 * kvw, LANES)

    y_a = _pool_mixer(p_in.reshape(B, S, pw), pool_w, pool_scale).reshape(T, pw)
    chunks = kv_cmp.reshape(B, S, 2, G, DH).transpose(2, 0, 3, 1, 4)
    chunks = chunks.reshape(2, B * G, S // CMP_STRIDE, CMP_STRIDE * DH)
    kvc = _compress(chunks, cmp_pos, jnp.stack([ck_w1, cv_w1]), jnp.stack([ck_w2, cv_w2]))
    y_b = _nsa(q, gates, kvc, kv.reshape(B, S, 4 * kvw), B, S)
    return _mm_norm_res([y_a, y_b], w_out.astype(BF16), h, ln_post, next_gain)


def _c_layer(h, u, ln_post, next_gain, w_in, b_if, head_norm, w_out, B, S):
    T, D = h.shape
    H = MLSTM_HEADS
    dv = D // H
    dk = dv // 2
    qk = H * dk
    w = w_in.astype(BF16)
    q = _mm(u, w, BF16, 0, qk, scale=dk ** -0.5)
    k = _mm(u, w, BF16, qk, qk)
    v = _mm(u, w, BF16, 2 * qk, D)
    o = _mm(u, w, F32, 2 * qk + D, D)
    gates = _mm(u, w, F32, 2 * qk + 2 * D, LANES, bias=_pad_cols(b_if.reshape(1, 2 * H), LANES))
    kt = k.reshape(B, S, qk).transpose(0, 2, 1)
    y = _mlstm(q, kt, v, o, gates, head_norm, B, S)
    return _mm_norm_res([y], w_out.astype(BF16), h, ln_post, next_gain)


def kernel(x, ln_pre, ln_post, w_in_ab, pool_w, pool_scale, cmp_pos, cmp_k_w1, cmp_k_w2, cmp_v_w1, cmp_v_w2,
           w_out_ab, w_in_c, b_if_c, head_norm_c, w_out_c, ffn_ln_pre, ffn_ln_post, ffn_w_in, ffn_conv_w,
           ffn_conv_b, ffn_w_out):
    B, S, D = x.shape
    depth = ln_pre.shape[0]
    h = x.reshape(B * S, D)
    u = _rmsnorm(h, ln_pre[0])
    ffn_w2, ffn_wo, fp = _ffn_weights(ffn_w_in, ffn_w_out)
    ffn_cb = ffn_conv_b.reshape(depth, 1, -1)
    for layer in range(depth):
        i = layer // 2
        if layer % 2 == 0:
            h, u = _ab_layer(h, u, ln_post[layer], ffn_ln_pre[layer], w_in_ab[i], pool_w[i], pool_scale[i],
                             cmp_pos[i], cmp_k_w1[i], cmp_k_w2[i], cmp_v_w1[i], cmp_v_w2[i], w_out_ab[i], B, S)
        else:
            h, u = _c_layer(h, u, ln_post[layer], ffn_ln_pre[layer], w_in_c[i], b_if_c[i], head_norm_c[i],
                            w_out_c[i], B, S)
        next_gain = ln_pre[layer + 1] if layer + 1 < depth else None
        h, u = _conv_ffn(h, u, ffn_ln_post[layer], next_gain, ffn_w2, ffn_wo, fp, ffn_conv_w, ffn_cb, layer, S)
    return h.reshape(B, S, D)
```

```python
import functools
import math

import jax
import jax.numpy as jnp
from jax import lax
from jax.experimental import pallas as pl
from jax.experimental.pallas import tpu as pltpu

F32 = jnp.float32
BF16 = jnp.bfloat16

RMS_EPS = 1e-6
POOL_GROUPS = 4
POOL_WINDOWS = (2, 4, 8, 16)
POOL_HALO = 16
NSA_HEAD_DIM = 128
NSA_KV_GROUPS = 4
N_BRANCH = 3
CMP_BLOCK = 32
CMP_STRIDE = 16
SLC_BLOCK = 64
SLC_TOPN = 16
SWA_WINDOW = 512
FORCE_BONUS = 1e4
NEG = -1e30
BIG = 1e30
MLSTM_HEADS = 8
CONV_WIDTH = 3
CONV_HALO = 16

LANES = 128
LOG2E = math.log2(math.e)
NSA_Q_TILE = LANES
SEL_TILE = 512
SEL_GROUP = 4
RANK_SIZES = 4
SWA_SPAN = SWA_WINDOW + NSA_Q_TILE
MLSTM_CHUNK = 256
FFN_COL_TILE = 512
VMEM_LIMIT = 56 * 1024 * 1024


def _cparams(*sem):
    return pltpu.CompilerParams(dimension_semantics=sem, vmem_limit_bytes=VMEM_LIMIT)


def _round_up(n, m):
    return (n + m - 1) // m * m


def _pad_cols(w, n):
    return jnp.pad(w, ((0, 0), (0, n - w.shape[1])))


def _rmsnorm_kernel(x_ref, g_ref, o_ref):
    x = x_ref[...]
    ms = jnp.mean(x * x, axis=-1, keepdims=True)
    o_ref[...] = (x * lax.rsqrt(ms + RMS_EPS) * g_ref[...]).astype(o_ref.dtype)


def _rmsnorm(x, gain, tm=512):
    T, D = x.shape
    return pl.pallas_call(
        _rmsnorm_kernel,
        grid=(T // tm,),
        in_specs=[pl.BlockSpec((tm, D), lambda i: (i, 0)), pl.BlockSpec((1, D), lambda i: (0, 0))],
        out_specs=pl.BlockSpec((tm, D), lambda i: (i, 0)),
        out_shape=jax.ShapeDtypeStruct((T, D), BF16),
        compiler_params=_cparams("parallel"),
        name="rmsnorm",
    )(x, gain.reshape(1, D))


def _mm_kernel(a_ref, b_ref, bias_ref, o_ref, *, scale, live_cols):
    acc = jnp.dot(a_ref[...], b_ref[...], preferred_element_type=F32)
    out = (acc + bias_ref[...]) * scale
    if live_cols is not None:
        col = pl.program_id(1) * out.shape[1] + lax.broadcasted_iota(jnp.int32, (1, out.shape[1]), 1)
        out = jnp.where(col < live_cols, out, 0.0)
    o_ref[...] = out.astype(o_ref.dtype)


def _mm(a, w, out_dtype, col0, ncols, *, bias=None, scale=1.0, tm=1024, tn=512):
    M, K = a.shape
    tm = min(tm, M)
    tn = min(tn, ncols)
    assert M % tm == 0 and ncols % tn == 0 and col0 % tn == 0
    jb = col0 // tn
    live_cols = w.shape[1] - col0 if col0 + ncols > w.shape[1] else None
    if bias is None:
        bias = jnp.zeros((1, ncols), F32)
    return pl.pallas_call(
        functools.partial(_mm_kernel, scale=scale, live_cols=live_cols),
        grid=(M // tm, ncols // tn),
        in_specs=[
            pl.BlockSpec((tm, K), lambda i, j: (i, 0)),
            pl.BlockSpec((K, tn), lambda i, j: (0, j + jb)),
            pl.BlockSpec((1, tn), lambda i, j: (0, j)),
        ],
        out_specs=pl.BlockSpec((tm, tn), lambda i, j: (i, j)),
        out_shape=jax.ShapeDtypeStruct((M, ncols), out_dtype),
        compiler_params=_cparams("parallel", "arbitrary"),
        name="proj",
    )(a, w, bias)


def _mm_norm_res_kernel(*refs, nk, nk1, n_a, emit_u, tm, rows):
    a_refs = refs[:n_a]
    w_ref, h_hbm, g_ref = refs[n_a:n_a + 3]
    rest = refs[n_a + 3:]
    if emit_u:
        ng_ref, o_ref, u_ref, h_buf, h_sem = rest
    else:
        o_ref, h_buf, h_sem = rest
    i = pl.program_id(0)
    k = pl.program_id(1)

    def h_copy():
        return pltpu.make_async_copy(h_hbm.at[pl.ds(i * tm, tm), :], h_buf, h_sem)

    def product(a_ref):
        return jnp.dot(a_ref[...], w_ref[...], preferred_element_type=F32)

    @pl.when(k == 0)
    def _():
        h_copy().start()
        o_ref[...] = product(a_refs[0])

    if nk1 > 1 or n_a == 1:
        @pl.when((k > 0) & (k < nk1))
        def _():
            o_ref[...] += product(a_refs[0])

    if n_a == 2:
        @pl.when(k >= nk1)
        def _():
            o_ref[...] += product(a_refs[1])

    @pl.when(k == nk - 1)
    def _():
        h_copy().wait()

        def norm_rows(c, carry):
            r = pl.ds(pl.multiple_of(c * rows, rows), rows)
            y = o_ref[r, :]
            ms = jnp.mean(y * y, axis=-1, keepdims=True)
            h_new = h_buf[r, :] + y * lax.rsqrt(ms + RMS_EPS) * g_ref[...]
            o_ref[r, :] = h_new
            if emit_u:
                ms2 = jnp.mean(h_new * h_new, axis=-1, keepdims=True)
                u_ref[r, :] = (h_new * lax.rsqrt(ms2 + RMS_EPS) * ng_ref[...]).astype(u_ref.dtype)
            return carry

        lax.fori_loop(0, tm // rows, norm_rows, 0)


def _mm_norm_res(a_list, w, h, gain, next_gain=None, w_lead=None, tm=512, tk=1024, rows=64):
    M = h.shape[0]
    N = w.shape[-1]
    n_a = len(a_list)
    tk = math.gcd(tk, *[a.shape[1] for a in a_list])
    assert n_a in (1, 2) and M % tm == 0 and tm % rows == 0
    nk1 = a_list[0].shape[1] // tk
    nk = sum(a.shape[1] for a in a_list) // tk
    emit_u = next_gain is not None
    a_specs = [pl.BlockSpec((tm, tk), lambda i, k: (i, jnp.minimum(k, nk1 - 1)))]
    if n_a == 2:
        a_specs.append(pl.BlockSpec((tm, tk), lambda i, k: (i, jnp.maximum(k - nk1, 0))))
    row_spec = pl.BlockSpec((tm, N), lambda i, k: (i, 0))
    vec_spec = pl.BlockSpec((1, N), lambda i, k: (0, 0))
    if w_lead is None:
        w_spec = pl.BlockSpec((tk, N), lambda i, k: (k, 0))
    else:
        w_spec = pl.BlockSpec((None, tk, N), lambda i, k: (w_lead, k, 0))
    in_specs = a_specs + [w_spec, pl.BlockSpec(memory_space=pl.ANY), vec_spec]
    args = list(a_list) + [w, h, gain.reshape(1, N)]
    out_specs, out_shape = row_spec, jax.ShapeDtypeStruct((M, N), F32)
    if emit_u:
        in_specs.append(vec_spec)
        args.append(next_gain.reshape(1, N))
        out_specs = [row_spec, row_spec]
        out_shape = [out_shape, jax.ShapeDtypeStruct((M, N), BF16)]
    out = pl.pallas_call(
        functools.partial(_mm_norm_res_kernel, nk=nk, nk1=nk1, n_a=n_a, emit_u=emit_u, tm=tm, rows=rows),
        grid=(M // tm, nk),
        in_specs=in_specs,
        out_specs=out_specs,
        out_shape=out_shape,
        scratch_shapes=[pltpu.VMEM((tm, N), F32), pltpu.SemaphoreType.DMA(())],
        compiler_params=_cparams("parallel", "arbitrary"),
        name="out_proj_norm_res",
    )(*args)
    return out if emit_u else (out, None)


def _pool_kernel(cur_ref, prev_ref, w_ref, scale_ref, o_ref, ext_ref, *, ts, gd):
    i = pl.program_id(1)
    ext_ref[0:POOL_HALO, :] = jnp.where(i > 0, prev_ref[0], 0.0)
    ext_ref[POOL_HALO:POOL_HALO + ts, :] = cur_ref[0]
    t = i * ts + lax.broadcasted_iota(jnp.int32, (ts, 1), 0)
    for g, win in enumerate(POOL_WINDOWS):
        cols = slice(g * gd, (g + 1) * gd)
        tok = ext_ref[POOL_HALO:POOL_HALO + ts, cols]
        acc = tok
        for back in range(1, win):
            acc = acc + ext_ref[POOL_HALO - back:POOL_HALO - back + ts, cols]
        count = jnp.minimum(t + 1, win).astype(F32)
        pooled = acc / count - tok
        y = jnp.dot(pooled.astype(BF16), w_ref[g], preferred_element_type=F32)
        o_ref[0, :, cols] = (y * scale_ref[:, cols]).astype(o_ref.dtype)


def _pool_mixer(p_in, w_pool, scale, ts=512):
    B, S, PW = p_in.shape
    gd = PW // POOL_GROUPS
    hb = ts // POOL_HALO
    return pl.pallas_call(
        functools.partial(_pool_kernel, ts=ts, gd=gd),
        grid=(B, S // ts),
        in_specs=[
            pl.BlockSpec((1, ts, PW), lambda b, i: (b, i, 0)),
            pl.BlockSpec((1, POOL_HALO, PW), lambda b, i: (b, jnp.maximum(i * hb - 1, 0), 0)),
            pl.BlockSpec((POOL_GROUPS, gd, gd), lambda b, i: (0, 0, 0)),
            pl.BlockSpec((1, PW), lambda b, i: (0, 0)),
        ],
        out_specs=pl.BlockSpec((1, ts, PW), lambda b, i: (b, i, 0)),
        out_shape=jax.ShapeDtypeStruct((B, S, PW), BF16),
        scratch_shapes=[pltpu.VMEM((POOL_HALO + ts, PW), F32)],
        compiler_params=_cparams("parallel", "arbitrary"),
        name="pool_mixer",
    )(p_in, p_in, w_pool.astype(BF16), scale.reshape(1, PW))


def _gelu_tanh(x):
    c = math.sqrt(2.0 / math.pi)
    return x * (0.5 * (1.0 + jnp.tanh(c * (x + 0.044715 * (x * x * x)))))


def _compress_kernel(x_ref, plo_ref, phi_ref, w1a_ref, w1b_ref, w2_ref, o_ref, tmp_ref, *, nch):
    x = x_ref[0, 0]
    a = jnp.dot((x + plo_ref[...]).astype(BF16), w1a_ref[0], preferred_element_type=F32)
    b = jnp.dot((x + phi_ref[...]).astype(BF16), w1b_ref[0], preferred_element_type=F32)
    tmp_ref[0:nch, :] = b
    tmp_ref[nch:nch + 8, :] = jnp.zeros((8, b.shape[1]), F32)
    pre = a + tmp_ref[1:nch + 1, :]
    out = jnp.dot(_gelu_tanh(pre).astype(BF16), w2_ref[0], preferred_element_type=F32)
    row = lax.broadcasted_iota(jnp.int32, out.shape, 0)
    o_ref[0, 0] = jnp.where(row < nch - 1, out, 0.0).astype(o_ref.dtype)


def _compress(x, pos, w1, w2):
    two, BG, nch, cw = x.shape
    hid = w1.shape[-1]
    dh = w2.shape[-1]
    half = CMP_BLOCK // 2
    plo = pos[:half].reshape(1, cw)
    phi = pos[half:].reshape(1, cw)
    w1 = w1.astype(BF16)
    return pl.pallas_call(
        functools.partial(_compress_kernel, nch=nch),
        grid=(two, BG),
        in_specs=[
            pl.BlockSpec((1, 1, nch, cw), lambda s, b: (s, b, 0, 0)),
            pl.BlockSpec((1, cw), lambda s, b: (0, 0)),
            pl.BlockSpec((1, cw), lambda s, b: (0, 0)),
            pl.BlockSpec((1, cw, hid), lambda s, b: (s, 0, 0)),
            pl.BlockSpec((1, cw, hid), lambda s, b: (s, 1, 0)),
            pl.BlockSpec((1, hid, dh), lambda s, b: (s, 0, 0)),
        ],
        out_specs=pl.BlockSpec((1, 1, nch, dh), lambda s, b: (s, b, 0, 0)),
        out_shape=jax.ShapeDtypeStruct((two, BG, nch, dh), BF16),
        scratch_shapes=[pltpu.VMEM((nch + 8, hid), F32)],
        compiler_params=_cparams("parallel", "arbitrary"),
        name="nsa_compress",
    )(x, plo, phi, w1, w1, w2.astype(BF16))


def _nt_dot(a, b):
    return lax.dot_general(a, b, (((1,), (1,)), ((), ())), preferred_element_type=F32)


def _split_dot(w, x):
    hi = x.astype(BF16)
    r1 = x - hi.astype(F32)
    mid = r1.astype(BF16)
    lo = (r1 - mid.astype(F32)).astype(BF16)
    return (jnp.dot(w, hi, preferred_element_type=F32) + jnp.dot(w, mid, preferred_element_type=F32)
            + jnp.dot(w, lo, preferred_element_type=F32))


def _nsa_kernel(q_ref, gl_ref, kc_ref, vct_ref, ks_ref, vst_ref, kw_ref, vwt_ref, ovt_ref, et_ref, o_ref,
                m_ref, l_ref, acc_ref, s_ref, bias_ref, *, hg, n_cmp, n_slc, top_n):
    QT, DH = NSA_Q_TILE, NSA_HEAD_DIM
    t0 = pl.program_id(1) * QT
    q = jnp.concatenate([q_ref[:, h * DH:(h + 1) * DH] for h in range(hg)], axis=0)
    lane = lax.broadcasted_iota(jnp.int32, (1, QT), 1)
    t = t0 + lane

    def capped(s, cap):
        return jnp.concatenate([jnp.minimum(s[:, h * QT:(h + 1) * QT], cap) for h in range(hg)], axis=1)

    def softmax_cols(s):
        p = jnp.exp2(s - jnp.max(s, axis=0, keepdims=True))
        return p, jnp.sum(p, axis=0, keepdims=True)

    ncp = kc_ref.shape[2]
    nrow = lax.broadcasted_iota(jnp.int32, (ncp, QT), 0)
    valid_c = (nrow * CMP_STRIDE + (CMP_BLOCK - 1) <= t) & (nrow < n_cmp)
    s_c = capped(_nt_dot(kc_ref[0, 0], q), jnp.where(valid_c, BIG, NEG))
    ws = pl.multiple_of(jnp.maximum(t0 + QT - SWA_SPAN, 0), LANES)
    wb = ws // LANES
    dist = t - (ws + lax.broadcasted_iota(jnp.int32, (SWA_SPAN, QT), 0))
    valid_w = (dist >= 0) & (dist < SWA_WINDOW)
    s_w = capped(_nt_dot(kw_ref[0, pl.ds(ws, SWA_SPAN), :], q), jnp.where(valid_w, BIG, NEG))

    p_c, l_c = softmax_cols(s_c)
    seen = jnp.concatenate([jnp.where(t >= CMP_BLOCK - 1, 1.0, 0.0)] * hg, axis=1)
    p_c = p_c * (seen / l_c)
    o_c = jnp.dot(vct_ref[0], p_c.astype(BF16), preferred_element_type=F32)

    p_w, l_w = softmax_cols(s_w)
    vwin = jnp.concatenate([vwt_ref[0, wb + j] for j in range(SWA_SPAN // LANES)], axis=1)
    o_w = jnp.dot(vwin, p_w.astype(BF16), preferred_element_type=F32) / l_w

    p_sum = p_c[:, 0:QT]
    for h in range(1, hg):
        p_sum = p_sum + p_c[:, h * QT:(h + 1) * QT]
    imp = _split_dot(ovt_ref[...], p_sum)
    blk = lax.broadcasted_iota(jnp.int32, (LANES, QT), 0)
    cur = t0 // SLC_BLOCK + jnp.zeros((1, QT), jnp.int32)
    for k in range(1, QT // SLC_BLOCK):
        cur = cur + jnp.where(lane >= k * SLC_BLOCK, 1, 0)
    forced = (blk == 0) | (blk == cur) | (blk == cur - 1)
    val = jnp.where(forced, imp + FORCE_BONUS, jnp.where(blk > cur, -FORCE_BONUS, imp))
    if n_slc < LANES:
        val = jnp.where(blk < n_slc, val, -jnp.inf)

    def rank_bias(nb):
        nslab = nb // 8
        slabs = [val[8 * r:8 * r + 8, :] for r in range(nslab)]
        ranks = [jnp.zeros((8, QT), F32) for _ in range(nslab)]
        sub = lax.broadcasted_iota(jnp.int32, (8, QT), 0)
        for other in range(nb):
            c = val[other:other + 1, :]
            for r in range(nslab):
                if 8 * r > other:
                    beat = c >= slabs[r]
                elif 8 * r + 7 < other:
                    beat = c > slabs[r]
                else:
                    beat = (c > slabs[r]) | ((c == slabs[r]) & (sub + 8 * r > other))
                ranks[r] = ranks[r] + jnp.where(beat, 1.0, 0.0)
        for r in range(nslab):
            chosen = (ranks[r] < top_n) & (blk[8 * r:8 * r + 8] <= cur)
            bias_ref[8 * r:8 * r + 8, :] = jnp.where(chosen, 0.0, NEG)
        if nb < LANES:
            bias_ref[nb:LANES, :] = jnp.full((LANES - nb, QT), NEG, F32)

    live = (t0 + QT - 1) // SLC_BLOCK + 1
    sizes = sorted({min(_round_up(-(-n_slc * k // RANK_SIZES), 8), LANES) for k in range(1, RANK_SIZES + 1)})
    for lo, nb in zip([0] + sizes[:-1], sizes):
        @pl.when((live > lo) & (live <= nb))
        def _(nb=nb):
            rank_bias(nb)

    bias_q = bias_ref[...].T.astype(BF16)

    q_sel = jnp.concatenate([q, jnp.concatenate([bias_q] * hg, axis=0)], axis=1)
    m_ref[...] = jnp.full(m_ref.shape, NEG, F32)
    l_ref[...] = jnp.zeros(l_ref.shape, F32)
    acc_ref[...] = jnp.zeros(acc_ref.shape, F32)

    def sel_tiles(tiles):
        for slot, (kb, _) in enumerate(tiles):
            start = pl.multiple_of(kb * SEL_TILE, SEL_TILE)
            keys = jnp.concatenate([ks_ref[0, pl.ds(start, SEL_TILE), :], et_ref[kb]], axis=1)
            s_ref[slot] = _nt_dot(keys, q_sel)
        for slot, (kb, diagonal) in enumerate(tiles):
            s = s_ref[slot]
            if diagonal:
                key = kb * SEL_TILE + lax.broadcasted_iota(jnp.int32, (SEL_TILE, QT), 0)
                s = capped(s, jnp.where(key <= t, BIG, NEG))
            m_prev = m_ref[...]
            m_new = jnp.maximum(m_prev, jnp.max(s, axis=0, keepdims=True))
            alpha = jnp.exp2(m_prev - m_new)
            p = jnp.exp2(s - m_new)
            l_ref[...] = alpha * l_ref[...] + jnp.sum(p, axis=0, keepdims=True)
            acc_ref[...] = alpha * acc_ref[...] + jnp.dot(vst_ref[0, kb], p.astype(BF16),
                                                          preferred_element_type=F32)
            m_ref[...] = m_new

    def past_group(i, carry):
        sel_tiles([(SEL_GROUP * i + slot, False) for slot in range(SEL_GROUP)])
        return carry

    diag = t0 // SEL_TILE
    groups = diag // SEL_GROUP
    lax.fori_loop(0, groups, past_group, 0)
    for rem in range(SEL_GROUP):
        @pl.when(diag - groups * SEL_GROUP == rem)
        def _(rem=rem):
            sel_tiles([(diag - rem + slot, slot == rem) for slot in range(rem + 1)])

    o_s = acc_ref[...] / l_ref[...]

    gate = jax.nn.sigmoid(gl_ref[0, 0])
    for h in range(hg):
        cols = slice(h * QT, (h + 1) * QT)
        r = N_BRANCH * h
        out = gate[r:r + 1] * o_c[:, cols] + gate[r + 1:r + 2] * o_s[:, cols] + gate[r + 2:r + 3] * o_w[:, cols]
        o_ref[:, h * DH:(h + 1) * DH] = out.T.astype(o_ref.dtype)


def _nsa(q, gates, kvc, kv, B, S):
    T, HW = q.shape
    G, DH, QT = NSA_KV_GROUPS, NSA_HEAD_DIM, NSA_Q_TILE
    hg = HW // DH // G
    n_cmp = (S - CMP_BLOCK) // CMP_STRIDE + 1
    ncp = kvc.shape[2]
    n_slc = S // SLC_BLOCK
    top_n = min(SLC_TOPN, n_slc)
    nqt = S // QT
    kvw = G * DH
    assert n_slc <= LANES and S % SEL_TILE == 0 and S >= SWA_SPAN and QT == LANES

    cs = jnp.arange(ncp)[None, :] * CMP_STRIDE
    ss = jnp.arange(LANES)[:, None] * SLC_BLOCK
    overlap = jnp.clip(jnp.minimum(cs + CMP_BLOCK, ss + SLC_BLOCK) - jnp.maximum(cs, ss), 0) // CMP_STRIDE
    overlap = jnp.where((jnp.arange(ncp)[None, :] < n_cmp) & (jnp.arange(LANES)[:, None] < n_slc), overlap, 0)
    overlap = overlap.astype(BF16)
    key_blk = (jnp.arange(S) // SLC_BLOCK).reshape(S // SEL_TILE, SEL_TILE, 1)
    expand = (key_blk == jnp.arange(LANES)[None, None, :]).astype(BF16)

    gr = _round_up(N_BRANCH * hg, 8)
    gl = gates[:, :G * hg * N_BRANCH].reshape(B, nqt, QT, G, hg * N_BRANCH)
    gl = jnp.pad(gl, ((0, 0),) * 4 + ((0, gr - hg * N_BRANCH),)).transpose(0, 3, 1, 4, 2).reshape(B * G, nqt, gr, QT)

    def v_tiles(which, tile):
        v = kv[:, :, which * kvw:(which + 1) * kvw].reshape(B, S // tile, tile, G, DH)
        return v.transpose(0, 3, 1, 4, 2).reshape(B * G, S // tile, DH, tile)

    def k_spec(which):
        return pl.BlockSpec((1, S, DH), lambda bg, i: (bg // G, 0, which * G + bg % G))

    return pl.pallas_call(
        functools.partial(_nsa_kernel, hg=hg, n_cmp=n_cmp, n_slc=n_slc, top_n=top_n),
        grid=(B * G, nqt),
        in_specs=[
            pl.BlockSpec((QT, hg * DH), lambda bg, i: ((bg // G) * nqt + i, bg % G)),
            pl.BlockSpec((1, 1, gr, QT), lambda bg, i: (bg, i, 0, 0)),
            pl.BlockSpec((1, 1, ncp, DH), lambda bg, i: (0, bg, 0, 0)),
            pl.BlockSpec((1, DH, ncp), lambda bg, i: (bg, 0, 0)),
            k_spec(0),
            pl.BlockSpec((1, S // SEL_TILE, DH, SEL_TILE), lambda bg, i: (bg, 0, 0, 0)),
            k_spec(2),
            pl.BlockSpec((1, S // LANES, DH, LANES), lambda bg, i: (bg, 0, 0, 0)),
            pl.BlockSpec((LANES, ncp), lambda bg, i: (0, 0)),
            pl.BlockSpec((S // SEL_TILE, SEL_TILE, LANES), lambda bg, i: (0, 0, 0)),
        ],
        out_specs=pl.BlockSpec((QT, hg * DH), lambda bg, i: ((bg // G) * nqt + i, bg % G)),
        out_shape=jax.ShapeDtypeStruct((T, HW), BF16),
        scratch_shapes=[pltpu.VMEM((1, hg * QT), F32), pltpu.VMEM((1, hg * QT), F32),
                        pltpu.VMEM((DH, hg * QT), F32), pltpu.VMEM((SEL_GROUP, SEL_TILE, hg * QT), F32),
                        pltpu.VMEM((LANES, QT), F32)],
        compiler_params=_cparams("parallel", "arbitrary"),
        name="nsa_attention",
    )(q, gl, kvc, jnp.swapaxes(kvc[1], 1, 2), kv, v_tiles(1, SEL_TILE), kv, v_tiles(3, LANES), overlap, expand)


def _mlstm_kernel(q_ref, kt_ref, v_ref, o_ref, gr_ref, gc_ref, hn_ref, y_ref, c_ref, n_ref, m_ref, *, L):
    ci = pl.program_id(1)

    @pl.when(ci == 0)
    def _():
        c_ref[...] = jnp.zeros_like(c_ref)
        n_ref[...] = jnp.zeros_like(n_ref)
        m_ref[...] = jnp.zeros_like(m_ref)

    q = q_ref[...]
    kt = kt_ref[0]
    v = v_ref[...]
    ig_row = gr_ref[0, 0, 0:1, :]
    lf_row = jax.nn.log_sigmoid(gr_ref[0, 0, 1:2, :])
    lf_col = jax.nn.log_sigmoid(gc_ref[0, 0, :, 1:2])
    m_prev = m_ref[0:1, 0:1]

    ti = lax.broadcasted_iota(jnp.int32, (L, L), 0)
    si = lax.broadcasted_iota(jnp.int32, (L, L), 1)
    causal = si <= ti
    b_col = jnp.sum(jnp.where(causal, lf_row, 0.0), axis=1, keepdims=True)
    b_row = jnp.sum(jnp.where(ti <= si, lf_col, 0.0), axis=0, keepdims=True)
    b_last = b_col[L - 1:L, :]

    dmat = jnp.where(causal, b_col - b_row + ig_row, -jnp.inf)
    a_col = b_col + m_prev
    m_t = jnp.maximum(a_col, jnp.max(dmat, axis=1, keepdims=True))
    wq = jnp.dot(q, kt, preferred_element_type=F32) * jnp.exp(dmat - m_t)
    inter = jnp.exp(a_col - m_t)
    q_c = jnp.dot(q, c_ref[...].astype(BF16), preferred_element_type=F32)
    q_n = jnp.dot(q, n_ref[...].astype(BF16), preferred_element_type=F32)[:, 0:1]
    num = inter * q_c + jnp.dot(wq.astype(BF16), v, preferred_element_type=F32)
    den = inter * q_n + jnp.sum(wq, axis=1, keepdims=True)
    h = num / jnp.maximum(jnp.abs(den), jnp.exp(-m_t))
    h = h * lax.rsqrt(jnp.mean(h * h, axis=-1, keepdims=True) + RMS_EPS) * hn_ref[0]
    y_ref[...] = (jax.nn.sigmoid(o_ref[...]) * h).astype(y_ref.dtype)

    g_row = b_last - b_row + ig_row
    m_new = jnp.maximum(b_last + m_prev, jnp.max(g_row, axis=1, keepdims=True))
    decay = jnp.exp(b_last + m_prev - m_new)
    kw_t = kt.astype(F32) * jnp.exp(g_row - m_new)
    c_ref[...] = decay * c_ref[...] + jnp.dot(kw_t.astype(BF16), v, preferred_element_type=F32)
    n_ref[...] = decay * n_ref[...] + jnp.sum(kw_t, axis=1, keepdims=True)
    m_ref[...] = jnp.broadcast_to(m_new, m_ref.shape)


def _mlstm(q, kt, v, o, gates, head_norm, B, S):
    T = q.shape[0]
    H = MLSTM_HEADS
    dk = q.shape[1] // H
    dv = v.shape[1] // H
    L = min(MLSTM_CHUNK, S)
    nc = S // L
    g = gates[:, :2 * H].reshape(B, nc, L, 2, H)
    g_row = jnp.transpose(g, (0, 4, 1, 3, 2)).reshape(B * H, nc, 2, L)
    g_col = jnp.transpose(g, (0, 4, 1, 2, 3)).reshape(B * H, nc, L, 2)
    return pl.pallas_call(
        functools.partial(_mlstm_kernel, L=L),
        grid=(B * H, nc),
        in_specs=[
            pl.BlockSpec((L, dk), lambda bh, c: ((bh // H) * nc + c, bh % H)),
            pl.BlockSpec((1, dk, L), lambda bh, c: (bh // H, bh % H, c)),
            pl.BlockSpec((L, dv), lambda bh, c: ((bh // H) * nc + c, bh % H)),
            pl.BlockSpec((L, dv), lambda bh, c: ((bh // H) * nc + c, bh % H)),
            pl.BlockSpec((1, 1, 2, L), lambda bh, c: (bh, c, 0, 0)),
            pl.BlockSpec((1, 1, L, 2), lambda bh, c: (bh, c, 0, 0)),
            pl.BlockSpec((1, 1, dv), lambda bh, c: (bh % H, 0, 0)),
        ],
        out_specs=pl.BlockSpec((L, dv), lambda bh, c: ((bh // H) * nc + c, bh % H)),
        out_shape=jax.ShapeDtypeStruct((T, H * dv), BF16),
        scratch_shapes=[pltpu.VMEM((dk, dv), F32), pltpu.VMEM((dk, LANES), F32), pltpu.VMEM((8, LANES), F32)],
        compiler_params=_cparams("parallel", "arbitrary"),
        name="mlstm",
    )(q, kt, v, o, g_row, g_col, head_norm.reshape(H, 1, dv))


def _ffn_in_kernel(u_ref, halo_ref, wg_ref, wu_ref, cw_ref, cb_ref, o_ref, lhs_ref, g_ref, *, tm, tn, ff,
                   tiles_per_seq):
    i = pl.program_id(0)
    j = pl.program_id(1)

    @pl.when(j == 0)
    def _():
        halo = halo_ref[...]
        lhs_ref[0:CONV_HALO, :] = jnp.where(i % tiles_per_seq != 0, halo, jnp.zeros_like(halo))
        lhs_ref[CONV_HALO:CONV_HALO + tm, :] = u_ref[...]

    g_ref[...] = jnp.dot(lhs_ref[...], wg_ref[...], preferred_element_type=F32)
    up = jnp.dot(u_ref[...], wu_ref[...], preferred_element_type=F32)
    gate = cb_ref[...] + cw_ref[0:1, :] * g_ref[CONV_HALO:CONV_HALO + tm, :]
    for back in range(1, CONV_WIDTH):
        gate = gate + cw_ref[back:back + 1, :] * g_ref[CONV_HALO - back:CONV_HALO - back + tm, :]
    act = gate * jax.nn.sigmoid(gate) * up
    col = j * tn + lax.broadcasted_iota(jnp.int32, (1, tn), 1)
    o_ref[...] = jnp.where(col < ff, act, 0.0).astype(o_ref.dtype)


def _ffn_in(u, w2, conv_w, conv_b, layer, fp, S, tm=1024, tn=FFN_COL_TILE):
    T, D = u.shape
    ff = w2.shape[-1]
    tm = min(tm, S)
    assert T % tm == 0 and S % tm == 0 and fp % tn == 0 and fp - ff < tn
    hb = tm // CONV_HALO
    return pl.pallas_call(
        functools.partial(_ffn_in_kernel, tm=tm, tn=tn, ff=ff, tiles_per_seq=S // tm),
        grid=(T // tm, fp // tn),
        in_specs=[
            pl.BlockSpec((tm, D), lambda i, j: (i, 0)),
            pl.BlockSpec((CONV_HALO, D), lambda i, j: (jnp.maximum(i * hb - 1, 0), 0)),
            pl.BlockSpec((None, None, D, tn), lambda i, j: (layer, 0, 0, j)),
            pl.BlockSpec((None, None, D, tn), lambda i, j: (layer, 1, 0, j)),
            pl.BlockSpec((None, CONV_WIDTH, tn), lambda i, j: (layer, 0, j)),
            pl.BlockSpec((None, 1, tn), lambda i, j: (layer, 0, j)),
        ],
        out_specs=pl.BlockSpec((tm, tn), lambda i, j: (i, j)),
        out_shape=jax.ShapeDtypeStruct((T, fp), BF16),
        scratch_shapes=[pltpu.VMEM((CONV_HALO + tm, D), BF16), pltpu.VMEM((CONV_HALO + tm, tn), F32)],
        compiler_params=_cparams("parallel", "arbitrary"),
        name="ffn_in_conv_act",
    )(u, u, w2, w2, conv_w, conv_b)


def _ffn_weights(w_in, w_out):
    depth, D, ff2 = w_in.shape
    ff = ff2 // 2
    fp = _round_up(ff, FFN_COL_TILE)
    w2 = w_in.reshape(depth, D, 2, ff).transpose(0, 2, 1, 3).astype(BF16)
    w_out = jnp.pad(w_out.astype(BF16), ((0, 0), (0, fp - ff), (0, 0)))
    return w2, w_out, fp


def _conv_ffn(h, u, ln_post, next_gain, w2, w_out, fp, conv_w, conv_b, layer, S):
    act = _ffn_in(u, w2, conv_w, conv_b, layer, fp, S)
    return _mm_norm_res([act], w_out, h, ln_post, next_gain, w_lead=layer)


def _ab_layer(h, u, ln_post, next_gain, w_in, pool_w, pool_scale, cmp_pos, ck_w1, ck_w2, cv_w1, cv_w2, w_out, B, S):
    T, D = h.shape
    G, DH = NSA_KV_GROUPS, NSA_HEAD_DIM
    pw = D // 4
    hw = D - pw
    kvw = G * DH
    w = w_in.astype(BF16)
    c0 = pw + hw
    p_in = _mm(u, w, F32, 0, pw)
    q = _mm(u, w, BF16, pw, hw, scale=DH ** -0.5 * LOG2E)
    kv_cmp = _mm(u, w, F32, c0, 2 * kvw)
    kv = _mm(u, w, BF16, c0 + 2 * kvw, 4 * kvw)
    gates = _mm(u, w, F32, c0 + 6 * kvw, LANES)

    y_a = _pool_mixer(p_in.reshape(B, S, pw), pool_w, pool_scale).reshape(T, pw)
    chunks = kv_cmp.reshape(B, S, 2, G, DH).transpose(2, 0, 3, 1, 4)
    chunks = chunks.reshape(2, B * G, S // CMP_STRIDE, CMP_STRIDE * DH)
    kvc = _compress(chunks, cmp_pos, jnp.stack([ck_w1, cv_w1]), jnp.stack([ck_w2, cv_w2]))
    y_b = _nsa(q, gates, kvc, kv.reshape(B, S, 4 * kvw), B, S)
    return _mm_norm_res([y_a, y_b], w_out.astype(BF16), h, ln_post, next_gain)


def _c_layer(h, u, ln_post, next_gain, w_in, b_if, head_norm, w_out, B, S):
    T, D = h.shape
    H = MLSTM_HEADS
    dv = D // H
    dk = dv // 2
    qk = H * dk
    w = w_in.astype(BF16)
    q = _mm(u, w, BF16, 0, qk, scale=dk ** -0.5)
    k = _mm(u, w, BF16, qk, qk)
    v = _mm(u, w, BF16, 2 * qk, D)
    o = _mm(u, w, F32, 2 * qk + D, D)
    gates = _mm(u, w, F32, 2 * qk + 2 * D, LANES, bias=_pad_cols(b_if.reshape(1, 2 * H), LANES))
    kt = k.reshape(B, S, qk).transpose(0, 2, 1)
    y = _mlstm(q, kt, v, o, gates, head_norm, B, S)
    return _mm_norm_res([y], w_out.astype(BF16), h, ln_post, next_gain)


def kernel(x, ln_pre, ln_post, w_in_ab, pool_w, pool_scale, cmp_pos, cmp_k_w1, cmp_k_w2, cmp_v_w1, cmp_v_w2,
           w_out_ab, w_in_c, b_if_c, head_norm_c, w_out_c, ffn_ln_pre, ffn_ln_post, ffn_w_in, ffn_conv_w,
           ffn_conv_b, ffn_w_out):
    B, S, D = x.shape
    depth = ln_pre.shape[0]
    h = x.reshape(B * S, D)
    u = _rmsnorm(h, ln_pre[0])
    ffn_w2, ffn_wo, fp = _ffn_weights(ffn_w_in, ffn_w_out)
    ffn_cb = ffn_conv_b.reshape(depth, 1, -1)
    for layer in range(depth):
        i = layer // 2
        if layer % 2 == 0:
            h, u = _ab_layer(h, u, ln_post[layer], ffn_ln_pre[layer], w_in_ab[i], pool_w[i], pool_scale[i],
                             cmp_pos[i], cmp_k_w1[i], cmp_k_w2[i], cmp_v_w1[i], cmp_v_w2[i], w_out_ab[i], B, S)
        else:
            h, u = _c_layer(h, u, ln_post[layer], ffn_ln_pre[layer], w_in_c[i], b_if_c[i], head_norm_c[i],
                            w_out_c[i], B, S)
        next_gain = ln_pre[layer + 1] if layer + 1 < depth else None
        h, u = _conv_ffn(h, u, ffn_ln_post[layer], next_gain, ffn_w2, ffn_wo, fp, ffn_conv_w, ffn_cb, layer, S)
    return h.reshape(B, S, D)
```

```python
import functools
import math

import jax
import jax.numpy as jnp
from jax import lax
from jax.experimental import pallas as pl
from jax.experimental.pallas import tpu as pltpu

F32 = jnp.float32
BF16 = jnp.bfloat16

RMS_EPS = 1e-6
POOL_GROUPS = 4
POOL_WINDOWS = (2, 4, 8, 16)
POOL_HALO = 16
NSA_HEAD_DIM = 128
NSA_KV_GROUPS = 4
N_BRANCH = 3
CMP_BLOCK = 32
CMP_STRIDE = 16
SLC_BLOCK = 64
SLC_TOPN = 16
SWA_WINDOW = 512
FORCE_BONUS = 1e4
NEG = -1e30
BIG = 1e30
MLSTM_HEADS = 8
CONV_WIDTH = 3
CONV_HALO = 16

LANES = 128
LOG2E = math.log2(math.e)
NSA_Q_TILE = LANES
SEL_TILE = 512
SEL_GROUP = 4
RANK_SIZES = 4
SWA_SPAN = SWA_WINDOW + NSA_Q_TILE
MLSTM_CHUNK = 256
FFN_COL_TILE = 512
VMEM_LIMIT = 56 * 1024 * 1024


def _cparams(*sem):
    return pltpu.CompilerParams(dimension_semantics=sem, vmem_limit_bytes=VMEM_LIMIT)


def _round_up(n, m):
    return (n + m - 1) // m * m


def _pad_cols(w, n):
    return jnp.pad(w, ((0, 0), (0, n - w.shape[1])))


def _rmsnorm_kernel(x_ref, g_ref, o_ref):
    x = x_ref[...]
    ms = jnp.mean(x * x, axis=-1, keepdims=True)
    o_ref[...] = (x * lax.rsqrt(ms + RMS_EPS) * g_ref[...]).astype(o_ref.dtype)


def _rmsnorm(x, gain, tm=512):
    T, D = x.shape
    return pl.pallas_call(
        _rmsnorm_kernel,
        grid=(T // tm,),
        in_specs=[pl.BlockSpec((tm, D), lambda i: (i, 0)), pl.BlockSpec((1, D), lambda i: (0, 0))],
        out_specs=pl.BlockSpec((tm, D), lambda i: (i, 0)),
        out_shape=jax.ShapeDtypeStruct((T, D), BF16),
        compiler_params=_cparams("parallel"),
        name="rmsnorm",
    )(x, gain.reshape(1, D))


def _mm_kernel(a_ref, b_ref, bias_ref, o_ref, *, scale, live_cols):
    acc = jnp.dot(a_ref[...], b_ref[...], preferred_element_type=F32)
    out = (acc + bias_ref[...]) * scale
    if live_cols is not None:
        col = pl.program_id(1) * out.shape[1] + lax.broadcasted_iota(jnp.int32, (1, out.shape[1]), 1)
        out = jnp.where(col < live_cols, out, 0.0)
    o_ref[...] = out.astype(o_ref.dtype)


def _mm(a, w, out_dtype, col0, ncols, *, bias=None, scale=1.0, tm=1024, tn=512):
    M, K = a.shape
    tm = min(tm, M)
    tn = min(tn, ncols)
    assert M % tm == 0 and ncols % tn == 0 and col0 % tn == 0
    jb = col0 // tn
    live_cols = w.shape[1] - col0 if col0 + ncols > w.shape[1] else None
    if bias is None:
        bias = jnp.zeros((1, ncols), F32)
    return pl.pallas_call(
        functools.partial(_mm_kernel, scale=scale, live_cols=live_cols),
        grid=(M // tm, ncols // tn),
        in_specs=[
            pl.BlockSpec((tm, K), lambda i, j: (i, 0)),
            pl.BlockSpec((K, tn), lambda i, j: (0, j + jb)),
            pl.BlockSpec((1, tn), lambda i, j: (0, j)),
        ],
        out_specs=pl.BlockSpec((tm, tn), lambda i, j: (i, j)),
        out_shape=jax.ShapeDtypeStruct((M, ncols), out_dtype),
        compiler_params=_cparams("parallel", "arbitrary"),
        name="proj",
    )(a, w, bias)


def _mm_norm_res_kernel(*refs, nk, nk1, n_a, emit_u, tm, rows):
    a_refs = refs[:n_a]
    w_ref, h_hbm, g_ref = refs[n_a:n_a + 3]
    rest = refs[n_a + 3:]
    if emit_u:
        ng_ref, o_ref, u_ref, h_buf, h_sem = rest
    else:
        o_ref, h_buf, h_sem = rest
    i = pl.program_id(0)
    k = pl.program_id(1)

    def h_copy():
        return pltpu.make_async_copy(h_hbm.at[pl.ds(i * tm, tm), :], h_buf, h_sem)

    def product(a_ref):
        return jnp.dot(a_ref[...], w_ref[...], preferred_element_type=F32)

    @pl.when(k == 0)
    def _():
        h_copy().start()
        o_ref[...] = product(a_refs[0])

    if nk1 > 1 or n_a == 1:
        @pl.when((k > 0) & (k < nk1))
        def _():
            o_ref[...] += product(a_refs[0])

    if n_a == 2:
        @pl.when(k >= nk1)
        def _():
            o_ref[...] += product(a_refs[1])

    @pl.when(k == nk - 1)
    def _():
        h_copy().wait()

        def norm_rows(c, carry):
            r = pl.ds(pl.multiple_of(c * rows, rows), rows)
            y = o_ref[r, :]
            ms = jnp.mean(y * y, axis=-1, keepdims=True)
            h_new = h_buf[r, :] + y * lax.rsqrt(ms + RMS_EPS) * g_ref[...]
            o_ref[r, :] = h_new
            if emit_u:
                ms2 = jnp.mean(h_new * h_new, axis=-1, keepdims=True)
                u_ref[r, :] = (h_new * lax.rsqrt(ms2 + RMS_EPS) * ng_ref[...]).astype(u_ref.dtype)
            return carry

        lax.fori_loop(0, tm // rows, norm_rows, 0)


def _mm_norm_res(a_list, w, h, gain, next_gain=None, w_lead=None, tm=512, tk=1024, rows=64):
    M = h.shape[0]
    N = w.shape[-1]
    n_a = len(a_list)
    tk = math.gcd(tk, *[a.shape[1] for a in a_list])
    assert n_a in (1, 2) and M % tm == 0 and tm % rows == 0
    nk1 = a_list[0].shape[1] // tk
    nk = sum(a.shape[1] for a in a_list) // tk
    emit_u = next_gain is not None
    a_specs = [pl.BlockSpec((tm, tk), lambda i, k: (i, jnp.minimum(k, nk1 - 1)))]
    if n_a == 2:
        a_specs.append(pl.BlockSpec((tm, tk), lambda i, k: (i, jnp.maximum(k - nk1, 0))))
    row_spec = pl.BlockSpec((tm, N), lambda i, k: (i, 0))
    vec_spec = pl.BlockSpec((1, N), lambda i, k: (0, 0))
    if w_lead is None:
        w_spec = pl.BlockSpec((tk, N), lambda i, k: (k, 0))
    else:
        w_spec = pl.BlockSpec((None, tk, N), lambda i, k: (w_lead, k, 0))
    in_specs = a_specs + [w_spec, pl.BlockSpec(memory_space=pl.ANY), vec_spec]
    args = list(a_list) + [w, h, gain.reshape(1, N)]
    out_specs, out_shape = row_spec, jax.ShapeDtypeStruct((M, N), F32)
    if emit_u:
        in_specs.append(vec_spec)
        args.append(next_gain.reshape(1, N))
        out_specs = [row_spec, row_spec]
        out_shape = [out_shape, jax.ShapeDtypeStruct((M, N), BF16)]
    out = pl.pallas_call(
        functools.partial(_mm_norm_res_kernel, nk=nk, nk1=nk1, n_a=n_a, emit_u=emit_u, tm=tm, rows=rows),
        grid=(M // tm, nk),
        in_specs=in_specs,
        out_specs=out_specs,
        out_shape=out_shape,
        scratch_shapes=[pltpu.VMEM((tm, N), F32), pltpu.SemaphoreType.DMA(())],
        compiler_params=_cparams("parallel", "arbitrary"),
        name="out_proj_norm_res",
    )(*args)
    return out if emit_u else (out, None)


def _pool_kernel(cur_ref, prev_ref, w_ref, scale_ref, o_ref, ext_ref, *, ts, gd):
    i = pl.program_id(1)
    ext_ref[0:POOL_HALO, :] = jnp.where(i > 0, prev_ref[0], 0.0)
    ext_ref[POOL_HALO:POOL_HALO + ts, :] = cur_ref[0]
    t = i * ts + lax.broadcasted_iota(jnp.int32, (ts, 1), 0)
    for g, win in enumerate(POOL_WINDOWS):
        cols = slice(g * gd, (g + 1) * gd)
        tok = ext_ref[POOL_HALO:POOL_HALO + ts, cols]
        acc = tok
        for back in range(1, win):
            acc = acc + ext_ref[POOL_HALO - back:POOL_HALO - back + ts, cols]
        count = jnp.minimum(t + 1, win).astype(F32)
        pooled = acc / count - tok
        y = jnp.dot(pooled.astype(BF16), w_ref[g], preferred_element_type=F32)
        o_ref[0, :, cols] = (y * scale_ref[:, cols]).astype(o_ref.dtype)


def _pool_mixer(p_in, w_pool, scale, ts=512):
    B, S, PW = p_in.shape
    gd = PW // POOL_GROUPS
    hb = ts // POOL_HALO
    return pl.pallas_call(
        functools.partial(_pool_kernel, ts=ts, gd=gd),
        grid=(B, S // ts),
        in_specs=[
            pl.BlockSpec((1, ts, PW), lambda b, i: (b, i, 0)),
            pl.BlockSpec((1, POOL_HALO, PW), lambda b, i: (b, jnp.maximum(i * hb - 1, 0), 0)),
            pl.BlockSpec((POOL_GROUPS, gd, gd), lambda b, i: (0, 0, 0)),
            pl.BlockSpec((1, PW), lambda b, i: (0, 0)),
        ],
        out_specs=pl.BlockSpec((1, ts, PW), lambda b, i: (b, i, 0)),
        out_shape=jax.ShapeDtypeStruct((B, S, PW), BF16),
        scratch_shapes=[pltpu.VMEM((POOL_HALO + ts, PW), F32)],
        compiler_params=_cparams("parallel", "arbitrary"),
        name="pool_mixer",
    )(p_in, p_in, w_pool.astype(BF16), scale.reshape(1, PW))


def _gelu_tanh(x):
    c = math.sqrt(2.0 / math.pi)
    return x * (0.5 * (1.0 + jnp.tanh(c * (x + 0.044715 * (x * x * x)))))


def _compress_kernel(x_ref, plo_ref, phi_ref, w1a_ref, w1b_ref, w2_ref, o_ref, tmp_ref, *, nch):
    x = x_ref[0, 0]
    a = jnp.dot((x + plo_ref[...]).astype(BF16), w1a_ref[0], preferred_element_type=F32)
    b = jnp.dot((x + phi_ref[...]).astype(BF16), w1b_ref[0], preferred_element_type=F32)
    tmp_ref[0:nch, :] = b
    tmp_ref[nch:nch + 8, :] = jnp.zeros((8, b.shape[1]), F32)
    pre = a + tmp_ref[1:nch + 1, :]
    out = jnp.dot(_gelu_tanh(pre).astype(BF16), w2_ref[0], preferred_element_type=F32)
    row = lax.broadcasted_iota(jnp.int32, out.shape, 0)
    o_ref[0, 0] = jnp.where(row < nch - 1, out, 0.0).astype(o_ref.dtype)


def _compress(x, pos, w1, w2):
    two, BG, nch, cw = x.shape
    hid = w1.shape[-1]
    dh = w2.shape[-1]
    half = CMP_BLOCK // 2
    plo = pos[:half].reshape(1, cw)
    phi = pos[half:].reshape(1, cw)
    w1 = w1.astype(BF16)
    return pl.pallas_call(
        functools.partial(_compress_kernel, nch=nch),
        grid=(two, BG),
        in_specs=[
            pl.BlockSpec((1, 1, nch, cw), lambda s, b: (s, b, 0, 0)),
            pl.BlockSpec((1, cw), lambda s, b: (0, 0)),
            pl.BlockSpec((1, cw), lambda s, b: (0, 0)),
            pl.BlockSpec((1, cw, hid), lambda s, b: (s, 0, 0)),
            pl.BlockSpec((1, cw, hid), lambda s, b: (s, 1, 0)),
            pl.BlockSpec((1, hid, dh), lambda s, b: (s, 0, 0)),
        ],
        out_specs=pl.BlockSpec((1, 1, nch, dh), lambda s, b: (s, b, 0, 0)),
        out_shape=jax.ShapeDtypeStruct((two, BG, nch, dh), BF16),
        scratch_shapes=[pltpu.VMEM((nch + 8, hid), F32)],
        compiler_params=_cparams("parallel", "arbitrary"),
        name="nsa_compress",
    )(x, plo, phi, w1, w1, w2.astype(BF16))


def _nt_dot(a, b):
    return lax.dot_general(a, b, (((1,), (1,)), ((), ())), preferred_element_type=F32)


def _split_dot(w, x):
    hi = x.astype(BF16)
    r1 = x - hi.astype(F32)
    mid = r1.astype(BF16)
    lo = (r1 - mid.astype(F32)).astype(BF16)
    return (jnp.dot(w, hi, preferred_element_type=F32) + jnp.dot(w, mid, preferred_element_type=F32)
            + jnp.dot(w, lo, preferred_element_type=F32))


def _nsa_kernel(q_ref, gl_ref, kc_ref, vct_ref, ks_ref, vst_ref, kw_ref, vwt_ref, ovt_ref, et_ref, o_ref,
                m_ref, l_ref, acc_ref, s_ref, bias_ref, *, hg, n_cmp, n_slc, top_n):
    QT, DH = NSA_Q_TILE, NSA_HEAD_DIM
    t0 = pl.program_id(1) * QT
    q = jnp.concatenate([q_ref[:, h * DH:(h + 1) * DH] for h in range(hg)], axis=0)
    lane = lax.broadcasted_iota(jnp.int32, (1, QT), 1)
    t = t0 + lane

    def capped(s, cap):
        return jnp.concatenate([jnp.minimum(s[:, h * QT:(h + 1) * QT], cap) for h in range(hg)], axis=1)

    def softmax_cols(s):
        p = jnp.exp2(s - jnp.max(s, axis=0, keepdims=True))
        return p, jnp.sum(p, axis=0, keepdims=True)

    ncp = kc_ref.shape[2]
    nrow = lax.broadcasted_iota(jnp.int32, (ncp, QT), 0)
    valid_c = (nrow * CMP_STRIDE + (CMP_BLOCK - 1) <= t) & (nrow < n_cmp)
    s_c = capped(_nt_dot(kc_ref[0, 0], q), jnp.where(valid_c, BIG, NEG))
    ws = pl.multiple_of(jnp.maximum(t0 + QT - SWA_SPAN, 0), LANES)
    wb = ws // LANES
    dist = t - (ws + lax.broadcasted_iota(jnp.int32, (SWA_SPAN, QT), 0))
    valid_w = (dist >= 0) & (dist < SWA_WINDOW)
    s_w = capped(_nt_dot(kw_ref[0, pl.ds(ws, SWA_SPAN), :], q), jnp.where(valid_w, BIG, NEG))

    p_c, l_c = softmax_cols(s_c)
    seen = jnp.concatenate([jnp.where(t >= CMP_BLOCK - 1, 1.0, 0.0)] * hg, axis=1)
    p_c = p_c * (seen / l_c)
    o_c = jnp.dot(vct_ref[0], p_c.astype(BF16), preferred_element_type=F32)

    p_w, l_w = softmax_cols(s_w)
    vwin = jnp.concatenate([vwt_ref[0, wb + j] for j in range(SWA_SPAN // LANES)], axis=1)
    o_w = jnp.dot(vwin, p_w.astype(BF16), preferred_element_type=F32) / l_w

    p_sum = p_c[:, 0:QT]
    for h in range(1, hg):
        p_sum = p_sum + p_c[:, h * QT:(h + 1) * QT]
    imp = _split_dot(ovt_ref[...], p_sum)
    blk = lax.broadcasted_iota(jnp.int32, (LANES, QT), 0)
    cur = t0 // SLC_BLOCK + jnp.zeros((1, QT), jnp.int32)
    for k in range(1, QT // SLC_BLOCK):
        cur = cur + jnp.where(lane >= k * SLC_BLOCK, 1, 0)
    forced = (blk == 0) | (blk == cur) | (blk == cur - 1)
    val = jnp.where(forced, imp + FORCE_BONUS, jnp.where(blk > cur, -FORCE_BONUS, imp))
    if n_slc < LANES:
        val = jnp.where(blk < n_slc, val, -jnp.inf)

    def rank_bias(nb):
        nslab = nb // 8
        slabs = [val[8 * r:8 * r + 8, :] for r in range(nslab)]
        ranks = [jnp.zeros((8, QT), F32) for _ in range(nslab)]
        sub = lax.broadcasted_iota(jnp.int32, (8, QT), 0)
        for other in range(nb):
            c = val[other:other + 1, :]
            for r in range(nslab):
                if 8 * r > other:
                    beat = c >= slabs[r]
                elif 8 * r + 7 < other:
                    beat = c > slabs[r]
                else:
                    beat = (c > slabs[r]) | ((c == slabs[r]) & (sub + 8 * r > other))
                ranks[r] = ranks[r] + jnp.where(beat, 1.0, 0.0)
        for r in range(nslab):
            chosen = (ranks[r] < top_n) & (blk[8 * r:8 * r + 8] <= cur)
            bias_ref[8 * r:8 * r + 8, :] = jnp.where(chosen, 0.0, NEG)
        if nb < LANES:
            bias_ref[nb:LANES, :] = jnp.full((LANES - nb, QT), NEG, F32)

    live = (t0 + QT - 1) // SLC_BLOCK + 1
    sizes = sorted({min(_round_up(-(-n_slc * k // RANK_SIZES), 8), LANES) for k in range(1, RANK_SIZES + 1)})
    for lo, nb in zip([0] + sizes[:-1], sizes):
        @pl.when((live > lo) & (live <= nb))
        def _(nb=nb):
            rank_bias(nb)

    bias_q = bias_ref[...].T.astype(BF16)

    q_sel = jnp.concatenate([q, jnp.concatenate([bias_q] * hg, axis=0)], axis=1)
    m_ref[...] = jnp.full(m_ref.shape, NEG, F32)
    l_ref[...] = jnp.zeros(l_ref.shape, F32)
    acc_ref[...] = jnp.zeros(acc_ref.shape, F32)

    def sel_tiles(tiles):
        for slot, (kb, _) in enumerate(tiles):
            start = pl.multiple_of(kb * SEL_TILE, SEL_TILE)
            keys = jnp.concatenate([ks_ref[0, pl.ds(start, SEL_TILE), :], et_ref[kb]], axis=1)
            s_ref[slot] = _nt_dot(keys, q_sel)
        for slot, (kb, diagonal) in enumerate(tiles):
            s = s_ref[slot]
            if diagonal:
                key = kb * SEL_TILE + lax.broadcasted_iota(jnp.int32, (SEL_TILE, QT), 0)
                s = capped(s, jnp.where(key <= t, BIG, NEG))
            m_prev = m_ref[...]
            m_new = jnp.maximum(m_prev, jnp.max(s, axis=0, keepdims=True))
            alpha = jnp.exp2(m_prev - m_new)
            p = jnp.exp2(s - m_new)
            l_ref[...] = alpha * l_ref[...] + jnp.sum(p, axis=0, keepdims=True)
            acc_ref[...] = alpha * acc_ref[...] + jnp.dot(vst_ref[0, kb], p.astype(BF16),
                                                          preferred_element_type=F32)
            m_ref[...] = m_new

    def past_group(i, carry):
        sel_tiles([(SEL_GROUP * i + slot, False) for slot in range(SEL_GROUP)])
        return carry

    diag = t0 // SEL_TILE
    groups = diag // SEL_GROUP
    lax.fori_loop(0, groups, past_group, 0)
    for rem in range(SEL_GROUP):
        @pl.when(diag - groups * SEL_GROUP == rem)
        def _(rem=rem):
            sel_tiles([(diag - rem + slot, slot == rem) for slot in range(rem + 1)])

    o_s = acc_ref[...] / l_ref[...]

    gate = jax.nn.sigmoid(gl_ref[0, 0])
    for h in range(hg):
        cols = slice(h * QT, (h + 1) * QT)
        r = N_BRANCH * h
        out = gate[r:r + 1] * o_c[:, cols] + gate[r + 1:r + 2] * o_s[:, cols] + gate[r + 2:r + 3] * o_w[:, cols]
        o_ref[:, h * DH:(h + 1) * DH] = out.T.astype(o_ref.dtype)


def _nsa(q, gates, kvc, kv, B, S):
    T, HW = q.shape
    G, DH, QT = NSA_KV_GROUPS, NSA_HEAD_DIM, NSA_Q_TILE
    hg = HW // DH // G
    n_cmp = (S - CMP_BLOCK) // CMP_STRIDE + 1
    ncp = kvc.shape[2]
    n_slc = S // SLC_BLOCK
    top_n = min(SLC_TOPN, n_slc)
    nqt = S // QT
    kvw = G * DH
    assert n_slc <= LANES and S % SEL_TILE == 0 and S >= SWA_SPAN and QT == LANES

    cs = jnp.arange(ncp)[None, :] * CMP_STRIDE
    ss = jnp.arange(LANES)[:, None] * SLC_BLOCK
    overlap = jnp.clip(jnp.minimum(cs + CMP_BLOCK, ss + SLC_BLOCK) - jnp.maximum(cs, ss), 0) // CMP_STRIDE
    overlap = jnp.where((jnp.arange(ncp)[None, :] < n_cmp) & (jnp.arange(LANES)[:, None] < n_slc), overlap, 0)
    overlap = overlap.astype(BF16)
    key_blk = (jnp.arange(S) // SLC_BLOCK).reshape(S // SEL_TILE, SEL_TILE, 1)
    expand = (key_blk == jnp.arange(LANES)[None, None, :]).astype(BF16)

    gr = _round_up(N_BRANCH * hg, 8)
    gl = gates[:, :G * hg * N_BRANCH].reshape(B, nqt, QT, G, hg * N_BRANCH)
    gl = jnp.pad(gl, ((0, 0),) * 4 + ((0, gr - hg * N_BRANCH),)).transpose(0, 3, 1, 4, 2).reshape(B * G, nqt, gr, QT)

    def v_tiles(which, tile):
        v = kv[:, :, which * kvw:(which + 1) * kvw].reshape(B, S // tile, tile, G, DH)
        return v.transpose(0, 3, 1, 4, 2).reshape(B * G, S // tile, DH, tile)

    def k_spec(which):
        return pl.BlockSpec((1, S, DH), lambda bg, i: (bg // G, 0, which * G + bg % G))

    return pl.pallas_call(
        functools.partial(_nsa_kernel, hg=hg, n_cmp=n_cmp, n_slc=n_slc, top_n=top_n),
        grid=(B * G, nqt),
        in_specs=[
            pl.BlockSpec((QT, hg * DH), lambda bg, i: ((bg // G) * nqt + i, bg % G)),
            pl.BlockSpec((1, 1, gr, QT), lambda bg, i: (bg, i, 0, 0)),
            pl.BlockSpec((1, 1, ncp, DH), lambda bg, i: (0, bg, 0, 0)),
            pl.BlockSpec((1, DH, ncp), lambda bg, i: (bg, 0, 0)),
            k_spec(0),
            pl.BlockSpec((1, S // SEL_TILE, DH, SEL_TILE), lambda bg, i: (bg, 0, 0, 0)),
            k_spec(2),
            pl.BlockSpec((1, S // LANES, DH, LANES), lambda bg, i: (bg, 0, 0, 0)),
            pl.BlockSpec((LANES, ncp), lambda bg, i: (0, 0)),
            pl.BlockSpec((S // SEL_TILE, SEL_TILE, LANES), lambda bg, i: (0, 0, 0)),
        ],
        out_specs=pl.BlockSpec((QT, hg * DH), lambda bg, i: ((bg // G) * nqt + i, bg % G)),
        out_shape=jax.ShapeDtypeStruct((T, HW), BF16),
        scratch_shapes=[pltpu.VMEM((1, hg * QT), F32), pltpu.VMEM((1, hg * QT), F32),
                        pltpu.VMEM((DH, hg * QT), F32), pltpu.VMEM((SEL_GROUP, SEL_TILE, hg * QT), F32),
                        pltpu.VMEM((LANES, QT), F32)],
        compiler_params=_cparams("parallel", "arbitrary"),
        name="nsa_attention",
    )(q, gl, kvc, jnp.swapaxes(kvc[1], 1, 2), kv, v_tiles(1, SEL_TILE), kv, v_tiles(3, LANES), overlap, expand)


def _mlstm_kernel(q_ref, kt_ref, v_ref, o_ref, gr_ref, gc_ref, hn_ref, y_ref, c_ref, n_ref, m_ref, *, L):
    ci = pl.program_id(1)

    @pl.when(ci == 0)
    def _():
        c_ref[...] = jnp.zeros_like(c_ref)
        n_ref[...] = jnp.zeros_like(n_ref)
        m_ref[...] = jnp.zeros_like(m_ref)

    q = q_ref[...]
    kt = kt_ref[0]
    v = v_ref[...]
    ig_row = gr_ref[0, 0, 0:1, :]
    lf_row = jax.nn.log_sigmoid(gr_ref[0, 0, 1:2, :])
    lf_col = jax.nn.log_sigmoid(gc_ref[0, 0, :, 1:2])
    m_prev = m_ref[0:1, 0:1]

    ti = lax.broadcasted_iota(jnp.int32, (L, L), 0)
    si = lax.broadcasted_iota(jnp.int32, (L, L), 1)
    causal = si <= ti
    b_col = jnp.sum(jnp.where(causal, lf_row, 0.0), axis=1, keepdims=True)
    b_row = jnp.sum(jnp.where(ti <= si, lf_col, 0.0), axis=0, keepdims=True)
    b_last = b_col[L - 1:L, :]

    dmat = jnp.where(causal, b_col - b_row + ig_row, -jnp.inf)
    a_col = b_col + m_prev
    m_t = jnp.maximum(a_col, jnp.max(dmat, axis=1, keepdims=True))
    wq = jnp.dot(q, kt, preferred_element_type=F32) * jnp.exp(dmat - m_t)
    inter = jnp.exp(a_col - m_t)
    q_c = jnp.dot(q, c_ref[...].astype(BF16), preferred_element_type=F32)
    q_n = jnp.dot(q, n_ref[...].astype(BF16), preferred_element_type=F32)[:, 0:1]
    num = inter * q_c + jnp.dot(wq.astype(BF16), v, preferred_element_type=F32)
    den = inter * q_n + jnp.sum(wq, axis=1, keepdims=True)
    h = num / jnp.maximum(jnp.abs(den), jnp.exp(-m_t))
    h = h * lax.rsqrt(jnp.mean(h * h, axis=-1, keepdims=True) + RMS_EPS) * hn_ref[0]
    y_ref[...] = (jax.nn.sigmoid(o_ref[...]) * h).astype(y_ref.dtype)

    g_row = b_last - b_row + ig_row
    m_new = jnp.maximum(b_last + m_prev, jnp.max(g_row, axis=1, keepdims=True))
    decay = jnp.exp(b_last + m_prev - m_new)
    kw_t = kt.astype(F32) * jnp.exp(g_row - m_new)
    c_ref[...] = decay * c_ref[...] + jnp.dot(kw_t.astype(BF16), v, preferred_element_type=F32)
    n_ref[...] = decay * n_ref[...] + jnp.sum(kw_t, axis=1, keepdims=True)
    m_ref[...] = jnp.broadcast_to(m_new, m_ref.shape)


def _mlstm(q, kt, v, o, gates, head_norm, B, S):
    T = q.shape[0]
    H = MLSTM_HEADS
    dk = q.shape[1] // H
    dv = v.shape[1] // H
    L = min(MLSTM_CHUNK, S)
    nc = S // L
    g = gates[:, :2 * H].reshape(B, nc, L, 2, H)
    g_row = jnp.transpose(g, (0, 4, 1, 3, 2)).reshape(B * H, nc, 2, L)
    g_col = jnp.transpose(g, (0, 4, 1, 2, 3)).reshape(B * H, nc, L, 2)
    return pl.pallas_call(
        functools.partial(_mlstm_kernel, L=L),
        grid=(B * H, nc),
        in_specs=[
            pl.BlockSpec((L, dk), lambda bh, c: ((bh // H) * nc + c, bh % H)),
            pl.BlockSpec((1, dk, L), lambda bh, c: (bh // H, bh % H, c)),
            pl.BlockSpec((L, dv), lambda bh, c: ((bh // H) * nc + c, bh % H)),
            pl.BlockSpec((L, dv), lambda bh, c: ((bh // H) * nc + c, bh % H)),
            pl.BlockSpec((1, 1, 2, L), lambda bh, c: (bh, c, 0, 0)),
            pl.BlockSpec((1, 1, L, 2), lambda bh, c: (bh, c, 0, 0)),
            pl.BlockSpec((1, 1, dv), lambda bh, c: (bh % H, 0, 0)),
        ],
        out_specs=pl.BlockSpec((L, dv), lambda bh, c: ((bh // H) * nc + c, bh % H)),
        out_shape=jax.ShapeDtypeStruct((T, H * dv), BF16),
        scratch_shapes=[pltpu.VMEM((dk, dv), F32), pltpu.VMEM((dk, LANES), F32), pltpu.VMEM((8, LANES), F32)],
        compiler_params=_cparams("parallel", "arbitrary"),
        name="mlstm",
    )(q, kt, v, o, g_row, g_col, head_norm.reshape(H, 1, dv))


def _ffn_in_kernel(u_ref, halo_ref, wg_ref, wu_ref, cw_ref, cb_ref, o_ref, lhs_ref, g_ref, tail_ref, *, tm, tn, ff,
                   sh, over, tiles_per_seq):
    i = pl.program_id(0)
    j = pl.program_id(1)

    @pl.when(j == 0)
    def _():
        halo = halo_ref[...]
        lhs_ref[0:CONV_HALO, :] = jnp.where(i % tiles_per_seq != 0, halo, jnp.zeros_like(halo))
        lhs_ref[CONV_HALO:CONV_HALO + tm, :] = u_ref[...]
        tail_ref[...] = jnp.zeros_like(tail_ref)

    g_ref[...] = jnp.dot(lhs_ref[...], wg_ref[...], preferred_element_type=F32)
    up = jnp.dot(u_ref[...], wu_ref[0], preferred_element_type=F32)
    if over:
        moved = jnp.concatenate([up[:, over:], up[:, :over]], axis=1)
        up = jnp.where(j == pl.num_programs(1) - 1, moved, up)
    gate = cb_ref[...] + cw_ref[0:1, :] * g_ref[CONV_HALO:CONV_HALO + tm, :]
    for back in range(1, CONV_WIDTH):
        gate = gate + cw_ref[back:back + 1, :] * g_ref[CONV_HALO - back:CONV_HALO - back + tm, :]
    if sh:
        lane = lax.broadcasted_iota(jnp.int32, (1, LANES), 1)
        tiles = [tail_ref[...]] + [gate[:, c * LANES:(c + 1) * LANES] for c in range(tn // LANES)]
        tail_ref[...] = tiles[-1]
        rolled = [pltpu.roll(x, sh, axis=1) for x in tiles]
        gate = jnp.concatenate([jnp.where(lane < sh, rolled[c], rolled[c + 1]) for c in range(tn // LANES)], axis=1)
    act = gate * jax.nn.sigmoid(gate) * up
    hidden = j * tn - sh + lax.broadcasted_iota(jnp.int32, (1, tn), 1)
    o_ref[...] = jnp.where((hidden >= 0) & (hidden < ff), act, 0.0).astype(o_ref.dtype)


def _ffn_in(u, w_in, conv_w, conv_b, layer, ff, sh, fp, S, tm=1024, tn=FFN_COL_TILE):
    T, D = u.shape
    tm = min(tm, S)
    assert T % tm == 0 and S % tm == 0 and fp % tn == 0 and tn % LANES == 0
    hb = tm // CONV_HALO
    up0 = (ff - sh) // LANES
    up_last = (w_in.shape[-1] - tn) // LANES
    over = max(up0 + (fp // tn - 1) * (tn // LANES) - up_last, 0) * LANES
    assert w_in.shape[-1] % LANES == 0 and over < tn
    return pl.pallas_call(
        functools.partial(_ffn_in_kernel, tm=tm, tn=tn, ff=ff, sh=sh, over=over, tiles_per_seq=S // tm),
        grid=(T // tm, fp // tn),
        in_specs=[
            pl.BlockSpec((tm, D), lambda i, j: (i, 0)),
            pl.BlockSpec((CONV_HALO, D), lambda i, j: (jnp.maximum(i * hb - 1, 0), 0)),
            pl.BlockSpec((None, D, tn), lambda i, j: (layer, 0, j)),
            pl.BlockSpec((pl.Element(1), pl.Element(D), pl.Element(tn)),
                         lambda i, j: (layer, 0, jnp.minimum(up0 + j * (tn // LANES), up_last) * LANES)),
            pl.BlockSpec((None, CONV_WIDTH, tn), lambda i, j: (layer, 0, j)),
            pl.BlockSpec((None, 1, tn), lambda i, j: (layer, 0, j)),
        ],
        out_specs=pl.BlockSpec((tm, tn), lambda i, j: (i, j)),
        out_shape=jax.ShapeDtypeStruct((T, fp), BF16),
        scratch_shapes=[pltpu.VMEM((CONV_HALO + tm, D), BF16), pltpu.VMEM((CONV_HALO + tm, tn), F32),
                        pltpu.VMEM((tm, LANES), F32)],
        compiler_params=_cparams("parallel", "arbitrary"),
        name="ffn_in_conv_act",
    )(u, u, w_in, w_in, conv_w, conv_b)


def _ffn_weights(w_in, w_out):
    depth, D, ff2 = w_in.shape
    ff = ff2 // 2
    sh = ff % LANES
    fp = _round_up(ff + sh, FFN_COL_TILE)
    w_in = w_in.astype(BF16)
    w_out = jnp.pad(w_out, ((0, 0), (sh, fp - ff - sh), (0, 0))).astype(BF16)
    return w_in, w_out, ff, sh, fp


def _conv_ffn(h, u, ln_post, next_gain, w_in, w_out, ff, sh, fp, conv_w, conv_b, layer, S):
    act = _ffn_in(u, w_in, conv_w, conv_b, layer, ff, sh, fp, S)
    return _mm_norm_res([act], w_out, h, ln_post, next_gain, w_lead=layer)


def _ab_layer(h, u, ln_post, next_gain, w_in, pool_w, pool_scale, cmp_pos, ck_w1, ck_w2, cv_w1, cv_w2, w_out, B, S):
    T, D = h.shape
    G, DH = NSA_KV_GROUPS, NSA_HEAD_DIM
    pw = D // 4
    hw = D - pw
    kvw = G * DH
    w = w_in.astype(BF16)
    c0 = pw + hw
    p_in = _mm(u, w, F32, 0, pw)
    q = _mm(u, w, BF16, pw, hw, scale=DH ** -0.5 * LOG2E)
    kv_cmp = _mm(u, w, F32, c0, 2 * kvw)
    kv = _mm(u, w, BF16, c0 + 2 * kvw, 4 * kvw)
    gates = _mm(u, w, F32, c0 + 6 * kvw, LANES)

    y_a = _pool_mixer(p_in.reshape(B, S, pw), pool_w, pool_scale).reshape(T, pw)
    chunks = kv_cmp.reshape(B, S, 2, G, DH).transpose(2, 0, 3, 1, 4)
    chunks = chunks.reshape(2, B * G, S // CMP_STRIDE, CMP_STRIDE * DH)
    kvc = _compress(chunks, cmp_pos, jnp.stack([ck_w1, cv_w1]), jnp.stack([ck_w2, cv_w2]))
    y_b = _nsa(q, gates, kvc, kv.reshape(B, S, 4 * kvw), B, S)
    return _mm_norm_res([y_a, y_b], w_out.astype(BF16), h, ln_post, next_gain)


def _c_layer(h, u, ln_post, next_gain, w_in, b_if, head_norm, w_out, B, S):
    T, D = h.shape
    H = MLSTM_HEADS
    dv = D // H
    dk = dv // 2
    qk = H * dk
    w = w_in.astype(BF16)
    q = _mm(u, w, BF16, 0, qk, scale=dk ** -0.5)
    k = _mm(u, w, BF16, qk, qk)
    v = _mm(u, w, BF16, 2 * qk, D)
    o = _mm(u, w, F32, 2 * qk + D, D)
    gates = _mm(u, w, F32, 2 * qk + 2 * D, LANES, bias=_pad_cols(b_if.reshape(1, 2 * H), LANES))
    kt = k.reshape(B, S, qk).transpose(0, 2, 1)
    y = _mlstm(q, kt, v, o, gates, head_norm, B, S)
    return _mm_norm_res([y], w_out.astype(BF16), h, ln_post, next_gain)


def kernel(x, ln_pre, ln_post, w_in_ab, pool_w, pool_scale, cmp_pos, cmp_k_w1, cmp_k_w2, cmp_v_w1, cmp_v_w2,
           w_out_ab, w_in_c, b_if_c, head_norm_c, w_out_c, ffn_ln_pre, ffn_ln_post, ffn_w_in, ffn_conv_w,
           ffn_conv_b, ffn_w_out):
    B, S, D = x.shape
    depth = ln_pre.shape[0]
    h = x.reshape(B * S, D)
    u = _rmsnorm(h, ln_pre[0])
    ffn_wi, ffn_wo, ff, sh, fp = _ffn_weights(ffn_w_in, ffn_w_out)
    ffn_cb = ffn_conv_b.reshape(depth, 1, -1)
    for layer in range(depth):
        i = layer // 2
        if layer % 2 == 0:
            h, u = _ab_layer(h, u, ln_post[layer], ffn_ln_pre[layer], w_in_ab[i], pool_w[i], pool_scale[i],
                             cmp_pos[i], cmp_k_w1[i], cmp_k_w2[i], cmp_v_w1[i], cmp_v_w2[i], w_out_ab[i], B, S)
        else:
            h, u = _c_layer(h, u, ln_post[layer], ffn_ln_pre[layer], w_in_c[i], b_if_c[i], head_norm_c[i],
                            w_out_c[i], B, S)
        next_gain = ln_pre[layer + 1] if layer + 1 < depth else None
        h, u = _conv_ffn(h, u, ffn_ln_post[layer], next_gain, ffn_wi, ffn_wo, ff, sh, fp, ffn_conv_w, ffn_cb, layer,
                         S)
    return h.reshape(B, S, D)
```

```python
import functools
import math

import jax
import jax.numpy as jnp
from jax import lax
from jax.experimental import pallas as pl
from jax.experimental.pallas import tpu as pltpu

F32 = jnp.float32
BF16 = jnp.bfloat16

RMS_EPS = 1e-6
POOL_GROUPS = 4
POOL_WINDOWS = (2, 4, 8, 16)
POOL_HALO = 16
NSA_HEAD_DIM = 128
NSA_KV_GROUPS = 4
N_BRANCH = 3
CMP_BLOCK = 32
CMP_STRIDE = 16
SLC_BLOCK = 64
SLC_TOPN = 16
SWA_WINDOW = 512
FORCE_BONUS = 1e4
NEG = -1e30
BIG = 1e30
MLSTM_HEADS = 8
CONV_WIDTH = 3
CONV_HALO = 16

LANES = 128
LOG2E = math.log2(math.e)
NSA_Q_TILE = LANES
SEL_TILE = 512
SEL_GROUP = 4
RANK_SIZES = 4
SWA_SPAN = SWA_WINDOW + NSA_Q_TILE
MLSTM_CHUNK = 256
FFN_COL_TILE = 512
VMEM_LIMIT = 56 * 1024 * 1024


def _cparams(*sem):
    return pltpu.CompilerParams(dimension_semantics=sem, vmem_limit_bytes=VMEM_LIMIT)


def _round_up(n, m):
    return (n + m - 1) // m * m


def _pad_cols(w, n):
    return jnp.pad(w, ((0, 0), (0, n - w.shape[1])))


def _rmsnorm_kernel(x_ref, g_ref, o_ref):
    x = x_ref[...]
    ms = jnp.mean(x * x, axis=-1, keepdims=True)
    o_ref[...] = (x * lax.rsqrt(ms + RMS_EPS) * g_ref[...]).astype(o_ref.dtype)


def _rmsnorm(x, gain, tm=512):
    T, D = x.shape
    return pl.pallas_call(
        _rmsnorm_kernel,
        grid=(T // tm,),
        in_specs=[pl.BlockSpec((tm, D), lambda i: (i, 0)), pl.BlockSpec((1, D), lambda i: (0, 0))],
        out_specs=pl.BlockSpec((tm, D), lambda i: (i, 0)),
        out_shape=jax.ShapeDtypeStruct((T, D), BF16),
        compiler_params=_cparams("parallel"),
        name="rmsnorm",
    )(x, gain.reshape(1, D))


def _mm_kernel(a_ref, b_ref, bias_ref, o_ref, *, scale, live_cols):
    acc = jnp.dot(a_ref[...], b_ref[...], preferred_element_type=F32)
    out = (acc + bias_ref[...]) * scale
    if live_cols is not None:
        col = pl.program_id(1) * out.shape[1] + lax.broadcasted_iota(jnp.int32, (1, out.shape[1]), 1)
        out = jnp.where(col < live_cols, out, 0.0)
    o_ref[...] = out.astype(o_ref.dtype)


def _mm(a, w, out_dtype, col0, ncols, *, bias=None, scale=1.0, tm=1024, tn=512):
    M, K = a.shape
    tm = min(tm, M)
    tn = min(tn, ncols)
    assert M % tm == 0 and ncols % tn == 0 and col0 % tn == 0
    jb = col0 // tn
    live_cols = w.shape[1] - col0 if col0 + ncols > w.shape[1] else None
    if bias is None:
        bias = jnp.zeros((1, ncols), F32)
    return pl.pallas_call(
        functools.partial(_mm_kernel, scale=scale, live_cols=live_cols),
        grid=(M // tm, ncols // tn),
        in_specs=[
            pl.BlockSpec((tm, K), lambda i, j: (i, 0)),
            pl.BlockSpec((K, tn), lambda i, j: (0, j + jb)),
            pl.BlockSpec((1, tn), lambda i, j: (0, j)),
        ],
        out_specs=pl.BlockSpec((tm, tn), lambda i, j: (i, j)),
        out_shape=jax.ShapeDtypeStruct((M, ncols), out_dtype),
        compiler_params=_cparams("parallel", "arbitrary"),
        name="proj",
    )(a, w, bias)


def _mm_norm_res_kernel(*refs, nk, nk1, n_a, emit_u, tm, rows):
    a_refs = refs[:n_a]
    w_ref, h_hbm, g_ref = refs[n_a:n_a + 3]
    rest = refs[n_a + 3:]
    if emit_u:
        ng_ref, o_hbm, u_hbm, acc_ref, h_buf, o_buf, u_buf, h_sem, o_sem, u_sem = rest
    else:
        o_hbm, acc_ref, h_buf, o_buf, h_sem, o_sem = rest
    i = pl.program_id(0)
    k = pl.program_id(1)
    nchunk = tm // rows

    def chunk_rows(c):
        return pl.ds(pl.multiple_of(i * tm + c * rows, rows), rows)

    def h_copy(c, slot):
        return pltpu.make_async_copy(h_hbm.at[chunk_rows(c), :], h_buf.at[slot], h_sem.at[slot])

    def o_copy(c, slot):
        return pltpu.make_async_copy(o_buf.at[slot], o_hbm.at[chunk_rows(c), :], o_sem.at[slot])

    def u_copy(c, slot):
        return pltpu.make_async_copy(u_buf.at[slot], u_hbm.at[chunk_rows(c), :], u_sem.at[slot])

    def product(a_ref):
        return jnp.dot(a_ref[...], w_ref[...], preferred_element_type=F32)

    @pl.when(k == nk - 1)
    def _():
        h_copy(0, 0).start()

    @pl.when(k == 0)
    def _():
        acc_ref[...] = product(a_refs[0])

    if nk1 > 1 or n_a == 1:
        @pl.when((k > 0) & (k < nk1))
        def _():
            acc_ref[...] += product(a_refs[0])

    if n_a == 2:
        @pl.when(k >= nk1)
        def _():
            acc_ref[...] += product(a_refs[1])

    @pl.when(k == nk - 1)
    def _():
        def norm_rows(c, carry):
            slot = c % 2
            h_copy(c, slot).wait()

            @pl.when(c + 1 < nchunk)
            def _():
                h_copy(c + 1, 1 - slot).start()

            @pl.when(c >= 2)
            def _():
                o_copy(c - 2, slot).wait()
                if emit_u:
                    u_copy(c - 2, slot).wait()

            y = acc_ref[pl.ds(pl.multiple_of(c * rows, rows), rows), :]
            ms = jnp.mean(y * y, axis=-1, keepdims=True)
            h_new = h_buf[slot] + y * lax.rsqrt(ms + RMS_EPS) * g_ref[...]
            o_buf[slot] = h_new
            o_copy(c, slot).start()
            if emit_u:
                ms2 = jnp.mean(h_new * h_new, axis=-1, keepdims=True)
                u_buf[slot] = (h_new * lax.rsqrt(ms2 + RMS_EPS) * ng_ref[...]).astype(u_buf.dtype)
                u_copy(c, slot).start()
            return carry

        lax.fori_loop(0, nchunk, norm_rows, 0)
        for c in range(max(nchunk - 2, 0), nchunk):
            o_copy(c, c % 2).wait()
            if emit_u:
                u_copy(c, c % 2).wait()


def _mm_norm_res(a_list, w, h, gain, next_gain=None, w_lead=None, tm=1024, tk=1024, rows=64):
    M = h.shape[0]
    N = w.shape[-1]
    n_a = len(a_list)
    tm = min(tm, M)
    tk = math.gcd(tk, *[a.shape[1] for a in a_list])
    assert n_a in (1, 2) and M % tm == 0 and tm % (2 * rows) == 0
    nk1 = a_list[0].shape[1] // tk
    nk = sum(a.shape[1] for a in a_list) // tk
    emit_u = next_gain is not None
    a_specs = [pl.BlockSpec((tm, tk), lambda i, k: (i, jnp.minimum(k, nk1 - 1)))]
    if n_a == 2:
        a_specs.append(pl.BlockSpec((tm, tk), lambda i, k: (i, jnp.maximum(k - nk1, 0))))
    vec_spec = pl.BlockSpec((1, N), lambda i, k: (0, 0))
    any_spec = pl.BlockSpec(memory_space=pl.ANY)
    if w_lead is None:
        w_spec = pl.BlockSpec((tk, N), lambda i, k: (k, 0))
    else:
        w_spec = pl.BlockSpec((None, tk, N), lambda i, k: (w_lead, k, 0))
    in_specs = a_specs + [w_spec, any_spec, vec_spec]
    args = list(a_list) + [w, h, gain.reshape(1, N)]
    out_specs, out_shape = any_spec, jax.ShapeDtypeStruct((M, N), F32)
    scratch = [pltpu.VMEM((tm, N), F32), pltpu.VMEM((2, rows, N), F32), pltpu.VMEM((2, rows, N), F32)]
    sems = [pltpu.SemaphoreType.DMA((2,)), pltpu.SemaphoreType.DMA((2,))]
    if emit_u:
        in_specs.append(vec_spec)
        args.append(next_gain.reshape(1, N))
        out_specs = [any_spec, any_spec]
        out_shape = [out_shape, jax.ShapeDtypeStruct((M, N), BF16)]
        scratch.append(pltpu.VMEM((2, rows, N), BF16))
        sems.append(pltpu.SemaphoreType.DMA((2,)))
    out = pl.pallas_call(
        functools.partial(_mm_norm_res_kernel, nk=nk, nk1=nk1, n_a=n_a, emit_u=emit_u, tm=tm, rows=rows),
        grid=(M // tm, nk),
        in_specs=in_specs,
        out_specs=out_specs,
        out_shape=out_shape,
        scratch_shapes=scratch + sems,
        compiler_params=_cparams("parallel", "arbitrary"),
        name="out_proj_norm_res",
    )(*args)
    return out if emit_u else (out, None)


def _pool_kernel(cur_ref, prev_ref, w_ref, scale_ref, o_ref, ext_ref, *, ts, gd):
    i = pl.program_id(1)
    ext_ref[0:POOL_HALO, :] = jnp.where(i > 0, prev_ref[0], 0.0)
    ext_ref[POOL_HALO:POOL_HALO + ts, :] = cur_ref[0]
    t = i * ts + lax.broadcasted_iota(jnp.int32, (ts, 1), 0)
    for g, win in enumerate(POOL_WINDOWS):
        cols = slice(g * gd, (g + 1) * gd)
        tok = ext_ref[POOL_HALO:POOL_HALO + ts, cols]
        acc = tok
        for back in range(1, win):
            acc = acc + ext_ref[POOL_HALO - back:POOL_HALO - back + ts, cols]
        count = jnp.minimum(t + 1, win).astype(F32)
        pooled = acc / count - tok
        y = jnp.dot(pooled.astype(BF16), w_ref[g], preferred_element_type=F32)
        o_ref[0, :, cols] = (y * scale_ref[:, cols]).astype(o_ref.dtype)


def _pool_mixer(p_in, w_pool, scale, ts=512):
    B, S, PW = p_in.shape
    gd = PW // POOL_GROUPS
    hb = ts // POOL_HALO
    return pl.pallas_call(
        functools.partial(_pool_kernel, ts=ts, gd=gd),
        grid=(B, S // ts),
        in_specs=[
            pl.BlockSpec((1, ts, PW), lambda b, i: (b, i, 0)),
            pl.BlockSpec((1, POOL_HALO, PW), lambda b, i: (b, jnp.maximum(i * hb - 1, 0), 0)),
            pl.BlockSpec((POOL_GROUPS, gd, gd), lambda b, i: (0, 0, 0)),
            pl.BlockSpec((1, PW), lambda b, i: (0, 0)),
        ],
        out_specs=pl.BlockSpec((1, ts, PW), lambda b, i: (b, i, 0)),
        out_shape=jax.ShapeDtypeStruct((B, S, PW), BF16),
        scratch_shapes=[pltpu.VMEM((POOL_HALO + ts, PW), F32)],
        compiler_params=_cparams("parallel", "arbitrary"),
        name="pool_mixer",
    )(p_in, p_in, w_pool.astype(BF16), scale.reshape(1, PW))


def _gelu_tanh(x):
    c = math.sqrt(2.0 / math.pi)
    return x * (0.5 * (1.0 + jnp.tanh(c * (x + 0.044715 * (x * x * x)))))


def _compress_kernel(x_ref, plo_ref, phi_ref, w1a_ref, w1b_ref, w2_ref, o_ref, tmp_ref, *, nch):
    x = x_ref[0, 0]
    a = jnp.dot((x + plo_ref[...]).astype(BF16), w1a_ref[0], preferred_element_type=F32)
    b = jnp.dot((x + phi_ref[...]).astype(BF16), w1b_ref[0], preferred_element_type=F32)
    tmp_ref[0:nch, :] = b
    tmp_ref[nch:nch + 8, :] = jnp.zeros((8, b.shape[1]), F32)
    pre = a + tmp_ref[1:nch + 1, :]
    out = jnp.dot(_gelu_tanh(pre).astype(BF16), w2_ref[0], preferred_element_type=F32)
    row = lax.broadcasted_iota(jnp.int32, out.shape, 0)
    o_ref[0, 0] = jnp.where(row < nch - 1, out, 0.0).astype(o_ref.dtype)


def _compress(x, pos, w1, w2):
    two, BG, nch, cw = x.shape
    hid = w1.shape[-1]
    dh = w2.shape[-1]
    half = CMP_BLOCK // 2
    plo = pos[:half].reshape(1, cw)
    phi = pos[half:].reshape(1, cw)
    w1 = w1.astype(BF16)
    return pl.pallas_call(
        functools.partial(_compress_kernel, nch=nch),
        grid=(two, BG),
        in_specs=[
            pl.BlockSpec((1, 1, nch, cw), lambda s, b: (s, b, 0, 0)),
            pl.BlockSpec((1, cw), lambda s, b: (0, 0)),
            pl.BlockSpec((1, cw), lambda s, b: (0, 0)),
            pl.BlockSpec((1, cw, hid), lambda s, b: (s, 0, 0)),
            pl.BlockSpec((1, cw, hid), lambda s, b: (s, 1, 0)),
            pl.BlockSpec((1, hid, dh), lambda s, b: (s, 0, 0)),
        ],
        out_specs=pl.BlockSpec((1, 1, nch, dh), lambda s, b: (s, b, 0, 0)),
        out_shape=jax.ShapeDtypeStruct((two, BG, nch, dh), BF16),
        scratch_shapes=[pltpu.VMEM((nch + 8, hid), F32)],
        compiler_params=_cparams("parallel", "arbitrary"),
        name="nsa_compress",
    )(x, plo, phi, w1, w1, w2.astype(BF16))


def _nt_dot(a, b):
    return lax.dot_general(a, b, (((1,), (1,)), ((), ())), preferred_element_type=F32)


def _split_dot(w, x):
    hi = x.astype(BF16)
    r1 = x - hi.astype(F32)
    mid = r1.astype(BF16)
    lo = (r1 - mid.astype(F32)).astype(BF16)
    return (jnp.dot(w, hi, preferred_element_type=F32) + jnp.dot(w, mid, preferred_element_type=F32)
            + jnp.dot(w, lo, preferred_element_type=F32))


def _nsa_kernel(q_ref, gl_ref, kc_ref, vct_ref, ks_ref, vst_ref, kw_ref, vwt_ref, ovt_ref, et_ref, o_ref,
                m_ref, l_ref, acc_ref, s_ref, bias_ref, *, hg, n_cmp, n_slc, top_n):
    QT, DH = NSA_Q_TILE, NSA_HEAD_DIM
    t0 = pl.program_id(1) * QT
    q = jnp.concatenate([q_ref[:, h * DH:(h + 1) * DH] for h in range(hg)], axis=0)
    lane = lax.broadcasted_iota(jnp.int32, (1, QT), 1)
    t = t0 + lane

    def capped(s, cap):
        return jnp.concatenate([jnp.minimum(s[:, h * QT:(h + 1) * QT], cap) for h in range(hg)], axis=1)

    def softmax_cols(s):
        p = jnp.exp2(s - jnp.max(s, axis=0, keepdims=True))
        return p, jnp.sum(p, axis=0, keepdims=True)

    ncp = kc_ref.shape[2]
    nrow = lax.broadcasted_iota(jnp.int32, (ncp, QT), 0)
    valid_c = (nrow * CMP_STRIDE + (CMP_BLOCK - 1) <= t) & (nrow < n_cmp)
    s_c = capped(_nt_dot(kc_ref[0, 0], q), jnp.where(valid_c, BIG, NEG))
    ws = pl.multiple_of(jnp.maximum(t0 + QT - SWA_SPAN, 0), LANES)
    wb = ws // LANES
    dist = t - (ws + lax.broadcasted_iota(jnp.int32, (SWA_SPAN, QT), 0))
    valid_w = (dist >= 0) & (dist < SWA_WINDOW)
    s_w = capped(_nt_dot(kw_ref[0, pl.ds(ws, SWA_SPAN), :], q), jnp.where(valid_w, BIG, NEG))

    p_c, l_c = softmax_cols(s_c)
    seen = jnp.concatenate([jnp.where(t >= CMP_BLOCK - 1, 1.0, 0.0)] * hg, axis=1)
    p_c = p_c * (seen / l_c)
    o_c = jnp.dot(vct_ref[0], p_c.astype(BF16), preferred_element_type=F32)

    p_w, l_w = softmax_cols(s_w)
    vwin = jnp.concatenate([vwt_ref[0, wb + j] for j in range(SWA_SPAN // LANES)], axis=1)
    o_w = jnp.dot(vwin, p_w.astype(BF16), preferred_element_type=F32) / l_w

    p_sum = p_c[:, 0:QT]
    for h in range(1, hg):
        p_sum = p_sum + p_c[:, h * QT:(h + 1) * QT]
    imp = _split_dot(ovt_ref[...], p_sum)
    blk = lax.broadcasted_iota(jnp.int32, (LANES, QT), 0)
    cur = t0 // SLC_BLOCK + jnp.zeros((1, QT), jnp.int32)
    for k in range(1, QT // SLC_BLOCK):
        cur = cur + jnp.where(lane >= k * SLC_BLOCK, 1, 0)
    forced = (blk == 0) | (blk == cur) | (blk == cur - 1)
    val = jnp.where(forced, imp + FORCE_BONUS, jnp.where(blk > cur, -FORCE_BONUS, imp))
    if n_slc < LANES:
        val = jnp.where(blk < n_slc, val, -jnp.inf)

    def rank_bias(nb):
        nslab = nb // 8
        slabs = [val[8 * r:8 * r + 8, :] for r in range(nslab)]
        ranks = [jnp.zeros((8, QT), F32) for _ in range(nslab)]
        sub = lax.broadcasted_iota(jnp.int32, (8, QT), 0)
        for other in range(nb):
            c = val[other:other + 1, :]
            for r in range(nslab):
                if 8 * r > other:
                    beat = c >= slabs[r]
                elif 8 * r + 7 < other:
                    beat = c > slabs[r]
                else:
                    beat = (c > slabs[r]) | ((c == slabs[r]) & (sub + 8 * r > other))
                ranks[r] = ranks[r] + jnp.where(beat, 1.0, 0.0)
        for r in range(nslab):
            chosen = (ranks[r] < top_n) & (blk[8 * r:8 * r + 8] <= cur)
            bias_ref[8 * r:8 * r + 8, :] = jnp.where(chosen, 0.0, NEG)
        if nb < LANES:
            bias_ref[nb:LANES, :] = jnp.full((LANES - nb, QT), NEG, F32)

    live = (t0 + QT - 1) // SLC_BLOCK + 1
    sizes = sorted({min(_round_up(-(-n_slc * k // RANK_SIZES), 8), LANES) for k in range(1, RANK_SIZES + 1)})
    for lo, nb in zip([0] + sizes[:-1], sizes):
        @pl.when((live > lo) & (live <= nb))
        def _(nb=nb):
            rank_bias(nb)

    bias_q = bias_ref[...].T.astype(BF16)

    q_sel = jnp.concatenate([q, jnp.concatenate([bias_q] * hg, axis=0)], axis=1)
    m_ref[...] = jnp.full(m_ref.shape, NEG, F32)
    l_ref[...] = jnp.zeros(l_ref.shape, F32)
    acc_ref[...] = jnp.zeros(acc_ref.shape, F32)

    def sel_tiles(tiles):
        for slot, (kb, _) in enumerate(tiles):
            start = pl.multiple_of(kb * SEL_TILE, SEL_TILE)
            keys = jnp.concatenate([ks_ref[0, pl.ds(start, SEL_TILE), :], et_ref[kb]], axis=1)
            s_ref[slot] = _nt_dot(keys, q_sel)
        for slot, (kb, diagonal) in enumerate(tiles):
            s = s_ref[slot]
            if diagonal:
                key = kb * SEL_TILE + lax.broadcasted_iota(jnp.int32, (SEL_TILE, QT), 0)
                s = capped(s, jnp.where(key <= t, BIG, NEG))
            m_prev = m_ref[...]
            m_new = jnp.maximum(m_prev, jnp.max(s, axis=0, keepdims=True))
            alpha = jnp.exp2(m_prev - m_new)
            p = jnp.exp2(s - m_new)
            l_ref[...] = alpha * l_ref[...] + jnp.sum(p, axis=0, keepdims=True)
            acc_ref[...] = alpha * acc_ref[...] + jnp.dot(vst_ref[0, kb], p.astype(BF16),
                                                          preferred_element_type=F32)
            m_ref[...] = m_new

    def past_group(i, carry):
        sel_tiles([(SEL_GROUP * i + slot, False) for slot in range(SEL_GROUP)])
        return carry

    diag = t0 // SEL_TILE
    groups = diag // SEL_GROUP
    lax.fori_loop(0, groups, past_group, 0)
    for rem in range(SEL_GROUP):
        @pl.when(diag - groups * SEL_GROUP == rem)
        def _(rem=rem):
            sel_tiles([(diag - rem + slot, slot == rem) for slot in range(rem + 1)])

    o_s = acc_ref[...] / l_ref[...]

    gate = jax.nn.sigmoid(gl_ref[0, 0])
    for h in range(hg):
        cols = slice(h * QT, (h + 1) * QT)
        r = N_BRANCH * h
        out = gate[r:r + 1] * o_c[:, cols] + gate[r + 1:r + 2] * o_s[:, cols] + gate[r + 2:r + 3] * o_w[:, cols]
        o_ref[:, h * DH:(h + 1) * DH] = out.T.astype(o_ref.dtype)


def _nsa(q, gates, kvc, kv, B, S):
    T, HW = q.shape
    G, DH, QT = NSA_KV_GROUPS, NSA_HEAD_DIM, NSA_Q_TILE
    hg = HW // DH // G
    n_cmp = (S - CMP_BLOCK) // CMP_STRIDE + 1
    ncp = kvc.shape[2]
    n_slc = S // SLC_BLOCK
    top_n = min(SLC_TOPN, n_slc)
    nqt = S // QT
    kvw = G * DH
    assert n_slc <= LANES and S % SEL_TILE == 0 and S >= SWA_SPAN and QT == LANES

    cs = jnp.arange(ncp)[None, :] * CMP_STRIDE
    ss = jnp.arange(LANES)[:, None] * SLC_BLOCK
    overlap = jnp.clip(jnp.minimum(cs + CMP_BLOCK, ss + SLC_BLOCK) - jnp.maximum(cs, ss), 0) // CMP_STRIDE
    overlap = jnp.where((jnp.arange(ncp)[None, :] < n_cmp) & (jnp.arange(LANES)[:, None] < n_slc), overlap, 0)
    overlap = overlap.astype(BF16)
    key_blk = (jnp.arange(S) // SLC_BLOCK).reshape(S // SEL_TILE, SEL_TILE, 1)
    expand = (key_blk == jnp.arange(LANES)[None, None, :]).astype(BF16)

    gr = _round_up(N_BRANCH * hg, 8)
    gl = gates[:, :G * hg * N_BRANCH].reshape(B, nqt, QT, G, hg * N_BRANCH)
    gl = jnp.pad(gl, ((0, 0),) * 4 + ((0, gr - hg * N_BRANCH),)).transpose(0, 3, 1, 4, 2).reshape(B * G, nqt, gr, QT)

    def v_tiles(which, tile):
        v = kv[:, :, which * kvw:(which + 1) * kvw].reshape(B, S // tile, tile, G, DH)
        return v.transpose(0, 3, 1, 4, 2).reshape(B * G, S // tile, DH, tile)

    def k_spec(which):
        return pl.BlockSpec((1, S, DH), lambda bg, i: (bg // G, 0, which * G + bg % G))

    return pl.pallas_call(
        functools.partial(_nsa_kernel, hg=hg, n_cmp=n_cmp, n_slc=n_slc, top_n=top_n),
        grid=(B * G, nqt),
        in_specs=[
            pl.BlockSpec((QT, hg * DH), lambda bg, i: ((bg // G) * nqt + i, bg % G)),
            pl.BlockSpec((1, 1, gr, QT), lambda bg, i: (bg, i, 0, 0)),
            pl.BlockSpec((1, 1, ncp, DH), lambda bg, i: (0, bg, 0, 0)),
            pl.BlockSpec((1, DH, ncp), lambda bg, i: (bg, 0, 0)),
            k_spec(0),
            pl.BlockSpec((1, S // SEL_TILE, DH, SEL_TILE), lambda bg, i: (bg, 0, 0, 0)),
            k_spec(2),
            pl.BlockSpec((1, S // LANES, DH, LANES), lambda bg, i: (bg, 0, 0, 0)),
            pl.BlockSpec((LANES, ncp), lambda bg, i: (0, 0)),
            pl.BlockSpec((S // SEL_TILE, SEL_TILE, LANES), lambda bg, i: (0, 0, 0)),
        ],
        out_specs=pl.BlockSpec((QT, hg * DH), lambda bg, i: ((bg // G) * nqt + i, bg % G)),
        out_shape=jax.ShapeDtypeStruct((T, HW), BF16),
        scratch_shapes=[pltpu.VMEM((1, hg * QT), F32), pltpu.VMEM((1, hg * QT), F32),
                        pltpu.VMEM((DH, hg * QT), F32), pltpu.VMEM((SEL_GROUP, SEL_TILE, hg * QT), F32),
                        pltpu.VMEM((LANES, QT), F32)],
        compiler_params=_cparams("parallel", "arbitrary"),
        name="nsa_attention",
    )(q, gl, kvc, jnp.swapaxes(kvc[1], 1, 2), kv, v_tiles(1, SEL_TILE), kv, v_tiles(3, LANES), overlap, expand)


def _mlstm_kernel(q_ref, kt_ref, v_ref, o_ref, gr_ref, gc_ref, hn_ref, y_ref, c_ref, n_ref, m_ref, *, L):
    ci = pl.program_id(1)

    @pl.when(ci == 0)
    def _():
        c_ref[...] = jnp.zeros_like(c_ref)
        n_ref[...] = jnp.zeros_like(n_ref)
        m_ref[...] = jnp.zeros_like(m_ref)

    q = q_ref[...]
    kt = kt_ref[0]
    v = v_ref[...]
    ig_row = gr_ref[0, 0, 0:1, :]
    lf_row = jax.nn.log_sigmoid(gr_ref[0, 0, 1:2, :])
    lf_col = jax.nn.log_sigmoid(gc_ref[0, 0, :, 1:2])
    m_prev = m_ref[0:1, 0:1]

    ti = lax.broadcasted_iota(jnp.int32, (L, L), 0)
    si = lax.broadcasted_iota(jnp.int32, (L, L), 1)
    causal = si <= ti
    b_col = jnp.sum(jnp.where(causal, lf_row, 0.0), axis=1, keepdims=True)
    b_row = jnp.sum(jnp.where(ti <= si, lf_col, 0.0), axis=0, keepdims=True)
    b_last = b_col[L - 1:L, :]

    dmat = jnp.where(causal, b_col - b_row + ig_row, -jnp.inf)
    a_col = b_col + m_prev
    m_t = jnp.maximum(a_col, jnp.max(dmat, axis=1, keepdims=True))
    wq = jnp.dot(q, kt, preferred_element_type=F32) * jnp.exp(dmat - m_t)
    inter = jnp.exp(a_col - m_t)
    q_c = jnp.dot(q, c_ref[...].astype(BF16), preferred_element_type=F32)
    q_n = jnp.dot(q, n_ref[...].astype(BF16), preferred_element_type=F32)[:, 0:1]
    num = inter * q_c + jnp.dot(wq.astype(BF16), v, preferred_element_type=F32)
    den = inter * q_n + jnp.sum(wq, axis=1, keepdims=True)
    h = num / jnp.maximum(jnp.abs(den), jnp.exp(-m_t))
    h = h * lax.rsqrt(jnp.mean(h * h, axis=-1, keepdims=True) + RMS_EPS) * hn_ref[0]
    y_ref[...] = (jax.nn.sigmoid(o_ref[...]) * h).astype(y_ref.dtype)

    g_row = b_last - b_row + ig_row
    m_new = jnp.maximum(b_last + m_prev, jnp.max(g_row, axis=1, keepdims=True))
    decay = jnp.exp(b_last + m_prev - m_new)
    kw_t = kt.astype(F32) * jnp.exp(g_row - m_new)
    c_ref[...] = decay * c_ref[...] + jnp.dot(kw_t.astype(BF16), v, preferred_element_type=F32)
    n_ref[...] = decay * n_ref[...] + jnp.sum(kw_t, axis=1, keepdims=True)
    m_ref[...] = jnp.broadcast_to(m_new, m_ref.shape)


def _mlstm(q, kt, v, o, gates, head_norm, B, S):
    T = q.shape[0]
    H = MLSTM_HEADS
    dk = q.shape[1] // H
    dv = v.shape[1] // H
    L = min(MLSTM_CHUNK, S)
    nc = S // L
    g = gates[:, :2 * H].reshape(B, nc, L, 2, H)
    g_row = jnp.transpose(g, (0, 4, 1, 3, 2)).reshape(B * H, nc, 2, L)
    g_col = jnp.transpose(g, (0, 4, 1, 2, 3)).reshape(B * H, nc, L, 2)
    return pl.pallas_call(
        functools.partial(_mlstm_kernel, L=L),
        grid=(B * H, nc),
        in_specs=[
            pl.BlockSpec((L, dk), lambda bh, c: ((bh // H) * nc + c, bh % H)),
            pl.BlockSpec((1, dk, L), lambda bh, c: (bh // H, bh % H, c)),
            pl.BlockSpec((L, dv), lambda bh, c: ((bh // H) * nc + c, bh % H)),
            pl.BlockSpec((L, dv), lambda bh, c: ((bh // H) * nc + c, bh % H)),
            pl.BlockSpec((1, 1, 2, L), lambda bh, c: (bh, c, 0, 0)),
            pl.BlockSpec((1, 1, L, 2), lambda bh, c: (bh, c, 0, 0)),
            pl.BlockSpec((1, 1, dv), lambda bh, c: (bh % H, 0, 0)),
        ],
        out_specs=pl.BlockSpec((L, dv), lambda bh, c: ((bh // H) * nc + c, bh % H)),
        out_shape=jax.ShapeDtypeStruct((T, H * dv), BF16),
        scratch_shapes=[pltpu.VMEM((dk, dv), F32), pltpu.VMEM((dk, LANES), F32), pltpu.VMEM((8, LANES), F32)],
        compiler_params=_cparams("parallel", "arbitrary"),
        name="mlstm",
    )(q, kt, v, o, g_row, g_col, head_norm.reshape(H, 1, dv))


def _ffn_in_kernel(u_ref, halo_ref, wg_ref, wu_ref, cw_ref, cb_ref, o_ref, lhs_ref, g_ref, tail_ref, *, tm, tn, ff,
                   sh, over, tiles_per_seq):
    i = pl.program_id(0)
    j = pl.program_id(1)

    @pl.when(j == 0)
    def _():
        halo = halo_ref[...]
        lhs_ref[0:CONV_HALO, :] = jnp.where(i % tiles_per_seq != 0, halo, jnp.zeros_like(halo))
        lhs_ref[CONV_HALO:CONV_HALO + tm, :] = u_ref[...]
        tail_ref[...] = jnp.zeros_like(tail_ref)

    g_ref[...] = jnp.dot(lhs_ref[...], wg_ref[...], preferred_element_type=F32)
    up = jnp.dot(u_ref[...], wu_ref[0], preferred_element_type=F32)
    if over:
        moved = jnp.concatenate([up[:, over:], up[:, :over]], axis=1)
        up = jnp.where(j == pl.num_programs(1) - 1, moved, up)
    gate = cb_ref[...] + cw_ref[0:1, :] * g_ref[CONV_HALO:CONV_HALO + tm, :]
    for back in range(1, CONV_WIDTH):
        gate = gate + cw_ref[back:back + 1, :] * g_ref[CONV_HALO - back:CONV_HALO - back + tm, :]
    if sh:
        lane = lax.broadcasted_iota(jnp.int32, (1, LANES), 1)
        tiles = [tail_ref[...]] + [gate[:, c * LANES:(c + 1) * LANES] for c in range(tn // LANES)]
        tail_ref[...] = tiles[-1]
        rolled = [pltpu.roll(x, sh, axis=1) for x in tiles]
        gate = jnp.concatenate([jnp.where(lane < sh, rolled[c], rolled[c + 1]) for c in range(tn // LANES)], axis=1)
    act = gate * jax.nn.sigmoid(gate) * up
    hidden = j * tn - sh + lax.broadcasted_iota(jnp.int32, (1, tn), 1)
    o_ref[...] = jnp.where((hidden >= 0) & (hidden < ff), act, 0.0).astype(o_ref.dtype)


def _ffn_in(u, w_in, conv_w, conv_b, layer, ff, sh, fp, S, tm=1024, tn=FFN_COL_TILE):
    T, D = u.shape
    tm = min(tm, S)
    assert T % tm == 0 and S % tm == 0 and fp % tn == 0 and tn % LANES == 0
    hb = tm // CONV_HALO
    up0 = (ff - sh) // LANES
    up_last = (w_in.shape[-1] - tn) // LANES
    over = max(up0 + (fp // tn - 1) * (tn // LANES) - up_last, 0) * LANES
    assert w_in.shape[-1] % LANES == 0 and over < tn
    return pl.pallas_call(
        functools.partial(_ffn_in_kernel, tm=tm, tn=tn, ff=ff, sh=sh, over=over, tiles_per_seq=S // tm),
        grid=(T // tm, fp // tn),
        in_specs=[
            pl.BlockSpec((tm, D), lambda i, j: (i, 0)),
            pl.BlockSpec((CONV_HALO, D), lambda i, j: (jnp.maximum(i * hb - 1, 0), 0)),
            pl.BlockSpec((None, D, tn), lambda i, j: (layer, 0, j)),
            pl.BlockSpec((pl.Element(1), pl.Element(D), pl.Element(tn)),
                         lambda i, j: (layer, 0, jnp.minimum(up0 + j * (tn // LANES), up_last) * LANES)),
            pl.BlockSpec((None, CONV_WIDTH, tn), lambda i, j: (layer, 0, j)),
            pl.BlockSpec((None, 1, tn), lambda i, j: (layer, 0, j)),
        ],
        out_specs=pl.BlockSpec((tm, tn), lambda i, j: (i, j)),
        out_shape=jax.ShapeDtypeStruct((T, fp), BF16),
        scratch_shapes=[pltpu.VMEM((CONV_HALO + tm, D), BF16), pltpu.VMEM((CONV_HALO + tm, tn), F32),
                        pltpu.VMEM((tm, LANES), F32)],
        compiler_params=_cparams("parallel", "arbitrary"),
        name="ffn_in_conv_act",
    )(u, u, w_in, w_in, conv_w, conv_b)


def _ffn_weights(w_in, w_out):
    depth, D, ff2 = w_in.shape
    ff = ff2 // 2
    sh = ff % LANES
    fp = _round_up(ff + sh, FFN_COL_TILE)
    w_in = w_in.astype(BF16)
    w_out = jnp.pad(w_out, ((0, 0), (sh, fp - ff - sh), (0, 0))).astype(BF16)
    return w_in, w_out, ff, sh, fp


def _conv_ffn(h, u, ln_post, next_gain, w_in, w_out, ff, sh, fp, conv_w, conv_b, layer, S):
    act = _ffn_in(u, w_in, conv_w, conv_b, layer, ff, sh, fp, S)
    return _mm_norm_res([act], w_out, h, ln_post, next_gain, w_lead=layer)


def _ab_layer(h, u, ln_post, next_gain, w_in, pool_w, pool_scale, cmp_pos, ck_w1, ck_w2, cv_w1, cv_w2, w_out, B, S):
    T, D = h.shape
    G, DH = NSA_KV_GROUPS, NSA_HEAD_DIM
    pw = D // 4
    hw = D - pw
    kvw = G * DH
    w = w_in.astype(BF16)
    c0 = pw + hw
    p_in = _mm(u, w, F32, 0, pw)
    q = _mm(u, w, BF16, pw, hw, scale=DH ** -0.5 * LOG2E)
    kv_cmp = _mm(u, w, F32, c0, 2 * kvw)
    kv = _mm(u, w, BF16, c0 + 2 * kvw, 4 * kvw)
    gates = _mm(u, w, F32, c0 + 6 * kvw, LANES)

    y_a = _pool_mixer(p_in.reshape(B, S, pw), pool_w, pool_scale).reshape(T, pw)
    chunks = kv_cmp.reshape(B, S, 2, G, DH).transpose(2, 0, 3, 1, 4)
    chunks = chunks.reshape(2, B * G, S // CMP_STRIDE, CMP_STRIDE * DH)
    kvc = _compress(chunks, cmp_pos, jnp.stack([ck_w1, cv_w1]), jnp.stack([ck_w2, cv_w2]))
    y_b = _nsa(q, gates, kvc, kv.reshape(B, S, 4 * kvw), B, S)
    return _mm_norm_res([y_a, y_b], w_out.astype(BF16), h, ln_post, next_gain)


def _c_layer(h, u, ln_post, next_gain, w_in, b_if, head_norm, w_out, B, S):
    T, D = h.shape
    H = MLSTM_HEADS
    dv = D // H
    dk = dv // 2
    qk = H * dk
    w = w_in.astype(BF16)
    q = _mm(u, w, BF16, 0, qk, scale=dk ** -0.5)
    k = _mm(u, w, BF16, qk, qk)
    v = _mm(u, w, BF16, 2 * qk, D)
    o = _mm(u, w, F32, 2 * qk + D, D)
    gates = _mm(u, w, F32, 2 * qk + 2 * D, LANES, bias=_pad_cols(b_if.reshape(1, 2 * H), LANES))
    kt = k.reshape(B, S, qk).transpose(0, 2, 1)
    y = _mlstm(q, kt, v, o, gates, head_norm, B, S)
    return _mm_norm_res([y], w_out.astype(BF16), h, ln_post, next_gain)


def kernel(x, ln_pre, ln_post, w_in_ab, pool_w, pool_scale, cmp_pos, cmp_k_w1, cmp_k_w2, cmp_v_w1, cmp_v_w2,
           w_out_ab, w_in_c, b_if_c, head_norm_c, w_out_c, ffn_ln_pre, ffn_ln_post, ffn_w_in, ffn_conv_w,
           ffn_conv_b, ffn_w_out):
    B, S, D = x.shape
    depth = ln_pre.shape[0]
    h = x.reshape(B * S, D)
    u = _rmsnorm(h, ln_pre[0])
    ffn_wi, ffn_wo, ff, sh, fp = _ffn_weights(ffn_w_in, ffn_w_out)
    ffn_cb = ffn_conv_b.reshape(depth, 1, -1)
    for layer in range(depth):
        i = layer // 2
        if layer % 2 == 0:
            h, u = _ab_layer(h, u, ln_post[layer], ffn_ln_pre[layer], w_in_ab[i], pool_w[i], pool_scale[i],
                             cmp_pos[i], cmp_k_w1[i], cmp_k_w2[i], cmp_v_w1[i], cmp_v_w2[i], w_out_ab[i], B, S)
        else:
            h, u = _c_layer(h, u, ln_post[layer], ffn_ln_pre[layer], w_in_c[i], b_if_c[i], head_norm_c[i],
                            w_out_c[i], B, S)
        next_gain = ln_pre[layer + 1] if layer + 1 < depth else None
        h, u = _conv_ffn(h, u, ffn_ln_post[layer], next_gain, ffn_wi, ffn_wo, ff, sh, fp, ffn_conv_w, ffn_cb, layer,
                         S)
    return h.reshape(B, S, D)
```

```python
import functools
import math

import jax
import jax.numpy as jnp
from jax import lax
from jax.experimental import pallas as pl
from jax.experimental.pallas import tpu as pltpu

F32 = jnp.float32
BF16 = jnp.bfloat16

RMS_EPS = 1e-6
POOL_GROUPS = 4
POOL_WINDOWS = (2, 4, 8, 16)
POOL_HALO = 16
NSA_HEAD_DIM = 128
NSA_KV_GROUPS = 4
N_BRANCH = 3
CMP_BLOCK = 32
CMP_STRIDE = 16
SLC_BLOCK = 64
SLC_TOPN = 16
SWA_WINDOW = 512
FORCE_BONUS = 1e4
NEG = -1e30
BIG = 1e30
MLSTM_HEADS = 8
CONV_WIDTH = 3
CONV_HALO = 16

LANES = 128
LOG2E = math.log2(math.e)
NSA_Q_TILE = LANES
SEL_TILE = 512
SEL_GROUP = 4
RANK_SIZES = 4
SWA_SPAN = SWA_WINDOW + NSA_Q_TILE
MLSTM_CHUNK = 256
MLSTM_HEADS_PER_STEP = 2
FFN_COL_TILE = 512
VMEM_LIMIT = 56 * 1024 * 1024


def _cparams(*sem):
    return pltpu.CompilerParams(dimension_semantics=sem, vmem_limit_bytes=VMEM_LIMIT)


def _round_up(n, m):
    return (n + m - 1) // m * m


def _pad_cols(w, n):
    return jnp.pad(w, ((0, 0), (0, n - w.shape[1])))


def _rmsnorm_kernel(x_ref, g_ref, o_ref):
    x = x_ref[...]
    ms = jnp.mean(x * x, axis=-1, keepdims=True)
    o_ref[...] = (x * lax.rsqrt(ms + RMS_EPS) * g_ref[...]).astype(o_ref.dtype)


def _rmsnorm(x, gain, tm=512):
    T, D = x.shape
    return pl.pallas_call(
        _rmsnorm_kernel,
        grid=(T // tm,),
        in_specs=[pl.BlockSpec((tm, D), lambda i: (i, 0)), pl.BlockSpec((1, D), lambda i: (0, 0))],
        out_specs=pl.BlockSpec((tm, D), lambda i: (i, 0)),
        out_shape=jax.ShapeDtypeStruct((T, D), BF16),
        compiler_params=_cparams("parallel"),
        name="rmsnorm",
    )(x, gain.reshape(1, D))


def _mm_kernel(a_ref, b_ref, bias_ref, o_ref, *, scale, live_cols):
    acc = jnp.dot(a_ref[...], b_ref[...], preferred_element_type=F32)
    out = (acc + bias_ref[...]) * scale
    if live_cols is not None:
        col = pl.program_id(1) * out.shape[1] + lax.broadcasted_iota(jnp.int32, (1, out.shape[1]), 1)
        out = jnp.where(col < live_cols, out, 0.0)
    o_ref[...] = out.astype(o_ref.dtype)


def _mm(a, w, out_dtype, col0, ncols, *, bias=None, scale=1.0, tm=1024, tn=512):
    M, K = a.shape
    tm = min(tm, M)
    tn = min(tn, ncols)
    assert M % tm == 0 and ncols % tn == 0 and col0 % tn == 0
    jb = col0 // tn
    live_cols = w.shape[1] - col0 if col0 + ncols > w.shape[1] else None
    if bias is None:
        bias = jnp.zeros((1, ncols), F32)
    return pl.pallas_call(
        functools.partial(_mm_kernel, scale=scale, live_cols=live_cols),
        grid=(M // tm, ncols // tn),
        in_specs=[
            pl.BlockSpec((tm, K), lambda i, j: (i, 0)),
            pl.BlockSpec((K, tn), lambda i, j: (0, j + jb)),
            pl.BlockSpec((1, tn), lambda i, j: (0, j)),
        ],
        out_specs=pl.BlockSpec((tm, tn), lambda i, j: (i, j)),
        out_shape=jax.ShapeDtypeStruct((M, ncols), out_dtype),
        compiler_params=_cparams("parallel", "arbitrary"),
        name="proj",
    )(a, w, bias)


def _mm_norm_res_kernel(*refs, nk, nk1, n_a, emit_u, tm, rows):
    a_refs = refs[:n_a]
    w_ref, h_hbm, g_ref = refs[n_a:n_a + 3]
    rest = refs[n_a + 3:]
    if emit_u:
        ng_ref, o_ref, u_ref, h_buf, h_sem = rest
    else:
        o_ref, h_buf, h_sem = rest
    i = pl.program_id(0)
    k = pl.program_id(1)

    def h_copy():
        return pltpu.make_async_copy(h_hbm.at[pl.ds(i * tm, tm), :], h_buf, h_sem)

    def product(a_ref):
        return jnp.dot(a_ref[...], w_ref[...], preferred_element_type=F32)

    @pl.when(k == 0)
    def _():
        h_copy().start()
        o_ref[...] = product(a_refs[0])

    if nk1 > 1 or n_a == 1:
        @pl.when((k > 0) & (k < nk1))
        def _():
            o_ref[...] += product(a_refs[0])

    if n_a == 2:
        @pl.when(k >= nk1)
        def _():
            o_ref[...] += product(a_refs[1])

    @pl.when(k == nk - 1)
    def _():
        h_copy().wait()

        def norm_rows(c, carry):
            r = pl.ds(pl.multiple_of(c * rows, rows), rows)
            y = o_ref[r, :]
            ms = jnp.mean(y * y, axis=-1, keepdims=True)
            h_new = h_buf[r, :] + y * lax.rsqrt(ms + RMS_EPS) * g_ref[...]
            o_ref[r, :] = h_new
            if emit_u:
                ms2 = jnp.mean(h_new * h_new, axis=-1, keepdims=True)
                u_ref[r, :] = (h_new * lax.rsqrt(ms2 + RMS_EPS) * ng_ref[...]).astype(u_ref.dtype)
            return carry

        lax.fori_loop(0, tm // rows, norm_rows, 0)


def _mm_norm_res(a_list, w, h, gain, next_gain=None, w_lead=None, tm=512, tk=1024, rows=64):
    M = h.shape[0]
    N = w.shape[-1]
    n_a = len(a_list)
    tk = math.gcd(tk, *[a.shape[1] for a in a_list])
    assert n_a in (1, 2) and M % tm == 0 and tm % rows == 0
    nk1 = a_list[0].shape[1] // tk
    nk = sum(a.shape[1] for a in a_list) // tk
    emit_u = next_gain is not None
    a_specs = [pl.BlockSpec((tm, tk), lambda i, k: (i, jnp.minimum(k, nk1 - 1)))]
    if n_a == 2:
        a_specs.append(pl.BlockSpec((tm, tk), lambda i, k: (i, jnp.maximum(k - nk1, 0))))
    row_spec = pl.BlockSpec((tm, N), lambda i, k: (i, 0))
    vec_spec = pl.BlockSpec((1, N), lambda i, k: (0, 0))
    if w_lead is None:
        w_spec = pl.BlockSpec((tk, N), lambda i, k: (k, 0))
    else:
        w_spec = pl.BlockSpec((None, tk, N), lambda i, k: (w_lead, k, 0))
    in_specs = a_specs + [w_spec, pl.BlockSpec(memory_space=pl.ANY), vec_spec]
    args = list(a_list) + [w, h, gain.reshape(1, N)]
    out_specs, out_shape = row_spec, jax.ShapeDtypeStruct((M, N), F32)
    if emit_u:
        in_specs.append(vec_spec)
        args.append(next_gain.reshape(1, N))
        out_specs = [row_spec, row_spec]
        out_shape = [out_shape, jax.ShapeDtypeStruct((M, N), BF16)]
    out = pl.pallas_call(
        functools.partial(_mm_norm_res_kernel, nk=nk, nk1=nk1, n_a=n_a, emit_u=emit_u, tm=tm, rows=rows),
        grid=(M // tm, nk),
        in_specs=in_specs,
        out_specs=out_specs,
        out_shape=out_shape,
        scratch_shapes=[pltpu.VMEM((tm, N), F32), pltpu.SemaphoreType.DMA(())],
        compiler_params=_cparams("parallel", "arbitrary"),
        name="out_proj_norm_res",
    )(*args)
    return out if emit_u else (out, None)


def _pool_kernel(cur_ref, prev_ref, w_ref, scale_ref, o_ref, ext_ref, *, ts, gd):
    i = pl.program_id(1)
    ext_ref[0:POOL_HALO, :] = jnp.where(i > 0, prev_ref[0], 0.0)
    ext_ref[POOL_HALO:POOL_HALO + ts, :] = cur_ref[0]
    t = i * ts + lax.broadcasted_iota(jnp.int32, (ts, 1), 0)
    for g, win in enumerate(POOL_WINDOWS):
        cols = slice(g * gd, (g + 1) * gd)
        tok = ext_ref[POOL_HALO:POOL_HALO + ts, cols]
        acc = tok
        for back in range(1, win):
            acc = acc + ext_ref[POOL_HALO - back:POOL_HALO - back + ts, cols]
        count = jnp.minimum(t + 1, win).astype(F32)
        pooled = acc / count - tok
        y = jnp.dot(pooled.astype(BF16), w_ref[g], preferred_element_type=F32)
        o_ref[0, :, cols] = (y * scale_ref[:, cols]).astype(o_ref.dtype)


def _pool_mixer(p_in, w_pool, scale, ts=512):
    B, S, PW = p_in.shape
    gd = PW // POOL_GROUPS
    hb = ts // POOL_HALO
    return pl.pallas_call(
        functools.partial(_pool_kernel, ts=ts, gd=gd),
        grid=(B, S // ts),
        in_specs=[
            pl.BlockSpec((1, ts, PW), lambda b, i: (b, i, 0)),
            pl.BlockSpec((1, POOL_HALO, PW), lambda b, i: (b, jnp.maximum(i * hb - 1, 0), 0)),
            pl.BlockSpec((POOL_GROUPS, gd, gd), lambda b, i: (0, 0, 0)),
            pl.BlockSpec((1, PW), lambda b, i: (0, 0)),
        ],
        out_specs=pl.BlockSpec((1, ts, PW), lambda b, i: (b, i, 0)),
        out_shape=jax.ShapeDtypeStruct((B, S, PW), BF16),
        scratch_shapes=[pltpu.VMEM((POOL_HALO + ts, PW), F32)],
        compiler_params=_cparams("parallel", "arbitrary"),
        name="pool_mixer",
    )(p_in, p_in, w_pool.astype(BF16), scale.reshape(1, PW))


def _gelu_tanh(x):
    c = math.sqrt(2.0 / math.pi)
    return x * (0.5 * (1.0 + jnp.tanh(c * (x + 0.044715 * (x * x * x)))))


def _compress_kernel(x_ref, pos_ref, w1_ref, w2_ref, o_ref, tmp_ref, *, nch):
    half = CMP_BLOCK // 2
    dh = x_ref.shape[2]
    a = b = None
    for r in range(half):
        x = x_ref[0, pl.ds(r, nch, stride=CMP_STRIDE), :]
        lo = jnp.dot((x + pos_ref[r:r + 1, :]).astype(BF16), w1_ref[0, r * dh:(r + 1) * dh, :],
                     preferred_element_type=F32)
        hi = jnp.dot((x + pos_ref[half + r:half + r + 1, :]).astype(BF16),
                     w1_ref[0, (half + r) * dh:(half + r + 1) * dh, :], preferred_element_type=F32)
        a = lo if a is None else a + lo
        b = hi if b is None else b + hi
    tmp_ref[0:nch, :] = b
    tmp_ref[nch:nch + 8, :] = jnp.zeros((8, b.shape[1]), F32)
    pre = a + tmp_ref[1:nch + 1, :]
    out = jnp.dot(_gelu_tanh(pre).astype(BF16), w2_ref[0], preferred_element_type=F32)
    row = lax.broadcasted_iota(jnp.int32, out.shape, 0)
    o_ref[0, 0] = jnp.where(row < nch - 1, out, 0.0).astype(o_ref.dtype)


def _compress(x, pos, w1, w2, G):
    B, S, _ = x.shape
    hid = w1.shape[-1]
    dh = w2.shape[-1]
    nch = S // CMP_STRIDE
    assert CMP_BLOCK == 2 * CMP_STRIDE
    return pl.pallas_call(
        functools.partial(_compress_kernel, nch=nch),
        grid=(2, B * G),
        in_specs=[
            pl.BlockSpec((1, S, dh), lambda s, b: (b // G, 0, s * G + b % G)),
            pl.BlockSpec((CMP_BLOCK, dh), lambda s, b: (0, 0)),
            pl.BlockSpec((1, CMP_BLOCK * dh, hid), lambda s, b: (s, 0, 0)),
            pl.BlockSpec((1, hid, dh), lambda s, b: (s, 0, 0)),
        ],
        out_specs=pl.BlockSpec((1, 1, nch, dh), lambda s, b: (s, b, 0, 0)),
        out_shape=jax.ShapeDtypeStruct((2, B * G, nch, dh), BF16),
        scratch_shapes=[pltpu.VMEM((nch + 8, hid), F32)],
        compiler_params=_cparams("parallel", "arbitrary"),
        name="nsa_compress",
    )(x, pos, w1.astype(BF16), w2.astype(BF16))


def _nt_dot(a, b):
    return lax.dot_general(a, b, (((1,), (1,)), ((), ())), preferred_element_type=F32)


def _split_dot(w, x):
    hi = x.astype(BF16)
    r1 = x - hi.astype(F32)
    mid = r1.astype(BF16)
    lo = (r1 - mid.astype(F32)).astype(BF16)
    return (jnp.dot(w, hi, preferred_element_type=F32) + jnp.dot(w, mid, preferred_element_type=F32)
            + jnp.dot(w, lo, preferred_element_type=F32))


def _nsa_kernel(q_ref, gl_ref, kc_ref, vct_ref, ks_ref, vst_ref, kw_ref, vwt_ref, ovt_ref, et_ref, o_ref,
                m_ref, l_ref, acc_ref, s_ref, bias_ref, *, hg, n_cmp, n_slc, top_n):
    QT, DH = NSA_Q_TILE, NSA_HEAD_DIM
    t0 = pl.program_id(1) * QT
    q = jnp.concatenate([q_ref[:, h * DH:(h + 1) * DH] for h in range(hg)], axis=0)
    lane = lax.broadcasted_iota(jnp.int32, (1, QT), 1)
    t = t0 + lane

    def capped(s, cap):
        return jnp.concatenate([jnp.minimum(s[:, h * QT:(h + 1) * QT], cap) for h in range(hg)], axis=1)

    def softmax_cols(s):
        p = jnp.exp2(s - jnp.max(s, axis=0, keepdims=True))
        return p, jnp.sum(p, axis=0, keepdims=True)

    ncp = kc_ref.shape[2]
    nrow = lax.broadcasted_iota(jnp.int32, (ncp, QT), 0)
    valid_c = (nrow * CMP_STRIDE + (CMP_BLOCK - 1) <= t) & (nrow < n_cmp)
    s_c = capped(_nt_dot(kc_ref[0, 0], q), jnp.where(valid_c, BIG, NEG))
    ws = pl.multiple_of(jnp.maximum(t0 + QT - SWA_SPAN, 0), LANES)
    wb = ws // LANES
    dist = t - (ws + lax.broadcasted_iota(jnp.int32, (SWA_SPAN, QT), 0))
    valid_w = (dist >= 0) & (dist < SWA_WINDOW)
    s_w = capped(_nt_dot(kw_ref[0, pl.ds(ws, SWA_SPAN), :], q), jnp.where(valid_w, BIG, NEG))

    p_c, l_c = softmax_cols(s_c)
    seen = jnp.concatenate([jnp.where(t >= CMP_BLOCK - 1, 1.0, 0.0)] * hg, axis=1)
    p_c = p_c * (seen / l_c)
    o_c = jnp.dot(vct_ref[0], p_c.astype(BF16), preferred_element_type=F32)

    p_w, l_w = softmax_cols(s_w)
    vwin = jnp.concatenate([vwt_ref[0, wb + j] for j in range(SWA_SPAN // LANES)], axis=1)
    o_w = jnp.dot(vwin, p_w.astype(BF16), preferred_element_type=F32) / l_w

    p_sum = p_c[:, 0:QT]
    for h in range(1, hg):
        p_sum = p_sum + p_c[:, h * QT:(h + 1) * QT]
    imp = _split_dot(ovt_ref[...], p_sum)
    blk = lax.broadcasted_iota(jnp.int32, (LANES, QT), 0)
    cur = t0 // SLC_BLOCK + jnp.zeros((1, QT), jnp.int32)
    for k in range(1, QT // SLC_BLOCK):
        cur = cur + jnp.where(lane >= k * SLC_BLOCK, 1, 0)
    forced = (blk == 0) | (blk == cur) | (blk == cur - 1)
    val = jnp.where(forced, imp + FORCE_BONUS, jnp.where(blk > cur, -FORCE_BONUS, imp))
    if n_slc < LANES:
        val = jnp.where(blk < n_slc, val, -jnp.inf)

    def rank_bias(nb):
        nslab = nb // 8
        slabs = [val[8 * r:8 * r + 8, :] for r in range(nslab)]
        ranks = [jnp.zeros((8, QT), F32) for _ in range(nslab)]
        sub = lax.broadcasted_iota(jnp.int32, (8, QT), 0)
        for other in range(nb):
            c = val[other:other + 1, :]
            for r in range(nslab):
                if 8 * r > other:
                    beat = c >= slabs[r]
                elif 8 * r + 7 < other:
                    beat = c > slabs[r]
                else:
                    beat = (c > slabs[r]) | ((c == slabs[r]) & (sub + 8 * r > other))
                ranks[r] = ranks[r] + jnp.where(beat, 1.0, 0.0)
        for r in range(nslab):
            chosen = (ranks[r] < top_n) & (blk[8 * r:8 * r + 8] <= cur)
            bias_ref[8 * r:8 * r + 8, :] = jnp.where(chosen, 0.0, NEG)
        if nb < LANES:
            bias_ref[nb:LANES, :] = jnp.full((LANES - nb, QT), NEG, F32)

    live = (t0 + QT - 1) // SLC_BLOCK + 1
    sizes = sorted({min(_round_up(-(-n_slc * k // RANK_SIZES), 8), LANES) for k in range(1, RANK_SIZES + 1)})
    for lo, nb in zip([0] + sizes[:-1], sizes):
        @pl.when((live > lo) & (live <= nb))
        def _(nb=nb):
            rank_bias(nb)

    bias_q = bias_ref[...].T.astype(BF16)

    q_sel = jnp.concatenate([q, jnp.concatenate([bias_q] * hg, axis=0)], axis=1)
    m_ref[...] = jnp.full(m_ref.shape, NEG, F32)
    l_ref[...] = jnp.zeros(l_ref.shape, F32)
    acc_ref[...] = jnp.zeros(acc_ref.shape, F32)

    def sel_tiles(tiles):
        for slot, (kb, _) in enumerate(tiles):
            start = pl.multiple_of(kb * SEL_TILE, SEL_TILE)
            keys = jnp.concatenate([ks_ref[0, pl.ds(start, SEL_TILE), :], et_ref[kb]], axis=1)
            s_ref[slot] = _nt_dot(keys, q_sel)
        for slot, (kb, diagonal) in enumerate(tiles):
            s = s_ref[slot]
            if diagonal:
                key = kb * SEL_TILE + lax.broadcasted_iota(jnp.int32, (SEL_TILE, QT), 0)
                s = capped(s, jnp.where(key <= t, BIG, NEG))
            m_prev = m_ref[...]
            m_new = jnp.maximum(m_prev, jnp.max(s, axis=0, keepdims=True))
            alpha = jnp.exp2(m_prev - m_new)
            p = jnp.exp2(s - m_new)
            l_ref[...] = alpha * l_ref[...] + jnp.sum(p, axis=0, keepdims=True)
            acc_ref[...] = alpha * acc_ref[...] + jnp.dot(vst_ref[0, kb], p.astype(BF16),
                                                          preferred_element_type=F32)
            m_ref[...] = m_new

    def past_group(i, carry):
        sel_tiles([(SEL_GROUP * i + slot, False) for slot in range(SEL_GROUP)])
        return carry

    diag = t0 // SEL_TILE
    groups = diag // SEL_GROUP
    lax.fori_loop(0, groups, past_group, 0)
    for rem in range(SEL_GROUP):
        @pl.when(diag - groups * SEL_GROUP == rem)
        def _(rem=rem):
            sel_tiles([(diag - rem + slot, slot == rem) for slot in range(rem + 1)])

    o_s = acc_ref[...] / l_ref[...]

    gate = jax.nn.sigmoid(gl_ref[0, 0])
    for h in range(hg):
        cols = slice(h * QT, (h + 1) * QT)
        r = N_BRANCH * h
        out = gate[r:r + 1] * o_c[:, cols] + gate[r + 1:r + 2] * o_s[:, cols] + gate[r + 2:r + 3] * o_w[:, cols]
        o_ref[:, h * DH:(h + 1) * DH] = out.T.astype(o_ref.dtype)


def _nsa(q, gates, kvc, kv, B, S):
    T, HW = q.shape
    G, DH, QT = NSA_KV_GROUPS, NSA_HEAD_DIM, NSA_Q_TILE
    hg = HW // DH // G
    n_cmp = (S - CMP_BLOCK) // CMP_STRIDE + 1
    ncp = kvc.shape[2]
    n_slc = S // SLC_BLOCK
    top_n = min(SLC_TOPN, n_slc)
    nqt = S // QT
    kvw = G * DH
    assert n_slc <= LANES and S % SEL_TILE == 0 and S >= SWA_SPAN and QT == LANES

    cs = jnp.arange(ncp)[None, :] * CMP_STRIDE
    ss = jnp.arange(LANES)[:, None] * SLC_BLOCK
    overlap = jnp.clip(jnp.minimum(cs + CMP_BLOCK, ss + SLC_BLOCK) - jnp.maximum(cs, ss), 0) // CMP_STRIDE
    overlap = jnp.where((jnp.arange(ncp)[None, :] < n_cmp) & (jnp.arange(LANES)[:, None] < n_slc), overlap, 0)
    overlap = overlap.astype(BF16)
    key_blk = (jnp.arange(S) // SLC_BLOCK).reshape(S // SEL_TILE, SEL_TILE, 1)
    expand = (key_blk == jnp.arange(LANES)[None, None, :]).astype(BF16)

    gr = _round_up(N_BRANCH * hg, 8)
    gl = gates[:, :G * hg * N_BRANCH].reshape(B, nqt, QT, G, hg * N_BRANCH)
    gl = jnp.pad(gl, ((0, 0),) * 4 + ((0, gr - hg * N_BRANCH),)).transpose(0, 3, 1, 4, 2).reshape(B * G, nqt, gr, QT)

    def v_tiles(which, tile):
        v = kv[:, :, which * kvw:(which + 1) * kvw].reshape(B, S // tile, tile, G, DH)
        return v.transpose(0, 3, 1, 4, 2).reshape(B * G, S // tile, DH, tile)

    def k_spec(which):
        return pl.BlockSpec((1, S, DH), lambda bg, i: (bg // G, 0, which * G + bg % G))

    return pl.pallas_call(
        functools.partial(_nsa_kernel, hg=hg, n_cmp=n_cmp, n_slc=n_slc, top_n=top_n),
        grid=(B * G, nqt),
        in_specs=[
            pl.BlockSpec((QT, hg * DH), lambda bg, i: ((bg // G) * nqt + i, bg % G)),
            pl.BlockSpec((1, 1, gr, QT), lambda bg, i: (bg, i, 0, 0)),
            pl.BlockSpec((1, 1, ncp, DH), lambda bg, i: (0, bg, 0, 0)),
            pl.BlockSpec((1, DH, ncp), lambda bg, i: (bg, 0, 0)),
            k_spec(0),
            pl.BlockSpec((1, S // SEL_TILE, DH, SEL_TILE), lambda bg, i: (bg, 0, 0, 0)),
            k_spec(2),
            pl.BlockSpec((1, S // LANES, DH, LANES), lambda bg, i: (bg, 0, 0, 0)),
            pl.BlockSpec((LANES, ncp), lambda bg, i: (0, 0)),
            pl.BlockSpec((S // SEL_TILE, SEL_TILE, LANES), lambda bg, i: (0, 0, 0)),
        ],
        out_specs=pl.BlockSpec((QT, hg * DH), lambda bg, i: ((bg // G) * nqt + i, bg % G)),
        out_shape=jax.ShapeDtypeStruct((T, HW), BF16),
        scratch_shapes=[pltpu.VMEM((1, hg * QT), F32), pltpu.VMEM((1, hg * QT), F32),
                        pltpu.VMEM((DH, hg * QT), F32), pltpu.VMEM((SEL_GROUP, SEL_TILE, hg * QT), F32),
                        pltpu.VMEM((LANES, QT), F32)],
        compiler_params=_cparams("parallel", "arbitrary"),
        name="nsa_attention",
    )(q, gl, kvc, jnp.swapaxes(kvc[1], 1, 2), kv, v_tiles(1, SEL_TILE), kv, v_tiles(3, LANES), overlap, expand)


def _mlstm_kernel(q_ref, kt_ref, v_ref, o_ref, gr_ref, gc_ref, hn_ref, y_ref, c_ref, n_ref, m_ref, *, L, hps):
    ci = pl.program_id(1)
    dk = kt_ref.shape[1] // hps
    dv = v_ref.shape[1] // hps

    @pl.when(ci == 0)
    def _():
        c_ref[...] = jnp.zeros_like(c_ref)
        n_ref[...] = jnp.zeros_like(n_ref)
        m_ref[...] = jnp.zeros_like(m_ref)

    ti = lax.broadcasted_iota(jnp.int32, (L, L), 0)
    si = lax.broadcasted_iota(jnp.int32, (L, L), 1)
    causal = si <= ti

    for hd in range(hps):
        q = q_ref[:, hd * dk:(hd + 1) * dk]
        kt = kt_ref[0, hd * dk:(hd + 1) * dk, :]
        v = v_ref[:, hd * dv:(hd + 1) * dv]
        ig_row = gr_ref[hd, 0, 0:1, :]
        lf_row = jax.nn.log_sigmoid(gr_ref[hd, 0, 1:2, :])
        lf_col = jax.nn.log_sigmoid(gc_ref[hd, 0, :, 1:2])
        m_prev = m_ref[hd, 0:1, 0:1]

        b_col = jnp.sum(jnp.where(causal, lf_row, 0.0), axis=1, keepdims=True)
        b_row = jnp.sum(jnp.where(ti <= si, lf_col, 0.0), axis=0, keepdims=True)
        b_last = b_col[L - 1:L, :]

        dmat = jnp.where(causal, b_col - b_row + ig_row, -jnp.inf)
        a_col = b_col + m_prev
        m_t = jnp.maximum(a_col, jnp.max(dmat, axis=1, keepdims=True))
        wq = jnp.dot(q, kt, preferred_element_type=F32) * jnp.exp(dmat - m_t)
        inter = jnp.exp(a_col - m_t)
        q_c = jnp.dot(q, c_ref[hd].astype(BF16), preferred_element_type=F32)
        q_n = jnp.dot(q, n_ref[hd].astype(BF16), preferred_element_type=F32)[:, 0:1]
        num = inter * q_c + jnp.dot(wq.astype(BF16), v, preferred_element_type=F32)
        den = inter * q_n + jnp.sum(wq, axis=1, keepdims=True)
        h = num / jnp.maximum(jnp.abs(den), jnp.exp(-m_t))
        h = h * lax.rsqrt(jnp.mean(h * h, axis=-1, keepdims=True) + RMS_EPS) * hn_ref[hd]
        y_ref[:, hd * dv:(hd + 1) * dv] = (jax.nn.sigmoid(o_ref[:, hd * dv:(hd + 1) * dv]) * h).astype(y_ref.dtype)

        g_row = b_last - b_row + ig_row
        m_new = jnp.maximum(b_last + m_prev, jnp.max(g_row, axis=1, keepdims=True))
        decay = jnp.exp(b_last + m_prev - m_new)
        kw_t = kt.astype(F32) * jnp.exp(g_row - m_new)
        c_ref[hd] = decay * c_ref[hd] + jnp.dot(kw_t.astype(BF16), v, preferred_element_type=F32)
        n_ref[hd] = decay * n_ref[hd] + jnp.sum(kw_t, axis=1, keepdims=True)
        m_ref[hd] = jnp.broadcast_to(m_new, m_ref.shape[1:])


def _mlstm(q, kt, v, o, gates, head_norm, B, S):
    T = q.shape[0]
    H = MLSTM_HEADS
    dk = q.shape[1] // H
    dv = v.shape[1] // H
    L = min(MLSTM_CHUNK, S)
    nc = S // L
    g = gates[:, :2 * H].reshape(B, nc, L, 2, H)
    g_row = jnp.transpose(g, (0, 4, 1, 3, 2)).reshape(B * H, nc, 2, L)
    g_col = jnp.transpose(g, (0, 4, 1, 2, 3)).reshape(B * H, nc, L, 2)
    hps = MLSTM_HEADS_PER_STEP
    hs = H // hps
    return pl.pallas_call(
        functools.partial(_mlstm_kernel, L=L, hps=hps),
        grid=(B * hs, nc),
        in_specs=[
            pl.BlockSpec((L, hps * dk), lambda bh, c: ((bh // hs) * nc + c, bh % hs)),
            pl.BlockSpec((1, hps * dk, L), lambda bh, c: (bh // hs, bh % hs, c)),
            pl.BlockSpec((L, hps * dv), lambda bh, c: ((bh // hs) * nc + c, bh % hs)),
            pl.BlockSpec((L, hps * dv), lambda bh, c: ((bh // hs) * nc + c, bh % hs)),
            pl.BlockSpec((hps, 1, 2, L), lambda bh, c: (bh, c, 0, 0)),
            pl.BlockSpec((hps, 1, L, 2), lambda bh, c: (bh, c, 0, 0)),
            pl.BlockSpec((hps, 1, dv), lambda bh, c: (bh % hs, 0, 0)),
        ],
        out_specs=pl.BlockSpec((L, hps * dv), lambda bh, c: ((bh // hs) * nc + c, bh % hs)),
        out_shape=jax.ShapeDtypeStruct((T, H * dv), BF16),
        scratch_shapes=[pltpu.VMEM((hps, dk, dv), F32), pltpu.VMEM((hps, dk, LANES), F32),
                        pltpu.VMEM((hps, 8, LANES), F32)],
        compiler_params=_cparams("parallel", "arbitrary"),
        name="mlstm",
    )(q, kt, v, o, g_row, g_col, head_norm.reshape(H, 1, dv))


def _ffn_in_kernel(u_ref, halo_ref, wg_ref, wu_ref, cw_ref, cb_ref, o_ref, lhs_ref, g_ref, tail_ref, *, tm, tn, ff,
                   sh, over, tiles_per_seq):
    i = pl.program_id(0)
    j = pl.program_id(1)

    @pl.when(j == 0)
    def _():
        halo = halo_ref[...]
        lhs_ref[0:CONV_HALO, :] = jnp.where(i % tiles_per_seq != 0, halo, jnp.zeros_like(halo))
        lhs_ref[CONV_HALO:CONV_HALO + tm, :] = u_ref[...]
        tail_ref[...] = jnp.zeros_like(tail_ref)

    g_ref[...] = jnp.dot(lhs_ref[...], wg_ref[...], preferred_element_type=F32)
    up = jnp.dot(u_ref[...], wu_ref[0], preferred_element_type=F32)
    if over:
        moved = jnp.concatenate([up[:, over:], up[:, :over]], axis=1)
        up = jnp.where(j == pl.num_programs(1) - 1, moved, up)
    gate = cb_ref[...] + cw_ref[0:1, :] * g_ref[CONV_HALO:CONV_HALO + tm, :]
    for back in range(1, CONV_WIDTH):
        gate = gate + cw_ref[back:back + 1, :] * g_ref[CONV_HALO - back:CONV_HALO - back + tm, :]
    if sh:
        lane = lax.broadcasted_iota(jnp.int32, (1, LANES), 1)
        tiles = [tail_ref[...]] + [gate[:, c * LANES:(c + 1) * LANES] for c in range(tn // LANES)]
        tail_ref[...] = tiles[-1]
        rolled = [pltpu.roll(x, sh, axis=1) for x in tiles]
        gate = jnp.concatenate([jnp.where(lane < sh, rolled[c], rolled[c + 1]) for c in range(tn // LANES)], axis=1)
    act = gate * jax.nn.sigmoid(gate) * up
    hidden = j * tn - sh + lax.broadcasted_iota(jnp.int32, (1, tn), 1)
    o_ref[...] = jnp.where((hidden >= 0) & (hidden < ff), act, 0.0).astype(o_ref.dtype)


def _ffn_in(u, w_in, conv_w, conv_b, layer, ff, sh, fp, S, tm=1024, tn=FFN_COL_TILE):
    T, D = u.shape
    tm = min(tm, S)
    assert T % tm == 0 and S % tm == 0 and fp % tn == 0 and tn % LANES == 0
    hb = tm // CONV_HALO
    up0 = (ff - sh) // LANES
    up_last = (w_in.shape[-1] - tn) // LANES
    over = max(up0 + (fp // tn - 1) * (tn // LANES) - up_last, 0) * LANES
    assert w_in.shape[-1] % LANES == 0 and over < tn
    return pl.pallas_call(
        functools.partial(_ffn_in_kernel, tm=tm, tn=tn, ff=ff, sh=sh, over=over, tiles_per_seq=S // tm),
        grid=(T // tm, fp // tn),
        in_specs=[
            pl.BlockSpec((tm, D), lambda i, j: (i, 0)),
            pl.BlockSpec((CONV_HALO, D), lambda i, j: (jnp.maximum(i * hb - 1, 0), 0)),
            pl.BlockSpec((None, D, tn), lambda i, j: (layer, 0, j)),
            pl.BlockSpec((pl.Element(1), pl.Element(D), pl.Element(tn)),
                         lambda i, j: (layer, 0, jnp.minimum(up0 + j * (tn // LANES), up_last) * LANES)),
            pl.BlockSpec((None, CONV_WIDTH, tn), lambda i, j: (layer, 0, j)),
            pl.BlockSpec((None, 1, tn), lambda i, j: (layer, 0, j)),
        ],
        out_specs=pl.BlockSpec((tm, tn), lambda i, j: (i, j)),
        out_shape=jax.ShapeDtypeStruct((T, fp), BF16),
        scratch_shapes=[pltpu.VMEM((CONV_HALO + tm, D), BF16), pltpu.VMEM((CONV_HALO + tm, tn), F32),
                        pltpu.VMEM((tm, LANES), F32)],
        compiler_params=_cparams("parallel", "arbitrary"),
        name="ffn_in_conv_act",
    )(u, u, w_in, w_in, conv_w, conv_b)


def _ffn_weights(w_in, w_out):
    depth, D, ff2 = w_in.shape
    ff = ff2 // 2
    sh = ff % LANES
    fp = _round_up(ff + sh, FFN_COL_TILE)
    w_in = w_in.astype(BF16)
    w_out = jnp.concatenate([jnp.zeros((depth, sh, D), BF16), w_out.astype(BF16),
                             jnp.zeros((depth, fp - ff - sh, D), BF16)], axis=1)
    return w_in, w_out, ff, sh, fp


def _conv_ffn(h, u, ln_post, next_gain, w_in, w_out, ff, sh, fp, conv_w, conv_b, layer, S):
    act = _ffn_in(u, w_in, conv_w, conv_b, layer, ff, sh, fp, S)
    return _mm_norm_res([act], w_out, h, ln_post, next_gain, w_lead=layer)


def _ab_layer(h, u, ln_post, next_gain, w_in, pool_w, pool_scale, cmp_pos, ck_w1, ck_w2, cv_w1, cv_w2, w_out, B, S):
    T, D = h.shape
    G, DH = NSA_KV_GROUPS, NSA_HEAD_DIM
    pw = D // 4
    hw = D - pw
    kvw = G * DH
    w = w_in.astype(BF16)
    c0 = pw + hw
    p_in = _mm(u, w, F32, 0, pw)
    q = _mm(u, w, BF16, pw, hw, scale=DH ** -0.5 * LOG2E)
    kv_cmp = _mm(u, w, F32, c0, 2 * kvw)
    kv = _mm(u, w, BF16, c0 + 2 * kvw, 4 * kvw)
    gates = _mm(u, w, F32, c0 + 6 * kvw, LANES)

    y_a = _pool_mixer(p_in.reshape(B, S, pw), pool_w, pool_scale).reshape(T, pw)
    kvc = _compress(kv_cmp.reshape(B, S, 2 * kvw), cmp_pos, jnp.stack([ck_w1, cv_w1]), jnp.stack([ck_w2, cv_w2]), G)
    y_b = _nsa(q, gates, kvc, kv.reshape(B, S, 4 * kvw), B, S)
    return _mm_norm_res([y_a, y_b], w_out.astype(BF16), h, ln_post, next_gain)


def _c_layer(h, u, ln_post, next_gain, w_in, b_if, head_norm, w_out, B, S):
    T, D = h.shape
    H = MLSTM_HEADS
    dv = D // H
    dk = dv // 2
    qk = H * dk
    w = w_in.astype(BF16)
    q = _mm(u, w, BF16, 0, qk, scale=dk ** -0.5)
    k = _mm(u, w, BF16, qk, qk)
    v = _mm(u, w, BF16, 2 * qk, D)
    o = _mm(u, w, F32, 2 * qk + D, D)
    gates = _mm(u, w, F32, 2 * qk + 2 * D, LANES, bias=_pad_cols(b_if.reshape(1, 2 * H), LANES))
    kt = k.reshape(B, S, qk).transpose(0, 2, 1)
    y = _mlstm(q, kt, v, o, gates, head_norm, B, S)
    return _mm_norm_res([y], w_out.astype(BF16), h, ln_post, next_gain)


def kernel(x, ln_pre, ln_post, w_in_ab, pool_w, pool_scale, cmp_pos, cmp_k_w1, cmp_k_w2, cmp_v_w1, cmp_v_w2,
           w_out_ab, w_in_c, b_if_c, head_norm_c, w_out_c, ffn_ln_pre, ffn_ln_post, ffn_w_in, ffn_conv_w,
           ffn_conv_b, ffn_w_out):
    B, S, D = x.shape
    depth = ln_pre.shape[0]
    h = x.reshape(B * S, D)
    u = _rmsnorm(h, ln_pre[0])
    ffn_wi, ffn_wo, ff, sh, fp = _ffn_weights(ffn_w_in, ffn_w_out)
    ffn_cb = ffn_conv_b.reshape(depth, 1, -1)
    for layer in range(depth):
        i = layer // 2
        if layer % 2 == 0:
            h, u = _ab_layer(h, u, ln_post[layer], ffn_ln_pre[layer], w_in_ab[i], pool_w[i], pool_scale[i],
                             cmp_pos[i], cmp_k_w1[i], cmp_k_w2[i], cmp_v_w1[i], cmp_v_w2[i], w_out_ab[i], B, S)
        else:
            h, u = _c_layer(h, u, ln_post[layer], ffn_ln_pre[layer], w_in_c[i], b_if_c[i], head_norm_c[i],
                            w_out_c[i], B, S)
        next_gain = ln_pre[layer + 1] if layer + 1 < depth else None
        h, u = _conv_ffn(h, u, ffn_ln_post[layer], next_gain, ffn_wi, ffn_wo, ff, sh, fp, ffn_conv_w, ffn_cb, layer,
                         S)
    return h.reshape(B, S, D)
```

```python
import functools
import math

import jax
import jax.numpy as jnp
from jax import lax
from jax.experimental import pallas as pl
from jax.experimental.pallas import tpu as pltpu

F32 = jnp.float32
BF16 = jnp.bfloat16

RMS_EPS = 1e-6
POOL_GROUPS = 4
POOL_WINDOWS = (2, 4, 8, 16)
POOL_HALO = 16
NSA_HEAD_DIM = 128
NSA_KV_GROUPS = 4
N_BRANCH = 3
CMP_BLOCK = 32
CMP_STRIDE = 16
SLC_BLOCK = 64
SLC_TOPN = 16
SWA_WINDOW = 512
FORCE_BONUS = 1e4
NEG = -1e30
BIG = 1e30
MLSTM_HEADS = 8
CONV_WIDTH = 3
CONV_HALO = 16

LANES = 128
LOG2E = math.log2(math.e)
NSA_Q_TILE = LANES
SEL_TILE = 512
SEL_GROUP = 4
RANK_SIZES = 4
SWA_SPAN = SWA_WINDOW + NSA_Q_TILE
MLSTM_CHUNK = 256
MLSTM_HEADS_PER_STEP = 4
FFN_COL_TILE = 512
VMEM_LIMIT = 56 * 1024 * 1024


def _cparams(*sem):
    return pltpu.CompilerParams(dimension_semantics=sem, vmem_limit_bytes=VMEM_LIMIT)


def _round_up(n, m):
    return (n + m - 1) // m * m


def _pad_cols(w, n):
    return jnp.pad(w, ((0, 0), (0, n - w.shape[1])))


def _rmsnorm_kernel(x_ref, g_ref, o_ref):
    x = x_ref[...]
    ms = jnp.mean(x * x, axis=-1, keepdims=True)
    o_ref[...] = (x * lax.rsqrt(ms + RMS_EPS) * g_ref[...]).astype(o_ref.dtype)


def _rmsnorm(x, gain, tm=512):
    T, D = x.shape
    return pl.pallas_call(
        _rmsnorm_kernel,
        grid=(T // tm,),
        in_specs=[pl.BlockSpec((tm, D), lambda i: (i, 0)), pl.BlockSpec((1, D), lambda i: (0, 0))],
        out_specs=pl.BlockSpec((tm, D), lambda i: (i, 0)),
        out_shape=jax.ShapeDtypeStruct((T, D), BF16),
        compiler_params=_cparams("parallel"),
        name="rmsnorm",
    )(x, gain.reshape(1, D))


def _mm_kernel(a_ref, b_ref, bias_ref, o_ref, *, scale, live_cols):
    acc = jnp.dot(a_ref[...], b_ref[...], preferred_element_type=F32)
    out = (acc + bias_ref[...]) * scale
    if live_cols is not None:
        col = pl.program_id(1) * out.shape[1] + lax.broadcasted_iota(jnp.int32, (1, out.shape[1]), 1)
        out = jnp.where(col < live_cols, out, 0.0)
    o_ref[...] = out.astype(o_ref.dtype)


def _mm(a, w, out_dtype, col0, ncols, *, bias=None, scale=1.0, tm=1024, tn=1024):
    M, K = a.shape
    tm = min(tm, M)
    tn = math.gcd(tn, ncols, col0)
    assert M % tm == 0 and tn % LANES == 0
    jb = col0 // tn
    live_cols = w.shape[1] - col0 if col0 + ncols > w.shape[1] else None
    if bias is None:
        bias = jnp.zeros((1, ncols), F32)
    return pl.pallas_call(
        functools.partial(_mm_kernel, scale=scale, live_cols=live_cols),
        grid=(M // tm, ncols // tn),
        in_specs=[
            pl.BlockSpec((tm, K), lambda i, j: (i, 0)),
            pl.BlockSpec((K, tn), lambda i, j: (0, j + jb)),
            pl.BlockSpec((1, tn), lambda i, j: (0, j)),
        ],
        out_specs=pl.BlockSpec((tm, tn), lambda i, j: (i, j)),
        out_shape=jax.ShapeDtypeStruct((M, ncols), out_dtype),
        compiler_params=_cparams("parallel", "arbitrary"),
        name="proj",
    )(a, w, bias)


def _mm_norm_res_kernel(*refs, nk, nk1, n_a, emit_u, tm, rows):
    a_refs = refs[:n_a]
    w_ref, h_hbm, g_ref = refs[n_a:n_a + 3]
    rest = refs[n_a + 3:]
    if emit_u:
        ng_ref, o_ref, u_ref, h_buf, h_sem = rest
    else:
        o_ref, h_buf, h_sem = rest
    i = pl.program_id(0)
    k = pl.program_id(1)

    def h_copy():
        return pltpu.make_async_copy(h_hbm.at[pl.ds(i * tm, tm), :], h_buf, h_sem)

    def product(a_ref):
        return jnp.dot(a_ref[...], w_ref[...], preferred_element_type=F32)

    @pl.when(k == 0)
    def _():
        h_copy().start()
        o_ref[...] = product(a_refs[0])

    if nk1 > 1 or n_a == 1:
        @pl.when((k > 0) & (k < nk1))
        def _():
            o_ref[...] += product(a_refs[0])

    if n_a == 2:
        @pl.when(k >= nk1)
        def _():
            o_ref[...] += product(a_refs[1])

    @pl.when(k == nk - 1)
    def _():
        h_copy().wait()

        def norm_rows(c, carry):
            r = pl.ds(pl.multiple_of(c * rows, rows), rows)
            y = o_ref[r, :]
            ms = jnp.mean(y * y, axis=-1, keepdims=True)
            h_new = h_buf[r, :] + y * lax.rsqrt(ms + RMS_EPS) * g_ref[...]
            o_ref[r, :] = h_new
            if emit_u:
                ms2 = jnp.mean(h_new * h_new, axis=-1, keepdims=True)
                u_ref[r, :] = (h_new * lax.rsqrt(ms2 + RMS_EPS) * ng_ref[...]).astype(u_ref.dtype)
            return carry

        lax.fori_loop(0, tm // rows, norm_rows, 0)


def _mm_norm_res(a_list, w, h, gain, next_gain=None, w_lead=None, tm=512, tk=1024, rows=64):
    M = h.shape[0]
    N = w.shape[-1]
    n_a = len(a_list)
    tk = math.gcd(tk, *[a.shape[1] for a in a_list])
    assert n_a in (1, 2) and M % tm == 0 and tm % rows == 0
    nk1 = a_list[0].shape[1] // tk
    nk = sum(a.shape[1] for a in a_list) // tk
    emit_u = next_gain is not None
    a_specs = [pl.BlockSpec((tm, tk), lambda i, k: (i, jnp.minimum(k, nk1 - 1)))]
    if n_a == 2:
        a_specs.append(pl.BlockSpec((tm, tk), lambda i, k: (i, jnp.maximum(k - nk1, 0))))
    row_spec = pl.BlockSpec((tm, N), lambda i, k: (i, 0))
    vec_spec = pl.BlockSpec((1, N), lambda i, k: (0, 0))
    if w_lead is None:
        w_spec = pl.BlockSpec((tk, N), lambda i, k: (k, 0))
    else:
        w_spec = pl.BlockSpec((None, tk, N), lambda i, k: (w_lead, k, 0))
    in_specs = a_specs + [w_spec, pl.BlockSpec(memory_space=pl.ANY), vec_spec]
    args = list(a_list) + [w, h, gain.reshape(1, N)]
    out_specs, out_shape = row_spec, jax.ShapeDtypeStruct((M, N), F32)
    if emit_u:
        in_specs.append(vec_spec)
        args.append(next_gain.reshape(1, N))
        out_specs = [row_spec, row_spec]
        out_shape = [out_shape, jax.ShapeDtypeStruct((M, N), BF16)]
    out = pl.pallas_call(
        functools.partial(_mm_norm_res_kernel, nk=nk, nk1=nk1, n_a=n_a, emit_u=emit_u, tm=tm, rows=rows),
        grid=(M // tm, nk),
        in_specs=in_specs,
        out_specs=out_specs,
        out_shape=out_shape,
        scratch_shapes=[pltpu.VMEM((tm, N), F32), pltpu.SemaphoreType.DMA(())],
        compiler_params=_cparams("parallel", "arbitrary"),
        name="out_proj_norm_res",
    )(*args)
    return out if emit_u else (out, None)


def _pool_kernel(cur_ref, prev_ref, w_ref, scale_ref, o_ref, ext_ref, *, ts, gd):
    i = pl.program_id(1)
    ext_ref[0:POOL_HALO, :] = jnp.where(i > 0, prev_ref[0], 0.0)
    ext_ref[POOL_HALO:POOL_HALO + ts, :] = cur_ref[0]
    t = i * ts + lax.broadcasted_iota(jnp.int32, (ts, 1), 0)
    for g, win in enumerate(POOL_WINDOWS):
        cols = slice(g * gd, (g + 1) * gd)
        tok = ext_ref[POOL_HALO:POOL_HALO + ts, cols]
        acc = tok
        for back in range(1, win):
            acc = acc + ext_ref[POOL_HALO - back:POOL_HALO - back + ts, cols]
        count = jnp.minimum(t + 1, win).astype(F32)
        pooled = acc / count - tok
        y = jnp.dot(pooled.astype(BF16), w_ref[g], preferred_element_type=F32)
        o_ref[0, :, cols] = (y * scale_ref[:, cols]).astype(o_ref.dtype)


def _pool_mixer(p_in, w_pool, scale, ts=512):
    B, S, PW = p_in.shape
    gd = PW // POOL_GROUPS
    hb = ts // POOL_HALO
    return pl.pallas_call(
        functools.partial(_pool_kernel, ts=ts, gd=gd),
        grid=(B, S // ts),
        in_specs=[
            pl.BlockSpec((1, ts, PW), lambda b, i: (b, i, 0)),
            pl.BlockSpec((1, POOL_HALO, PW), lambda b, i: (b, jnp.maximum(i * hb - 1, 0), 0)),
            pl.BlockSpec((POOL_GROUPS, gd, gd), lambda b, i: (0, 0, 0)),
            pl.BlockSpec((1, PW), lambda b, i: (0, 0)),
        ],
        out_specs=pl.BlockSpec((1, ts, PW), lambda b, i: (b, i, 0)),
        out_shape=jax.ShapeDtypeStruct((B, S, PW), BF16),
        scratch_shapes=[pltpu.VMEM((POOL_HALO + ts, PW), F32)],
        compiler_params=_cparams("parallel", "arbitrary"),
        name="pool_mixer",
    )(p_in, p_in, w_pool.astype(BF16), scale.reshape(1, PW))


def _gelu_tanh(x):
    c = math.sqrt(2.0 / math.pi)
    return x * (0.5 * (1.0 + jnp.tanh(c * (x + 0.044715 * (x * x * x)))))


def _compress_kernel(x_ref, pos_ref, w1_ref, w2_ref, o_ref, tmp_ref, *, nch):
    half = CMP_BLOCK // 2
    dh = x_ref.shape[2]
    a = b = None
    for r in range(half):
        x = x_ref[0, pl.ds(r, nch, stride=CMP_STRIDE), :]
        lo = jnp.dot((x + pos_ref[r:r + 1, :]).astype(BF16), w1_ref[0, r * dh:(r + 1) * dh, :],
                     preferred_element_type=F32)
        hi = jnp.dot((x + pos_ref[half + r:half + r + 1, :]).astype(BF16),
                     w1_ref[0, (half + r) * dh:(half + r + 1) * dh, :], preferred_element_type=F32)
        a = lo if a is None else a + lo
        b = hi if b is None else b + hi
    tmp_ref[0:nch, :] = b
    tmp_ref[nch:nch + 8, :] = jnp.zeros((8, b.shape[1]), F32)
    pre = a + tmp_ref[1:nch + 1, :]
    out = jnp.dot(_gelu_tanh(pre).astype(BF16), w2_ref[0], preferred_element_type=F32)
    row = lax.broadcasted_iota(jnp.int32, out.shape, 0)
    o_ref[0, 0] = jnp.where(row < nch - 1, out, 0.0).astype(o_ref.dtype)


def _compress(x, pos, w1, w2, G):
    B, S, _ = x.shape
    hid = w1.shape[-1]
    dh = w2.shape[-1]
    nch = S // CMP_STRIDE
    assert CMP_BLOCK == 2 * CMP_STRIDE
    return pl.pallas_call(
        functools.partial(_compress_kernel, nch=nch),
        grid=(2, B * G),
        in_specs=[
            pl.BlockSpec((1, S, dh), lambda s, b: (b // G, 0, s * G + b % G)),
            pl.BlockSpec((CMP_BLOCK, dh), lambda s, b: (0, 0)),
            pl.BlockSpec((1, CMP_BLOCK * dh, hid), lambda s, b: (s, 0, 0)),
            pl.BlockSpec((1, hid, dh), lambda s, b: (s, 0, 0)),
        ],
        out_specs=pl.BlockSpec((1, 1, nch, dh), lambda s, b: (s, b, 0, 0)),
        out_shape=jax.ShapeDtypeStruct((2, B * G, nch, dh), BF16),
        scratch_shapes=[pltpu.VMEM((nch + 8, hid), F32)],
        compiler_params=_cparams("parallel", "arbitrary"),
        name="nsa_compress",
    )(x, pos, w1.astype(BF16), w2.astype(BF16))


def _nt_dot(a, b):
    return lax.dot_general(a, b, (((1,), (1,)), ((), ())), preferred_element_type=F32)


def _split_dot(w, x):
    hi = x.astype(BF16)
    r1 = x - hi.astype(F32)
    mid = r1.astype(BF16)
    lo = (r1 - mid.astype(F32)).astype(BF16)
    return (jnp.dot(w, hi, preferred_element_type=F32) + jnp.dot(w, mid, preferred_element_type=F32)
            + jnp.dot(w, lo, preferred_element_type=F32))


def _nsa_kernel(q_ref, gl_ref, kc_ref, vct_ref, ks_ref, vst_ref, kw_ref, vwt_ref, ovt_ref, et_ref, o_ref,
                m_ref, l_ref, acc_ref, s_ref, bias_ref, *, hg, n_cmp, n_slc, top_n):
    QT, DH = NSA_Q_TILE, NSA_HEAD_DIM
    t0 = pl.program_id(1) * QT
    q = jnp.concatenate([q_ref[:, h * DH:(h + 1) * DH] for h in range(hg)], axis=0)
    lane = lax.broadcasted_iota(jnp.int32, (1, QT), 1)
    t = t0 + lane

    def capped(s, cap):
        return jnp.concatenate([jnp.minimum(s[:, h * QT:(h + 1) * QT], cap) for h in range(hg)], axis=1)

    def softmax_cols(s):
        p = jnp.exp2(s - jnp.max(s, axis=0, keepdims=True))
        return p, jnp.sum(p, axis=0, keepdims=True)

    ncp = kc_ref.shape[2]
    nrow = lax.broadcasted_iota(jnp.int32, (ncp, QT), 0)
    valid_c = (nrow * CMP_STRIDE + (CMP_BLOCK - 1) <= t) & (nrow < n_cmp)
    s_c = capped(_nt_dot(kc_ref[0, 0], q), jnp.where(valid_c, BIG, NEG))
    ws = pl.multiple_of(jnp.maximum(t0 + QT - SWA_SPAN, 0), LANES)
    wb = ws // LANES
    dist = t - (ws + lax.broadcasted_iota(jnp.int32, (SWA_SPAN, QT), 0))
    valid_w = (dist >= 0) & (dist < SWA_WINDOW)
    s_w = capped(_nt_dot(kw_ref[0, pl.ds(ws, SWA_SPAN), :], q), jnp.where(valid_w, BIG, NEG))

    p_c, l_c = softmax_cols(s_c)
    seen = jnp.concatenate([jnp.where(t >= CMP_BLOCK - 1, 1.0, 0.0)] * hg, axis=1)
    p_c = p_c * (seen / l_c)
    o_c = jnp.dot(vct_ref[0], p_c.astype(BF16), preferred_element_type=F32)

    p_w, l_w = softmax_cols(s_w)
    vwin = jnp.concatenate([vwt_ref[0, wb + j] for j in range(SWA_SPAN // LANES)], axis=1)
    o_w = jnp.dot(vwin, p_w.astype(BF16), preferred_element_type=F32) / l_w

    p_sum = p_c[:, 0:QT]
    for h in range(1, hg):
        p_sum = p_sum + p_c[:, h * QT:(h + 1) * QT]
    imp = _split_dot(ovt_ref[...], p_sum)
    blk = lax.broadcasted_iota(jnp.int32, (LANES, QT), 0)
    cur = t0 // SLC_BLOCK + jnp.zeros((1, QT), jnp.int32)
    for k in range(1, QT // SLC_BLOCK):
        cur = cur + jnp.where(lane >= k * SLC_BLOCK, 1, 0)
    forced = (blk == 0) | (blk == cur) | (blk == cur - 1)
    val = jnp.where(forced, imp + FORCE_BONUS, jnp.where(blk > cur, -FORCE_BONUS, imp))
    if n_slc < LANES:
        val = jnp.where(blk < n_slc, val, -jnp.inf)

    def rank_bias(nb):
        nslab = nb // 8
        slabs = [val[8 * r:8 * r + 8, :] for r in range(nslab)]
        ranks = [jnp.zeros((8, QT), F32) for _ in range(nslab)]
        sub = lax.broadcasted_iota(jnp.int32, (8, QT), 0)
        for other in range(nb):
            c = val[other:other + 1, :]
            for r in range(nslab):
                if 8 * r > other:
                    beat = c >= slabs[r]
                elif 8 * r + 7 < other:
                    beat = c > slabs[r]
                else:
                    beat = (c > slabs[r]) | ((c == slabs[r]) & (sub + 8 * r > other))
                ranks[r] = ranks[r] + jnp.where(beat, 1.0, 0.0)
        for r in range(nslab):
            chosen = (ranks[r] < top_n) & (blk[8 * r:8 * r + 8] <= cur)
            bias_ref[8 * r:8 * r + 8, :] = jnp.where(chosen, 0.0, NEG)
        if nb < LANES:
            bias_ref[nb:LANES, :] = jnp.full((LANES - nb, QT), NEG, F32)

    live = (t0 + QT - 1) // SLC_BLOCK + 1
    sizes = sorted({min(_round_up(-(-n_slc * k // RANK_SIZES), 8), LANES) for k in range(1, RANK_SIZES + 1)})
    for lo, nb in zip([0] + sizes[:-1], sizes):
        @pl.when((live > lo) & (live <= nb))
        def _(nb=nb):
            rank_bias(nb)

    bias_q = bias_ref[...].T.astype(BF16)

    q_sel = jnp.concatenate([q, jnp.concatenate([bias_q] * hg, axis=0)], axis=1)
    m_ref[...] = jnp.full(m_ref.shape, NEG, F32)
    l_ref[...] = jnp.zeros(l_ref.shape, F32)
    acc_ref[...] = jnp.zeros(acc_ref.shape, F32)

    def sel_tiles(tiles):
        for slot, (kb, _) in enumerate(tiles):
            start = pl.multiple_of(kb * SEL_TILE, SEL_TILE)
            keys = jnp.concatenate([ks_ref[0, pl.ds(start, SEL_TILE), :], et_ref[kb]], axis=1)
            s_ref[slot] = _nt_dot(keys, q_sel)
        for slot, (kb, diagonal) in enumerate(tiles):
            s = s_ref[slot]
            if diagonal:
                key = kb * SEL_TILE + lax.broadcasted_iota(jnp.int32, (SEL_TILE, QT), 0)
                s = capped(s, jnp.where(key <= t, BIG, NEG))
            m_prev = m_ref[...]
            m_new = jnp.maximum(m_prev, jnp.max(s, axis=0, keepdims=True))
            alpha = jnp.exp2(m_prev - m_new)
            p = jnp.exp2(s - m_new)
            l_ref[...] = alpha * l_ref[...] + jnp.sum(p, axis=0, keepdims=True)
            acc_ref[...] = alpha * acc_ref[...] + jnp.dot(vst_ref[0, kb], p.astype(BF16),
                                                          preferred_element_type=F32)
            m_ref[...] = m_new

    def past_group(i, carry):
        sel_tiles([(SEL_GROUP * i + slot, False) for slot in range(SEL_GROUP)])
        return carry

    diag = t0 // SEL_TILE
    groups = diag // SEL_GROUP
    lax.fori_loop(0, groups, past_group, 0)
    for rem in range(SEL_GROUP):
        @pl.when(diag - groups * SEL_GROUP == rem)
        def _(rem=rem):
            sel_tiles([(diag - rem + slot, slot == rem) for slot in range(rem + 1)])

    o_s = acc_ref[...] / l_ref[...]

    gate = jax.nn.sigmoid(gl_ref[0, 0])
    for h in range(hg):
        cols = slice(h * QT, (h + 1) * QT)
        r = N_BRANCH * h
        out = gate[r:r + 1] * o_c[:, cols] + gate[r + 1:r + 2] * o_s[:, cols] + gate[r + 2:r + 3] * o_w[:, cols]
        o_ref[:, h * DH:(h + 1) * DH] = out.T.astype(o_ref.dtype)


def _nsa(q, gates, kvc, kv, B, S):
    T, HW = q.shape
    G, DH, QT = NSA_KV_GROUPS, NSA_HEAD_DIM, NSA_Q_TILE
    hg = HW // DH // G
    n_cmp = (S - CMP_BLOCK) // CMP_STRIDE + 1
    ncp = kvc.shape[2]
    n_slc = S // SLC_BLOCK
    top_n = min(SLC_TOPN, n_slc)
    nqt = S // QT
    kvw = G * DH
    assert n_slc <= LANES and S % SEL_TILE == 0 and S >= SWA_SPAN and QT == LANES

    cs = jnp.arange(ncp)[None, :] * CMP_STRIDE
    ss = jnp.arange(LANES)[:, None] * SLC_BLOCK
    overlap = jnp.clip(jnp.minimum(cs + CMP_BLOCK, ss + SLC_BLOCK) - jnp.maximum(cs, ss), 0) // CMP_STRIDE
    overlap = jnp.where((jnp.arange(ncp)[None, :] < n_cmp) & (jnp.arange(LANES)[:, None] < n_slc), overlap, 0)
    overlap = overlap.astype(BF16)
    key_blk = (jnp.arange(S) // SLC_BLOCK).reshape(S // SEL_TILE, SEL_TILE, 1)
    expand = (key_blk == jnp.arange(LANES)[None, None, :]).astype(BF16)

    gr = _round_up(N_BRANCH * hg, 8)
    gl = gates[:, :G * hg * N_BRANCH].reshape(B, nqt, QT, G, hg * N_BRANCH)
    gl = jnp.pad(gl, ((0, 0),) * 4 + ((0, gr - hg * N_BRANCH),)).transpose(0, 3, 1, 4, 2).reshape(B * G, nqt, gr, QT)

    def v_tiles(which, tile):
        v = kv[:, :, which * kvw:(which + 1) * kvw].reshape(B, S // tile, tile, G, DH)
        return v.transpose(0, 3, 1, 4, 2).reshape(B * G, S // tile, DH, tile)

    def k_spec(which):
        return pl.BlockSpec((1, S, DH), lambda bg, i: (bg // G, 0, which * G + bg % G))

    return pl.pallas_call(
        functools.partial(_nsa_kernel, hg=hg, n_cmp=n_cmp, n_slc=n_slc, top_n=top_n),
        grid=(B * G, nqt),
        in_specs=[
            pl.BlockSpec((QT, hg * DH), lambda bg, i: ((bg // G) * nqt + i, bg % G)),
            pl.BlockSpec((1, 1, gr, QT), lambda bg, i: (bg, i, 0, 0)),
            pl.BlockSpec((1, 1, ncp, DH), lambda bg, i: (0, bg, 0, 0)),
            pl.BlockSpec((1, DH, ncp), lambda bg, i: (bg, 0, 0)),
            k_spec(0),
            pl.BlockSpec((1, S // SEL_TILE, DH, SEL_TILE), lambda bg, i: (bg, 0, 0, 0)),
            k_spec(2),
            pl.BlockSpec((1, S // LANES, DH, LANES), lambda bg, i: (bg, 0, 0, 0)),
            pl.BlockSpec((LANES, ncp), lambda bg, i: (0, 0)),
            pl.BlockSpec((S // SEL_TILE, SEL_TILE, LANES), lambda bg, i: (0, 0, 0)),
        ],
        out_specs=pl.BlockSpec((QT, hg * DH), lambda bg, i: ((bg // G) * nqt + i, bg % G)),
        out_shape=jax.ShapeDtypeStruct((T, HW), BF16),
        scratch_shapes=[pltpu.VMEM((1, hg * QT), F32), pltpu.VMEM((1, hg * QT), F32),
                        pltpu.VMEM((DH, hg * QT), F32), pltpu.VMEM((SEL_GROUP, SEL_TILE, hg * QT), F32),
                        pltpu.VMEM((LANES, QT), F32)],
        compiler_params=_cparams("parallel", "arbitrary"),
        name="nsa_attention",
    )(q, gl, kvc, jnp.swapaxes(kvc[1], 1, 2), kv, v_tiles(1, SEL_TILE), kv, v_tiles(3, LANES), overlap, expand)


def _mlstm_kernel(q_ref, kt_ref, v_ref, o_ref, gr_ref, gc_ref, hn_ref, y_ref, c_ref, n_ref, m_ref, *, L, hps):
    ci = pl.program_id(1)
    dk = kt_ref.shape[1] // hps
    dv = v_ref.shape[1] // hps

    @pl.when(ci == 0)
    def _():
        c_ref[...] = jnp.zeros_like(c_ref)
        n_ref[...] = jnp.zeros_like(n_ref)
        m_ref[...] = jnp.zeros_like(m_ref)

    ti = lax.broadcasted_iota(jnp.int32, (L, L), 0)
    si = lax.broadcasted_iota(jnp.int32, (L, L), 1)
    causal = si <= ti

    for hd in range(hps):
        q = q_ref[:, hd * dk:(hd + 1) * dk]
        kt = kt_ref[0, hd * dk:(hd + 1) * dk, :]
        v = v_ref[:, hd * dv:(hd + 1) * dv]
        ig_row = gr_ref[hd, 0, 0:1, :]
        lf_row = jax.nn.log_sigmoid(gr_ref[hd, 0, 1:2, :])
        lf_col = jax.nn.log_sigmoid(gc_ref[hd, 0, :, 1:2])
        m_prev = m_ref[hd, 0:1, 0:1]

        b_col = jnp.sum(jnp.where(causal, lf_row, 0.0), axis=1, keepdims=True)
        b_row = jnp.sum(jnp.where(ti <= si, lf_col, 0.0), axis=0, keepdims=True)
        b_last = b_col[L - 1:L, :]

        dmat = jnp.where(causal, b_col - b_row + ig_row, -jnp.inf)
        a_col = b_col + m_prev
        m_t = jnp.maximum(a_col, jnp.max(dmat, axis=1, keepdims=True))
        wq = jnp.dot(q, kt, preferred_element_type=F32) * jnp.exp(dmat - m_t)
        inter = jnp.exp(a_col - m_t)
        q_c = jnp.dot(q, c_ref[hd].astype(BF16), preferred_element_type=F32)
        q_n = jnp.dot(q, n_ref[hd].astype(BF16), preferred_element_type=F32)[:, 0:1]
        num = inter * q_c + jnp.dot(wq.astype(BF16), v, preferred_element_type=F32)
        den = inter * q_n + jnp.sum(wq, axis=1, keepdims=True)
        h = num / jnp.maximum(jnp.abs(den), jnp.exp(-m_t))
        h = h * lax.rsqrt(jnp.mean(h * h, axis=-1, keepdims=True) + RMS_EPS) * hn_ref[hd]
        y_ref[:, hd * dv:(hd + 1) * dv] = (jax.nn.sigmoid(o_ref[:, hd * dv:(hd + 1) * dv]) * h).astype(y_ref.dtype)

        g_row = b_last - b_row + ig_row
        m_new = jnp.maximum(b_last + m_prev, jnp.max(g_row, axis=1, keepdims=True))
        decay = jnp.exp(b_last + m_prev - m_new)
        kw_t = kt.astype(F32) * jnp.exp(g_row - m_new)
        c_ref[hd] = decay * c_ref[hd] + jnp.dot(kw_t.astype(BF16), v, preferred_element_type=F32)
        n_ref[hd] = decay * n_ref[hd] + jnp.sum(kw_t, axis=1, keepdims=True)
        m_ref[hd] = jnp.broadcast_to(m_new, m_ref.shape[1:])


def _mlstm(q, kt, v, o, gates, head_norm, B, S):
    T = q.shape[0]
    H = MLSTM_HEADS
    dk = q.shape[1] // H
    dv = v.shape[1] // H
    L = min(MLSTM_CHUNK, S)
    nc = S // L
    g = gates[:, :2 * H].reshape(B, nc, L, 2, H)
    g_row = jnp.transpose(g, (0, 4, 1, 3, 2)).reshape(B * H, nc, 2, L)
    g_col = jnp.transpose(g, (0, 4, 1, 2, 3)).reshape(B * H, nc, L, 2)
    hps = MLSTM_HEADS_PER_STEP
    hs = H // hps
    return pl.pallas_call(
        functools.partial(_mlstm_kernel, L=L, hps=hps),
        grid=(B * hs, nc),
        in_specs=[
            pl.BlockSpec((L, hps * dk), lambda bh, c: ((bh // hs) * nc + c, bh % hs)),
            pl.BlockSpec((1, hps * dk, L), lambda bh, c: (bh // hs, bh % hs, c)),
            pl.BlockSpec((L, hps * dv), lambda bh, c: ((bh // hs) * nc + c, bh % hs)),
            pl.BlockSpec((L, hps * dv), lambda bh, c: ((bh // hs) * nc + c, bh % hs)),
            pl.BlockSpec((hps, 1, 2, L), lambda bh, c: (bh, c, 0, 0)),
            pl.BlockSpec((hps, 1, L, 2), lambda bh, c: (bh, c, 0, 0)),
            pl.BlockSpec((hps, 1, dv), lambda bh, c: (bh % hs, 0, 0)),
        ],
        out_specs=pl.BlockSpec((L, hps * dv), lambda bh, c: ((bh // hs) * nc + c, bh % hs)),
        out_shape=jax.ShapeDtypeStruct((T, H * dv), BF16),
        scratch_shapes=[pltpu.VMEM((hps, dk, dv), F32), pltpu.VMEM((hps, dk, LANES), F32),
                        pltpu.VMEM((hps, 8, LANES), F32)],
        compiler_params=_cparams("parallel", "arbitrary"),
        name="mlstm",
    )(q, kt, v, o, g_row, g_col, head_norm.reshape(H, 1, dv))


def _ffn_in_kernel(u_ref, halo_ref, wg_ref, wu_ref, cw_ref, cb_ref, o_ref, lhs_ref, g_ref, tail_ref, *, tm, tn, ff,
                   sh, over, tiles_per_seq):
    i = pl.program_id(0)
    j = pl.program_id(1)

    @pl.when(j == 0)
    def _():
        halo = halo_ref[...]
        lhs_ref[0:CONV_HALO, :] = jnp.where(i % tiles_per_seq != 0, halo, jnp.zeros_like(halo))
        lhs_ref[CONV_HALO:CONV_HALO + tm, :] = u_ref[...]
        tail_ref[...] = jnp.zeros_like(tail_ref)

    g_ref[...] = jnp.dot(lhs_ref[...], wg_ref[...], preferred_element_type=F32)
    up = jnp.dot(u_ref[...], wu_ref[0], preferred_element_type=F32)
    if over:
        moved = jnp.concatenate([up[:, over:], up[:, :over]], axis=1)
        up = jnp.where(j == pl.num_programs(1) - 1, moved, up)
    gate = cb_ref[...] + cw_ref[0:1, :] * g_ref[CONV_HALO:CONV_HALO + tm, :]
    for back in range(1, CONV_WIDTH):
        gate = gate + cw_ref[back:back + 1, :] * g_ref[CONV_HALO - back:CONV_HALO - back + tm, :]
    if sh:
        lane = lax.broadcasted_iota(jnp.int32, (1, LANES), 1)
        tiles = [tail_ref[...]] + [gate[:, c * LANES:(c + 1) * LANES] for c in range(tn // LANES)]
        tail_ref[...] = tiles[-1]
        rolled = [pltpu.roll(x, sh, axis=1) for x in tiles]
        gate = jnp.concatenate([jnp.where(lane < sh, rolled[c], rolled[c + 1]) for c in range(tn // LANES)], axis=1)
    act = gate * jax.nn.sigmoid(gate) * up
    hidden = j * tn - sh + lax.broadcasted_iota(jnp.int32, (1, tn), 1)
    o_ref[...] = jnp.where((hidden >= 0) & (hidden < ff), act, 0.0).astype(o_ref.dtype)


def _ffn_in(u, w_in, conv_w, conv_b, layer, ff, sh, fp, S, tm=1024, tn=FFN_COL_TILE):
    T, D = u.shape
    tm = min(tm, S)
    assert T % tm == 0 and S % tm == 0 and fp % tn == 0 and tn % LANES == 0
    hb = tm // CONV_HALO
    up0 = (ff - sh) // LANES
    up_last = (w_in.shape[-1] - tn) // LANES
    over = max(up0 + (fp // tn - 1) * (tn // LANES) - up_last, 0) * LANES
    assert w_in.shape[-1] % LANES == 0 and over < tn
    return pl.pallas_call(
        functools.partial(_ffn_in_kernel, tm=tm, tn=tn, ff=ff, sh=sh, over=over, tiles_per_seq=S // tm),
        grid=(T // tm, fp // tn),
        in_specs=[
            pl.BlockSpec((tm, D), lambda i, j: (i, 0)),
            pl.BlockSpec((CONV_HALO, D), lambda i, j: (jnp.maximum(i * hb - 1, 0), 0)),
            pl.BlockSpec((None, D, tn), lambda i, j: (layer, 0, j)),
            pl.BlockSpec((pl.Element(1), pl.Element(D), pl.Element(tn)),
                         lambda i, j: (layer, 0, jnp.minimum(up0 + j * (tn // LANES), up_last) * LANES)),
            pl.BlockSpec((None, CONV_WIDTH, tn), lambda i, j: (layer, 0, j)),
            pl.BlockSpec((None, 1, tn), lambda i, j: (layer, 0, j)),
        ],
        out_specs=pl.BlockSpec((tm, tn), lambda i, j: (i, j)),
        out_shape=jax.ShapeDtypeStruct((T, fp), BF16),
        scratch_shapes=[pltpu.VMEM((CONV_HALO + tm, D), BF16), pltpu.VMEM((CONV_HALO + tm, tn), F32),
                        pltpu.VMEM((tm, LANES), F32)],
        compiler_params=_cparams("parallel", "arbitrary"),
        name="ffn_in_conv_act",
    )(u, u, w_in, w_in, conv_w, conv_b)


def _ffn_weights(w_in, w_out):
    depth, D, ff2 = w_in.shape
    ff = ff2 // 2
    sh = ff % LANES
    fp = _round_up(ff + sh, FFN_COL_TILE)
    w_in = w_in.astype(BF16)
    w_out = jnp.concatenate([jnp.zeros((depth, sh, D), BF16), w_out.astype(BF16),
                             jnp.zeros((depth, fp - ff - sh, D), BF16)], axis=1)
    return w_in, w_out, ff, sh, fp


def _conv_ffn(h, u, ln_post, next_gain, w_in, w_out, ff, sh, fp, conv_w, conv_b, layer, S):
    act = _ffn_in(u, w_in, conv_w, conv_b, layer, ff, sh, fp, S)
    return _mm_norm_res([act], w_out, h, ln_post, next_gain, w_lead=layer)


def _ab_layer(h, u, ln_post, next_gain, w_in, pool_w, pool_scale, cmp_pos, ck_w1, ck_w2, cv_w1, cv_w2, w_out, B, S):
    T, D = h.shape
    G, DH = NSA_KV_GROUPS, NSA_HEAD_DIM
    pw = D // 4
    hw = D - pw
    kvw = G * DH
    w = w_in.astype(BF16)
    c0 = pw + hw
    p_in = _mm(u, w, F32, 0, pw)
    q = _mm(u, w, BF16, pw, hw, scale=DH ** -0.5 * LOG2E)
    kv_cmp = _mm(u, w, F32, c0, 2 * kvw)
    kv = _mm(u, w, BF16, c0 + 2 * kvw, 4 * kvw)
    gates = _mm(u, w, F32, c0 + 6 * kvw, LANES)

    y_a = _pool_mixer(p_in.reshape(B, S, pw), pool_w, pool_scale).reshape(T, pw)
    kvc = _compress(kv_cmp.reshape(B, S, 2 * kvw), cmp_pos, jnp.stack([ck_w1, cv_w1]), jnp.stack([ck_w2, cv_w2]), G)
    y_b = _nsa(q, gates, kvc, kv.reshape(B, S, 4 * kvw), B, S)
    return _mm_norm_res([y_a, y_b], w_out.astype(BF16), h, ln_post, next_gain)


def _c_layer(h, u, ln_post, next_gain, w_in, b_if, head_norm, w_out, B, S):
    T, D = h.shape
    H = MLSTM_HEADS
    dv = D // H
    dk = dv // 2
    qk = H * dk
    w = w_in.astype(BF16)
    q = _mm(u, w, BF16, 0, qk, scale=dk ** -0.5)
    k = _mm(u, w, BF16, qk, qk)
    v = _mm(u, w, BF16, 2 * qk, D)
    o = _mm(u, w, F32, 2 * qk + D, D)
    gates = _mm(u, w, F32, 2 * qk + 2 * D, LANES, bias=_pad_cols(b_if.reshape(1, 2 * H), LANES))
    kt = k.reshape(B, S, qk).transpose(0, 2, 1)
    y = _mlstm(q, kt, v, o, gates, head_norm, B, S)
    return _mm_norm_res([y], w_out.astype(BF16), h, ln_post, next_gain)


def kernel(x, ln_pre, ln_post, w_in_ab, pool_w, pool_scale, cmp_pos, cmp_k_w1, cmp_k_w2, cmp_v_w1, cmp_v_w2,
           w_out_ab, w_in_c, b_if_c, head_norm_c, w_out_c, ffn_ln_pre, ffn_ln_post, ffn_w_in, ffn_conv_w,
           ffn_conv_b, ffn_w_out):
    B, S, D = x.shape
    depth = ln_pre.shape[0]
    h = x.reshape(B * S, D)
    u = _rmsnorm(h, ln_pre[0])
    ffn_wi, ffn_wo, ff, sh, fp = _ffn_weights(ffn_w_in, ffn_w_out)
    ffn_cb = ffn_conv_b.reshape(depth, 1, -1)
    for layer in range(depth):
        i = layer // 2
        if layer % 2 == 0:
            h, u = _ab_layer(h, u, ln_post[layer], ffn_ln_pre[layer], w_in_ab[i], pool_w[i], pool_scale[i],
                             cmp_pos[i], cmp_k_w1[i], cmp_k_w2[i], cmp_v_w1[i], cmp_v_w2[i], w_out_ab[i], B, S)
        else:
            h, u = _c_layer(h, u, ln_post[layer], ffn_ln_pre[layer], w_in_c[i], b_if_c[i], head_norm_c[i],
                            w_out_c[i], B, S)
        next_gain = ln_pre[layer + 1] if layer + 1 < depth else None
        h, u = _conv_ffn(h, u, ffn_ln_post[layer], next_gain, ffn_wi, ffn_wo, ff, sh, fp, ffn_conv_w, ffn_cb, layer,
                         S)
    return h.reshape(B, S, D)
```

```python
import functools
import math

import jax
import jax.numpy as jnp
from jax import lax
from jax.experimental import pallas as pl
from jax.experimental.pallas import tpu as pltpu

F32 = jnp.float32
BF16 = jnp.bfloat16

RMS_EPS = 1e-6
POOL_GROUPS = 4
POOL_WINDOWS = (2, 4, 8, 16)
POOL_HALO = 16
NSA_HEAD_DIM = 128
NSA_KV_GROUPS = 4
N_BRANCH = 3
CMP_BLOCK = 32
CMP_STRIDE = 16
SLC_BLOCK = 64
SLC_TOPN = 16
SWA_WINDOW = 512
FORCE_BONUS = 1e4
NEG = -1e30
BIG = 1e30
MLSTM_HEADS = 8
CONV_WIDTH = 3
CONV_HALO = 16

LANES = 128
SUBLANES = 8
LOG2E = math.log2(math.e)
NSA_Q_TILE = LANES
SEL_TILE = 512
SEL_GROUP = 4
RANK_SIZES = 8
SWA_SPAN = SWA_WINDOW + NSA_Q_TILE
MLSTM_CHUNK = 256
MLSTM_HEADS_PER_STEP = 4
FFN_COL_TILE = 512
VMEM_LIMIT = 56 * 1024 * 1024


def _cparams(*sem):
    return pltpu.CompilerParams(dimension_semantics=sem, vmem_limit_bytes=VMEM_LIMIT)


def _round_up(n, m):
    return (n + m - 1) // m * m


def _pad_cols(w, n):
    return jnp.pad(w, ((0, 0), (0, n - w.shape[1])))


def _rmsnorm_kernel(x_ref, g_ref, o_ref):
    x = x_ref[...]
    ms = jnp.mean(x * x, axis=-1, keepdims=True)
    o_ref[...] = (x * lax.rsqrt(ms + RMS_EPS) * g_ref[...]).astype(o_ref.dtype)


def _rmsnorm(x, gain, tm=512):
    T, D = x.shape
    return pl.pallas_call(
        _rmsnorm_kernel,
        grid=(T // tm,),
        in_specs=[pl.BlockSpec((tm, D), lambda i: (i, 0)), pl.BlockSpec((1, D), lambda i: (0, 0))],
        out_specs=pl.BlockSpec((tm, D), lambda i: (i, 0)),
        out_shape=jax.ShapeDtypeStruct((T, D), BF16),
        compiler_params=_cparams("parallel"),
        name="rmsnorm",
    )(x, gain.reshape(1, D))


def _mm_kernel(a_ref, b_ref, bias_ref, o_ref, *, scale, live_cols):
    acc = jnp.dot(a_ref[...], b_ref[...], preferred_element_type=F32)
    out = (acc + bias_ref[...]) * scale
    if live_cols is not None:
        col = pl.program_id(1) * out.shape[1] + lax.broadcasted_iota(jnp.int32, (1, out.shape[1]), 1)
        out = jnp.where(col < live_cols, out, 0.0)
    o_ref[...] = out.astype(o_ref.dtype)


def _mm(a, w, out_dtype, col0, ncols, *, bias=None, scale=1.0, tm=1024, tn=1024):
    M, K = a.shape
    tm = min(tm, M)
    tn = math.gcd(tn, ncols, col0)
    assert M % tm == 0 and tn % LANES == 0
    jb = col0 // tn
    live_cols = w.shape[1] - col0 if col0 + ncols > w.shape[1] else None
    if bias is None:
        bias = jnp.zeros((1, ncols), F32)
    return pl.pallas_call(
        functools.partial(_mm_kernel, scale=scale, live_cols=live_cols),
        grid=(M // tm, ncols // tn),
        in_specs=[
            pl.BlockSpec((tm, K), lambda i, j: (i, 0)),
            pl.BlockSpec((K, tn), lambda i, j: (0, j + jb)),
            pl.BlockSpec((1, tn), lambda i, j: (0, j)),
        ],
        out_specs=pl.BlockSpec((tm, tn), lambda i, j: (i, j)),
        out_shape=jax.ShapeDtypeStruct((M, ncols), out_dtype),
        compiler_params=_cparams("parallel", "arbitrary"),
        name="proj",
    )(a, w, bias)


def _mm_norm_res_kernel(*refs, nk, nk1, n_a, emit_u, tm, rows):
    a_refs = refs[:n_a]
    w_ref, h_hbm, g_ref = refs[n_a:n_a + 3]
    rest = refs[n_a + 3:]
    if emit_u:
        ng_ref, o_ref, u_ref, h_buf, h_sem = rest
    else:
        o_ref, h_buf, h_sem = rest
    i = pl.program_id(0)
    k = pl.program_id(1)

    def h_copy():
        return pltpu.make_async_copy(h_hbm.at[pl.ds(i * tm, tm), :], h_buf, h_sem)

    def product(a_ref):
        return jnp.dot(a_ref[...], w_ref[...], preferred_element_type=F32)

    @pl.when(k == 0)
    def _():
        h_copy().start()
        o_ref[...] = product(a_refs[0])

    if nk1 > 1 or n_a == 1:
        @pl.when((k > 0) & (k < nk1))
        def _():
            o_ref[...] += product(a_refs[0])

    if n_a == 2:
        @pl.when(k >= nk1)
        def _():
            o_ref[...] += product(a_refs[1])

    @pl.when(k == nk - 1)
    def _():
        h_copy().wait()

        def norm_rows(c, carry):
            r = pl.ds(pl.multiple_of(c * rows, rows), rows)
            y = o_ref[r, :]
            ms = jnp.mean(y * y, axis=-1, keepdims=True)
            h_new = h_buf[r, :] + y * lax.rsqrt(ms + RMS_EPS) * g_ref[...]
            o_ref[r, :] = h_new
            if emit_u:
                ms2 = jnp.mean(h_new * h_new, axis=-1, keepdims=True)
                u_ref[r, :] = (h_new * lax.rsqrt(ms2 + RMS_EPS) * ng_ref[...]).astype(u_ref.dtype)
            return carry

        lax.fori_loop(0, tm // rows, norm_rows, 0)


def _mm_norm_res(a_list, w, h, gain, next_gain=None, w_lead=None, tm=512, tk=1024, rows=64):
    M = h.shape[0]
    N = w.shape[-1]
    n_a = len(a_list)
    tk = math.gcd(tk, *[a.shape[1] for a in a_list])
    assert n_a in (1, 2) and M % tm == 0 and tm % rows == 0
    nk1 = a_list[0].shape[1] // tk
    nk = sum(a.shape[1] for a in a_list) // tk
    emit_u = next_gain is not None
    a_specs = [pl.BlockSpec((tm, tk), lambda i, k: (i, jnp.minimum(k, nk1 - 1)))]
    if n_a == 2:
        a_specs.append(pl.BlockSpec((tm, tk), lambda i, k: (i, jnp.maximum(k - nk1, 0))))
    row_spec = pl.BlockSpec((tm, N), lambda i, k: (i, 0))
    vec_spec = pl.BlockSpec((1, N), lambda i, k: (0, 0))
    if w_lead is None:
        w_spec = pl.BlockSpec((tk, N), lambda i, k: (k, 0))
    else:
        w_spec = pl.BlockSpec((None, tk, N), lambda i, k: (w_lead, k, 0))
    in_specs = a_specs + [w_spec, pl.BlockSpec(memory_space=pl.ANY), vec_spec]
    args = list(a_list) + [w, h, gain.reshape(1, N)]
    out_specs, out_shape = row_spec, jax.ShapeDtypeStruct((M, N), F32)
    if emit_u:
        in_specs.append(vec_spec)
        args.append(next_gain.reshape(1, N))
        out_specs = [row_spec, row_spec]
        out_shape = [out_shape, jax.ShapeDtypeStruct((M, N), BF16)]
    out = pl.pallas_call(
        functools.partial(_mm_norm_res_kernel, nk=nk, nk1=nk1, n_a=n_a, emit_u=emit_u, tm=tm, rows=rows),
        grid=(M // tm, nk),
        in_specs=in_specs,
        out_specs=out_specs,
        out_shape=out_shape,
        scratch_shapes=[pltpu.VMEM((tm, N), F32), pltpu.SemaphoreType.DMA(())],
        compiler_params=_cparams("parallel", "arbitrary"),
        name="out_proj_norm_res",
    )(*args)
    return out if emit_u else (out, None)


def _pool_kernel(cur_ref, prev_ref, w_ref, scale_ref, o_ref, ext_ref, *, ts, gd):
    i = pl.program_id(1)
    ext_ref[0:POOL_HALO, :] = jnp.where(i > 0, prev_ref[0], 0.0)
    ext_ref[POOL_HALO:POOL_HALO + ts, :] = cur_ref[0]
    t = i * ts + lax.broadcasted_iota(jnp.int32, (ts, 1), 0)
    for g, win in enumerate(POOL_WINDOWS):
        cols = slice(g * gd, (g + 1) * gd)
        tok = ext_ref[POOL_HALO:POOL_HALO + ts, cols]
        acc = tok
        for back in range(1, win):
            acc = acc + ext_ref[POOL_HALO - back:POOL_HALO - back + ts, cols]
        count = jnp.minimum(t + 1, win).astype(F32)
        pooled = acc / count - tok
        y = jnp.dot(pooled.astype(BF16), w_ref[g], preferred_element_type=F32)
        o_ref[0, :, cols] = (y * scale_ref[:, cols]).astype(o_ref.dtype)


def _pool_mixer(p_in, w_pool, scale, ts=512):
    B, S, PW = p_in.shape
    gd = PW // POOL_GROUPS
    hb = ts // POOL_HALO
    return pl.pallas_call(
        functools.partial(_pool_kernel, ts=ts, gd=gd),
        grid=(B, S // ts),
        in_specs=[
            pl.BlockSpec((1, ts, PW), lambda b, i: (b, i, 0)),
            pl.BlockSpec((1, POOL_HALO, PW), lambda b, i: (b, jnp.maximum(i * hb - 1, 0), 0)),
            pl.BlockSpec((POOL_GROUPS, gd, gd), lambda b, i: (0, 0, 0)),
            pl.BlockSpec((1, PW), lambda b, i: (0, 0)),
        ],
        out_specs=pl.BlockSpec((1, ts, PW), lambda b, i: (b, i, 0)),
        out_shape=jax.ShapeDtypeStruct((B, S, PW), BF16),
        scratch_shapes=[pltpu.VMEM((POOL_HALO + ts, PW), F32)],
        compiler_params=_cparams("parallel", "arbitrary"),
        name="pool_mixer",
    )(p_in, p_in, w_pool.astype(BF16), scale.reshape(1, PW))


def _gelu_tanh(x):
    c = math.sqrt(2.0 / math.pi)
    return x * (0.5 * (1.0 + jnp.tanh(c * (x + 0.044715 * (x * x * x)))))


def _compress_kernel(x_ref, pos_ref, w1_ref, w2_ref, o_ref, tmp_ref, *, nch):
    half = CMP_BLOCK // 2
    dh = x_ref.shape[2]
    a = b = None
    for r in range(half):
        x = x_ref[0, pl.ds(r, nch, stride=CMP_STRIDE), :]
        lo = jnp.dot((x + pos_ref[r:r + 1, :]).astype(BF16), w1_ref[0, r * dh:(r + 1) * dh, :],
                     preferred_element_type=F32)
        hi = jnp.dot((x + pos_ref[half + r:half + r + 1, :]).astype(BF16),
                     w1_ref[0, (half + r) * dh:(half + r + 1) * dh, :], preferred_element_type=F32)
        a = lo if a is None else a + lo
        b = hi if b is None else b + hi
    tmp_ref[0:nch, :] = b
    tmp_ref[nch:nch + SUBLANES, :] = jnp.zeros((SUBLANES, b.shape[1]), F32)
    pre = a + tmp_ref[1:nch + 1, :]
    out = jnp.dot(_gelu_tanh(pre).astype(BF16), w2_ref[0], preferred_element_type=F32)
    row = lax.broadcasted_iota(jnp.int32, out.shape, 0)
    o_ref[0, 0] = jnp.where(row < nch - 1, out, 0.0).astype(o_ref.dtype)


def _compress(x, pos, w1, w2, G):
    B, S, _ = x.shape
    hid = w1.shape[-1]
    dh = w2.shape[-1]
    nch = S // CMP_STRIDE
    assert CMP_BLOCK == 2 * CMP_STRIDE
    return pl.pallas_call(
        functools.partial(_compress_kernel, nch=nch),
        grid=(2, B * G),
        in_specs=[
            pl.BlockSpec((1, S, dh), lambda s, b: (b // G, 0, s * G + b % G)),
            pl.BlockSpec((CMP_BLOCK, dh), lambda s, b: (0, 0)),
            pl.BlockSpec((1, CMP_BLOCK * dh, hid), lambda s, b: (s, 0, 0)),
            pl.BlockSpec((1, hid, dh), lambda s, b: (s, 0, 0)),
        ],
        out_specs=pl.BlockSpec((1, 1, nch, dh), lambda s, b: (s, b, 0, 0)),
        out_shape=jax.ShapeDtypeStruct((2, B * G, nch, dh), BF16),
        scratch_shapes=[pltpu.VMEM((nch + SUBLANES, hid), F32)],
        compiler_params=_cparams("parallel", "arbitrary"),
        name="nsa_compress",
    )(x, pos, w1.astype(BF16), w2.astype(BF16))


def _nt_dot(a, b):
    return lax.dot_general(a, b, (((1,), (1,)), ((), ())), preferred_element_type=F32)


def _split_dot(w, x):
    hi = x.astype(BF16)
    r1 = x - hi.astype(F32)
    mid = r1.astype(BF16)
    lo = (r1 - mid.astype(F32)).astype(BF16)
    return (jnp.dot(w, hi, preferred_element_type=F32) + jnp.dot(w, mid, preferred_element_type=F32)
            + jnp.dot(w, lo, preferred_element_type=F32))


def _nsa_kernel(q_ref, gl_ref, kc_ref, vct_ref, ks_ref, vst_ref, kw_ref, vwt_ref, ovt_ref, et_ref, o_ref,
                m_ref, l_ref, acc_ref, s_ref, bias_ref, *, hg, n_cmp, n_slc, top_n):
    QT, DH = NSA_Q_TILE, NSA_HEAD_DIM
    t0 = pl.program_id(1) * QT
    q = jnp.concatenate([q_ref[:, h * DH:(h + 1) * DH] for h in range(hg)], axis=0)
    lane = lax.broadcasted_iota(jnp.int32, (1, QT), 1)
    t = t0 + lane

    def capped(s, cap):
        return jnp.concatenate([jnp.minimum(s[:, h * QT:(h + 1) * QT], cap) for h in range(hg)], axis=1)

    def softmax_cols(s):
        p = jnp.exp2(s - jnp.max(s, axis=0, keepdims=True))
        return p, jnp.sum(p, axis=0, keepdims=True)

    ncp = kc_ref.shape[2]
    nrow = lax.broadcasted_iota(jnp.int32, (ncp, QT), 0)
    valid_c = (nrow * CMP_STRIDE + (CMP_BLOCK - 1) <= t) & (nrow < n_cmp)
    s_c = capped(_nt_dot(kc_ref[0, 0], q), jnp.where(valid_c, BIG, NEG))
    ws = pl.multiple_of(jnp.maximum(t0 + QT - SWA_SPAN, 0), LANES)
    wb = ws // LANES
    dist = t - (ws + lax.broadcasted_iota(jnp.int32, (SWA_SPAN, QT), 0))
    valid_w = (dist >= 0) & (dist < SWA_WINDOW)
    s_w = capped(_nt_dot(kw_ref[0, pl.ds(ws, SWA_SPAN), :], q), jnp.where(valid_w, BIG, NEG))

    p_c, l_c = softmax_cols(s_c)
    seen = jnp.concatenate([jnp.where(t >= CMP_BLOCK - 1, 1.0, 0.0)] * hg, axis=1)
    p_c = p_c * (seen / l_c)
    o_c = jnp.dot(vct_ref[0], p_c.astype(BF16), preferred_element_type=F32)

    p_w, l_w = softmax_cols(s_w)
    vwin = jnp.concatenate([vwt_ref[0, wb + j] for j in range(SWA_SPAN // LANES)], axis=1)
    o_w = jnp.dot(vwin, p_w.astype(BF16), preferred_element_type=F32) / l_w

    p_sum = p_c[:, 0:QT]
    for h in range(1, hg):
        p_sum = p_sum + p_c[:, h * QT:(h + 1) * QT]
    imp = _split_dot(ovt_ref[...], p_sum)
    blk = lax.broadcasted_iota(jnp.int32, (LANES, QT), 0)
    cur = t0 // SLC_BLOCK + jnp.zeros((1, QT), jnp.int32)
    for k in range(1, QT // SLC_BLOCK):
        cur = cur + jnp.where(lane >= k * SLC_BLOCK, 1, 0)
    forced = (blk == 0) | (blk == cur) | (blk == cur - 1)
    val = jnp.where(forced, imp + FORCE_BONUS, jnp.where(blk > cur, -FORCE_BONUS, imp))
    if n_slc < LANES:
        val = jnp.where(blk < n_slc, val, -jnp.inf)

    def rank_bias(nb):
        SL = SUBLANES
        nslab = nb // SL
        slabs = [val[SL * r:SL * r + SL, :] for r in range(nslab)]
        ranks = [jnp.zeros((SL, QT), F32) for _ in range(nslab)]
        sub = lax.broadcasted_iota(jnp.int32, (SL, QT), 0)
        for other in range(nb):
            c = val[other:other + 1, :]
            for r in range(nslab):
                if SL * r > other:
                    beat = c >= slabs[r]
                elif SL * r + SL - 1 < other:
                    beat = c > slabs[r]
                else:
                    beat = (c > slabs[r]) | ((c == slabs[r]) & (sub + SL * r > other))
                ranks[r] = ranks[r] + jnp.where(beat, 1.0, 0.0)
        for r in range(nslab):
            chosen = (ranks[r] < top_n) & (blk[SL * r:SL * r + SL] <= cur)
            bias_ref[SL * r:SL * r + SL, :] = jnp.where(chosen, 0.0, NEG)
        if nb < LANES:
            bias_ref[nb:LANES, :] = jnp.full((LANES - nb, QT), NEG, F32)

    live = (t0 + QT - 1) // SLC_BLOCK + 1
    sizes = sorted({min(_round_up(-(-n_slc * k // RANK_SIZES), SUBLANES), LANES) for k in range(1, RANK_SIZES + 1)})
    for lo, nb in zip([0] + sizes[:-1], sizes):
        @pl.when((live > lo) & (live <= nb))
        def _(nb=nb):
            rank_bias(nb)

    bias_q = bias_ref[...].T.astype(BF16)

    q_sel = jnp.concatenate([q, jnp.concatenate([bias_q] * hg, axis=0)], axis=1)
    m_ref[...] = jnp.full(m_ref.shape, NEG, F32)
    l_ref[...] = jnp.zeros(l_ref.shape, F32)
    acc_ref[...] = jnp.zeros(acc_ref.shape, F32)

    def sel_tiles(tiles):
        for slot, (kb, _) in enumerate(tiles):
            start = pl.multiple_of(kb * SEL_TILE, SEL_TILE)
            keys = jnp.concatenate([ks_ref[0, pl.ds(start, SEL_TILE), :], et_ref[kb]], axis=1)
            s_ref[slot] = _nt_dot(keys, q_sel)
        for slot, (kb, diagonal) in enumerate(tiles):
            s = s_ref[slot]
            if diagonal:
                key = kb * SEL_TILE + lax.broadcasted_iota(jnp.int32, (SEL_TILE, QT), 0)
                s = capped(s, jnp.where(key <= t, BIG, NEG))
            m_prev = m_ref[...]
            m_new = jnp.maximum(m_prev, jnp.max(s, axis=0, keepdims=True))
            alpha = jnp.exp2(m_prev - m_new)
            p = jnp.exp2(s - m_new)
            l_ref[...] = alpha * l_ref[...] + jnp.sum(p, axis=0, keepdims=True)
            acc_ref[...] = alpha * acc_ref[...] + jnp.dot(vst_ref[0, kb], p.astype(BF16),
                                                          preferred_element_type=F32)
            m_ref[...] = m_new

    def past_group(i, carry):
        sel_tiles([(SEL_GROUP * i + slot, False) for slot in range(SEL_GROUP)])
        return carry

    diag = t0 // SEL_TILE
    groups = diag // SEL_GROUP
    lax.fori_loop(0, groups, past_group, 0)
    for rem in range(SEL_GROUP):
        @pl.when(diag - groups * SEL_GROUP == rem)
        def _(rem=rem):
            sel_tiles([(diag - rem + slot, slot == rem) for slot in range(rem + 1)])

    o_s = acc_ref[...] / l_ref[...]

    gate = jax.nn.sigmoid(gl_ref[0, 0])
    for h in range(hg):
        cols = slice(h * QT, (h + 1) * QT)
        r = N_BRANCH * h
        out = gate[r:r + 1] * o_c[:, cols] + gate[r + 1:r + 2] * o_s[:, cols] + gate[r + 2:r + 3] * o_w[:, cols]
        o_ref[:, h * DH:(h + 1) * DH] = out.T.astype(o_ref.dtype)


def _nsa(q, gates, kvc, kv, B, S):
    T, HW = q.shape
    G, DH, QT = NSA_KV_GROUPS, NSA_HEAD_DIM, NSA_Q_TILE
    hg = HW // DH // G
    n_cmp = (S - CMP_BLOCK) // CMP_STRIDE + 1
    ncp = kvc.shape[2]
    n_slc = S // SLC_BLOCK
    top_n = min(SLC_TOPN, n_slc)
    nqt = S // QT
    kvw = G * DH
    assert n_slc <= LANES and S % SEL_TILE == 0 and S >= SWA_SPAN and QT == LANES

    cs = jnp.arange(ncp)[None, :] * CMP_STRIDE
    ss = jnp.arange(LANES)[:, None] * SLC_BLOCK
    overlap = jnp.clip(jnp.minimum(cs + CMP_BLOCK, ss + SLC_BLOCK) - jnp.maximum(cs, ss), 0) // CMP_STRIDE
    overlap = jnp.where((jnp.arange(ncp)[None, :] < n_cmp) & (jnp.arange(LANES)[:, None] < n_slc), overlap, 0)
    overlap = overlap.astype(BF16)
    key_blk = (jnp.arange(S) // SLC_BLOCK).reshape(S // SEL_TILE, SEL_TILE, 1)
    expand = (key_blk == jnp.arange(LANES)[None, None, :]).astype(BF16)

    gr = _round_up(N_BRANCH * hg, 8)
    gl = gates[:, :G * hg * N_BRANCH].reshape(B, nqt, QT, G, hg * N_BRANCH)
    gl = jnp.pad(gl, ((0, 0),) * 4 + ((0, gr - hg * N_BRANCH),)).transpose(0, 3, 1, 4, 2).reshape(B * G, nqt, gr, QT)

    def v_tiles(which, tile):
        v = kv[:, :, which * kvw:(which + 1) * kvw].reshape(B, S // tile, tile, G, DH)
        return v.transpose(0, 3, 1, 4, 2).reshape(B * G, S // tile, DH, tile)

    def k_spec(which):
        return pl.BlockSpec((1, S, DH), lambda bg, i: (bg // G, 0, which * G + bg % G))

    return pl.pallas_call(
        functools.partial(_nsa_kernel, hg=hg, n_cmp=n_cmp, n_slc=n_slc, top_n=top_n),
        grid=(B * G, nqt),
        in_specs=[
            pl.BlockSpec((QT, hg * DH), lambda bg, i: ((bg // G) * nqt + i, bg % G)),
            pl.BlockSpec((1, 1, gr, QT), lambda bg, i: (bg, i, 0, 0)),
            pl.BlockSpec((1, 1, ncp, DH), lambda bg, i: (0, bg, 0, 0)),
            pl.BlockSpec((1, DH, ncp), lambda bg, i: (bg, 0, 0)),
            k_spec(0),
            pl.BlockSpec((1, S // SEL_TILE, DH, SEL_TILE), lambda bg, i: (bg, 0, 0, 0)),
            k_spec(2),
            pl.BlockSpec((1, S // LANES, DH, LANES), lambda bg, i: (bg, 0, 0, 0)),
            pl.BlockSpec((LANES, ncp), lambda bg, i: (0, 0)),
            pl.BlockSpec((S // SEL_TILE, SEL_TILE, LANES), lambda bg, i: (0, 0, 0)),
        ],
        out_specs=pl.BlockSpec((QT, hg * DH), lambda bg, i: ((bg // G) * nqt + i, bg % G)),
        out_shape=jax.ShapeDtypeStruct((T, HW), BF16),
        scratch_shapes=[pltpu.VMEM((1, hg * QT), F32), pltpu.VMEM((1, hg * QT), F32),
                        pltpu.VMEM((DH, hg * QT), F32), pltpu.VMEM((SEL_GROUP, SEL_TILE, hg * QT), F32),
                        pltpu.VMEM((LANES, QT), F32)],
        compiler_params=_cparams("parallel", "arbitrary"),
        name="nsa_attention",
    )(q, gl, kvc, jnp.swapaxes(kvc[1], 1, 2), kv, v_tiles(1, SEL_TILE), kv, v_tiles(3, LANES), overlap, expand)


def _mlstm_kernel(q_ref, kt_ref, v_ref, o_ref, gr_ref, gc_ref, hn_ref, y_ref, c_ref, n_ref, m_ref, *, L, hps):
    ci = pl.program_id(1)
    dk = kt_ref.shape[1] // hps
    dv = v_ref.shape[1] // hps

    @pl.when(ci == 0)
    def _():
        c_ref[...] = jnp.zeros_like(c_ref)
        n_ref[...] = jnp.zeros_like(n_ref)
        m_ref[...] = jnp.zeros_like(m_ref)

    ti = lax.broadcasted_iota(jnp.int32, (L, L), 0)
    si = lax.broadcasted_iota(jnp.int32, (L, L), 1)
    causal = si <= ti

    for hd in range(hps):
        q = q_ref[:, hd * dk:(hd + 1) * dk]
        kt = kt_ref[0, hd * dk:(hd + 1) * dk, :]
        v = v_ref[:, hd * dv:(hd + 1) * dv]
        ig_row = gr_ref[hd, 0, 0:1, :]
        lf_row = jax.nn.log_sigmoid(gr_ref[hd, 0, 1:2, :])
        lf_col = jax.nn.log_sigmoid(gc_ref[hd, 0, :, 1:2])
        m_prev = m_ref[hd, 0:1, 0:1]

        b_col = jnp.sum(jnp.where(causal, lf_row, 0.0), axis=1, keepdims=True)
        b_row = jnp.sum(jnp.where(ti <= si, lf_col, 0.0), axis=0, keepdims=True)
        b_last = b_col[L - 1:L, :]

        dmat = jnp.where(causal, b_col - b_row + ig_row, -jnp.inf)
        a_col = b_col + m_prev
        m_t = jnp.maximum(a_col, jnp.max(dmat, axis=1, keepdims=True))
        wq = jnp.dot(q, kt, preferred_element_type=F32) * jnp.exp(dmat - m_t)
        inter = jnp.exp(a_col - m_t)
        q_c = jnp.dot(q, c_ref[hd].astype(BF16), preferred_element_type=F32)
        q_n = jnp.dot(q, n_ref[hd].astype(BF16), preferred_element_type=F32)[:, 0:1]
        num = inter * q_c + jnp.dot(wq.astype(BF16), v, preferred_element_type=F32)
        den = inter * q_n + jnp.sum(wq, axis=1, keepdims=True)
        h = num / jnp.maximum(jnp.abs(den), jnp.exp(-m_t))
        h = h * lax.rsqrt(jnp.mean(h * h, axis=-1, keepdims=True) + RMS_EPS) * hn_ref[hd]
        y_ref[:, hd * dv:(hd + 1) * dv] = (jax.nn.sigmoid(o_ref[:, hd * dv:(hd + 1) * dv]) * h).astype(y_ref.dtype)

        g_row = b_last - b_row + ig_row
        m_new = jnp.maximum(b_last + m_prev, jnp.max(g_row, axis=1, keepdims=True))
        decay = jnp.exp(b_last + m_prev - m_new)
        kw_t = kt.astype(F32) * jnp.exp(g_row - m_new)
        c_ref[hd] = decay * c_ref[hd] + jnp.dot(kw_t.astype(BF16), v, preferred_element_type=F32)
        n_ref[hd] = decay * n_ref[hd] + jnp.sum(kw_t, axis=1, keepdims=True)
        m_ref[hd] = jnp.broadcast_to(m_new, m_ref.shape[1:])


def _mlstm(q, kt, v, o, gates, head_norm, B, S):
    T = q.shape[0]
    H = MLSTM_HEADS
    dk = q.shape[1] // H
    dv = v.shape[1] // H
    L = min(MLSTM_CHUNK, S)
    nc = S // L
    g = gates[:, :2 * H].reshape(B, nc, L, 2, H)
    g_row = jnp.transpose(g, (0, 4, 1, 3, 2)).reshape(B * H, nc, 2, L)
    g_col = jnp.transpose(g, (0, 4, 1, 2, 3)).reshape(B * H, nc, L, 2)
    hps = MLSTM_HEADS_PER_STEP
    hs = H // hps
    return pl.pallas_call(
        functools.partial(_mlstm_kernel, L=L, hps=hps),
        grid=(B * hs, nc),
        in_specs=[
            pl.BlockSpec((L, hps * dk), lambda bh, c: ((bh // hs) * nc + c, bh % hs)),
            pl.BlockSpec((1, hps * dk, L), lambda bh, c: (bh // hs, bh % hs, c)),
            pl.BlockSpec((L, hps * dv), lambda bh, c: ((bh // hs) * nc + c, bh % hs)),
            pl.BlockSpec((L, hps * dv), lambda bh, c: ((bh // hs) * nc + c, bh % hs)),
            pl.BlockSpec((hps, 1, 2, L), lambda bh, c: (bh, c, 0, 0)),
            pl.BlockSpec((hps, 1, L, 2), lambda bh, c: (bh, c, 0, 0)),
            pl.BlockSpec((hps, 1, dv), lambda bh, c: (bh % hs, 0, 0)),
        ],
        out_specs=pl.BlockSpec((L, hps * dv), lambda bh, c: ((bh // hs) * nc + c, bh % hs)),
        out_shape=jax.ShapeDtypeStruct((T, H * dv), BF16),
        scratch_shapes=[pltpu.VMEM((hps, dk, dv), F32), pltpu.VMEM((hps, dk, LANES), F32),
                        pltpu.VMEM((hps, SUBLANES, LANES), F32)],
        compiler_params=_cparams("parallel", "arbitrary"),
        name="mlstm",
    )(q, kt, v, o, g_row, g_col, head_norm.reshape(H, 1, dv))


def _ffn_in_kernel(u_ref, halo_ref, wg_ref, wu_ref, cw_ref, cb_ref, o_ref, lhs_ref, g_ref, tail_ref, *, tm, tn, ff,
                   sh, over, tiles_per_seq):
    i = pl.program_id(0)
    j = pl.program_id(1)

    @pl.when(j == 0)
    def _():
        halo = halo_ref[...]
        lhs_ref[0:CONV_HALO, :] = jnp.where(i % tiles_per_seq != 0, halo, jnp.zeros_like(halo))
        lhs_ref[CONV_HALO:CONV_HALO + tm, :] = u_ref[...]
        tail_ref[...] = jnp.zeros_like(tail_ref)

    g_ref[...] = jnp.dot(lhs_ref[...], wg_ref[...], preferred_element_type=F32)
    up = jnp.dot(u_ref[...], wu_ref[0], preferred_element_type=F32)
    if over:
        moved = jnp.concatenate([up[:, over:], up[:, :over]], axis=1)
        up = jnp.where(j == pl.num_programs(1) - 1, moved, up)
    gate = cb_ref[...] + cw_ref[0:1, :] * g_ref[CONV_HALO:CONV_HALO + tm, :]
    for back in range(1, CONV_WIDTH):
        gate = gate + cw_ref[back:back + 1, :] * g_ref[CONV_HALO - back:CONV_HALO - back + tm, :]
    if sh:
        lane = lax.broadcasted_iota(jnp.int32, (1, LANES), 1)
        tiles = [tail_ref[...]] + [gate[:, c * LANES:(c + 1) * LANES] for c in range(tn // LANES)]
        tail_ref[...] = tiles[-1]
        rolled = [pltpu.roll(x, sh, axis=1) for x in tiles]
        gate = jnp.concatenate([jnp.where(lane < sh, rolled[c], rolled[c + 1]) for c in range(tn // LANES)], axis=1)
    act = gate * jax.nn.sigmoid(gate) * up
    hidden = j * tn - sh + lax.broadcasted_iota(jnp.int32, (1, tn), 1)
    o_ref[...] = jnp.where((hidden >= 0) & (hidden < ff), act, 0.0).astype(o_ref.dtype)


def _ffn_in(u, w_in, conv_w, conv_b, layer, ff, sh, fp, S, tm=1024, tn=FFN_COL_TILE):
    T, D = u.shape
    tm = min(tm, S)
    assert T % tm == 0 and S % tm == 0 and fp % tn == 0 and tn % LANES == 0
    hb = tm // CONV_HALO
    up0 = (ff - sh) // LANES
    up_last = (w_in.shape[-1] - tn) // LANES
    over = max(up0 + (fp // tn - 1) * (tn // LANES) - up_last, 0) * LANES
    assert w_in.shape[-1] % LANES == 0 and over < tn
    return pl.pallas_call(
        functools.partial(_ffn_in_kernel, tm=tm, tn=tn, ff=ff, sh=sh, over=over, tiles_per_seq=S // tm),
        grid=(T // tm, fp // tn),
        in_specs=[
            pl.BlockSpec((tm, D), lambda i, j: (i, 0)),
            pl.BlockSpec((CONV_HALO, D), lambda i, j: (jnp.maximum(i * hb - 1, 0), 0)),
            pl.BlockSpec((None, D, tn), lambda i, j: (layer, 0, j)),
            pl.BlockSpec((pl.Element(1), pl.Element(D), pl.Element(tn)),
                         lambda i, j: (layer, 0, jnp.minimum(up0 + j * (tn // LANES), up_last) * LANES)),
            pl.BlockSpec((None, CONV_WIDTH, tn), lambda i, j: (layer, 0, j)),
            pl.BlockSpec((None, 1, tn), lambda i, j: (layer, 0, j)),
        ],
        out_specs=pl.BlockSpec((tm, tn), lambda i, j: (i, j)),
        out_shape=jax.ShapeDtypeStruct((T, fp), BF16),
        scratch_shapes=[pltpu.VMEM((CONV_HALO + tm, D), BF16), pltpu.VMEM((CONV_HALO + tm, tn), F32),
                        pltpu.VMEM((tm, LANES), F32)],
        compiler_params=_cparams("parallel", "arbitrary"),
        name="ffn_in_conv_act",
    )(u, u, w_in, w_in, conv_w, conv_b)


def _ffn_weights(w_in, w_out):
    depth, D, ff2 = w_in.shape
    ff = ff2 // 2
    sh = ff % LANES
    fp = _round_up(ff + sh, FFN_COL_TILE)
    w_in = w_in.astype(BF16)
    w_out = jnp.concatenate([jnp.zeros((depth, sh, D), BF16), w_out.astype(BF16),
                             jnp.zeros((depth, fp - ff - sh, D), BF16)], axis=1)
    return w_in, w_out, ff, sh, fp


def _conv_ffn(h, u, ln_post, next_gain, w_in, w_out, ff, sh, fp, conv_w, conv_b, layer, S):
    act = _ffn_in(u, w_in, conv_w, conv_b, layer, ff, sh, fp, S)
    return _mm_norm_res([act], w_out, h, ln_post, next_gain, w_lead=layer)


def _ab_layer(h, u, ln_post, next_gain, w_in, pool_w, pool_scale, cmp_pos, ck_w1, ck_w2, cv_w1, cv_w2, w_out, B, S):
    T, D = h.shape
    G, DH = NSA_KV_GROUPS, NSA_HEAD_DIM
    pw = D // 4
    hw = D - pw
    kvw = G * DH
    w = w_in.astype(BF16)
    c0 = pw + hw
    p_in = _mm(u, w, F32, 0, pw)
    q = _mm(u, w, BF16, pw, hw, scale=DH ** -0.5 * LOG2E)
    kv_cmp = _mm(u, w, F32, c0, 2 * kvw)
    kv = _mm(u, w, BF16, c0 + 2 * kvw, 4 * kvw)
    gates = _mm(u, w, F32, c0 + 6 * kvw, LANES)

    y_a = _pool_mixer(p_in.reshape(B, S, pw), pool_w, pool_scale).reshape(T, pw)
    kvc = _compress(kv_cmp.reshape(B, S, 2 * kvw), cmp_pos, jnp.stack([ck_w1, cv_w1]), jnp.stack([ck_w2, cv_w2]), G)
    y_b = _nsa(q, gates, kvc, kv.reshape(B, S, 4 * kvw), B, S)
    return _mm_norm_res([y_a, y_b], w_out.astype(BF16), h, ln_post, next_gain)


def _c_layer(h, u, ln_post, next_gain, w_in, b_if, head_norm, w_out, B, S):
    T, D = h.shape
    H = MLSTM_HEADS
    dv = D // H
    dk = dv // 2
    qk = H * dk
    w = w_in.astype(BF16)
    q = _mm(u, w, BF16, 0, qk, scale=dk ** -0.5)
    k = _mm(u, w, BF16, qk, qk)
    v = _mm(u, w, BF16, 2 * qk, D)
    o = _mm(u, w, F32, 2 * qk + D, D)
    gates = _mm(u, w, F32, 2 * qk + 2 * D, LANES, bias=_pad_cols(b_if.reshape(1, 2 * H), LANES))
    kt = k.reshape(B, S, qk).transpose(0, 2, 1)
    y = _mlstm(q, kt, v, o, gates, head_norm, B, S)
    return _mm_norm_res([y], w_out.astype(BF16), h, ln_post, next_gain)


def kernel(x, ln_pre, ln_post, w_in_ab, pool_w, pool_scale, cmp_pos, cmp_k_w1, cmp_k_w2, cmp_v_w1, cmp_v_w2,
           w_out_ab, w_in_c, b_if_c, head_norm_c, w_out_c, ffn_ln_pre, ffn_ln_post, ffn_w_in, ffn_conv_w,
           ffn_conv_b, ffn_w_out):
    B, S, D = x.shape
    depth = ln_pre.shape[0]
    h = x.reshape(B * S, D)
    u = _rmsnorm(h, ln_pre[0])
    ffn_wi, ffn_wo, ff, sh, fp = _ffn_weights(ffn_w_in, ffn_w_out)
    ffn_cb = ffn_conv_b.reshape(depth, 1, -1)
    for layer in range(depth):
        i = layer // 2
        if layer % 2 == 0:
            h, u = _ab_layer(h, u, ln_post[layer], ffn_ln_pre[layer], w_in_ab[i], pool_w[i], pool_scale[i],
                             cmp_pos[i], cmp_k_w1[i], cmp_k_w2[i], cmp_v_w1[i], cmp_v_w2[i], w_out_ab[i], B, S)
        else:
            h, u = _c_layer(h, u, ln_post[layer], ffn_ln_pre[layer], w_in_c[i], b_if_c[i], head_norm_c[i],
                            w_out_c[i], B, S)
        next_gain = ln_pre[layer + 1] if layer + 1 < depth else None
        h, u = _conv_ffn(h, u, ffn_ln_post[layer], next_gain, ffn_wi, ffn_wo, ff, sh, fp, ffn_conv_w, ffn_cb, layer,
                         S)
    return h.reshape(B, S, D)
```

```python
import functools
import math

import jax
import jax.numpy as jnp
from jax import lax
from jax.experimental import pallas as pl
from jax.experimental.pallas import tpu as pltpu

F32 = jnp.float32
BF16 = jnp.bfloat16

RMS_EPS = 1e-6
POOL_GROUPS = 4
POOL_WINDOWS = (2, 4, 8, 16)
POOL_HALO = 16
NSA_HEAD_DIM = 128
NSA_KV_GROUPS = 4
N_BRANCH = 3
CMP_BLOCK = 32
CMP_STRIDE = 16
SLC_BLOCK = 64
SLC_TOPN = 16
SWA_WINDOW = 512
FORCE_BONUS = 1e4
NEG = -1e30
BIG = 1e30
MLSTM_HEADS = 8
CONV_WIDTH = 3
CONV_HALO = 16

LANES = 128
SUBLANES = 8
LOG2E = math.log2(math.e)
NSA_Q_TILE = LANES
SEL_TILE = 512
SEL_GROUP = 4
RANK_SIZES = 8
SWA_SPAN = SWA_WINDOW + NSA_Q_TILE
MLSTM_CHUNK = 256
MLSTM_HEADS_PER_STEP = 4
FFN_COL_TILE = 512
VMEM_LIMIT = 56 * 1024 * 1024


def _cparams(*sem):
    return pltpu.CompilerParams(dimension_semantics=sem, vmem_limit_bytes=VMEM_LIMIT)


def _round_up(n, m):
    return (n + m - 1) // m * m


def _pad_cols(w, n):
    return jnp.pad(w, ((0, 0), (0, n - w.shape[1])))


def _rmsnorm_kernel(x_ref, g_ref, o_ref):
    x = x_ref[...]
    ms = jnp.mean(x * x, axis=-1, keepdims=True)
    o_ref[...] = (x * lax.rsqrt(ms + RMS_EPS) * g_ref[...]).astype(o_ref.dtype)


def _rmsnorm(x, gain, tm=512):
    T, D = x.shape
    return pl.pallas_call(
        _rmsnorm_kernel,
        grid=(T // tm,),
        in_specs=[pl.BlockSpec((tm, D), lambda i: (i, 0)), pl.BlockSpec((1, D), lambda i: (0, 0))],
        out_specs=pl.BlockSpec((tm, D), lambda i: (i, 0)),
        out_shape=jax.ShapeDtypeStruct((T, D), BF16),
        compiler_params=_cparams("parallel"),
        name="rmsnorm",
    )(x, gain.reshape(1, D))


def _mm_kernel(a_ref, b_ref, bias_ref, o_ref, *, scale, live_cols):
    acc = jnp.dot(a_ref[...], b_ref[...], preferred_element_type=F32)
    out = (acc + bias_ref[...]) * scale
    if live_cols is not None:
        col = pl.program_id(1) * out.shape[1] + lax.broadcasted_iota(jnp.int32, (1, out.shape[1]), 1)
        out = jnp.where(col < live_cols, out, 0.0)
    o_ref[...] = out.astype(o_ref.dtype)


def _mm(a, w, out_dtype, col0, ncols, *, bias=None, scale=1.0, tm=1024, tn=1024):
    M, K = a.shape
    tm = min(tm, M)
    tn = math.gcd(tn, ncols, col0)
    assert M % tm == 0 and tn % LANES == 0
    jb = col0 // tn
    live_cols = w.shape[1] - col0 if col0 + ncols > w.shape[1] else None
    if bias is None:
        bias = jnp.zeros((1, ncols), F32)
    return pl.pallas_call(
        functools.partial(_mm_kernel, scale=scale, live_cols=live_cols),
        grid=(M // tm, ncols // tn),
        in_specs=[
            pl.BlockSpec((tm, K), lambda i, j: (i, 0)),
            pl.BlockSpec((K, tn), lambda i, j: (0, j + jb)),
            pl.BlockSpec((1, tn), lambda i, j: (0, j)),
        ],
        out_specs=pl.BlockSpec((tm, tn), lambda i, j: (i, j)),
        out_shape=jax.ShapeDtypeStruct((M, ncols), out_dtype),
        compiler_params=_cparams("parallel", "arbitrary"),
        name="proj",
    )(a, w, bias)


def _mm_norm_res_kernel(*refs, nk, nk1, n_a, emit_u, tm, rows):
    a_refs = refs[:n_a]
    w_ref, h_hbm, g_ref = refs[n_a:n_a + 3]
    rest = refs[n_a + 3:]
    if emit_u:
        ng_ref, o_ref, u_ref, h_buf, h_sem = rest
    else:
        o_ref, h_buf, h_sem = rest
    i = pl.program_id(0)
    k = pl.program_id(1)

    def h_copy():
        return pltpu.make_async_copy(h_hbm.at[pl.ds(i * tm, tm), :], h_buf, h_sem)

    def product(a_ref):
        return jnp.dot(a_ref[...], w_ref[...], preferred_element_type=F32)

    @pl.when(k == 0)
    def _():
        h_copy().start()
        o_ref[...] = product(a_refs[0])

    if nk1 > 1 or n_a == 1:
        @pl.when((k > 0) & (k < nk1))
        def _():
            o_ref[...] += product(a_refs[0])

    if n_a == 2:
        @pl.when(k >= nk1)
        def _():
            o_ref[...] += product(a_refs[1])

    @pl.when(k == nk - 1)
    def _():
        h_copy().wait()

        def norm_rows(c, carry):
            r = pl.ds(pl.multiple_of(c * rows, rows), rows)
            y = o_ref[r, :]
            ms = jnp.mean(y * y, axis=-1, keepdims=True)
            h_new = h_buf[r, :] + y * lax.rsqrt(ms + RMS_EPS) * g_ref[...]
            o_ref[r, :] = h_new
            if emit_u:
                ms2 = jnp.mean(h_new * h_new, axis=-1, keepdims=True)
                u_ref[r, :] = (h_new * lax.rsqrt(ms2 + RMS_EPS) * ng_ref[...]).astype(u_ref.dtype)
            return carry

        lax.fori_loop(0, tm // rows, norm_rows, 0)


def _mm_norm_res(a_list, w, h, gain, next_gain=None, w_lead=None, tm=512, tk=1024, rows=64):
    M = h.shape[0]
    N = w.shape[-1]
    n_a = len(a_list)
    tk = math.gcd(tk, *[a.shape[1] for a in a_list])
    assert n_a in (1, 2) and M % tm == 0 and tm % rows == 0
    nk1 = a_list[0].shape[1] // tk
    nk = sum(a.shape[1] for a in a_list) // tk
    emit_u = next_gain is not None
    a_specs = [pl.BlockSpec((tm, tk), lambda i, k: (i, jnp.minimum(k, nk1 - 1)))]
    if n_a == 2:
        a_specs.append(pl.BlockSpec((tm, tk), lambda i, k: (i, jnp.maximum(k - nk1, 0))))
    row_spec = pl.BlockSpec((tm, N), lambda i, k: (i, 0))
    vec_spec = pl.BlockSpec((1, N), lambda i, k: (0, 0))
    if w_lead is None:
        w_spec = pl.BlockSpec((tk, N), lambda i, k: (k, 0))
    else:
        w_spec = pl.BlockSpec((None, tk, N), lambda i, k: (w_lead, k, 0))
    in_specs = a_specs + [w_spec, pl.BlockSpec(memory_space=pl.ANY), vec_spec]
    args = list(a_list) + [w, h, gain.reshape(1, N)]
    out_specs, out_shape = row_spec, jax.ShapeDtypeStruct((M, N), F32)
    if emit_u:
        in_specs.append(vec_spec)
        args.append(next_gain.reshape(1, N))
        out_specs = [row_spec, row_spec]
        out_shape = [out_shape, jax.ShapeDtypeStruct((M, N), BF16)]
    out = pl.pallas_call(
        functools.partial(_mm_norm_res_kernel, nk=nk, nk1=nk1, n_a=n_a, emit_u=emit_u, tm=tm, rows=rows),
        grid=(M // tm, nk),
        in_specs=in_specs,
        out_specs=out_specs,
        out_shape=out_shape,
        scratch_shapes=[pltpu.VMEM((tm, N), F32), pltpu.SemaphoreType.DMA(())],
        compiler_params=_cparams("parallel", "arbitrary"),
        name="out_proj_norm_res",
    )(*args)
    return out if emit_u else (out, None)


def _pool_kernel(cur_ref, prev_ref, w_ref, scale_ref, o_ref, ext_ref, *, ts, gd):
    i = pl.program_id(1)
    ext_ref[0:POOL_HALO, :] = jnp.where(i > 0, prev_ref[0], 0.0)
    ext_ref[POOL_HALO:POOL_HALO + ts, :] = cur_ref[0]
    t = i * ts + lax.broadcasted_iota(jnp.int32, (ts, 1), 0)
    for g, win in enumerate(POOL_WINDOWS):
        cols = slice(g * gd, (g + 1) * gd)
        tok = ext_ref[POOL_HALO:POOL_HALO + ts, cols]
        acc = tok
        for back in range(1, win):
            acc = acc + ext_ref[POOL_HALO - back:POOL_HALO - back + ts, cols]
        count = jnp.minimum(t + 1, win).astype(F32)
        pooled = acc / count - tok
        y = jnp.dot(pooled.astype(BF16), w_ref[g], preferred_element_type=F32)
        o_ref[0, :, cols] = (y * scale_ref[:, cols]).astype(o_ref.dtype)


def _pool_mixer(p_in, w_pool, scale, ts=512):
    B, S, PW = p_in.shape
    gd = PW // POOL_GROUPS
    hb = ts // POOL_HALO
    return pl.pallas_call(
        functools.partial(_pool_kernel, ts=ts, gd=gd),
        grid=(B, S // ts),
        in_specs=[
            pl.BlockSpec((1, ts, PW), lambda b, i: (b, i, 0)),
            pl.BlockSpec((1, POOL_HALO, PW), lambda b, i: (b, jnp.maximum(i * hb - 1, 0), 0)),
            pl.BlockSpec((POOL_GROUPS, gd, gd), lambda b, i: (0, 0, 0)),
            pl.BlockSpec((1, PW), lambda b, i: (0, 0)),
        ],
        out_specs=pl.BlockSpec((1, ts, PW), lambda b, i: (b, i, 0)),
        out_shape=jax.ShapeDtypeStruct((B, S, PW), BF16),
        scratch_shapes=[pltpu.VMEM((POOL_HALO + ts, PW), F32)],
        compiler_params=_cparams("parallel", "arbitrary"),
        name="pool_mixer",
    )(p_in, p_in, w_pool.astype(BF16), scale.reshape(1, PW))


def _gelu_tanh(x):
    c = math.sqrt(2.0 / math.pi)
    return x * (0.5 * (1.0 + jnp.tanh(c * (x + 0.044715 * (x * x * x)))))


def _compress_kernel(x_ref, pos_ref, w1_ref, w2_ref, o_ref, tmp_ref, *, nch):
    half = CMP_BLOCK // 2
    dh = x_ref.shape[2]
    a = b = None
    for r in range(half):
        x = x_ref[0, pl.ds(r, nch, stride=CMP_STRIDE), :]
        lo = jnp.dot((x + pos_ref[r:r + 1, :]).astype(BF16), w1_ref[0, r * dh:(r + 1) * dh, :],
                     preferred_element_type=F32)
        hi = jnp.dot((x + pos_ref[half + r:half + r + 1, :]).astype(BF16),
                     w1_ref[0, (half + r) * dh:(half + r + 1) * dh, :], preferred_element_type=F32)
        a = lo if a is None else a + lo
        b = hi if b is None else b + hi
    tmp_ref[0:nch, :] = b
    tmp_ref[nch:nch + SUBLANES, :] = jnp.zeros((SUBLANES, b.shape[1]), F32)
    pre = a + tmp_ref[1:nch + 1, :]
    out = jnp.dot(_gelu_tanh(pre).astype(BF16), w2_ref[0], preferred_element_type=F32)
    row = lax.broadcasted_iota(jnp.int32, out.shape, 0)
    o_ref[0, 0] = jnp.where(row < nch - 1, out, 0.0).astype(o_ref.dtype)


def _compress(x, pos, w1, w2, G):
    B, S, _ = x.shape
    hid = w1.shape[-1]
    dh = w2.shape[-1]
    nch = S // CMP_STRIDE
    assert CMP_BLOCK == 2 * CMP_STRIDE
    return pl.pallas_call(
        functools.partial(_compress_kernel, nch=nch),
        grid=(2, B * G),
        in_specs=[
            pl.BlockSpec((1, S, dh), lambda s, b: (b // G, 0, s * G + b % G)),
            pl.BlockSpec((CMP_BLOCK, dh), lambda s, b: (0, 0)),
            pl.BlockSpec((1, CMP_BLOCK * dh, hid), lambda s, b: (s, 0, 0)),
            pl.BlockSpec((1, hid, dh), lambda s, b: (s, 0, 0)),
        ],
        out_specs=pl.BlockSpec((1, 1, nch, dh), lambda s, b: (s, b, 0, 0)),
        out_shape=jax.ShapeDtypeStruct((2, B * G, nch, dh), BF16),
        scratch_shapes=[pltpu.VMEM((nch + SUBLANES, hid), F32)],
        compiler_params=_cparams("parallel", "arbitrary"),
        name="nsa_compress",
    )(x, pos, w1.astype(BF16), w2.astype(BF16))


def _nt_dot(a, b):
    return lax.dot_general(a, b, (((1,), (1,)), ((), ())), preferred_element_type=F32)


def _split_dot(w, x):
    hi = x.astype(BF16)
    r1 = x - hi.astype(F32)
    mid = r1.astype(BF16)
    lo = (r1 - mid.astype(F32)).astype(BF16)
    return (jnp.dot(w, hi, preferred_element_type=F32) + jnp.dot(w, mid, preferred_element_type=F32)
            + jnp.dot(w, lo, preferred_element_type=F32))


def _nsa_kernel(q_ref, gl_ref, kc_ref, vct_ref, ks_ref, vst_ref, kw_ref, vwt_ref, ovt_ref, et_ref, o_ref,
                m_ref, l_ref, acc_ref, s_ref, bias_ref, *, hg, n_cmp, n_slc, top_n):
    QT, DH = NSA_Q_TILE, NSA_HEAD_DIM
    t0 = pl.program_id(1) * QT
    q = jnp.concatenate([q_ref[:, h * DH:(h + 1) * DH] for h in range(hg)], axis=0)
    lane = lax.broadcasted_iota(jnp.int32, (1, QT), 1)
    t = t0 + lane

    def capped(s, cap):
        return jnp.concatenate([jnp.minimum(s[:, h * QT:(h + 1) * QT], cap) for h in range(hg)], axis=1)

    def softmax_cols(s):
        p = jnp.exp2(s - jnp.max(s, axis=0, keepdims=True))
        return p, jnp.sum(p, axis=0, keepdims=True)

    ncp = kc_ref.shape[2]
    nrow = lax.broadcasted_iota(jnp.int32, (ncp, QT), 0)
    valid_c = (nrow * CMP_STRIDE + (CMP_BLOCK - 1) <= t) & (nrow < n_cmp)
    s_c = capped(_nt_dot(kc_ref[0, 0], q), jnp.where(valid_c, BIG, NEG))
    ws = pl.multiple_of(jnp.maximum(t0 + QT - SWA_SPAN, 0), LANES)
    wb = ws // LANES
    dist = t - (ws + lax.broadcasted_iota(jnp.int32, (SWA_SPAN, QT), 0))
    valid_w = (dist >= 0) & (dist < SWA_WINDOW)
    s_w = capped(_nt_dot(kw_ref[0, pl.ds(ws, SWA_SPAN), :], q), jnp.where(valid_w, BIG, NEG))

    p_c, l_c = softmax_cols(s_c)
    seen = jnp.concatenate([jnp.where(t >= CMP_BLOCK - 1, 1.0, 0.0)] * hg, axis=1)
    p_c = p_c * (seen / l_c)
    o_c = jnp.dot(vct_ref[0], p_c.astype(BF16), preferred_element_type=F32)

    p_w, l_w = softmax_cols(s_w)
    vwin = jnp.concatenate([vwt_ref[0, wb + j] for j in range(SWA_SPAN // LANES)], axis=1)
    o_w = jnp.dot(vwin, p_w.astype(BF16), preferred_element_type=F32) / l_w

    p_sum = p_c[:, 0:QT]
    for h in range(1, hg):
        p_sum = p_sum + p_c[:, h * QT:(h + 1) * QT]
    imp = _split_dot(ovt_ref[...], p_sum)
    blk = lax.broadcasted_iota(jnp.int32, (LANES, QT), 0)
    cur = t0 // SLC_BLOCK + jnp.zeros((1, QT), jnp.int32)
    for k in range(1, QT // SLC_BLOCK):
        cur = cur + jnp.where(lane >= k * SLC_BLOCK, 1, 0)
    forced = (blk == 0) | (blk == cur) | (blk == cur - 1)
    val = jnp.where(forced, imp + FORCE_BONUS, jnp.where(blk > cur, -FORCE_BONUS, imp))
    if n_slc < LANES:
        val = jnp.where(blk < n_slc, val, -jnp.inf)

    def rank_bias(nb):
        SL = SUBLANES
        nslab = nb // SL
        slabs = [val[SL * r:SL * r + SL, :] for r in range(nslab)]
        ranks = [jnp.zeros((SL, QT), F32) for _ in range(nslab)]
        sub = lax.broadcasted_iota(jnp.int32, (SL, QT), 0)
        for other in range(nb):
            c = val[other:other + 1, :]
            for r in range(nslab):
                if SL * r > other:
                    beat = c >= slabs[r]
                elif SL * r + SL - 1 < other:
                    beat = c > slabs[r]
                else:
                    beat = (c > slabs[r]) | ((c == slabs[r]) & (sub + SL * r > other))
                ranks[r] = ranks[r] + jnp.where(beat, 1.0, 0.0)
        for r in range(nslab):
            chosen = (ranks[r] < top_n) & (blk[SL * r:SL * r + SL] <= cur)
            bias_ref[SL * r:SL * r + SL, :] = jnp.where(chosen, 0.0, NEG)
        if nb < LANES:
            bias_ref[nb:LANES, :] = jnp.full((LANES - nb, QT), NEG, F32)

    live = (t0 + QT - 1) // SLC_BLOCK + 1
    sizes = sorted({min(_round_up(-(-n_slc * k // RANK_SIZES), SUBLANES), LANES) for k in range(1, RANK_SIZES + 1)})
    for lo, nb in zip([0] + sizes[:-1], sizes):
        @pl.when((live > lo) & (live <= nb))
        def _(nb=nb):
            rank_bias(nb)

    bias_q = bias_ref[...].T.astype(BF16)

    q_sel = jnp.concatenate([q, jnp.concatenate([bias_q] * hg, axis=0)], axis=1)
    m_ref[...] = jnp.full(m_ref.shape, NEG, F32)
    l_ref[...] = jnp.zeros(l_ref.shape, F32)
    acc_ref[...] = jnp.zeros(acc_ref.shape, F32)

    def sel_tiles(tiles):
        for slot, (kb, _) in enumerate(tiles):
            start = pl.multiple_of(kb * SEL_TILE, SEL_TILE)
            keys = jnp.concatenate([ks_ref[0, pl.ds(start, SEL_TILE), :], et_ref[kb]], axis=1)
            s_ref[slot] = _nt_dot(keys, q_sel)
        for slot, (kb, diagonal) in enumerate(tiles):
            s = s_ref[slot]
            if diagonal:
                key = kb * SEL_TILE + lax.broadcasted_iota(jnp.int32, (SEL_TILE, QT), 0)
                s = capped(s, jnp.where(key <= t, BIG, NEG))
            m_prev = m_ref[...]
            m_new = jnp.maximum(m_prev, jnp.max(s, axis=0, keepdims=True))
            alpha = jnp.exp2(m_prev - m_new)
            p = jnp.exp2(s - m_new)
            l_ref[...] = alpha * l_ref[...] + jnp.sum(p, axis=0, keepdims=True)
            acc_ref[...] = alpha * acc_ref[...] + jnp.dot(vst_ref[0, kb], p.astype(BF16),
                                                          preferred_element_type=F32)
            m_ref[...] = m_new

    def past_group(i, carry):
        sel_tiles([(SEL_GROUP * i + slot, False) for slot in range(SEL_GROUP)])
        return carry

    diag = t0 // SEL_TILE
    groups = diag // SEL_GROUP
    lax.fori_loop(0, groups, past_group, 0)
    for rem in range(SEL_GROUP):
        @pl.when(diag - groups * SEL_GROUP == rem)
        def _(rem=rem):
            sel_tiles([(diag - rem + slot, slot == rem) for slot in range(rem + 1)])

    o_s = acc_ref[...] / l_ref[...]

    gate = jax.nn.sigmoid(gl_ref[0, 0])
    for h in range(hg):
        cols = slice(h * QT, (h + 1) * QT)
        r = N_BRANCH * h
        out = gate[r:r + 1] * o_c[:, cols] + gate[r + 1:r + 2] * o_s[:, cols] + gate[r + 2:r + 3] * o_w[:, cols]
        o_ref[:, h * DH:(h + 1) * DH] = out.T.astype(o_ref.dtype)


def _nsa(q, gates, kvc, kv, B, S):
    T, HW = q.shape
    G, DH, QT = NSA_KV_GROUPS, NSA_HEAD_DIM, NSA_Q_TILE
    hg = HW // DH // G
    n_cmp = (S - CMP_BLOCK) // CMP_STRIDE + 1
    ncp = kvc.shape[2]
    n_slc = S // SLC_BLOCK
    top_n = min(SLC_TOPN, n_slc)
    nqt = S // QT
    kvw = G * DH
    assert n_slc <= LANES and S % SEL_TILE == 0 and S >= SWA_SPAN and QT == LANES

    cs = jnp.arange(ncp)[None, :] * CMP_STRIDE
    ss = jnp.arange(LANES)[:, None] * SLC_BLOCK
    overlap = jnp.clip(jnp.minimum(cs + CMP_BLOCK, ss + SLC_BLOCK) - jnp.maximum(cs, ss), 0) // CMP_STRIDE
    overlap = jnp.where((jnp.arange(ncp)[None, :] < n_cmp) & (jnp.arange(LANES)[:, None] < n_slc), overlap, 0)
    overlap = overlap.astype(BF16)
    key_blk = (jnp.arange(S) // SLC_BLOCK).reshape(S // SEL_TILE, SEL_TILE, 1)
    expand = (key_blk == jnp.arange(LANES)[None, None, :]).astype(BF16)

    gr = _round_up(N_BRANCH * hg, 8)
    gl = gates[:, :G * hg * N_BRANCH].reshape(B, nqt, QT, G, hg * N_BRANCH)
    gl = jnp.pad(gl, ((0, 0),) * 4 + ((0, gr - hg * N_BRANCH),)).transpose(0, 3, 1, 4, 2).reshape(B * G, nqt, gr, QT)

    def v_tiles(which, tile):
        v = kv[:, :, which * kvw:(which + 1) * kvw].reshape(B, S // tile, tile, G, DH)
        return v.transpose(0, 3, 1, 4, 2).reshape(B * G, S // tile, DH, tile)

    def k_spec(which):
        return pl.BlockSpec((1, S, DH), lambda bg, i: (bg // G, 0, which * G + bg % G))

    return pl.pallas_call(
        functools.partial(_nsa_kernel, hg=hg, n_cmp=n_cmp, n_slc=n_slc, top_n=top_n),
        grid=(B * G, nqt),
        in_specs=[
            pl.BlockSpec((QT, hg * DH), lambda bg, i: ((bg // G) * nqt + i, bg % G)),
            pl.BlockSpec((1, 1, gr, QT), lambda bg, i: (bg, i, 0, 0)),
            pl.BlockSpec((1, 1, ncp, DH), lambda bg, i: (0, bg, 0, 0)),
            pl.BlockSpec((1, DH, ncp), lambda bg, i: (bg, 0, 0)),
            k_spec(0),
            pl.BlockSpec((1, S // SEL_TILE, DH, SEL_TILE), lambda bg, i: (bg, 0, 0, 0)),
            k_spec(2),
            pl.BlockSpec((1, S // LANES, DH, LANES), lambda bg, i: (bg, 0, 0, 0)),
            pl.BlockSpec((LANES, ncp), lambda bg, i: (0, 0)),
            pl.BlockSpec((S // SEL_TILE, SEL_TILE, LANES), lambda bg, i: (0, 0, 0)),
        ],
        out_specs=pl.BlockSpec((QT, hg * DH), lambda bg, i: ((bg // G) * nqt + i, bg % G)),
        out_shape=jax.ShapeDtypeStruct((T, HW), BF16),
        scratch_shapes=[pltpu.VMEM((1, hg * QT), F32), pltpu.VMEM((1, hg * QT), F32),
                        pltpu.VMEM((DH, hg * QT), F32), pltpu.VMEM((SEL_GROUP, SEL_TILE, hg * QT), F32),
                        pltpu.VMEM((LANES, QT), F32)],
        compiler_params=_cparams("parallel", "arbitrary"),
        name="nsa_attention",
    )(q, gl, kvc, jnp.swapaxes(kvc[1], 1, 2), kv, v_tiles(1, SEL_TILE), kv, v_tiles(3, LANES), overlap, expand)


def _mlstm_kernel(q_ref, kt_ref, v_ref, o_ref, gr_ref, gc_ref, hn_ref, y_ref, c_ref, n_ref, m_ref, *, L, hps):
    ci = pl.program_id(1)
    dk = kt_ref.shape[1] // hps
    dv = v_ref.shape[1] // hps

    @pl.when(ci == 0)
    def _():
        c_ref[...] = jnp.zeros_like(c_ref)
        n_ref[...] = jnp.zeros_like(n_ref)
        m_ref[...] = jnp.zeros_like(m_ref)

    ti = lax.broadcasted_iota(jnp.int32, (L, L), 0)
    si = lax.broadcasted_iota(jnp.int32, (L, L), 1)
    causal = si <= ti

    for hd in range(hps):
        q = q_ref[:, hd * dk:(hd + 1) * dk]
        kt = kt_ref[0, hd * dk:(hd + 1) * dk, :]
        v = v_ref[:, hd * dv:(hd + 1) * dv]
        ig_row = gr_ref[hd, 0, 0:1, :]
        lf_row = jax.nn.log_sigmoid(gr_ref[hd, 0, 1:2, :])
        lf_col = jax.nn.log_sigmoid(gc_ref[hd, 0, :, 1:2])
        m_prev = m_ref[hd, 0:1, 0:1]

        b_col = jnp.sum(jnp.where(causal, lf_row, 0.0), axis=1, keepdims=True)
        b_row = jnp.sum(jnp.where(ti <= si, lf_col, 0.0), axis=0, keepdims=True)
        b_last = b_col[L - 1:L, :]

        dmat = jnp.where(causal, b_col - b_row + ig_row, -jnp.inf)
        a_col = b_col + m_prev
        m_t = jnp.maximum(a_col, jnp.max(dmat, axis=1, keepdims=True))
        wq = jnp.dot(q, kt, preferred_element_type=F32) * jnp.exp(dmat - m_t)
        inter = jnp.exp(a_col - m_t)
        q_c = jnp.dot(q, c_ref[hd].astype(BF16), preferred_element_type=F32)
        q_n = jnp.dot(q, n_ref[hd].astype(BF16), preferred_element_type=F32)[:, 0:1]
        num = inter * q_c + jnp.dot(wq.astype(BF16), v, preferred_element_type=F32)
        den = inter * q_n + jnp.sum(wq, axis=1, keepdims=True)
        h = num / jnp.maximum(jnp.abs(den), jnp.exp(-m_t))
        h = h * lax.rsqrt(jnp.mean(h * h, axis=-1, keepdims=True) + RMS_EPS) * hn_ref[hd]
        gate_o = jax.nn.sigmoid(o_ref[:, hd * dv:(hd + 1) * dv].astype(F32))
        y_ref[:, hd * dv:(hd + 1) * dv] = (gate_o * h).astype(y_ref.dtype)

        g_row = b_last - b_row + ig_row
        m_new = jnp.maximum(b_last + m_prev, jnp.max(g_row, axis=1, keepdims=True))
        decay = jnp.exp(b_last + m_prev - m_new)
        kw_t = kt.astype(F32) * jnp.exp(g_row - m_new)
        c_ref[hd] = decay * c_ref[hd] + jnp.dot(kw_t.astype(BF16), v, preferred_element_type=F32)
        n_ref[hd] = decay * n_ref[hd] + jnp.sum(kw_t, axis=1, keepdims=True)
        m_ref[hd] = jnp.broadcast_to(m_new, m_ref.shape[1:])


def _mlstm(q, kt, v, o, gates, head_norm, B, S):
    T = q.shape[0]
    H = MLSTM_HEADS
    dk = q.shape[1] // H
    dv = v.shape[1] // H
    L = min(MLSTM_CHUNK, S)
    nc = S // L
    g = gates[:, :2 * H].reshape(B, nc, L, 2, H)
    g_row = jnp.transpose(g, (0, 4, 1, 3, 2)).reshape(B * H, nc, 2, L)
    g_col = jnp.transpose(g, (0, 4, 1, 2, 3)).reshape(B * H, nc, L, 2)
    hps = MLSTM_HEADS_PER_STEP
    hs = H // hps
    return pl.pallas_call(
        functools.partial(_mlstm_kernel, L=L, hps=hps),
        grid=(B * hs, nc),
        in_specs=[
            pl.BlockSpec((L, hps * dk), lambda bh, c: ((bh // hs) * nc + c, bh % hs)),
            pl.BlockSpec((1, hps * dk, L), lambda bh, c: (bh // hs, bh % hs, c)),
            pl.BlockSpec((L, hps * dv), lambda bh, c: ((bh // hs) * nc + c, bh % hs)),
            pl.BlockSpec((L, hps * dv), lambda bh, c: ((bh // hs) * nc + c, bh % hs)),
            pl.BlockSpec((hps, 1, 2, L), lambda bh, c: (bh, c, 0, 0)),
            pl.BlockSpec((hps, 1, L, 2), lambda bh, c: (bh, c, 0, 0)),
            pl.BlockSpec((hps, 1, dv), lambda bh, c: (bh % hs, 0, 0)),
        ],
        out_specs=pl.BlockSpec((L, hps * dv), lambda bh, c: ((bh // hs) * nc + c, bh % hs)),
        out_shape=jax.ShapeDtypeStruct((T, H * dv), BF16),
        scratch_shapes=[pltpu.VMEM((hps, dk, dv), F32), pltpu.VMEM((hps, dk, LANES), F32),
                        pltpu.VMEM((hps, SUBLANES, LANES), F32)],
        compiler_params=_cparams("parallel", "arbitrary"),
        name="mlstm",
    )(q, kt, v, o, g_row, g_col, head_norm.reshape(H, 1, dv))


def _ffn_in_kernel(u_ref, halo_ref, wg_ref, wu_ref, cw_ref, cb_ref, o_ref, lhs_ref, g_ref, tail_ref, *, tm, tn, ff,
                   sh, over, tiles_per_seq):
    i = pl.program_id(0)
    j = pl.program_id(1)

    @pl.when(j == 0)
    def _():
        halo = halo_ref[...]
        lhs_ref[0:CONV_HALO, :] = jnp.where(i % tiles_per_seq != 0, halo, jnp.zeros_like(halo))
        lhs_ref[CONV_HALO:CONV_HALO + tm, :] = u_ref[...]
        tail_ref[...] = jnp.zeros_like(tail_ref)

    g_ref[...] = jnp.dot(lhs_ref[...], wg_ref[...], preferred_element_type=F32)
    up = jnp.dot(u_ref[...], wu_ref[0], preferred_element_type=F32)
    if over:
        moved = jnp.concatenate([up[:, over:], up[:, :over]], axis=1)
        up = jnp.where(j == pl.num_programs(1) - 1, moved, up)
    gate = cb_ref[...] + cw_ref[0:1, :] * g_ref[CONV_HALO:CONV_HALO + tm, :]
    for back in range(1, CONV_WIDTH):
        gate = gate + cw_ref[back:back + 1, :] * g_ref[CONV_HALO - back:CONV_HALO - back + tm, :]
    if sh:
        lane = lax.broadcasted_iota(jnp.int32, (1, LANES), 1)
        tiles = [tail_ref[...]] + [gate[:, c * LANES:(c + 1) * LANES] for c in range(tn // LANES)]
        tail_ref[...] = tiles[-1]
        rolled = [pltpu.roll(x, sh, axis=1) for x in tiles]
        gate = jnp.concatenate([jnp.where(lane < sh, rolled[c], rolled[c + 1]) for c in range(tn // LANES)], axis=1)
    act = gate * jax.nn.sigmoid(gate) * up
    hidden = j * tn - sh + lax.broadcasted_iota(jnp.int32, (1, tn), 1)
    o_ref[...] = jnp.where((hidden >= 0) & (hidden < ff), act, 0.0).astype(o_ref.dtype)


def _ffn_in(u, w_in, conv_w, conv_b, layer, ff, sh, fp, S, tm=1024, tn=FFN_COL_TILE):
    T, D = u.shape
    tm = min(tm, S)
    assert T % tm == 0 and S % tm == 0 and fp % tn == 0 and tn % LANES == 0
    hb = tm // CONV_HALO
    up0 = (ff - sh) // LANES
    up_last = (w_in.shape[-1] - tn) // LANES
    over = max(up0 + (fp // tn - 1) * (tn // LANES) - up_last, 0) * LANES
    assert w_in.shape[-1] % LANES == 0 and over < tn
    return pl.pallas_call(
        functools.partial(_ffn_in_kernel, tm=tm, tn=tn, ff=ff, sh=sh, over=over, tiles_per_seq=S // tm),
        grid=(T // tm, fp // tn),
        in_specs=[
            pl.BlockSpec((tm, D), lambda i, j: (i, 0)),
            pl.BlockSpec((CONV_HALO, D), lambda i, j: (jnp.maximum(i * hb - 1, 0), 0)),
            pl.BlockSpec((None, D, tn), lambda i, j: (layer, 0, j)),
            pl.BlockSpec((pl.Element(1), pl.Element(D), pl.Element(tn)),
                         lambda i, j: (layer, 0, jnp.minimum(up0 + j * (tn // LANES), up_last) * LANES)),
            pl.BlockSpec((None, CONV_WIDTH, tn), lambda i, j: (layer, 0, j)),
            pl.BlockSpec((None, 1, tn), lambda i, j: (layer, 0, j)),
        ],
        out_specs=pl.BlockSpec((tm, tn), lambda i, j: (i, j)),
        out_shape=jax.ShapeDtypeStruct((T, fp), BF16),
        scratch_shapes=[pltpu.VMEM((CONV_HALO + tm, D), BF16), pltpu.VMEM((CONV_HALO + tm, tn), F32),
                        pltpu.VMEM((tm, LANES), F32)],
        compiler_params=_cparams("parallel", "arbitrary"),
        name="ffn_in_conv_act",
    )(u, u, w_in, w_in, conv_w, conv_b)


def _place_rows_kernel(x_ref, o_ref, *, rb, ff, sh, nb):
    k = pl.program_id(1)
    x = x_ref[0]
    o_ref[...] = x.astype(o_ref.dtype)

    @pl.when(k == 0)
    def _():
        o_ref[0:sh, :] = jnp.zeros((sh, x.shape[1]), o_ref.dtype)
        o_ref[sh:rb, :] = x[0:rb - sh].astype(o_ref.dtype)

    up = rb * (nb - 1) - sh - (ff - rb)
    if up > 0:
        @pl.when(k == nb - 1)
        def _():
            o_ref[0:rb - up, :] = x[up:rb].astype(o_ref.dtype)
            o_ref[rb - up:rb, :] = jnp.zeros((up, x.shape[1]), o_ref.dtype)


def _place_rows(w, sh, fp, rb=512):
    depth, ff, D = w.shape
    nb = fp // rb
    assert fp % rb == 0 and sh % SUBLANES == 0 and 0 < sh < rb and (ff - rb) % SUBLANES == 0
    assert rb * (nb - 2) - sh <= ff - rb < rb * (nb - 1) - sh + rb
    last = (ff - rb) // SUBLANES
    return pl.pallas_call(
        functools.partial(_place_rows_kernel, rb=rb, ff=ff, sh=sh, nb=nb),
        grid=(depth, nb),
        in_specs=[pl.BlockSpec(
            (pl.Element(1), pl.Element(rb), pl.Element(D)),
            lambda d, k: (d, jnp.clip(k * (rb // SUBLANES) - sh // SUBLANES, 0, last) * SUBLANES, 0))],
        out_specs=pl.BlockSpec((None, rb, D), lambda d, k: (d, k, 0)),
        out_shape=jax.ShapeDtypeStruct((depth, fp, D), BF16),
        compiler_params=_cparams("parallel", "arbitrary"),
        name="ffn_w_out_place",
    )(w)


def _ffn_weights(w_in, w_out):
    depth, D, ff2 = w_in.shape
    ff = ff2 // 2
    sh = ff % LANES
    fp = _round_up(ff + sh, FFN_COL_TILE)
    w_in = w_in.astype(BF16)
    if sh:
        w_out = _place_rows(w_out, sh, fp)
    else:
        w_out = jnp.pad(w_out.astype(BF16), ((0, 0), (0, fp - ff), (0, 0)))
    return w_in, w_out, ff, sh, fp


def _conv_ffn(h, u, ln_post, next_gain, w_in, w_out, ff, sh, fp, conv_w, conv_b, layer, S):
    act = _ffn_in(u, w_in, conv_w, conv_b, layer, ff, sh, fp, S)
    return _mm_norm_res([act], w_out, h, ln_post, next_gain, w_lead=layer)


def _ab_layer(h, u, ln_post, next_gain, w_in, pool_w, pool_scale, cmp_pos, ck_w1, ck_w2, cv_w1, cv_w2, w_out, B, S):
    T, D = h.shape
    G, DH = NSA_KV_GROUPS, NSA_HEAD_DIM
    pw = D // 4
    hw = D - pw
    kvw = G * DH
    w = w_in.astype(BF16)
    c0 = pw + hw
    p_in = _mm(u, w, F32, 0, pw)
    q = _mm(u, w, BF16, pw, hw, scale=DH ** -0.5 * LOG2E)
    kv_cmp = _mm(u, w, F32, c0, 2 * kvw)
    kv = _mm(u, w, BF16, c0 + 2 * kvw, 4 * kvw)
    gates = _mm(u, w, F32, c0 + 6 * kvw, LANES)

    y_a = _pool_mixer(p_in.reshape(B, S, pw), pool_w, pool_scale).reshape(T, pw)
    kvc = _compress(kv_cmp.reshape(B, S, 2 * kvw), cmp_pos, jnp.stack([ck_w1, cv_w1]), jnp.stack([ck_w2, cv_w2]), G)
    y_b = _nsa(q, gates, kvc, kv.reshape(B, S, 4 * kvw), B, S)
    return _mm_norm_res([y_a, y_b], w_out.astype(BF16), h, ln_post, next_gain)


def _c_layer(h, u, ln_post, next_gain, w_in, b_if, head_norm, w_out, B, S):
    T, D = h.shape
    H = MLSTM_HEADS
    dv = D // H
    dk = dv // 2
    qk = H * dk
    w = w_in.astype(BF16)
    q = _mm(u, w, BF16, 0, qk, scale=dk ** -0.5)
    k = _mm(u, w, BF16, qk, qk)
    v = _mm(u, w, BF16, 2 * qk, D)
    o = _mm(u, w, BF16, 2 * qk + D, D)
    gates = _mm(u, w, F32, 2 * qk + 2 * D, LANES, bias=_pad_cols(b_if.reshape(1, 2 * H), LANES))
    kt = k.reshape(B, S, qk).transpose(0, 2, 1)
    y = _mlstm(q, kt, v, o, gates, head_norm, B, S)
    return _mm_norm_res([y], w_out.astype(BF16), h, ln_post, next_gain)


def kernel(x, ln_pre, ln_post, w_in_ab, pool_w, pool_scale, cmp_pos, cmp_k_w1, cmp_k_w2, cmp_v_w1, cmp_v_w2,
           w_out_ab, w_in_c, b_if_c, head_norm_c, w_out_c, ffn_ln_pre, ffn_ln_post, ffn_w_in, ffn_conv_w,
           ffn_conv_b, ffn_w_out):
    B, S, D = x.shape
    depth = ln_pre.shape[0]
    h = x.reshape(B * S, D)
    u = _rmsnorm(h, ln_pre[0])
    ffn_wi, ffn_wo, ff, sh, fp = _ffn_weights(ffn_w_in, ffn_w_out)
    ffn_cb = ffn_conv_b.reshape(depth, 1, -1)
    for layer in range(depth):
        i = layer // 2
        if layer % 2 == 0:
            h, u = _ab_layer(h, u, ln_post[layer], ffn_ln_pre[layer], w_in_ab[i], pool_w[i], pool_scale[i],
                             cmp_pos[i], cmp_k_w1[i], cmp_k_w2[i], cmp_v_w1[i], cmp_v_w2[i], w_out_ab[i], B, S)
        else:
            h, u = _c_layer(h, u, ln_post[layer], ffn_ln_pre[layer], w_in_c[i], b_if_c[i], head_norm_c[i],
                            w_out_c[i], B, S)
        next_gain = ln_pre[layer + 1] if layer + 1 < depth else None
        h, u = _conv_ffn(h, u, ffn_ln_post[layer], next_gain, ffn_wi, ffn_wo, ff, sh, fp, ffn_conv_w, ffn_cb, layer,
                         S)
    return h.reshape(B, S, D)
```

```python
import functools
import math

import jax
import jax.numpy as jnp
from jax import lax
from jax.experimental import pallas as pl
from jax.experimental.pallas import tpu as pltpu

F32 = jnp.float32
BF16 = jnp.bfloat16

RMS_EPS = 1e-6
POOL_GROUPS = 4
POOL_WINDOWS = (2, 4, 8, 16)
POOL_HALO = 16
NSA_HEAD_DIM = 128
NSA_KV_GROUPS = 4
N_BRANCH = 3
CMP_BLOCK = 32
CMP_STRIDE = 16
SLC_BLOCK = 64
SLC_TOPN = 16
SWA_WINDOW = 512
FORCE_BONUS = 1e4
NEG = -1e30
BIG = 1e30
MLSTM_HEADS = 8
CONV_WIDTH = 3
CONV_HALO = 16

LANES = 128
SUBLANES = 8
LOG2E = math.log2(math.e)
NSA_Q_TILE = LANES
SEL_TILE = 512
SEL_GROUP = 4
RANK_SIZES = 8
SWA_SPAN = SWA_WINDOW + NSA_Q_TILE
MLSTM_CHUNK = 256
MLSTM_HEADS_PER_STEP = 4
FFN_COL_TILE = 512
VMEM_LIMIT = 56 * 1024 * 1024


def _cparams(*sem):
    return pltpu.CompilerParams(dimension_semantics=sem, vmem_limit_bytes=VMEM_LIMIT)


def _round_up(n, m):
    return (n + m - 1) // m * m


def _pad_cols(w, n):
    return jnp.pad(w, ((0, 0), (0, n - w.shape[1])))


def _rmsnorm_kernel(x_ref, g_ref, o_ref):
    x = x_ref[...]
    ms = jnp.mean(x * x, axis=-1, keepdims=True)
    o_ref[...] = (x * lax.rsqrt(ms + RMS_EPS) * g_ref[...]).astype(o_ref.dtype)


def _rmsnorm(x, gain, tm=512):
    T, D = x.shape
    return pl.pallas_call(
        _rmsnorm_kernel,
        grid=(T // tm,),
        in_specs=[pl.BlockSpec((tm, D), lambda i: (i, 0)), pl.BlockSpec((1, D), lambda i: (0, 0))],
        out_specs=pl.BlockSpec((tm, D), lambda i: (i, 0)),
        out_shape=jax.ShapeDtypeStruct((T, D), BF16),
        compiler_params=_cparams("parallel"),
        name="rmsnorm",
    )(x, gain.reshape(1, D))


def _cast_kernel(x_ref, o_ref):
    o_ref[...] = x_ref[...].astype(o_ref.dtype)


def _to_bf16(w, rows=256):
    R, C = w.shape
    rows = min(rows, R)
    assert R % rows == 0
    return pl.pallas_call(
        _cast_kernel,
        grid=(R // rows,),
        in_specs=[pl.BlockSpec((rows, C), lambda i: (i, 0))],
        out_specs=pl.BlockSpec((rows, C), lambda i: (i, 0)),
        out_shape=jax.ShapeDtypeStruct((R, C), BF16),
        compiler_params=_cparams("parallel"),
        name="weight_cast",
    )(w)


def _mm_kernel(a_ref, b_ref, bias_ref, o_ref, *, scale, live_cols):
    acc = jnp.dot(a_ref[...], b_ref[...], preferred_element_type=F32)
    out = (acc + bias_ref[...]) * scale
    if live_cols is not None:
        col = pl.program_id(1) * out.shape[1] + lax.broadcasted_iota(jnp.int32, (1, out.shape[1]), 1)
        out = jnp.where(col < live_cols, out, 0.0)
    o_ref[...] = out.astype(o_ref.dtype)


def _mm(a, w, out_dtype, col0, ncols, *, bias=None, scale=1.0, tm=1024, tn=1024):
    M, K = a.shape
    tm = min(tm, M)
    tn = math.gcd(tn, ncols, col0)
    assert M % tm == 0 and tn % LANES == 0
    jb = col0 // tn
    live_cols = w.shape[1] - col0 if col0 + ncols > w.shape[1] else None
    if bias is None:
        bias = jnp.zeros((1, ncols), F32)
    return pl.pallas_call(
        functools.partial(_mm_kernel, scale=scale, live_cols=live_cols),
        grid=(M // tm, ncols // tn),
        in_specs=[
            pl.BlockSpec((tm, K), lambda i, j: (i, 0)),
            pl.BlockSpec((K, tn), lambda i, j: (0, j + jb)),
            pl.BlockSpec((1, tn), lambda i, j: (0, j)),
        ],
        out_specs=pl.BlockSpec((tm, tn), lambda i, j: (i, j)),
        out_shape=jax.ShapeDtypeStruct((M, ncols), out_dtype),
        compiler_params=_cparams("parallel", "arbitrary"),
        name="proj",
    )(a, w, bias)


def _mm_norm_res_kernel(*refs, nk, nk1, n_a, emit_u, tm, rows):
    a_refs = refs[:n_a]
    w_ref, h_hbm, g_ref = refs[n_a:n_a + 3]
    rest = refs[n_a + 3:]
    if emit_u:
        ng_ref, o_ref, u_ref, h_buf, h_sem = rest
    else:
        o_ref, h_buf, h_sem = rest
    i = pl.program_id(0)
    k = pl.program_id(1)

    def h_copy():
        return pltpu.make_async_copy(h_hbm.at[pl.ds(i * tm, tm), :], h_buf, h_sem)

    def product(a_ref):
        return jnp.dot(a_ref[...], w_ref[...], preferred_element_type=F32)

    @pl.when(k == 0)
    def _():
        h_copy().start()
        o_ref[...] = product(a_refs[0])

    if nk1 > 1 or n_a == 1:
        @pl.when((k > 0) & (k < nk1))
        def _():
            o_ref[...] += product(a_refs[0])

    if n_a == 2:
        @pl.when(k >= nk1)
        def _():
            o_ref[...] += product(a_refs[1])

    @pl.when(k == nk - 1)
    def _():
        h_copy().wait()

        def norm_rows(c, carry):
            r = pl.ds(pl.multiple_of(c * rows, rows), rows)
            y = o_ref[r, :]
            ms = jnp.mean(y * y, axis=-1, keepdims=True)
            h_new = h_buf[r, :] + y * lax.rsqrt(ms + RMS_EPS) * g_ref[...]
            o_ref[r, :] = h_new
            if emit_u:
                ms2 = jnp.mean(h_new * h_new, axis=-1, keepdims=True)
                u_ref[r, :] = (h_new * lax.rsqrt(ms2 + RMS_EPS) * ng_ref[...]).astype(u_ref.dtype)
            return carry

        lax.fori_loop(0, tm // rows, norm_rows, 0)


def _mm_norm_res(a_list, w, h, gain, next_gain=None, w_lead=None, tm=512, tk=1024, rows=64):
    M = h.shape[0]
    N = w.shape[-1]
    n_a = len(a_list)
    tk = math.gcd(tk, *[a.shape[1] for a in a_list])
    assert n_a in (1, 2) and M % tm == 0 and tm % rows == 0
    nk1 = a_list[0].shape[1] // tk
    nk = sum(a.shape[1] for a in a_list) // tk
    emit_u = next_gain is not None
    a_specs = [pl.BlockSpec((tm, tk), lambda i, k: (i, jnp.minimum(k, nk1 - 1)))]
    if n_a == 2:
        a_specs.append(pl.BlockSpec((tm, tk), lambda i, k: (i, jnp.maximum(k - nk1, 0))))
    row_spec = pl.BlockSpec((tm, N), lambda i, k: (i, 0))
    vec_spec = pl.BlockSpec((1, N), lambda i, k: (0, 0))
    if w_lead is None:
        w_spec = pl.BlockSpec((tk, N), lambda i, k: (k, 0))
    else:
        w_spec = pl.BlockSpec((None, tk, N), lambda i, k: (w_lead, k, 0))
    in_specs = a_specs + [w_spec, pl.BlockSpec(memory_space=pl.ANY), vec_spec]
    args = list(a_list) + [w, h, gain.reshape(1, N)]
    out_specs, out_shape = row_spec, jax.ShapeDtypeStruct((M, N), F32)
    if emit_u:
        in_specs.append(vec_spec)
        args.append(next_gain.reshape(1, N))
        out_specs = [row_spec, row_spec]
        out_shape = [out_shape, jax.ShapeDtypeStruct((M, N), BF16)]
    out = pl.pallas_call(
        functools.partial(_mm_norm_res_kernel, nk=nk, nk1=nk1, n_a=n_a, emit_u=emit_u, tm=tm, rows=rows),
        grid=(M // tm, nk),
        in_specs=in_specs,
        out_specs=out_specs,
        out_shape=out_shape,
        scratch_shapes=[pltpu.VMEM((tm, N), F32), pltpu.SemaphoreType.DMA(())],
        compiler_params=_cparams("parallel", "arbitrary"),
        name="out_proj_norm_res",
    )(*args)
    return out if emit_u else (out, None)


def _pool_kernel(cur_ref, prev_ref, w_ref, scale_ref, o_ref, ext_ref, *, ts, gd):
    i = pl.program_id(1)
    ext_ref[0:POOL_HALO, :] = jnp.where(i > 0, prev_ref[0], 0.0)
    ext_ref[POOL_HALO:POOL_HALO + ts, :] = cur_ref[0]
    t = i * ts + lax.broadcasted_iota(jnp.int32, (ts, 1), 0)
    for g, win in enumerate(POOL_WINDOWS):
        cols = slice(g * gd, (g + 1) * gd)
        tok = ext_ref[POOL_HALO:POOL_HALO + ts, cols]
        acc = tok
        for back in range(1, win):
            acc = acc + ext_ref[POOL_HALO - back:POOL_HALO - back + ts, cols]
        count = jnp.minimum(t + 1, win).astype(F32)
        pooled = acc / count - tok
        y = jnp.dot(pooled.astype(BF16), w_ref[g], preferred_element_type=F32)
        o_ref[0, :, cols] = (y * scale_ref[:, cols]).astype(o_ref.dtype)


def _pool_mixer(p_in, w_pool, scale, ts=512):
    B, S, PW = p_in.shape
    gd = PW // POOL_GROUPS
    hb = ts // POOL_HALO
    return pl.pallas_call(
        functools.partial(_pool_kernel, ts=ts, gd=gd),
        grid=(B, S // ts),
        in_specs=[
            pl.BlockSpec((1, ts, PW), lambda b, i: (b, i, 0)),
            pl.BlockSpec((1, POOL_HALO, PW), lambda b, i: (b, jnp.maximum(i * hb - 1, 0), 0)),
            pl.BlockSpec((POOL_GROUPS, gd, gd), lambda b, i: (0, 0, 0)),
            pl.BlockSpec((1, PW), lambda b, i: (0, 0)),
        ],
        out_specs=pl.BlockSpec((1, ts, PW), lambda b, i: (b, i, 0)),
        out_shape=jax.ShapeDtypeStruct((B, S, PW), BF16),
        scratch_shapes=[pltpu.VMEM((POOL_HALO + ts, PW), F32)],
        compiler_params=_cparams("parallel", "arbitrary"),
        name="pool_mixer",
    )(p_in, p_in, w_pool.astype(BF16), scale.reshape(1, PW))


def _gelu_tanh(x):
    c = math.sqrt(2.0 / math.pi)
    return x * (0.5 * (1.0 + jnp.tanh(c * (x + 0.044715 * (x * x * x)))))


def _compress_kernel(x_ref, pos_ref, w1_ref, w2_ref, o_ref, tmp_ref, *, nch):
    half = CMP_BLOCK // 2
    dh = x_ref.shape[2]
    a = b = None
    for r in range(half):
        x = x_ref[0, pl.ds(r, nch, stride=CMP_STRIDE), :]
        lo = jnp.dot((x + pos_ref[r:r + 1, :]).astype(BF16), w1_ref[0, r * dh:(r + 1) * dh, :],
                     preferred_element_type=F32)
        hi = jnp.dot((x + pos_ref[half + r:half + r + 1, :]).astype(BF16),
                     w1_ref[0, (half + r) * dh:(half + r + 1) * dh, :], preferred_element_type=F32)
        a = lo if a is None else a + lo
        b = hi if b is None else b + hi
    tmp_ref[0:nch, :] = b
    tmp_ref[nch:nch + SUBLANES, :] = jnp.zeros((SUBLANES, b.shape[1]), F32)
    pre = a + tmp_ref[1:nch + 1, :]
    out = jnp.dot(_gelu_tanh(pre).astype(BF16), w2_ref[0], preferred_element_type=F32)
    row = lax.broadcasted_iota(jnp.int32, out.shape, 0)
    o_ref[0, 0] = jnp.where(row < nch - 1, out, 0.0).astype(o_ref.dtype)


def _compress(x, pos, w1, w2, G):
    B, S, _ = x.shape
    hid = w1.shape[-1]
    dh = w2.shape[-1]
    nch = S // CMP_STRIDE
    assert CMP_BLOCK == 2 * CMP_STRIDE
    return pl.pallas_call(
        functools.partial(_compress_kernel, nch=nch),
        grid=(2, B * G),
        in_specs=[
            pl.BlockSpec((1, S, dh), lambda s, b: (b // G, 0, s * G + b % G)),
            pl.BlockSpec((CMP_BLOCK, dh), lambda s, b: (0, 0)),
            pl.BlockSpec((1, CMP_BLOCK * dh, hid), lambda s, b: (s, 0, 0)),
            pl.BlockSpec((1, hid, dh), lambda s, b: (s, 0, 0)),
        ],
        out_specs=pl.BlockSpec((1, 1, nch, dh), lambda s, b: (s, b, 0, 0)),
        out_shape=jax.ShapeDtypeStruct((2, B * G, nch, dh), BF16),
        scratch_shapes=[pltpu.VMEM((nch + SUBLANES, hid), F32)],
        compiler_params=_cparams("parallel", "arbitrary"),
        name="nsa_compress",
    )(x, pos, w1.astype(BF16), w2.astype(BF16))


def _nt_dot(a, b):
    return lax.dot_general(a, b, (((1,), (1,)), ((), ())), preferred_element_type=F32)


def _split_dot(w, x):
    hi = x.astype(BF16)
    r1 = x - hi.astype(F32)
    mid = r1.astype(BF16)
    lo = (r1 - mid.astype(F32)).astype(BF16)
    return (jnp.dot(w, hi, preferred_element_type=F32) + jnp.dot(w, mid, preferred_element_type=F32)
            + jnp.dot(w, lo, preferred_element_type=F32))


def _nsa_kernel(q_ref, gl_ref, kc_ref, vct_ref, ks_ref, vst_ref, kw_ref, vwt_ref, ovt_ref, et_ref, o_ref,
                m_ref, l_ref, acc_ref, s_ref, bias_ref, *, hg, n_cmp, n_slc, top_n):
    QT, DH = NSA_Q_TILE, NSA_HEAD_DIM
    t0 = pl.program_id(1) * QT
    q = jnp.concatenate([q_ref[:, h * DH:(h + 1) * DH] for h in range(hg)], axis=0)
    lane = lax.broadcasted_iota(jnp.int32, (1, QT), 1)
    t = t0 + lane

    def capped(s, cap):
        return jnp.concatenate([jnp.minimum(s[:, h * QT:(h + 1) * QT], cap) for h in range(hg)], axis=1)

    def softmax_cols(s):
        p = jnp.exp2(s - jnp.max(s, axis=0, keepdims=True))
        return p, jnp.sum(p, axis=0, keepdims=True)

    ncp = kc_ref.shape[2]
    nrow = lax.broadcasted_iota(jnp.int32, (ncp, QT), 0)
    valid_c = (nrow * CMP_STRIDE + (CMP_BLOCK - 1) <= t) & (nrow < n_cmp)
    s_c = capped(_nt_dot(kc_ref[0, 0], q), jnp.where(valid_c, BIG, NEG))
    ws = pl.multiple_of(jnp.maximum(t0 + QT - SWA_SPAN, 0), LANES)
    wb = ws // LANES
    dist = t - (ws + lax.broadcasted_iota(jnp.int32, (SWA_SPAN, QT), 0))
    valid_w = (dist >= 0) & (dist < SWA_WINDOW)
    s_w = capped(_nt_dot(kw_ref[0, pl.ds(ws, SWA_SPAN), :], q), jnp.where(valid_w, BIG, NEG))

    p_c, l_c = softmax_cols(s_c)
    seen = jnp.concatenate([jnp.where(t >= CMP_BLOCK - 1, 1.0, 0.0)] * hg, axis=1)
    p_c = p_c * (seen / l_c)
    o_c = jnp.dot(vct_ref[0], p_c.astype(BF16), preferred_element_type=F32)

    p_w, l_w = softmax_cols(s_w)
    vwin = jnp.concatenate([vwt_ref[0, wb + j] for j in range(SWA_SPAN // LANES)], axis=1)
    o_w = jnp.dot(vwin, p_w.astype(BF16), preferred_element_type=F32) / l_w

    p_sum = p_c[:, 0:QT]
    for h in range(1, hg):
        p_sum = p_sum + p_c[:, h * QT:(h + 1) * QT]
    imp = _split_dot(ovt_ref[...], p_sum)
    blk = lax.broadcasted_iota(jnp.int32, (LANES, QT), 0)
    cur = t0 // SLC_BLOCK + jnp.zeros((1, QT), jnp.int32)
    for k in range(1, QT // SLC_BLOCK):
        cur = cur + jnp.where(lane >= k * SLC_BLOCK, 1, 0)
    forced = (blk == 0) | (blk == cur) | (blk == cur - 1)
    val = jnp.where(forced, imp + FORCE_BONUS, jnp.where(blk > cur, -FORCE_BONUS, imp))
    if n_slc < LANES:
        val = jnp.where(blk < n_slc, val, -jnp.inf)

    def rank_bias(nb):
        SL = SUBLANES
        nslab = nb // SL
        slabs = [val[SL * r:SL * r + SL, :] for r in range(nslab)]
        ranks = [jnp.zeros((SL, QT), F32) for _ in range(nslab)]
        sub = lax.broadcasted_iota(jnp.int32, (SL, QT), 0)
        for other in range(nb):
            c = val[other:other + 1, :]
            for r in range(nslab):
                if SL * r > other:
                    beat = c >= slabs[r]
                elif SL * r + SL - 1 < other:
                    beat = c > slabs[r]
                else:
                    beat = (c > slabs[r]) | ((c == slabs[r]) & (sub + SL * r > other))
                ranks[r] = ranks[r] + jnp.where(beat, 1.0, 0.0)
        for r in range(nslab):
            chosen = (ranks[r] < top_n) & (blk[SL * r:SL * r + SL] <= cur)
            bias_ref[SL * r:SL * r + SL, :] = jnp.where(chosen, 0.0, NEG)
        if nb < LANES:
            bias_ref[nb:LANES, :] = jnp.full((LANES - nb, QT), NEG, F32)

    live = (t0 + QT - 1) // SLC_BLOCK + 1
    sizes = sorted({min(_round_up(-(-n_slc * k // RANK_SIZES), SUBLANES), LANES) for k in range(1, RANK_SIZES + 1)})
    for lo, nb in zip([0] + sizes[:-1], sizes):
        @pl.when((live > lo) & (live <= nb))
        def _(nb=nb):
            rank_bias(nb)

    bias_q = bias_ref[...].T.astype(BF16)

    q_sel = jnp.concatenate([q, jnp.concatenate([bias_q] * hg, axis=0)], axis=1)
    m_ref[...] = jnp.full(m_ref.shape, NEG, F32)
    l_ref[...] = jnp.zeros(l_ref.shape, F32)
    acc_ref[...] = jnp.zeros(acc_ref.shape, F32)

    def sel_tiles(tiles):
        for slot, (kb, _) in enumerate(tiles):
            start = pl.multiple_of(kb * SEL_TILE, SEL_TILE)
            keys = jnp.concatenate([ks_ref[0, pl.ds(start, SEL_TILE), :], et_ref[kb]], axis=1)
            s_ref[slot] = _nt_dot(keys, q_sel)
        for slot, (kb, diagonal) in enumerate(tiles):
            s = s_ref[slot]
            if diagonal:
                key = kb * SEL_TILE + lax.broadcasted_iota(jnp.int32, (SEL_TILE, QT), 0)
                s = capped(s, jnp.where(key <= t, BIG, NEG))
            m_prev = m_ref[...]
            m_new = jnp.maximum(m_prev, jnp.max(s, axis=0, keepdims=True))
            alpha = jnp.exp2(m_prev - m_new)
            p = jnp.exp2(s - m_new)
            l_ref[...] = alpha * l_ref[...] + jnp.sum(p, axis=0, keepdims=True)
            acc_ref[...] = alpha * acc_ref[...] + jnp.dot(vst_ref[0, kb], p.astype(BF16),
                                                          preferred_element_type=F32)
            m_ref[...] = m_new

    def past_group(i, carry):
        sel_tiles([(SEL_GROUP * i + slot, False) for slot in range(SEL_GROUP)])
        return carry

    diag = t0 // SEL_TILE
    groups = diag // SEL_GROUP
    lax.fori_loop(0, groups, past_group, 0)
    for rem in range(SEL_GROUP):
        @pl.when(diag - groups * SEL_GROUP == rem)
        def _(rem=rem):
            sel_tiles([(diag - rem + slot, slot == rem) for slot in range(rem + 1)])

    o_s = acc_ref[...] / l_ref[...]

    gate = jax.nn.sigmoid(gl_ref[0, 0])
    for h in range(hg):
        cols = slice(h * QT, (h + 1) * QT)
        r = N_BRANCH * h
        out = gate[r:r + 1] * o_c[:, cols] + gate[r + 1:r + 2] * o_s[:, cols] + gate[r + 2:r + 3] * o_w[:, cols]
        o_ref[:, h * DH:(h + 1) * DH] = out.T.astype(o_ref.dtype)


def _nsa(q, gates, kvc, kv, B, S):
    T, HW = q.shape
    G, DH, QT = NSA_KV_GROUPS, NSA_HEAD_DIM, NSA_Q_TILE
    hg = HW // DH // G
    n_cmp = (S - CMP_BLOCK) // CMP_STRIDE + 1
    ncp = kvc.shape[2]
    n_slc = S // SLC_BLOCK
    top_n = min(SLC_TOPN, n_slc)
    nqt = S // QT
    kvw = G * DH
    assert n_slc <= LANES and S % SEL_TILE == 0 and S >= SWA_SPAN and QT == LANES

    cs = jnp.arange(ncp)[None, :] * CMP_STRIDE
    ss = jnp.arange(LANES)[:, None] * SLC_BLOCK
    overlap = jnp.clip(jnp.minimum(cs + CMP_BLOCK, ss + SLC_BLOCK) - jnp.maximum(cs, ss), 0) // CMP_STRIDE
    overlap = jnp.where((jnp.arange(ncp)[None, :] < n_cmp) & (jnp.arange(LANES)[:, None] < n_slc), overlap, 0)
    overlap = overlap.astype(BF16)
    key_blk = (jnp.arange(S) // SLC_BLOCK).reshape(S // SEL_TILE, SEL_TILE, 1)
    expand = (key_blk == jnp.arange(LANES)[None, None, :]).astype(BF16)

    gr = _round_up(N_BRANCH * hg, 8)
    gl = gates[:, :G * hg * N_BRANCH].reshape(B, nqt, QT, G, hg * N_BRANCH)
    gl = jnp.pad(gl, ((0, 0),) * 4 + ((0, gr - hg * N_BRANCH),)).transpose(0, 3, 1, 4, 2).reshape(B * G, nqt, gr, QT)

    def v_tiles(which, tile):
        v = kv[:, :, which * kvw:(which + 1) * kvw].reshape(B, S // tile, tile, G, DH)
        return v.transpose(0, 3, 1, 4, 2).reshape(B * G, S // tile, DH, tile)

    def k_spec(which):
        return pl.BlockSpec((1, S, DH), lambda bg, i: (bg // G, 0, which * G + bg % G))

    return pl.pallas_call(
        functools.partial(_nsa_kernel, hg=hg, n_cmp=n_cmp, n_slc=n_slc, top_n=top_n),
        grid=(B * G, nqt),
        in_specs=[
            pl.BlockSpec((QT, hg * DH), lambda bg, i: ((bg // G) * nqt + i, bg % G)),
            pl.BlockSpec((1, 1, gr, QT), lambda bg, i: (bg, i, 0, 0)),
            pl.BlockSpec((1, 1, ncp, DH), lambda bg, i: (0, bg, 0, 0)),
            pl.BlockSpec((1, DH, ncp), lambda bg, i: (bg, 0, 0)),
            k_spec(0),
            pl.BlockSpec((1, S // SEL_TILE, DH, SEL_TILE), lambda bg, i: (bg, 0, 0, 0)),
            k_spec(2),
            pl.BlockSpec((1, S // LANES, DH, LANES), lambda bg, i: (bg, 0, 0, 0)),
            pl.BlockSpec((LANES, ncp), lambda bg, i: (0, 0)),
            pl.BlockSpec((S // SEL_TILE, SEL_TILE, LANES), lambda bg, i: (0, 0, 0)),
        ],
        out_specs=pl.BlockSpec((QT, hg * DH), lambda bg, i: ((bg // G) * nqt + i, bg % G)),
        out_shape=jax.ShapeDtypeStruct((T, HW), BF16),
        scratch_shapes=[pltpu.VMEM((1, hg * QT), F32), pltpu.VMEM((1, hg * QT), F32),
                        pltpu.VMEM((DH, hg * QT), F32), pltpu.VMEM((SEL_GROUP, SEL_TILE, hg * QT), F32),
                        pltpu.VMEM((LANES, QT), F32)],
        compiler_params=_cparams("parallel", "arbitrary"),
        name="nsa_attention",
    )(q, gl, kvc, jnp.swapaxes(kvc[1], 1, 2), kv, v_tiles(1, SEL_TILE), kv, v_tiles(3, LANES), overlap, expand)


def _mlstm_kernel(q_ref, kt_ref, v_ref, o_ref, gr_ref, gc_ref, hn_ref, y_ref, c_ref, n_ref, m_ref, *, L, hps):
    ci = pl.program_id(1)
    dk = kt_ref.shape[1] // hps
    dv = v_ref.shape[1] // hps

    @pl.when(ci == 0)
    def _():
        c_ref[...] = jnp.zeros_like(c_ref)
        n_ref[...] = jnp.zeros_like(n_ref)
        m_ref[...] = jnp.zeros_like(m_ref)

    ti = lax.broadcasted_iota(jnp.int32, (L, L), 0)
    si = lax.broadcasted_iota(jnp.int32, (L, L), 1)
    causal = si <= ti

    for hd in range(hps):
        q = q_ref[:, hd * dk:(hd + 1) * dk]
        kt = kt_ref[0, hd * dk:(hd + 1) * dk, :]
        v = v_ref[:, hd * dv:(hd + 1) * dv]
        ig_row = gr_ref[hd, 0, 0:1, :]
        lf_row = jax.nn.log_sigmoid(gr_ref[hd, 0, 1:2, :])
        lf_col = jax.nn.log_sigmoid(gc_ref[hd, 0, :, 1:2])
        m_prev = m_ref[hd, 0:1, 0:1]

        b_col = jnp.sum(jnp.where(causal, lf_row, 0.0), axis=1, keepdims=True)
        b_row = jnp.sum(jnp.where(ti <= si, lf_col, 0.0), axis=0, keepdims=True)
        b_last = b_col[L - 1:L, :]

        dmat = jnp.where(causal, b_col - b_row + ig_row, -jnp.inf)
        a_col = b_col + m_prev
        m_t = jnp.maximum(a_col, jnp.max(dmat, axis=1, keepdims=True))
        wq = jnp.dot(q, kt, preferred_element_type=F32) * jnp.exp(dmat - m_t)
        inter = jnp.exp(a_col - m_t)
        q_c = jnp.dot(q, c_ref[hd].astype(BF16), preferred_element_type=F32)
        q_n = jnp.dot(q, n_ref[hd].astype(BF16), preferred_element_type=F32)[:, 0:1]
        num = inter * q_c + jnp.dot(wq.astype(BF16), v, preferred_element_type=F32)
        den = inter * q_n + jnp.sum(wq, axis=1, keepdims=True)
        h = num / jnp.maximum(jnp.abs(den), jnp.exp(-m_t))
        h = h * lax.rsqrt(jnp.mean(h * h, axis=-1, keepdims=True) + RMS_EPS) * hn_ref[hd]
        gate_o = jax.nn.sigmoid(o_ref[:, hd * dv:(hd + 1) * dv])
        y_ref[:, hd * dv:(hd + 1) * dv] = (gate_o * h).astype(y_ref.dtype)

        g_row = b_last - b_row + ig_row
        m_new = jnp.maximum(b_last + m_prev, jnp.max(g_row, axis=1, keepdims=True))
        decay = jnp.exp(b_last + m_prev - m_new)
        kw_t = kt.astype(F32) * jnp.exp(g_row - m_new)
        c_ref[hd] = decay * c_ref[hd] + jnp.dot(kw_t.astype(BF16), v, preferred_element_type=F32)
        n_ref[hd] = decay * n_ref[hd] + jnp.sum(kw_t, axis=1, keepdims=True)
        m_ref[hd] = jnp.broadcast_to(m_new, m_ref.shape[1:])


def _mlstm(q, kt, v, o, gates, head_norm, B, S):
    T = q.shape[0]
    H = MLSTM_HEADS
    dk = q.shape[1] // H
    dv = v.shape[1] // H
    L = min(MLSTM_CHUNK, S)
    nc = S // L
    g = gates[:, :2 * H].reshape(B, nc, L, 2, H)
    g_row = jnp.transpose(g, (0, 4, 1, 3, 2)).reshape(B * H, nc, 2, L)
    g_col = jnp.transpose(g, (0, 4, 1, 2, 3)).reshape(B * H, nc, L, 2)
    hps = MLSTM_HEADS_PER_STEP
    hs = H // hps
    return pl.pallas_call(
        functools.partial(_mlstm_kernel, L=L, hps=hps),
        grid=(B * hs, nc),
        in_specs=[
            pl.BlockSpec((L, hps * dk), lambda bh, c: ((bh // hs) * nc + c, bh % hs)),
            pl.BlockSpec((1, hps * dk, L), lambda bh, c: (bh // hs, bh % hs, c)),
            pl.BlockSpec((L, hps * dv), lambda bh, c: ((bh // hs) * nc + c, bh % hs)),
            pl.BlockSpec((L, hps * dv), lambda bh, c: ((bh // hs) * nc + c, bh % hs)),
            pl.BlockSpec((hps, 1, 2, L), lambda bh, c: (bh, c, 0, 0)),
            pl.BlockSpec((hps, 1, L, 2), lambda bh, c: (bh, c, 0, 0)),
            pl.BlockSpec((hps, 1, dv), lambda bh, c: (bh % hs, 0, 0)),
        ],
        out_specs=pl.BlockSpec((L, hps * dv), lambda bh, c: ((bh // hs) * nc + c, bh % hs)),
        out_shape=jax.ShapeDtypeStruct((T, H * dv), BF16),
        scratch_shapes=[pltpu.VMEM((hps, dk, dv), F32), pltpu.VMEM((hps, dk, LANES), F32),
                        pltpu.VMEM((hps, SUBLANES, LANES), F32)],
        compiler_params=_cparams("parallel", "arbitrary"),
        name="mlstm",
    )(q, kt, v, o, g_row, g_col, head_norm.reshape(H, 1, dv))


def _ffn_in_kernel(u_ref, halo_ref, wg_ref, wu_ref, cw_ref, cb_ref, o_ref, lhs_ref, g_ref, tail_ref, *, tm, tn, ff,
                   sh, over, tiles_per_seq):
    i = pl.program_id(0)
    j = pl.program_id(1)

    @pl.when(j == 0)
    def _():
        halo = halo_ref[...]
        lhs_ref[0:CONV_HALO, :] = jnp.where(i % tiles_per_seq != 0, halo, jnp.zeros_like(halo))
        lhs_ref[CONV_HALO:CONV_HALO + tm, :] = u_ref[...]
        tail_ref[...] = jnp.zeros_like(tail_ref)

    g_ref[...] = jnp.dot(lhs_ref[...], wg_ref[...], preferred_element_type=F32)
    up = jnp.dot(u_ref[...], wu_ref[0], preferred_element_type=F32)
    if over:
        moved = jnp.concatenate([up[:, over:], up[:, :over]], axis=1)
        up = jnp.where(j == pl.num_programs(1) - 1, moved, up)
    gate = cb_ref[...] + cw_ref[0:1, :] * g_ref[CONV_HALO:CONV_HALO + tm, :]
    for back in range(1, CONV_WIDTH):
        gate = gate + cw_ref[back:back + 1, :] * g_ref[CONV_HALO - back:CONV_HALO - back + tm, :]
    if sh:
        lane = lax.broadcasted_iota(jnp.int32, (1, LANES), 1)
        tiles = [tail_ref[...]] + [gate[:, c * LANES:(c + 1) * LANES] for c in range(tn // LANES)]
        tail_ref[...] = tiles[-1]
        rolled = [pltpu.roll(x, sh, axis=1) for x in tiles]
        gate = jnp.concatenate([jnp.where(lane < sh, rolled[c], rolled[c + 1]) for c in range(tn // LANES)], axis=1)
    act = gate * jax.nn.sigmoid(gate) * up
    hidden = j * tn - sh + lax.broadcasted_iota(jnp.int32, (1, tn), 1)
    o_ref[...] = jnp.where((hidden >= 0) & (hidden < ff), act, 0.0).astype(o_ref.dtype)


def _ffn_in(u, w_in, conv_w, conv_b, layer, ff, sh, fp, S, tm=1024, tn=FFN_COL_TILE):
    T, D = u.shape
    tm = min(tm, S)
    assert T % tm == 0 and S % tm == 0 and fp % tn == 0 and tn % LANES == 0
    hb = tm // CONV_HALO
    up0 = (ff - sh) // LANES
    up_last = (w_in.shape[-1] - tn) // LANES
    over = max(up0 + (fp // tn - 1) * (tn // LANES) - up_last, 0) * LANES
    assert w_in.shape[-1] % LANES == 0 and over < tn
    return pl.pallas_call(
        functools.partial(_ffn_in_kernel, tm=tm, tn=tn, ff=ff, sh=sh, over=over, tiles_per_seq=S // tm),
        grid=(T // tm, fp // tn),
        in_specs=[
            pl.BlockSpec((tm, D), lambda i, j: (i, 0)),
            pl.BlockSpec((CONV_HALO, D), lambda i, j: (jnp.maximum(i * hb - 1, 0), 0)),
            pl.BlockSpec((None, D, tn), lambda i, j: (layer, 0, j)),
            pl.BlockSpec((pl.Element(1), pl.Element(D), pl.Element(tn)),
                         lambda i, j: (layer, 0, jnp.minimum(up0 + j * (tn // LANES), up_last) * LANES)),
            pl.BlockSpec((None, CONV_WIDTH, tn), lambda i, j: (layer, 0, j)),
            pl.BlockSpec((None, 1, tn), lambda i, j: (layer, 0, j)),
        ],
        out_specs=pl.BlockSpec((tm, tn), lambda i, j: (i, j)),
        out_shape=jax.ShapeDtypeStruct((T, fp), BF16),
        scratch_shapes=[pltpu.VMEM((CONV_HALO + tm, D), BF16), pltpu.VMEM((CONV_HALO + tm, tn), F32),
                        pltpu.VMEM((tm, LANES), F32)],
        compiler_params=_cparams("parallel", "arbitrary"),
        name="ffn_in_conv_act",
    )(u, u, w_in, w_in, conv_w, conv_b)


def _place_rows_kernel(x_ref, o_ref, *, rb, ff, sh, nb):
    k = pl.program_id(1)
    x = x_ref[0]
    o_ref[...] = x.astype(o_ref.dtype)

    @pl.when(k == 0)
    def _():
        o_ref[0:sh, :] = jnp.zeros((sh, x.shape[1]), o_ref.dtype)
        o_ref[sh:rb, :] = x[0:rb - sh].astype(o_ref.dtype)

    up = rb * (nb - 1) - sh - (ff - rb)
    if up > 0:
        @pl.when(k == nb - 1)
        def _():
            o_ref[0:rb - up, :] = x[up:rb].astype(o_ref.dtype)
            o_ref[rb - up:rb, :] = jnp.zeros((up, x.shape[1]), o_ref.dtype)


def _place_rows(w, sh, fp, rb=512):
    depth, ff, D = w.shape
    nb = fp // rb
    assert fp % rb == 0 and sh % SUBLANES == 0 and 0 < sh < rb and (ff - rb) % SUBLANES == 0
    assert rb * (nb - 2) - sh <= ff - rb < rb * (nb - 1) - sh + rb
    last = (ff - rb) // SUBLANES
    return pl.pallas_call(
        functools.partial(_place_rows_kernel, rb=rb, ff=ff, sh=sh, nb=nb),
        grid=(depth, nb),
        in_specs=[pl.BlockSpec(
            (pl.Element(1), pl.Element(rb), pl.Element(D)),
            lambda d, k: (d, jnp.clip(k * (rb // SUBLANES) - sh // SUBLANES, 0, last) * SUBLANES, 0))],
        out_specs=pl.BlockSpec((None, rb, D), lambda d, k: (d, k, 0)),
        out_shape=jax.ShapeDtypeStruct((depth, fp, D), BF16),
        compiler_params=_cparams("parallel", "arbitrary"),
        name="ffn_w_out_place",
    )(w)


def _ffn_weights(w_in, w_out):
    depth, D, ff2 = w_in.shape
    ff = ff2 // 2
    sh = ff % LANES
    fp = _round_up(ff + sh, FFN_COL_TILE)
    w_in = w_in.astype(BF16)
    if sh:
        w_out = _place_rows(w_out, sh, fp)
    else:
        w_out = jnp.pad(w_out.astype(BF16), ((0, 0), (0, fp - ff), (0, 0)))
    return w_in, w_out, ff, sh, fp


def _conv_ffn(h, u, ln_post, next_gain, w_in, w_out, ff, sh, fp, conv_w, conv_b, layer, S):
    act = _ffn_in(u, w_in, conv_w, conv_b, layer, ff, sh, fp, S)
    return _mm_norm_res([act], w_out, h, ln_post, next_gain, w_lead=layer)


def _ab_layer(h, u, ln_post, next_gain, w_in, pool_w, pool_scale, cmp_pos, ck_w1, ck_w2, cv_w1, cv_w2, w_out, B, S):
    T, D = h.shape
    G, DH = NSA_KV_GROUPS, NSA_HEAD_DIM
    pw = D // 4
    hw = D - pw
    kvw = G * DH
    w = _to_bf16(w_in)
    c0 = pw + hw
    p_in = _mm(u, w, F32, 0, pw)
    q = _mm(u, w, BF16, pw, hw, scale=DH ** -0.5 * LOG2E)
    kv_cmp = _mm(u, w, F32, c0, 2 * kvw)
    kv = _mm(u, w, BF16, c0 + 2 * kvw, 4 * kvw)
    gates = _mm(u, w, F32, c0 + 6 * kvw, LANES)

    y_a = _pool_mixer(p_in.reshape(B, S, pw), pool_w, pool_scale).reshape(T, pw)
    kvc = _compress(kv_cmp.reshape(B, S, 2 * kvw), cmp_pos, jnp.stack([ck_w1, cv_w1]), jnp.stack([ck_w2, cv_w2]), G)
    y_b = _nsa(q, gates, kvc, kv.reshape(B, S, 4 * kvw), B, S)
    return _mm_norm_res([y_a, y_b], _to_bf16(w_out), h, ln_post, next_gain)


def _c_layer(h, u, ln_post, next_gain, w_in, b_if, head_norm, w_out, B, S):
    T, D = h.shape
    H = MLSTM_HEADS
    dv = D // H
    dk = dv // 2
    qk = H * dk
    w = _to_bf16(w_in)
    q = _mm(u, w, BF16, 0, qk, scale=dk ** -0.5)
    k = _mm(u, w, BF16, qk, qk)
    v = _mm(u, w, BF16, 2 * qk, D)
    o = _mm(u, w, F32, 2 * qk + D, D)
    gates = _mm(u, w, F32, 2 * qk + 2 * D, LANES, bias=_pad_cols(b_if.reshape(1, 2 * H), LANES))
    kt = k.reshape(B, S, qk).transpose(0, 2, 1)
    y = _mlstm(q, kt, v, o, gates, head_norm, B, S)
    return _mm_norm_res([y], _to_bf16(w_out), h, ln_post, next_gain)


def kernel(x, ln_pre, ln_post, w_in_ab, pool_w, pool_scale, cmp_pos, cmp_k_w1, cmp_k_w2, cmp_v_w1, cmp_v_w2,
           w_out_ab, w_in_c, b_if_c, head_norm_c, w_out_c, ffn_ln_pre, ffn_ln_post, ffn_w_in, ffn_conv_w,
           ffn_conv_b, ffn_w_out):
    B, S, D = x.shape
    depth = ln_pre.shape[0]
    h = x.reshape(B * S, D)
    u = _rmsnorm(h, ln_pre[0])
    ffn_wi, ffn_wo, ff, sh, fp = _ffn_weights(ffn_w_in, ffn_w_out)
    ffn_cb = ffn_conv_b.reshape(depth, 1, -1)
    for layer in range(depth):
        i = layer // 2
        if layer % 2 == 0:
            h, u = _ab_layer(h, u, ln_post[layer], ffn_ln_pre[layer], w_in_ab[i], pool_w[i], pool_scale[i],
                             cmp_pos[i], cmp_k_w1[i], cmp_k_w2[i], cmp_v_w1[i], cmp_v_w2[i], w_out_ab[i], B, S)
        else:
            h, u = _c_layer(h, u, ln_post[layer], ffn_ln_pre[layer], w_in_c[i], b_if_c[i], head_norm_c[i],
                            w_out_c[i], B, S)
        next_gain = ln_pre[layer + 1] if layer + 1 < depth else None
        h, u = _conv_ffn(h, u, ffn_ln_post[layer], next_gain, ffn_wi, ffn_wo, ff, sh, fp, ffn_conv_w, ffn_cb, layer,
                         S)
    return h.reshape(B, S, D)
```

```python
import functools
import math

import jax
import jax.numpy as jnp
from jax import lax
from jax.experimental import pallas as pl
from jax.experimental.pallas import tpu as pltpu

F32 = jnp.float32
BF16 = jnp.bfloat16

RMS_EPS = 1e-6
POOL_GROUPS = 4
POOL_WINDOWS = (2, 4, 8, 16)
POOL_HALO = 16
NSA_HEAD_DIM = 128
NSA_KV_GROUPS = 4
N_BRANCH = 3
CMP_BLOCK = 32
CMP_STRIDE = 16
SLC_BLOCK = 64
SLC_TOPN = 16
SWA_WINDOW = 512
FORCE_BONUS = 1e4
NEG = -1e30
BIG = 1e30
MLSTM_HEADS = 8
CONV_WIDTH = 3
CONV_HALO = 16

LANES = 128
SUBLANES = 8
LOG2E = math.log2(math.e)
NSA_Q_TILE = LANES
SEL_TILE = 512
SEL_GROUP = 4
RANK_SIZES = 8
SWA_SPAN = SWA_WINDOW + NSA_Q_TILE
MLSTM_CHUNK = 256
MLSTM_HEADS_PER_STEP = 4
FFN_COL_TILE = 512
VMEM_LIMIT = 56 * 1024 * 1024


def _cparams(*sem):
    return pltpu.CompilerParams(dimension_semantics=sem, vmem_limit_bytes=VMEM_LIMIT)


def _round_up(n, m):
    return (n + m - 1) // m * m


def _pad_cols(w, n):
    return jnp.pad(w, ((0, 0), (0, n - w.shape[1])))


def _rmsnorm_kernel(x_ref, g_ref, o_ref):
    x = x_ref[...]
    ms = jnp.mean(x * x, axis=-1, keepdims=True)
    o_ref[...] = (x * lax.rsqrt(ms + RMS_EPS) * g_ref[...]).astype(o_ref.dtype)


def _rmsnorm(x, gain, tm=512):
    T, D = x.shape
    return pl.pallas_call(
        _rmsnorm_kernel,
        grid=(T // tm,),
        in_specs=[pl.BlockSpec((tm, D), lambda i: (i, 0)), pl.BlockSpec((1, D), lambda i: (0, 0))],
        out_specs=pl.BlockSpec((tm, D), lambda i: (i, 0)),
        out_shape=jax.ShapeDtypeStruct((T, D), BF16),
        compiler_params=_cparams("parallel"),
        name="rmsnorm",
    )(x, gain.reshape(1, D))


def _mm_kernel(a_ref, b_ref, bias_ref, o_ref, *, scale, live_cols):
    acc = jnp.dot(a_ref[...], b_ref[...], preferred_element_type=F32)
    out = (acc + bias_ref[...]) * scale
    if live_cols is not None:
        col = pl.program_id(1) * out.shape[1] + lax.broadcasted_iota(jnp.int32, (1, out.shape[1]), 1)
        out = jnp.where(col < live_cols, out, 0.0)
    o_ref[...] = out.astype(o_ref.dtype)


def _mm(a, w, out_dtype, col0, ncols, *, bias=None, scale=1.0, tm=1024, tn=1024):
    M, K = a.shape
    tm = min(tm, M)
    tn = math.gcd(tn, ncols, col0)
    assert M % tm == 0 and tn % LANES == 0
    jb = col0 // tn
    live_cols = w.shape[1] - col0 if col0 + ncols > w.shape[1] else None
    if bias is None:
        bias = jnp.zeros((1, ncols), F32)
    return pl.pallas_call(
        functools.partial(_mm_kernel, scale=scale, live_cols=live_cols),
        grid=(M // tm, ncols // tn),
        in_specs=[
            pl.BlockSpec((tm, K), lambda i, j: (i, 0)),
            pl.BlockSpec((K, tn), lambda i, j: (0, j + jb)),
            pl.BlockSpec((1, tn), lambda i, j: (0, j)),
        ],
        out_specs=pl.BlockSpec((tm, tn), lambda i, j: (i, j)),
        out_shape=jax.ShapeDtypeStruct((M, ncols), out_dtype),
        compiler_params=_cparams("parallel", "arbitrary"),
        name="proj",
    )(a, w, bias)


def _mm_norm_res_kernel(*refs, nk, nk1, n_a, emit_u, tm, rows):
    a_refs = refs[:n_a]
    w_ref, h_hbm, g_ref = refs[n_a:n_a + 3]
    rest = refs[n_a + 3:]
    if emit_u:
        ng_ref, o_ref, u_ref, h_buf, h_sem = rest
    else:
        o_ref, h_buf, h_sem = rest
    i = pl.program_id(0)
    k = pl.program_id(1)

    def h_copy():
        return pltpu.make_async_copy(h_hbm.at[pl.ds(i * tm, tm), :], h_buf, h_sem)

    def product(a_ref):
        return jnp.dot(a_ref[...], w_ref[...], preferred_element_type=F32)

    @pl.when(k == 0)
    def _():
        h_copy().start()
        o_ref[...] = product(a_refs[0])

    if nk1 > 1 or n_a == 1:
        @pl.when((k > 0) & (k < nk1))
        def _():
            o_ref[...] += product(a_refs[0])

    if n_a == 2:
        @pl.when(k >= nk1)
        def _():
            o_ref[...] += product(a_refs[1])

    @pl.when(k == nk - 1)
    def _():
        h_copy().wait()

        def norm_rows(c, carry):
            r = pl.ds(pl.multiple_of(c * rows, rows), rows)
            y = o_ref[r, :]
            ms = jnp.mean(y * y, axis=-1, keepdims=True)
            h_new = h_buf[r, :] + y * lax.rsqrt(ms + RMS_EPS) * g_ref[...]
            o_ref[r, :] = h_new
            if emit_u:
                ms2 = jnp.mean(h_new * h_new, axis=-1, keepdims=True)
                u_ref[r, :] = (h_new * lax.rsqrt(ms2 + RMS_EPS) * ng_ref[...]).astype(u_ref.dtype)
            return carry

        lax.fori_loop(0, tm // rows, norm_rows, 0)


def _mm_norm_res(a_list, w, h, gain, next_gain=None, w_lead=None, tm=512, tk=1024, rows=64):
    M = h.shape[0]
    N = w.shape[-1]
    n_a = len(a_list)
    tk = math.gcd(tk, *[a.shape[1] for a in a_list])
    assert n_a in (1, 2) and M % tm == 0 and tm % rows == 0
    nk1 = a_list[0].shape[1] // tk
    nk = sum(a.shape[1] for a in a_list) // tk
    emit_u = next_gain is not None
    a_specs = [pl.BlockSpec((tm, tk), lambda i, k: (i, jnp.minimum(k, nk1 - 1)))]
    if n_a == 2:
        a_specs.append(pl.BlockSpec((tm, tk), lambda i, k: (i, jnp.maximum(k - nk1, 0))))
    row_spec = pl.BlockSpec((tm, N), lambda i, k: (i, 0))
    vec_spec = pl.BlockSpec((1, N), lambda i, k: (0, 0))
    if w_lead is None:
        w_spec = pl.BlockSpec((tk, N), lambda i, k: (k, 0))
    else:
        w_spec = pl.BlockSpec((None, tk, N), lambda i, k: (w_lead, k, 0))
    in_specs = a_specs + [w_spec, pl.BlockSpec(memory_space=pl.ANY), vec_spec]
    args = list(a_list) + [w, h, gain.reshape(1, N)]
    out_specs, out_shape = row_spec, jax.ShapeDtypeStruct((M, N), F32)
    if emit_u:
        in_specs.append(vec_spec)
        args.append(next_gain.reshape(1, N))
        out_specs = [row_spec, row_spec]
        out_shape = [out_shape, jax.ShapeDtypeStruct((M, N), BF16)]
    out = pl.pallas_call(
        functools.partial(_mm_norm_res_kernel, nk=nk, nk1=nk1, n_a=n_a, emit_u=emit_u, tm=tm, rows=rows),
        grid=(M // tm, nk),
        in_specs=in_specs,
        out_specs=out_specs,
        out_shape=out_shape,
        scratch_shapes=[pltpu.VMEM((tm, N), F32), pltpu.SemaphoreType.DMA(())],
        compiler_params=_cparams("parallel", "arbitrary"),
        name="out_proj_norm_res",
    )(*args)
    return out if emit_u else (out, None)


def _pool_kernel(cur_ref, prev_ref, w_ref, scale_ref, o_ref, ext_ref, *, ts, gd):
    i = pl.program_id(1)
    ext_ref[0:POOL_HALO, :] = jnp.where(i > 0, prev_ref[0], 0.0)
    ext_ref[POOL_HALO:POOL_HALO + ts, :] = cur_ref[0]
    t = i * ts + lax.broadcasted_iota(jnp.int32, (ts, 1), 0)
    for g, win in enumerate(POOL_WINDOWS):
        cols = slice(g * gd, (g + 1) * gd)
        tok = ext_ref[POOL_HALO:POOL_HALO + ts, cols]
        acc = tok
        for back in range(1, win):
            acc = acc + ext_ref[POOL_HALO - back:POOL_HALO - back + ts, cols]
        count = jnp.minimum(t + 1, win).astype(F32)
        pooled = acc / count - tok
        y = jnp.dot(pooled.astype(BF16), w_ref[g], preferred_element_type=F32)
        o_ref[0, :, cols] = (y * scale_ref[:, cols]).astype(o_ref.dtype)


def _pool_mixer(p_in, w_pool, scale, ts=512):
    B, S, PW = p_in.shape
    gd = PW // POOL_GROUPS
    hb = ts // POOL_HALO
    return pl.pallas_call(
        functools.partial(_pool_kernel, ts=ts, gd=gd),
        grid=(B, S // ts),
        in_specs=[
            pl.BlockSpec((1, ts, PW), lambda b, i: (b, i, 0)),
            pl.BlockSpec((1, POOL_HALO, PW), lambda b, i: (b, jnp.maximum(i * hb - 1, 0), 0)),
            pl.BlockSpec((POOL_GROUPS, gd, gd), lambda b, i: (0, 0, 0)),
            pl.BlockSpec((1, PW), lambda b, i: (0, 0)),
        ],
        out_specs=pl.BlockSpec((1, ts, PW), lambda b, i: (b, i, 0)),
        out_shape=jax.ShapeDtypeStruct((B, S, PW), BF16),
        scratch_shapes=[pltpu.VMEM((POOL_HALO + ts, PW), F32)],
        compiler_params=_cparams("parallel", "arbitrary"),
        name="pool_mixer",
    )(p_in, p_in, w_pool.astype(BF16), scale.reshape(1, PW))


def _gelu_tanh(x):
    c = math.sqrt(2.0 / math.pi)
    return x * (0.5 * (1.0 + jnp.tanh(c * (x + 0.044715 * (x * x * x)))))


def _compress_kernel(x_ref, pos_ref, w1_ref, w2_ref, o_ref, tmp_ref, *, nch):
    half = CMP_BLOCK // 2
    dh = x_ref.shape[2]
    a = b = None
    for r in range(half):
        x = x_ref[0, pl.ds(r, nch, stride=CMP_STRIDE), :]
        lo = jnp.dot((x + pos_ref[r:r + 1, :]).astype(BF16), w1_ref[0, r * dh:(r + 1) * dh, :],
                     preferred_element_type=F32)
        hi = jnp.dot((x + pos_ref[half + r:half + r + 1, :]).astype(BF16),
                     w1_ref[0, (half + r) * dh:(half + r + 1) * dh, :], preferred_element_type=F32)
        a = lo if a is None else a + lo
        b = hi if b is None else b + hi
    tmp_ref[0:nch, :] = b
    tmp_ref[nch:nch + SUBLANES, :] = jnp.zeros((SUBLANES, b.shape[1]), F32)
    pre = a + tmp_ref[1:nch + 1, :]
    out = jnp.dot(_gelu_tanh(pre).astype(BF16), w2_ref[0], preferred_element_type=F32)
    row = lax.broadcasted_iota(jnp.int32, out.shape, 0)
    o_ref[0, 0] = jnp.where(row < nch - 1, out, 0.0).astype(o_ref.dtype)


def _compress(x, pos, w1, w2, G):
    B, S, _ = x.shape
    hid = w1.shape[-1]
    dh = w2.shape[-1]
    nch = S // CMP_STRIDE
    assert CMP_BLOCK == 2 * CMP_STRIDE
    return pl.pallas_call(
        functools.partial(_compress_kernel, nch=nch),
        grid=(2, B * G),
        in_specs=[
            pl.BlockSpec((1, S, dh), lambda s, b: (b // G, 0, s * G + b % G)),
            pl.BlockSpec((CMP_BLOCK, dh), lambda s, b: (0, 0)),
            pl.BlockSpec((1, CMP_BLOCK * dh, hid), lambda s, b: (s, 0, 0)),
            pl.BlockSpec((1, hid, dh), lambda s, b: (s, 0, 0)),
        ],
        out_specs=pl.BlockSpec((1, 1, nch, dh), lambda s, b: (s, b, 0, 0)),
        out_shape=jax.ShapeDtypeStruct((2, B * G, nch, dh), BF16),
        scratch_shapes=[pltpu.VMEM((nch + SUBLANES, hid), F32)],
        compiler_params=_cparams("parallel", "arbitrary"),
        name="nsa_compress",
    )(x, pos, w1.astype(BF16), w2.astype(BF16))


def _nt_dot(a, b):
    return lax.dot_general(a, b, (((1,), (1,)), ((), ())), preferred_element_type=F32)


def _split_dot(w, x):
    hi = x.astype(BF16)
    r1 = x - hi.astype(F32)
    mid = r1.astype(BF16)
    lo = (r1 - mid.astype(F32)).astype(BF16)
    return (jnp.dot(w, hi, preferred_element_type=F32) + jnp.dot(w, mid, preferred_element_type=F32)
            + jnp.dot(w, lo, preferred_element_type=F32))


def _nsa_kernel(q_ref, gl_ref, kc_ref, vct_ref, ks_ref, vst_ref, kw_ref, vwt_ref, ovt_ref, et_ref, o_ref,
                m_ref, l_ref, acc_ref, s_ref, bias_ref, *, hg, n_cmp, n_slc, top_n):
    QT, DH = NSA_Q_TILE, NSA_HEAD_DIM
    t0 = pl.program_id(1) * QT
    q = jnp.concatenate([q_ref[:, h * DH:(h + 1) * DH] for h in range(hg)], axis=0)
    lane = lax.broadcasted_iota(jnp.int32, (1, QT), 1)
    t = t0 + lane

    def capped(s, cap):
        return jnp.concatenate([jnp.minimum(s[:, h * QT:(h + 1) * QT], cap) for h in range(hg)], axis=1)

    def softmax_cols(s):
        p = jnp.exp2(s - jnp.max(s, axis=0, keepdims=True))
        return p, jnp.sum(p, axis=0, keepdims=True)

    ncp = kc_ref.shape[2]
    nrow = lax.broadcasted_iota(jnp.int32, (ncp, QT), 0)
    valid_c = (nrow * CMP_STRIDE + (CMP_BLOCK - 1) <= t) & (nrow < n_cmp)
    s_c = capped(_nt_dot(kc_ref[0, 0], q), jnp.where(valid_c, BIG, NEG))
    ws = pl.multiple_of(jnp.maximum(t0 + QT - SWA_SPAN, 0), LANES)
    wb = ws // LANES
    dist = t - (ws + lax.broadcasted_iota(jnp.int32, (SWA_SPAN, QT), 0))
    valid_w = (dist >= 0) & (dist < SWA_WINDOW)
    s_w = capped(_nt_dot(kw_ref[0, pl.ds(ws, SWA_SPAN), :], q), jnp.where(valid_w, BIG, NEG))

    p_c, l_c = softmax_cols(s_c)
    seen = jnp.concatenate([jnp.where(t >= CMP_BLOCK - 1, 1.0, 0.0)] * hg, axis=1)
    p_c = p_c * (seen / l_c)
    o_c = jnp.dot(vct_ref[0], p_c.astype(BF16), preferred_element_type=F32)

    p_w, l_w = softmax_cols(s_w)
    vwin = jnp.concatenate([vwt_ref[0, wb + j] for j in range(SWA_SPAN // LANES)], axis=1)
    o_w = jnp.dot(vwin, p_w.astype(BF16), preferred_element_type=F32) / l_w

    p_sum = p_c[:, 0:QT]
    for h in range(1, hg):
        p_sum = p_sum + p_c[:, h * QT:(h + 1) * QT]
    imp = _split_dot(ovt_ref[...], p_sum)
    blk = lax.broadcasted_iota(jnp.int32, (LANES, QT), 0)
    cur = t0 // SLC_BLOCK + jnp.zeros((1, QT), jnp.int32)
    for k in range(1, QT // SLC_BLOCK):
        cur = cur + jnp.where(lane >= k * SLC_BLOCK, 1, 0)
    forced = (blk == 0) | (blk == cur) | (blk == cur - 1)
    val = jnp.where(forced, imp + FORCE_BONUS, jnp.where(blk > cur, -FORCE_BONUS, imp))
    if n_slc < LANES:
        val = jnp.where(blk < n_slc, val, -jnp.inf)

    def rank_bias(nb):
        SL = SUBLANES
        nslab = nb // SL
        slabs = [val[SL * r:SL * r + SL, :] for r in range(nslab)]
        ranks = [jnp.zeros((SL, QT), F32) for _ in range(nslab)]
        sub = lax.broadcasted_iota(jnp.int32, (SL, QT), 0)
        for other in range(nb):
            c = val[other:other + 1, :]
            for r in range(nslab):
                if SL * r > other:
                    beat = c >= slabs[r]
                elif SL * r + SL - 1 < other:
                    beat = c > slabs[r]
                else:
                    beat = (c > slabs[r]) | ((c == slabs[r]) & (sub + SL * r > other))
                ranks[r] = ranks[r] + jnp.where(beat, 1.0, 0.0)
        for r in range(nslab):
            chosen = (ranks[r] < top_n) & (blk[SL * r:SL * r + SL] <= cur)
            bias_ref[SL * r:SL * r + SL, :] = jnp.where(chosen, 0.0, NEG)
        if nb < LANES:
            bias_ref[nb:LANES, :] = jnp.full((LANES - nb, QT), NEG, F32)

    live = (t0 + QT - 1) // SLC_BLOCK + 1
    sizes = sorted({min(_round_up(-(-n_slc * k // RANK_SIZES), SUBLANES), LANES) for k in range(1, RANK_SIZES + 1)})
    for lo, nb in zip([0] + sizes[:-1], sizes):
        @pl.when((live > lo) & (live <= nb))
        def _(nb=nb):
            rank_bias(nb)

    bias_q = bias_ref[...].T.astype(BF16)

    q_sel = jnp.concatenate([q, jnp.concatenate([bias_q] * hg, axis=0)], axis=1)
    m_ref[...] = jnp.full(m_ref.shape, NEG, F32)
    l_ref[...] = jnp.zeros(l_ref.shape, F32)
    acc_ref[...] = jnp.zeros(acc_ref.shape, F32)

    def sel_tiles(tiles):
        for slot, (kb, _) in enumerate(tiles):
            start = pl.multiple_of(kb * SEL_TILE, SEL_TILE)
            keys = jnp.concatenate([ks_ref[0, pl.ds(start, SEL_TILE), :], et_ref[kb]], axis=1)
            s_ref[slot] = _nt_dot(keys, q_sel)
        for slot, (kb, diagonal) in enumerate(tiles):
            s = s_ref[slot]
            if diagonal:
                key = kb * SEL_TILE + lax.broadcasted_iota(jnp.int32, (SEL_TILE, QT), 0)
                s = capped(s, jnp.where(key <= t, BIG, NEG))
            m_prev = m_ref[...]
            m_new = jnp.maximum(m_prev, jnp.max(s, axis=0, keepdims=True))
            alpha = jnp.exp2(m_prev - m_new)
            p = jnp.exp2(s - m_new)
            l_ref[...] = alpha * l_ref[...] + jnp.sum(p, axis=0, keepdims=True)
            acc_ref[...] = alpha * acc_ref[...] + jnp.dot(vst_ref[0, kb], p.astype(BF16),
                                                          preferred_element_type=F32)
            m_ref[...] = m_new

    def past_group(i, carry):
        sel_tiles([(SEL_GROUP * i + slot, False) for slot in range(SEL_GROUP)])
        return carry

    diag = t0 // SEL_TILE
    groups = diag // SEL_GROUP
    lax.fori_loop(0, groups, past_group, 0)
    for rem in range(SEL_GROUP):
        @pl.when(diag - groups * SEL_GROUP == rem)
        def _(rem=rem):
            sel_tiles([(diag - rem + slot, slot == rem) for slot in range(rem + 1)])

    o_s = acc_ref[...] / l_ref[...]

    gate = jax.nn.sigmoid(gl_ref[0, 0])
    for h in range(hg):
        cols = slice(h * QT, (h + 1) * QT)
        r = N_BRANCH * h
        out = gate[r:r + 1] * o_c[:, cols] + gate[r + 1:r + 2] * o_s[:, cols] + gate[r + 2:r + 3] * o_w[:, cols]
        o_ref[:, h * DH:(h + 1) * DH] = out.T.astype(o_ref.dtype)


def _nsa(q, gates, kvc, kv, B, S):
    T, HW = q.shape
    G, DH, QT = NSA_KV_GROUPS, NSA_HEAD_DIM, NSA_Q_TILE
    hg = HW // DH // G
    n_cmp = (S - CMP_BLOCK) // CMP_STRIDE + 1
    ncp = kvc.shape[2]
    n_slc = S // SLC_BLOCK
    top_n = min(SLC_TOPN, n_slc)
    nqt = S // QT
    kvw = G * DH
    assert n_slc <= LANES and S % SEL_TILE == 0 and S >= SWA_SPAN and QT == LANES

    cs = jnp.arange(ncp)[None, :] * CMP_STRIDE
    ss = jnp.arange(LANES)[:, None] * SLC_BLOCK
    overlap = jnp.clip(jnp.minimum(cs + CMP_BLOCK, ss + SLC_BLOCK) - jnp.maximum(cs, ss), 0) // CMP_STRIDE
    overlap = jnp.where((jnp.arange(ncp)[None, :] < n_cmp) & (jnp.arange(LANES)[:, None] < n_slc), overlap, 0)
    overlap = overlap.astype(BF16)
    key_blk = (jnp.arange(S) // SLC_BLOCK).reshape(S // SEL_TILE, SEL_TILE, 1)
    expand = (key_blk == jnp.arange(LANES)[None, None, :]).astype(BF16)

    gr = _round_up(N_BRANCH * hg, 8)
    gl = gates[:, :G * hg * N_BRANCH].reshape(B, nqt, QT, G, hg * N_BRANCH)
    gl = jnp.pad(gl, ((0, 0),) * 4 + ((0, gr - hg * N_BRANCH),)).transpose(0, 3, 1, 4, 2).reshape(B * G, nqt, gr, QT)

    def v_tiles(which, tile):
        v = kv[:, :, which * kvw:(which + 1) * kvw].reshape(B, S // tile, tile, G, DH)
        return v.transpose(0, 3, 1, 4, 2).reshape(B * G, S // tile, DH, tile)

    def k_spec(which):
        return pl.BlockSpec((1, S, DH), lambda bg, i: (bg // G, 0, which * G + bg % G))

    return pl.pallas_call(
        functools.partial(_nsa_kernel, hg=hg, n_cmp=n_cmp, n_slc=n_slc, top_n=top_n),
        grid=(B * G, nqt),
        in_specs=[
            pl.BlockSpec((QT, hg * DH), lambda bg, i: ((bg // G) * nqt + i, bg % G)),
            pl.BlockSpec((1, 1, gr, QT), lambda bg, i: (bg, i, 0, 0)),
            pl.BlockSpec((1, 1, ncp, DH), lambda bg, i: (0, bg, 0, 0)),
            pl.BlockSpec((1, DH, ncp), lambda bg, i: (bg, 0, 0)),
            k_spec(0),
            pl.BlockSpec((1, S // SEL_TILE, DH, SEL_TILE), lambda bg, i: (bg, 0, 0, 0)),
            k_spec(2),
            pl.BlockSpec((1, S // LANES, DH, LANES), lambda bg, i: (bg, 0, 0, 0)),
            pl.BlockSpec((LANES, ncp), lambda bg, i: (0, 0)),
            pl.BlockSpec((S // SEL_TILE, SEL_TILE, LANES), lambda bg, i: (0, 0, 0)),
        ],
        out_specs=pl.BlockSpec((QT, hg * DH), lambda bg, i: ((bg // G) * nqt + i, bg % G)),
        out_shape=jax.ShapeDtypeStruct((T, HW), BF16),
        scratch_shapes=[pltpu.VMEM((1, hg * QT), F32), pltpu.VMEM((1, hg * QT), F32),
                        pltpu.VMEM((DH, hg * QT), F32), pltpu.VMEM((SEL_GROUP, SEL_TILE, hg * QT), F32),
                        pltpu.VMEM((LANES, QT), F32)],
        compiler_params=_cparams("parallel", "arbitrary"),
        name="nsa_attention",
    )(q, gl, kvc, jnp.swapaxes(kvc[1], 1, 2), kv, v_tiles(1, SEL_TILE), kv, v_tiles(3, LANES), overlap, expand)


def _mlstm_kernel(q_ref, kt_ref, v_ref, o_ref, gr_ref, gc_ref, hn_ref, y_ref, c_ref, n_ref, m_ref, *, L, hps):
    ci = pl.program_id(1)
    dk = kt_ref.shape[1] // hps
    dv = v_ref.shape[1] // hps

    @pl.when(ci == 0)
    def _():
        c_ref[...] = jnp.zeros_like(c_ref)
        n_ref[...] = jnp.zeros_like(n_ref)
        m_ref[...] = jnp.zeros_like(m_ref)

    ti = lax.broadcasted_iota(jnp.int32, (L, L), 0)
    si = lax.broadcasted_iota(jnp.int32, (L, L), 1)
    causal = si <= ti

    for hd in range(hps):
        q = q_ref[:, hd * dk:(hd + 1) * dk]
        kt = kt_ref[0, hd * dk:(hd + 1) * dk, :]
        v = v_ref[:, hd * dv:(hd + 1) * dv]
        ig_row = gr_ref[hd, 0, 0:1, :]
        lf_row = jax.nn.log_sigmoid(gr_ref[hd, 0, 1:2, :])
        lf_col = jax.nn.log_sigmoid(gc_ref[hd, 0, :, 1:2])
        m_prev = m_ref[hd, 0:1, 0:1]

        b_col = jnp.sum(jnp.where(causal, lf_row, 0.0), axis=1, keepdims=True)
        b_row = jnp.sum(jnp.where(ti <= si, lf_col, 0.0), axis=0, keepdims=True)
        b_last = b_col[L - 1:L, :]

        dmat = jnp.where(causal, b_col - b_row + ig_row, -jnp.inf)
        a_col = b_col + m_prev
        m_t = jnp.maximum(a_col, jnp.max(dmat, axis=1, keepdims=True))
        wq = jnp.dot(q, kt, preferred_element_type=F32) * jnp.exp(dmat - m_t)
        inter = jnp.exp(a_col - m_t)
        q_c = jnp.dot(q, c_ref[hd].astype(BF16), preferred_element_type=F32)
        q_n = jnp.dot(q, n_ref[hd].astype(BF16), preferred_element_type=F32)[:, 0:1]
        num = inter * q_c + jnp.dot(wq.astype(BF16), v, preferred_element_type=F32)
        den = inter * q_n + jnp.sum(wq, axis=1, keepdims=True)
        h = num / jnp.maximum(jnp.abs(den), jnp.exp(-m_t))
        h = h * lax.rsqrt(jnp.mean(h * h, axis=-1, keepdims=True) + RMS_EPS) * hn_ref[hd]
        gate_o = jax.nn.sigmoid(o_ref[:, hd * dv:(hd + 1) * dv])
        y_ref[:, hd * dv:(hd + 1) * dv] = (gate_o * h).astype(y_ref.dtype)

        g_row = b_last - b_row + ig_row
        m_new = jnp.maximum(b_last + m_prev, jnp.max(g_row, axis=1, keepdims=True))
        decay = jnp.exp(b_last + m_prev - m_new)
        kw_t = kt.astype(F32) * jnp.exp(g_row - m_new)
        c_ref[hd] = decay * c_ref[hd] + jnp.dot(kw_t.astype(BF16), v, preferred_element_type=F32)
        n_ref[hd] = decay * n_ref[hd] + jnp.sum(kw_t, axis=1, keepdims=True)
        m_ref[hd] = jnp.broadcast_to(m_new, m_ref.shape[1:])


def _mlstm(q, kt, v, o, gates, head_norm, B, S):
    T = q.shape[0]
    H = MLSTM_HEADS
    dk = q.shape[1] // H
    dv = v.shape[1] // H
    L = min(MLSTM_CHUNK, S)
    nc = S // L
    g = gates[:, :2 * H].reshape(B, nc, L, 2, H)
    g_row = jnp.transpose(g, (0, 4, 1, 3, 2)).reshape(B * H, nc, 2, L)
    g_col = jnp.transpose(g, (0, 4, 1, 2, 3)).reshape(B * H, nc, L, 2)
    hps = MLSTM_HEADS_PER_STEP
    hs = H // hps
    return pl.pallas_call(
        functools.partial(_mlstm_kernel, L=L, hps=hps),
        grid=(B * hs, nc),
        in_specs=[
            pl.BlockSpec((L, hps * dk), lambda bh, c: ((bh // hs) * nc + c, bh % hs)),
            pl.BlockSpec((1, hps * dk, L), lambda bh, c: (bh // hs, bh % hs, c)),
            pl.BlockSpec((L, hps * dv), lambda bh, c: ((bh // hs) * nc + c, bh % hs)),
            pl.BlockSpec((L, hps * dv), lambda bh, c: ((bh // hs) * nc + c, bh % hs)),
            pl.BlockSpec((hps, 1, 2, L), lambda bh, c: (bh, c, 0, 0)),
            pl.BlockSpec((hps, 1, L, 2), lambda bh, c: (bh, c, 0, 0)),
            pl.BlockSpec((hps, 1, dv), lambda bh, c: (bh % hs, 0, 0)),
        ],
        out_specs=pl.BlockSpec((L, hps * dv), lambda bh, c: ((bh // hs) * nc + c, bh % hs)),
        out_shape=jax.ShapeDtypeStruct((T, H * dv), BF16),
        scratch_shapes=[pltpu.VMEM((hps, dk, dv), F32), pltpu.VMEM((hps, dk, LANES), F32),
                        pltpu.VMEM((hps, SUBLANES, LANES), F32)],
        compiler_params=_cparams("parallel", "arbitrary"),
        name="mlstm",
    )(q, kt, v, o, g_row, g_col, head_norm.reshape(H, 1, dv))


def _ffn_in_kernel(u_ref, halo_ref, wg_ref, wu_ref, cw_ref, cb_ref, o_ref, lhs_ref, g_ref, tail_ref, *, tm, tn, ff,
                   sh, over, tiles_per_seq):
    i = pl.program_id(0)
    j = pl.program_id(1)

    @pl.when(j == 0)
    def _():
        halo = halo_ref[...]
        lhs_ref[0:CONV_HALO, :] = jnp.where(i % tiles_per_seq != 0, halo, jnp.zeros_like(halo))
        lhs_ref[CONV_HALO:CONV_HALO + tm, :] = u_ref[...]
        tail_ref[...] = jnp.zeros_like(tail_ref)

    g_ref[...] = jnp.dot(lhs_ref[...], wg_ref[...], preferred_element_type=F32)
    up = jnp.dot(u_ref[...], wu_ref[0], preferred_element_type=F32)
    if over:
        moved = jnp.concatenate([up[:, over:], up[:, :over]], axis=1)
        up = jnp.where(j == pl.num_programs(1) - 1, moved, up)
    gate = cb_ref[...] + cw_ref[0:1, :] * g_ref[CONV_HALO:CONV_HALO + tm, :]
    for back in range(1, CONV_WIDTH):
        gate = gate + cw_ref[back:back + 1, :] * g_ref[CONV_HALO - back:CONV_HALO - back + tm, :]
    if sh:
        lane = lax.broadcasted_iota(jnp.int32, (1, LANES), 1)
        tiles = [tail_ref[...]] + [gate[:, c * LANES:(c + 1) * LANES] for c in range(tn // LANES)]
        tail_ref[...] = tiles[-1]
        rolled = [pltpu.roll(x, sh, axis=1) for x in tiles]
        gate = jnp.concatenate([jnp.where(lane < sh, rolled[c], rolled[c + 1]) for c in range(tn // LANES)], axis=1)
    act = gate * jax.nn.sigmoid(gate) * up
    hidden = j * tn - sh + lax.broadcasted_iota(jnp.int32, (1, tn), 1)
    o_ref[...] = jnp.where((hidden >= 0) & (hidden < ff), act, 0.0).astype(o_ref.dtype)


def _ffn_in(u, w_in, conv_w, conv_b, layer, ff, sh, fp, S, tm=1024, tn=FFN_COL_TILE):
    T, D = u.shape
    tm = min(tm, S)
    assert T % tm == 0 and S % tm == 0 and fp % tn == 0 and tn % LANES == 0
    hb = tm // CONV_HALO
    up0 = (ff - sh) // LANES
    up_last = (w_in.shape[-1] - tn) // LANES
    over = max(up0 + (fp // tn - 1) * (tn // LANES) - up_last, 0) * LANES
    assert w_in.shape[-1] % LANES == 0 and over < tn
    return pl.pallas_call(
        functools.partial(_ffn_in_kernel, tm=tm, tn=tn, ff=ff, sh=sh, over=over, tiles_per_seq=S // tm),
        grid=(T // tm, fp // tn),
        in_specs=[
            pl.BlockSpec((tm, D), lambda i, j: (i, 0)),
            pl.BlockSpec((CONV_HALO, D), lambda i, j: (jnp.maximum(i * hb - 1, 0), 0)),
            pl.BlockSpec((None, D, tn), lambda i, j: (layer, 0, j)),
            pl.BlockSpec((pl.Element(1), pl.Element(D), pl.Element(tn)),
                         lambda i, j: (layer, 0, jnp.minimum(up0 + j * (tn // LANES), up_last) * LANES)),
            pl.BlockSpec((None, CONV_WIDTH, tn), lambda i, j: (layer, 0, j)),
            pl.BlockSpec((None, 1, tn), lambda i, j: (layer, 0, j)),
        ],
        out_specs=pl.BlockSpec((tm, tn), lambda i, j: (i, j)),
        out_shape=jax.ShapeDtypeStruct((T, fp), BF16),
        scratch_shapes=[pltpu.VMEM((CONV_HALO + tm, D), BF16), pltpu.VMEM((CONV_HALO + tm, tn), F32),
                        pltpu.VMEM((tm, LANES), F32)],
        compiler_params=_cparams("parallel", "arbitrary"),
        name="ffn_in_conv_act",
    )(u, u, w_in, w_in, conv_w, conv_b)


def _place_rows_kernel(x_ref, o_ref, *, rb, ff, sh, nb):
    k = pl.program_id(1)
    x = x_ref[0]
    o_ref[...] = x.astype(o_ref.dtype)

    @pl.when(k == 0)
    def _():
        o_ref[0:sh, :] = jnp.zeros((sh, x.shape[1]), o_ref.dtype)
        o_ref[sh:rb, :] = x[0:rb - sh].astype(o_ref.dtype)

    up = rb * (nb - 1) - sh - (ff - rb)
    if up > 0:
        @pl.when(k == nb - 1)
        def _():
            o_ref[0:rb - up, :] = x[up:rb].astype(o_ref.dtype)
            o_ref[rb - up:rb, :] = jnp.zeros((up, x.shape[1]), o_ref.dtype)


def _place_rows(w, sh, fp, rb=512):
    depth, ff, D = w.shape
    nb = fp // rb
    assert fp % rb == 0 and sh % SUBLANES == 0 and 0 < sh < rb and (ff - rb) % SUBLANES == 0
    assert rb * (nb - 2) - sh <= ff - rb < rb * (nb - 1) - sh + rb
    last = (ff - rb) // SUBLANES
    return pl.pallas_call(
        functools.partial(_place_rows_kernel, rb=rb, ff=ff, sh=sh, nb=nb),
        grid=(depth, nb),
        in_specs=[pl.BlockSpec(
            (pl.Element(1), pl.Element(rb), pl.Element(D)),
            lambda d, k: (d, jnp.clip(k * (rb // SUBLANES) - sh // SUBLANES, 0, last) * SUBLANES, 0))],
        out_specs=pl.BlockSpec((None, rb, D), lambda d, k: (d, k, 0)),
        out_shape=jax.ShapeDtypeStruct((depth, fp, D), BF16),
        compiler_params=_cparams("parallel", "arbitrary"),
        name="ffn_w_out_place",
    )(w)


def _ffn_weights(w_in, w_out):
    depth, D, ff2 = w_in.shape
    ff = ff2 // 2
    sh = ff % LANES
    fp = _round_up(ff + sh, FFN_COL_TILE)
    w_in = w_in.astype(BF16)
    if sh:
        w_out = _place_rows(w_out, sh, fp)
    else:
        w_out = jnp.pad(w_out.astype(BF16), ((0, 0), (0, fp - ff), (0, 0)))
    return w_in, w_out, ff, sh, fp


def _conv_ffn(h, u, ln_post, next_gain, w_in, w_out, ff, sh, fp, conv_w, conv_b, layer, S):
    act = _ffn_in(u, w_in, conv_w, conv_b, layer, ff, sh, fp, S)
    return _mm_norm_res([act], w_out, h, ln_post, next_gain, w_lead=layer)


def _ab_layer(h, u, ln_post, next_gain, w_in, pool_w, pool_scale, cmp_pos, ck_w1, ck_w2, cv_w1, cv_w2, w_out, B, S):
    T, D = h.shape
    G, DH = NSA_KV_GROUPS, NSA_HEAD_DIM
    pw = D // 4
    hw = D - pw
    kvw = G * DH
    w = w_in.astype(BF16)
    c0 = pw + hw
    p_in = _mm(u, w, F32, 0, pw)
    q = _mm(u, w, BF16, pw, hw, scale=DH ** -0.5 * LOG2E)
    kv_cmp = _mm(u, w, F32, c0, 2 * kvw)
    kv = _mm(u, w, BF16, c0 + 2 * kvw, 4 * kvw)
    gates = _mm(u, w, F32, c0 + 6 * kvw, LANES)

    y_a = _pool_mixer(p_in.reshape(B, S, pw), pool_w, pool_scale).reshape(T, pw)
    kvc = _compress(kv_cmp.reshape(B, S, 2 * kvw), cmp_pos, jnp.stack([ck_w1, cv_w1]), jnp.stack([ck_w2, cv_w2]), G)
    y_b = _nsa(q, gates, kvc, kv.reshape(B, S, 4 * kvw), B, S)
    return _mm_norm_res([y_a, y_b], w_out.astype(BF16), h, ln_post, next_gain)


def _c_layer(h, u, ln_post, next_gain, w_in, b_if, head_norm, w_out, B, S):
    T, D = h.shape
    H = MLSTM_HEADS
    dv = D // H
    dk = dv // 2
    qk = H * dk
    w = w_in.astype(BF16)
    q = _mm(u, w, BF16, 0, qk, scale=dk ** -0.5)
    k = _mm(u, w, BF16, qk, qk)
    v = _mm(u, w, BF16, 2 * qk, D)
    o = _mm(u, w, F32, 2 * qk + D, D)
    gates = _mm(u, w, F32, 2 * qk + 2 * D, LANES, bias=_pad_cols(b_if.reshape(1, 2 * H), LANES))
    kt = k.reshape(B, S, qk).transpose(0, 2, 1)
    y = _mlstm(q, kt, v, o, gates, head_norm, B, S)
    return _mm_norm_res([y], w_out.astype(BF16), h, ln_post, next_gain)


def kernel(x, ln_pre, ln_post, w_in_ab, pool_w, pool_scale, cmp_pos, cmp_k_w1, cmp_k_w2, cmp_v_w1, cmp_v_w2,
           w_out_ab, w_in_c, b_if_c, head_norm_c, w_out_c, ffn_ln_pre, ffn_ln_post, ffn_w_in, ffn_conv_w,
           ffn_conv_b, ffn_w_out):
    B, S, D = x.shape
    depth = ln_pre.shape[0]
    h = x.reshape(B * S, D)
    u = _rmsnorm(h, ln_pre[0])
    ffn_wi, ffn_wo, ff, sh, fp = _ffn_weights(ffn_w_in, ffn_w_out)
    ffn_cb = ffn_conv_b.reshape(depth, 1, -1)
    for layer in range(depth):
        i = layer // 2
        if layer % 2 == 0:
            h, u = _ab_layer(h, u, ln_post[layer], ffn_ln_pre[layer], w_in_ab[i], pool_w[i], pool_scale[i],
                             cmp_pos[i], cmp_k_w1[i], cmp_k_w2[i], cmp_v_w1[i], cmp_v_w2[i], w_out_ab[i], B, S)
        else:
            h, u = _c_layer(h, u, ln_post[layer], ffn_ln_pre[layer], w_in_c[i], b_if_c[i], head_norm_c[i],
                            w_out_c[i], B, S)
        next_gain = ln_pre[layer + 1] if layer + 1 < depth else None
        h, u = _conv_ffn(h, u, ffn_ln_post[layer], next_gain, ffn_wi, ffn_wo, ff, sh, fp, ffn_conv_w, ffn_cb, layer,
                         S)
    return h.reshape(B, S, D)
```

```python
import functools
import math

import jax
import jax.numpy as jnp
from jax import lax
from jax.experimental import pallas as pl
from jax.experimental.pallas import tpu as pltpu

F32 = jnp.float32
BF16 = jnp.bfloat16

RMS_EPS = 1e-6
POOL_GROUPS = 4
POOL_WINDOWS = (2, 4, 8, 16)
POOL_HALO = 16
NSA_HEAD_DIM = 128
NSA_KV_GROUPS = 4
N_BRANCH = 3
CMP_BLOCK = 32
CMP_STRIDE = 16
SLC_BLOCK = 64
SLC_TOPN = 16
SWA_WINDOW = 512
FORCE_BONUS = 1e4
NEG = -1e30
BIG = 1e30
MLSTM_HEADS = 8
CONV_WIDTH = 3
CONV_HALO = 16

LANES = 128
SUBLANES = 8
LOG2E = math.log2(math.e)
NSA_Q_TILE = LANES
SEL_TILE = 512
SEL_GROUP = 8
RANK_SIZES = 8
SWA_SPAN = SWA_WINDOW + NSA_Q_TILE
MLSTM_CHUNK = 256
MLSTM_HEADS_PER_STEP = 4
FFN_COL_TILE = 512
VMEM_LIMIT = 56 * 1024 * 1024


def _cparams(*sem):
    return pltpu.CompilerParams(dimension_semantics=sem, vmem_limit_bytes=VMEM_LIMIT)


def _round_up(n, m):
    return (n + m - 1) // m * m


def _pad_cols(w, n):
    return jnp.pad(w, ((0, 0), (0, n - w.shape[1])))


def _rmsnorm_kernel(x_ref, g_ref, o_ref):
    x = x_ref[...]
    ms = jnp.mean(x * x, axis=-1, keepdims=True)
    o_ref[...] = (x * lax.rsqrt(ms + RMS_EPS) * g_ref[...]).astype(o_ref.dtype)


def _rmsnorm(x, gain, tm=512):
    T, D = x.shape
    return pl.pallas_call(
        _rmsnorm_kernel,
        grid=(T // tm,),
        in_specs=[pl.BlockSpec((tm, D), lambda i: (i, 0)), pl.BlockSpec((1, D), lambda i: (0, 0))],
        out_specs=pl.BlockSpec((tm, D), lambda i: (i, 0)),
        out_shape=jax.ShapeDtypeStruct((T, D), BF16),
        compiler_params=_cparams("parallel"),
        name="rmsnorm",
    )(x, gain.reshape(1, D))


def _mm_kernel(a_ref, b_ref, bias_ref, o_ref, *, scale, live_cols):
    acc = jnp.dot(a_ref[...], b_ref[...], preferred_element_type=F32)
    out = (acc + bias_ref[...]) * scale
    if live_cols is not None:
        col = pl.program_id(1) * out.shape[1] + lax.broadcasted_iota(jnp.int32, (1, out.shape[1]), 1)
        out = jnp.where(col < live_cols, out, 0.0)
    o_ref[...] = out.astype(o_ref.dtype)


def _mm(a, w, out_dtype, col0, ncols, *, bias=None, scale=1.0, tm=1024, tn=1024):
    M, K = a.shape
    tm = min(tm, M)
    tn = math.gcd(tn, ncols, col0)
    assert M % tm == 0 and tn % LANES == 0
    jb = col0 // tn
    live_cols = w.shape[1] - col0 if col0 + ncols > w.shape[1] else None
    if bias is None:
        bias = jnp.zeros((1, ncols), F32)
    return pl.pallas_call(
        functools.partial(_mm_kernel, scale=scale, live_cols=live_cols),
        grid=(M // tm, ncols // tn),
        in_specs=[
            pl.BlockSpec((tm, K), lambda i, j: (i, 0)),
            pl.BlockSpec((K, tn), lambda i, j: (0, j + jb)),
            pl.BlockSpec((1, tn), lambda i, j: (0, j)),
        ],
        out_specs=pl.BlockSpec((tm, tn), lambda i, j: (i, j)),
        out_shape=jax.ShapeDtypeStruct((M, ncols), out_dtype),
        compiler_params=_cparams("parallel", "arbitrary"),
        name="proj",
    )(a, w, bias)


def _mm_norm_res_kernel(*refs, nk, nk1, n_a, emit_u, tm, rows):
    a_refs = refs[:n_a]
    w_ref, h_hbm, g_ref = refs[n_a:n_a + 3]
    rest = refs[n_a + 3:]
    if emit_u:
        ng_ref, o_ref, u_ref, h_buf, h_sem = rest
    else:
        o_ref, h_buf, h_sem = rest
    i = pl.program_id(0)
    k = pl.program_id(1)

    def h_copy():
        return pltpu.make_async_copy(h_hbm.at[pl.ds(i * tm, tm), :], h_buf, h_sem)

    def product(a_ref):
        return jnp.dot(a_ref[...], w_ref[...], preferred_element_type=F32)

    @pl.when(k == 0)
    def _():
        h_copy().start()
        o_ref[...] = product(a_refs[0])

    if nk1 > 1 or n_a == 1:
        @pl.when((k > 0) & (k < nk1))
        def _():
            o_ref[...] += product(a_refs[0])

    if n_a == 2:
        @pl.when(k >= nk1)
        def _():
            o_ref[...] += product(a_refs[1])

    @pl.when(k == nk - 1)
    def _():
        h_copy().wait()

        def norm_rows(c, carry):
            r = pl.ds(pl.multiple_of(c * rows, rows), rows)
            y = o_ref[r, :]
            ms = jnp.mean(y * y, axis=-1, keepdims=True)
            h_new = h_buf[r, :] + y * lax.rsqrt(ms + RMS_EPS) * g_ref[...]
            o_ref[r, :] = h_new
            if emit_u:
                ms2 = jnp.mean(h_new * h_new, axis=-1, keepdims=True)
                u_ref[r, :] = (h_new * lax.rsqrt(ms2 + RMS_EPS) * ng_ref[...]).astype(u_ref.dtype)
            return carry

        lax.fori_loop(0, tm // rows, norm_rows, 0)


def _mm_norm_res(a_list, w, h, gain, next_gain=None, w_lead=None, tm=512, tk=1024, rows=64):
    M = h.shape[0]
    N = w.shape[-1]
    n_a = len(a_list)
    tk = math.gcd(tk, *[a.shape[1] for a in a_list])
    assert n_a in (1, 2) and M % tm == 0 and tm % rows == 0
    nk1 = a_list[0].shape[1] // tk
    nk = sum(a.shape[1] for a in a_list) // tk
    emit_u = next_gain is not None
    a_specs = [pl.BlockSpec((tm, tk), lambda i, k: (i, jnp.minimum(k, nk1 - 1)))]
    if n_a == 2:
        a_specs.append(pl.BlockSpec((tm, tk), lambda i, k: (i, jnp.maximum(k - nk1, 0))))
    row_spec = pl.BlockSpec((tm, N), lambda i, k: (i, 0))
    vec_spec = pl.BlockSpec((1, N), lambda i, k: (0, 0))
    if w_lead is None:
        w_spec = pl.BlockSpec((tk, N), lambda i, k: (k, 0))
    else:
        w_spec = pl.BlockSpec((None, tk, N), lambda i, k: (w_lead, k, 0))
    in_specs = a_specs + [w_spec, pl.BlockSpec(memory_space=pl.ANY), vec_spec]
    args = list(a_list) + [w, h, gain.reshape(1, N)]
    out_specs, out_shape = row_spec, jax.ShapeDtypeStruct((M, N), F32)
    if emit_u:
        in_specs.append(vec_spec)
        args.append(next_gain.reshape(1, N))
        out_specs = [row_spec, row_spec]
        out_shape = [out_shape, jax.ShapeDtypeStruct((M, N), BF16)]
    out = pl.pallas_call(
        functools.partial(_mm_norm_res_kernel, nk=nk, nk1=nk1, n_a=n_a, emit_u=emit_u, tm=tm, rows=rows),
        grid=(M // tm, nk),
        in_specs=in_specs,
        out_specs=out_specs,
        out_shape=out_shape,
        scratch_shapes=[pltpu.VMEM((tm, N), F32), pltpu.SemaphoreType.DMA(())],
        compiler_params=_cparams("parallel", "arbitrary"),
        name="out_proj_norm_res",
    )(*args)
    return out if emit_u else (out, None)


def _pool_kernel(cur_ref, prev_ref, w_ref, scale_ref, o_ref, ext_ref, *, ts, gd):
    i = pl.program_id(1)
    ext_ref[0:POOL_HALO, :] = jnp.where(i > 0, prev_ref[0], 0.0)
    ext_ref[POOL_HALO:POOL_HALO + ts, :] = cur_ref[0]
    t = i * ts + lax.broadcasted_iota(jnp.int32, (ts, 1), 0)
    for g, win in enumerate(POOL_WINDOWS):
        cols = slice(g * gd, (g + 1) * gd)
        tok = ext_ref[POOL_HALO:POOL_HALO + ts, cols]
        acc = tok
        for back in range(1, win):
            acc = acc + ext_ref[POOL_HALO - back:POOL_HALO - back + ts, cols]
        count = jnp.minimum(t + 1, win).astype(F32)
        pooled = acc / count - tok
        y = jnp.dot(pooled.astype(BF16), w_ref[g], preferred_element_type=F32)
        o_ref[0, :, cols] = (y * scale_ref[:, cols]).astype(o_ref.dtype)


def _pool_mixer(p_in, w_pool, scale, ts=512):
    B, S, PW = p_in.shape
    gd = PW // POOL_GROUPS
    hb = ts // POOL_HALO
    return pl.pallas_call(
        functools.partial(_pool_kernel, ts=ts, gd=gd),
        grid=(B, S // ts),
        in_specs=[
            pl.BlockSpec((1, ts, PW), lambda b, i: (b, i, 0)),
            pl.BlockSpec((1, POOL_HALO, PW), lambda b, i: (b, jnp.maximum(i * hb - 1, 0), 0)),
            pl.BlockSpec((POOL_GROUPS, gd, gd), lambda b, i: (0, 0, 0)),
            pl.BlockSpec((1, PW), lambda b, i: (0, 0)),
        ],
        out_specs=pl.BlockSpec((1, ts, PW), lambda b, i: (b, i, 0)),
        out_shape=jax.ShapeDtypeStruct((B, S, PW), BF16),
        scratch_shapes=[pltpu.VMEM((POOL_HALO + ts, PW), F32)],
        compiler_params=_cparams("parallel", "arbitrary"),
        name="pool_mixer",
    )(p_in, p_in, w_pool.astype(BF16), scale.reshape(1, PW))


def _gelu_tanh(x):
    c = math.sqrt(2.0 / math.pi)
    return x * (0.5 * (1.0 + jnp.tanh(c * (x + 0.044715 * (x * x * x)))))


def _compress_kernel(x_ref, pos_ref, w1_ref, w2_ref, o_ref, tmp_ref, *, nch):
    half = CMP_BLOCK // 2
    dh = x_ref.shape[2]
    a = b = None
    for r in range(half):
        x = x_ref[0, pl.ds(r, nch, stride=CMP_STRIDE), :]
        lo = jnp.dot((x + pos_ref[r:r + 1, :]).astype(BF16), w1_ref[0, r * dh:(r + 1) * dh, :],
                     preferred_element_type=F32)
        hi = jnp.dot((x + pos_ref[half + r:half + r + 1, :]).astype(BF16),
                     w1_ref[0, (half + r) * dh:(half + r + 1) * dh, :], preferred_element_type=F32)
        a = lo if a is None else a + lo
        b = hi if b is None else b + hi
    tmp_ref[0:nch, :] = b
    tmp_ref[nch:nch + SUBLANES, :] = jnp.zeros((SUBLANES, b.shape[1]), F32)
    pre = a + tmp_ref[1:nch + 1, :]
    out = jnp.dot(_gelu_tanh(pre).astype(BF16), w2_ref[0], preferred_element_type=F32)
    row = lax.broadcasted_iota(jnp.int32, out.shape, 0)
    o_ref[0, 0] = jnp.where(row < nch - 1, out, 0.0).astype(o_ref.dtype)


def _compress(x, pos, w1, w2, G):
    B, S, _ = x.shape
    hid = w1.shape[-1]
    dh = w2.shape[-1]
    nch = S // CMP_STRIDE
    assert CMP_BLOCK == 2 * CMP_STRIDE
    return pl.pallas_call(
        functools.partial(_compress_kernel, nch=nch),
        grid=(2, B * G),
        in_specs=[
            pl.BlockSpec((1, S, dh), lambda s, b: (b // G, 0, s * G + b % G)),
            pl.BlockSpec((CMP_BLOCK, dh), lambda s, b: (0, 0)),
            pl.BlockSpec((1, CMP_BLOCK * dh, hid), lambda s, b: (s, 0, 0)),
            pl.BlockSpec((1, hid, dh), lambda s, b: (s, 0, 0)),
        ],
        out_specs=pl.BlockSpec((1, 1, nch, dh), lambda s, b: (s, b, 0, 0)),
        out_shape=jax.ShapeDtypeStruct((2, B * G, nch, dh), BF16),
        scratch_shapes=[pltpu.VMEM((nch + SUBLANES, hid), F32)],
        compiler_params=_cparams("parallel", "arbitrary"),
        name="nsa_compress",
    )(x, pos, w1.astype(BF16), w2.astype(BF16))


def _nt_dot(a, b):
    return lax.dot_general(a, b, (((1,), (1,)), ((), ())), preferred_element_type=F32)


def _split_dot(w, x):
    hi = x.astype(BF16)
    r1 = x - hi.astype(F32)
    mid = r1.astype(BF16)
    lo = (r1 - mid.astype(F32)).astype(BF16)
    return (jnp.dot(w, hi, preferred_element_type=F32) + jnp.dot(w, mid, preferred_element_type=F32)
            + jnp.dot(w, lo, preferred_element_type=F32))


def _nsa_kernel(q_ref, gl_ref, kc_ref, vct_ref, ks_ref, vst_ref, kw_ref, vwt_ref, ovt_ref, et_ref, o_ref,
                m_ref, l_ref, acc_ref, s_ref, bias_ref, *, hg, n_cmp, n_slc, top_n):
    QT, DH = NSA_Q_TILE, NSA_HEAD_DIM
    t0 = pl.program_id(1) * QT
    q = jnp.concatenate([q_ref[:, h * DH:(h + 1) * DH] for h in range(hg)], axis=0)
    lane = lax.broadcasted_iota(jnp.int32, (1, QT), 1)
    t = t0 + lane

    def capped(s, cap):
        return jnp.concatenate([jnp.minimum(s[:, h * QT:(h + 1) * QT], cap) for h in range(hg)], axis=1)

    def softmax_cols(s):
        p = jnp.exp2(s - jnp.max(s, axis=0, keepdims=True))
        return p, jnp.sum(p, axis=0, keepdims=True)

    ncp = kc_ref.shape[2]
    nrow = lax.broadcasted_iota(jnp.int32, (ncp, QT), 0)
    valid_c = (nrow * CMP_STRIDE + (CMP_BLOCK - 1) <= t) & (nrow < n_cmp)
    s_c = capped(_nt_dot(kc_ref[0, 0], q), jnp.where(valid_c, BIG, NEG))
    ws = pl.multiple_of(jnp.maximum(t0 + QT - SWA_SPAN, 0), LANES)
    wb = ws // LANES
    dist = t - (ws + lax.broadcasted_iota(jnp.int32, (SWA_SPAN, QT), 0))
    valid_w = (dist >= 0) & (dist < SWA_WINDOW)
    s_w = capped(_nt_dot(kw_ref[0, pl.ds(ws, SWA_SPAN), :], q), jnp.where(valid_w, BIG, NEG))

    p_c, l_c = softmax_cols(s_c)
    seen = jnp.concatenate([jnp.where(t >= CMP_BLOCK - 1, 1.0, 0.0)] * hg, axis=1)
    p_c = p_c * (seen / l_c)
    o_c = jnp.dot(vct_ref[0], p_c.astype(BF16), preferred_element_type=F32)

    p_w, l_w = softmax_cols(s_w)
    vwin = jnp.concatenate([vwt_ref[0, wb + j] for j in range(SWA_SPAN // LANES)], axis=1)
    o_w = jnp.dot(vwin, p_w.astype(BF16), preferred_element_type=F32) / l_w

    p_sum = p_c[:, 0:QT]
    for h in range(1, hg):
        p_sum = p_sum + p_c[:, h * QT:(h + 1) * QT]
    imp = _split_dot(ovt_ref[...], p_sum)
    blk = lax.broadcasted_iota(jnp.int32, (LANES, QT), 0)
    cur = t0 // SLC_BLOCK + jnp.zeros((1, QT), jnp.int32)
    for k in range(1, QT // SLC_BLOCK):
        cur = cur + jnp.where(lane >= k * SLC_BLOCK, 1, 0)
    forced = (blk == 0) | (blk == cur) | (blk == cur - 1)
    val = jnp.where(forced, imp + FORCE_BONUS, jnp.where(blk > cur, -FORCE_BONUS, imp))
    if n_slc < LANES:
        val = jnp.where(blk < n_slc, val, -jnp.inf)

    def rank_bias(nb):
        SL = SUBLANES
        nslab = nb // SL
        slabs = [val[SL * r:SL * r + SL, :] for r in range(nslab)]
        ranks = [jnp.zeros((SL, QT), F32) for _ in range(nslab)]
        sub = lax.broadcasted_iota(jnp.int32, (SL, QT), 0)
        for other in range(nb):
            c = val[other:other + 1, :]
            for r in range(nslab):
                if SL * r > other:
                    beat = c >= slabs[r]
                elif SL * r + SL - 1 < other:
                    beat = c > slabs[r]
                else:
                    beat = (c > slabs[r]) | ((c == slabs[r]) & (sub + SL * r > other))
                ranks[r] = ranks[r] + jnp.where(beat, 1.0, 0.0)
        for r in range(nslab):
            chosen = (ranks[r] < top_n) & (blk[SL * r:SL * r + SL] <= cur)
            bias_ref[SL * r:SL * r + SL, :] = jnp.where(chosen, 0.0, NEG)
        if nb < LANES:
            bias_ref[nb:LANES, :] = jnp.full((LANES - nb, QT), NEG, F32)

    live = (t0 + QT - 1) // SLC_BLOCK + 1
    sizes = sorted({min(_round_up(-(-n_slc * k // RANK_SIZES), SUBLANES), LANES) for k in range(1, RANK_SIZES + 1)})
    for lo, nb in zip([0] + sizes[:-1], sizes):
        @pl.when((live > lo) & (live <= nb))
        def _(nb=nb):
            rank_bias(nb)

    bias_q = bias_ref[...].T.astype(BF16)

    q_sel = jnp.concatenate([q, jnp.concatenate([bias_q] * hg, axis=0)], axis=1)
    m_ref[...] = jnp.full(m_ref.shape, NEG, F32)
    l_ref[...] = jnp.zeros(l_ref.shape, F32)
    acc_ref[...] = jnp.zeros(acc_ref.shape, F32)

    def sel_tiles(tiles):
        for slot, (kb, _) in enumerate(tiles):
            start = pl.multiple_of(kb * SEL_TILE, SEL_TILE)
            keys = jnp.concatenate([ks_ref[0, pl.ds(start, SEL_TILE), :], et_ref[kb]], axis=1)
            s_ref[slot] = _nt_dot(keys, q_sel)
        for slot, (kb, diagonal) in enumerate(tiles):
            s = s_ref[slot]
            if diagonal:
                key = kb * SEL_TILE + lax.broadcasted_iota(jnp.int32, (SEL_TILE, QT), 0)
                s = capped(s, jnp.where(key <= t, BIG, NEG))
            m_prev = m_ref[...]
            m_new = jnp.maximum(m_prev, jnp.max(s, axis=0, keepdims=True))
            alpha = jnp.exp2(m_prev - m_new)
            p = jnp.exp2(s - m_new)
            l_ref[...] = alpha * l_ref[...] + jnp.sum(p, axis=0, keepdims=True)
            acc_ref[...] = alpha * acc_ref[...] + jnp.dot(vst_ref[0, kb], p.astype(BF16),
                                                          preferred_element_type=F32)
            m_ref[...] = m_new

    def past_group(i, carry):
        sel_tiles([(SEL_GROUP * i + slot, False) for slot in range(SEL_GROUP)])
        return carry

    diag = t0 // SEL_TILE
    groups = diag // SEL_GROUP
    lax.fori_loop(0, groups, past_group, 0)
    for rem in range(SEL_GROUP):
        @pl.when(diag - groups * SEL_GROUP == rem)
        def _(rem=rem):
            sel_tiles([(diag - rem + slot, slot == rem) for slot in range(rem + 1)])

    o_s = acc_ref[...] / l_ref[...]

    gate = jax.nn.sigmoid(gl_ref[0, 0])
    for h in range(hg):
        cols = slice(h * QT, (h + 1) * QT)
        r = N_BRANCH * h
        out = gate[r:r + 1] * o_c[:, cols] + gate[r + 1:r + 2] * o_s[:, cols] + gate[r + 2:r + 3] * o_w[:, cols]
        o_ref[:, h * DH:(h + 1) * DH] = out.T.astype(o_ref.dtype)


def _nsa(q, gates, kvc, kv, B, S):
    T, HW = q.shape
    G, DH, QT = NSA_KV_GROUPS, NSA_HEAD_DIM, NSA_Q_TILE
    hg = HW // DH // G
    n_cmp = (S - CMP_BLOCK) // CMP_STRIDE + 1
    ncp = kvc.shape[2]
    n_slc = S // SLC_BLOCK
    top_n = min(SLC_TOPN, n_slc)
    nqt = S // QT
    kvw = G * DH
    assert n_slc <= LANES and S % SEL_TILE == 0 and S >= SWA_SPAN and QT == LANES

    cs = jnp.arange(ncp)[None, :] * CMP_STRIDE
    ss = jnp.arange(LANES)[:, None] * SLC_BLOCK
    overlap = jnp.clip(jnp.minimum(cs + CMP_BLOCK, ss + SLC_BLOCK) - jnp.maximum(cs, ss), 0) // CMP_STRIDE
    overlap = jnp.where((jnp.arange(ncp)[None, :] < n_cmp) & (jnp.arange(LANES)[:, None] < n_slc), overlap, 0)
    overlap = overlap.astype(BF16)
    key_blk = (jnp.arange(S) // SLC_BLOCK).reshape(S // SEL_TILE, SEL_TILE, 1)
    expand = (key_blk == jnp.arange(LANES)[None, None, :]).astype(BF16)

    gr = _round_up(N_BRANCH * hg, 8)
    gl = gates[:, :G * hg * N_BRANCH].reshape(B, nqt, QT, G, hg * N_BRANCH)
    gl = jnp.pad(gl, ((0, 0),) * 4 + ((0, gr - hg * N_BRANCH),)).transpose(0, 3, 1, 4, 2).reshape(B * G, nqt, gr, QT)

    def v_tiles(which, tile):
        v = kv[:, :, which * kvw:(which + 1) * kvw].reshape(B, S // tile, tile, G, DH)
        return v.transpose(0, 3, 1, 4, 2).reshape(B * G, S // tile, DH, tile)

    def k_spec(which):
        return pl.BlockSpec((1, S, DH), lambda bg, i: (bg // G, 0, which * G + bg % G))

    return pl.pallas_call(
        functools.partial(_nsa_kernel, hg=hg, n_cmp=n_cmp, n_slc=n_slc, top_n=top_n),
        grid=(B * G, nqt),
        in_specs=[
            pl.BlockSpec((QT, hg * DH), lambda bg, i: ((bg // G) * nqt + i, bg % G)),
            pl.BlockSpec((1, 1, gr, QT), lambda bg, i: (bg, i, 0, 0)),
            pl.BlockSpec((1, 1, ncp, DH), lambda bg, i: (0, bg, 0, 0)),
            pl.BlockSpec((1, DH, ncp), lambda bg, i: (bg, 0, 0)),
            k_spec(0),
            pl.BlockSpec((1, S // SEL_TILE, DH, SEL_TILE), lambda bg, i: (bg, 0, 0, 0)),
            k_spec(2),
            pl.BlockSpec((1, S // LANES, DH, LANES), lambda bg, i: (bg, 0, 0, 0)),
            pl.BlockSpec((LANES, ncp), lambda bg, i: (0, 0)),
            pl.BlockSpec((S // SEL_TILE, SEL_TILE, LANES), lambda bg, i: (0, 0, 0)),
        ],
        out_specs=pl.BlockSpec((QT, hg * DH), lambda bg, i: ((bg // G) * nqt + i, bg % G)),
        out_shape=jax.ShapeDtypeStruct((T, HW), BF16),
        scratch_shapes=[pltpu.VMEM((1, hg * QT), F32), pltpu.VMEM((1, hg * QT), F32),
                        pltpu.VMEM((DH, hg * QT), F32), pltpu.VMEM((SEL_GROUP, SEL_TILE, hg * QT), F32),
                        pltpu.VMEM((LANES, QT), F32)],
        compiler_params=_cparams("parallel", "arbitrary"),
        name="nsa_attention",
    )(q, gl, kvc, jnp.swapaxes(kvc[1], 1, 2), kv, v_tiles(1, SEL_TILE), kv, v_tiles(3, LANES), overlap, expand)


def _mlstm_kernel(q_ref, kt_ref, v_ref, o_ref, gr_ref, gc_ref, hn_ref, y_ref, c_ref, n_ref, m_ref, *, L, hps):
    ci = pl.program_id(1)
    dk = kt_ref.shape[1] // hps
    dv = v_ref.shape[1] // hps

    @pl.when(ci == 0)
    def _():
        c_ref[...] = jnp.zeros_like(c_ref)
        n_ref[...] = jnp.zeros_like(n_ref)
        m_ref[...] = jnp.zeros_like(m_ref)

    ti = lax.broadcasted_iota(jnp.int32, (L, L), 0)
    si = lax.broadcasted_iota(jnp.int32, (L, L), 1)
    causal = si <= ti

    for hd in range(hps):
        q = q_ref[:, hd * dk:(hd + 1) * dk]
        kt = kt_ref[0, hd * dk:(hd + 1) * dk, :]
        v = v_ref[:, hd * dv:(hd + 1) * dv]
        ig_row = gr_ref[hd, 0, 0:1, :]
        lf_row = jax.nn.log_sigmoid(gr_ref[hd, 0, 1:2, :])
        lf_col = jax.nn.log_sigmoid(gc_ref[hd, 0, :, 1:2])
        m_prev = m_ref[hd, 0:1, 0:1]

        b_col = jnp.sum(jnp.where(causal, lf_row, 0.0), axis=1, keepdims=True)
        b_row = jnp.sum(jnp.where(ti <= si, lf_col, 0.0), axis=0, keepdims=True)
        b_last = b_col[L - 1:L, :]

        dmat = jnp.where(causal, b_col - b_row + ig_row, -jnp.inf)
        a_col = b_col + m_prev
        m_t = jnp.maximum(a_col, jnp.max(dmat, axis=1, keepdims=True))
        wq = jnp.dot(q, kt, preferred_element_type=F32) * jnp.exp(dmat - m_t)
        inter = jnp.exp(a_col - m_t)
        q_c = jnp.dot(q, c_ref[hd].astype(BF16), preferred_element_type=F32)
        q_n = jnp.dot(q, n_ref[hd].astype(BF16), preferred_element_type=F32)[:, 0:1]
        num = inter * q_c + jnp.dot(wq.astype(BF16), v, preferred_element_type=F32)
        den = inter * q_n + jnp.sum(wq, axis=1, keepdims=True)
        h = num / jnp.maximum(jnp.abs(den), jnp.exp(-m_t))
        h = h * lax.rsqrt(jnp.mean(h * h, axis=-1, keepdims=True) + RMS_EPS) * hn_ref[hd]
        gate_o = jax.nn.sigmoid(o_ref[:, hd * dv:(hd + 1) * dv])
        y_ref[:, hd * dv:(hd + 1) * dv] = (gate_o * h).astype(y_ref.dtype)

        g_row = b_last - b_row + ig_row
        m_new = jnp.maximum(b_last + m_prev, jnp.max(g_row, axis=1, keepdims=True))
        decay = jnp.exp(b_last + m_prev - m_new)
        kw_t = kt.astype(F32) * jnp.exp(g_row - m_new)
        c_ref[hd] = decay * c_ref[hd] + jnp.dot(kw_t.astype(BF16), v, preferred_element_type=F32)
        n_ref[hd] = decay * n_ref[hd] + jnp.sum(kw_t, axis=1, keepdims=True)
        m_ref[hd] = jnp.broadcast_to(m_new, m_ref.shape[1:])


def _mlstm(q, kt, v, o, gates, head_norm, B, S):
    T = q.shape[0]
    H = MLSTM_HEADS
    dk = q.shape[1] // H
    dv = v.shape[1] // H
    L = min(MLSTM_CHUNK, S)
    nc = S // L
    g = gates[:, :2 * H].reshape(B, nc, L, 2, H)
    g_row = jnp.transpose(g, (0, 4, 1, 3, 2)).reshape(B * H, nc, 2, L)
    g_col = jnp.transpose(g, (0, 4, 1, 2, 3)).reshape(B * H, nc, L, 2)
    hps = MLSTM_HEADS_PER_STEP
    hs = H // hps
    return pl.pallas_call(
        functools.partial(_mlstm_kernel, L=L, hps=hps),
        grid=(B * hs, nc),
        in_specs=[
            pl.BlockSpec((L, hps * dk), lambda bh, c: ((bh // hs) * nc + c, bh % hs)),
            pl.BlockSpec((1, hps * dk, L), lambda bh, c: (bh // hs, bh % hs, c)),
            pl.BlockSpec((L, hps * dv), lambda bh, c: ((bh // hs) * nc + c, bh % hs)),
            pl.BlockSpec((L, hps * dv), lambda bh, c: ((bh // hs) * nc + c, bh % hs)),
            pl.BlockSpec((hps, 1, 2, L), lambda bh, c: (bh, c, 0, 0)),
            pl.BlockSpec((hps, 1, L, 2), lambda bh, c: (bh, c, 0, 0)),
            pl.BlockSpec((hps, 1, dv), lambda bh, c: (bh % hs, 0, 0)),
        ],
        out_specs=pl.BlockSpec((L, hps * dv), lambda bh, c: ((bh // hs) * nc + c, bh % hs)),
        out_shape=jax.ShapeDtypeStruct((T, H * dv), BF16),
        scratch_shapes=[pltpu.VMEM((hps, dk, dv), F32), pltpu.VMEM((hps, dk, LANES), F32),
                        pltpu.VMEM((hps, SUBLANES, LANES), F32)],
        compiler_params=_cparams("parallel", "arbitrary"),
        name="mlstm",
    )(q, kt, v, o, g_row, g_col, head_norm.reshape(H, 1, dv))


def _ffn_in_kernel(u_ref, halo_ref, wg_ref, wu_ref, cw_ref, cb_ref, o_ref, lhs_ref, g_ref, tail_ref, *, tm, tn, ff,
                   sh, over, tiles_per_seq):
    i = pl.program_id(0)
    j = pl.program_id(1)

    @pl.when(j == 0)
    def _():
        halo = halo_ref[...]
        lhs_ref[0:CONV_HALO, :] = jnp.where(i % tiles_per_seq != 0, halo, jnp.zeros_like(halo))
        lhs_ref[CONV_HALO:CONV_HALO + tm, :] = u_ref[...]
        tail_ref[...] = jnp.zeros_like(tail_ref)

    g_ref[...] = jnp.dot(lhs_ref[...], wg_ref[...], preferred_element_type=F32)
    up = jnp.dot(u_ref[...], wu_ref[0], preferred_element_type=F32)
    if over:
        moved = jnp.concatenate([up[:, over:], up[:, :over]], axis=1)
        up = jnp.where(j == pl.num_programs(1) - 1, moved, up)
    gate = cb_ref[...] + cw_ref[0:1, :] * g_ref[CONV_HALO:CONV_HALO + tm, :]
    for back in range(1, CONV_WIDTH):
        gate = gate + cw_ref[back:back + 1, :] * g_ref[CONV_HALO - back:CONV_HALO - back + tm, :]
    if sh:
        lane = lax.broadcasted_iota(jnp.int32, (1, LANES), 1)
        tiles = [tail_ref[...]] + [gate[:, c * LANES:(c + 1) * LANES] for c in range(tn // LANES)]
        tail_ref[...] = tiles[-1]
        rolled = [pltpu.roll(x, sh, axis=1) for x in tiles]
        gate = jnp.concatenate([jnp.where(lane < sh, rolled[c], rolled[c + 1]) for c in range(tn // LANES)], axis=1)
    act = gate * jax.nn.sigmoid(gate) * up
    hidden = j * tn - sh + lax.broadcasted_iota(jnp.int32, (1, tn), 1)
    o_ref[...] = jnp.where((hidden >= 0) & (hidden < ff), act, 0.0).astype(o_ref.dtype)


def _ffn_in(u, w_in, conv_w, conv_b, layer, ff, sh, fp, S, tm=1024, tn=FFN_COL_TILE):
    T, D = u.shape
    tm = min(tm, S)
    assert T % tm == 0 and S % tm == 0 and fp % tn == 0 and tn % LANES == 0
    hb = tm // CONV_HALO
    up0 = (ff - sh) // LANES
    up_last = (w_in.shape[-1] - tn) // LANES
    over = max(up0 + (fp // tn - 1) * (tn // LANES) - up_last, 0) * LANES
    assert w_in.shape[-1] % LANES == 0 and over < tn
    return pl.pallas_call(
        functools.partial(_ffn_in_kernel, tm=tm, tn=tn, ff=ff, sh=sh, over=over, tiles_per_seq=S // tm),
        grid=(T // tm, fp // tn),
        in_specs=[
            pl.BlockSpec((tm, D), lambda i, j: (i, 0)),
            pl.BlockSpec((CONV_HALO, D), lambda i, j: (jnp.maximum(i * hb - 1, 0), 0)),
            pl.BlockSpec((None, D, tn), lambda i, j: (layer, 0, j)),
            pl.BlockSpec((pl.Element(1), pl.Element(D), pl.Element(tn)),
                         lambda i, j: (layer, 0, jnp.minimum(up0 + j * (tn // LANES), up_last) * LANES)),
            pl.BlockSpec((None, CONV_WIDTH, tn), lambda i, j: (layer, 0, j)),
            pl.BlockSpec((None, 1, tn), lambda i, j: (layer, 0, j)),
        ],
        out_specs=pl.BlockSpec((tm, tn), lambda i, j: (i, j)),
        out_shape=jax.ShapeDtypeStruct((T, fp), BF16),
        scratch_shapes=[pltpu.VMEM((CONV_HALO + tm, D), BF16), pltpu.VMEM((CONV_HALO + tm, tn), F32),
                        pltpu.VMEM((tm, LANES), F32)],
        compiler_params=_cparams("parallel", "arbitrary"),
        name="ffn_in_conv_act",
    )(u, u, w_in, w_in, conv_w, conv_b)


def _place_rows_kernel(x_ref, o_ref, *, rb, ff, sh, nb):
    k = pl.program_id(1)
    x = x_ref[0]
    o_ref[...] = x.astype(o_ref.dtype)

    @pl.when(k == 0)
    def _():
        o_ref[0:sh, :] = jnp.zeros((sh, x.shape[1]), o_ref.dtype)
        o_ref[sh:rb, :] = x[0:rb - sh].astype(o_ref.dtype)

    up = rb * (nb - 1) - sh - (ff - rb)
    if up > 0:
        @pl.when(k == nb - 1)
        def _():
            o_ref[0:rb - up, :] = x[up:rb].astype(o_ref.dtype)
            o_ref[rb - up:rb, :] = jnp.zeros((up, x.shape[1]), o_ref.dtype)


def _place_rows(w, sh, fp, rb=512):
    depth, ff, D = w.shape
    nb = fp // rb
    assert fp % rb == 0 and sh % SUBLANES == 0 and 0 < sh < rb and (ff - rb) % SUBLANES == 0
    assert rb * (nb - 2) - sh <= ff - rb < rb * (nb - 1) - sh + rb
    last = (ff - rb) // SUBLANES
    return pl.pallas_call(
        functools.partial(_place_rows_kernel, rb=rb, ff=ff, sh=sh, nb=nb),
        grid=(depth, nb),
        in_specs=[pl.BlockSpec(
            (pl.Element(1), pl.Element(rb), pl.Element(D)),
            lambda d, k: (d, jnp.clip(k * (rb // SUBLANES) - sh // SUBLANES, 0, last) * SUBLANES, 0))],
        out_specs=pl.BlockSpec((None, rb, D), lambda d, k: (d, k, 0)),
        out_shape=jax.ShapeDtypeStruct((depth, fp, D), BF16),
        compiler_params=_cparams("parallel", "arbitrary"),
        name="ffn_w_out_place",
    )(w)


def _ffn_weights(w_in, w_out):
    depth, D, ff2 = w_in.shape
    ff = ff2 // 2
    sh = ff % LANES
    fp = _round_up(ff + sh, FFN_COL_TILE)
    w_in = w_in.astype(BF16)
    if sh:
        w_out = _place_rows(w_out, sh, fp)
    else:
        w_out = jnp.pad(w_out.astype(BF16), ((0, 0), (0, fp - ff), (0, 0)))
    return w_in, w_out, ff, sh, fp


def _conv_ffn(h, u, ln_post, next_gain, w_in, w_out, ff, sh, fp, conv_w, conv_b, layer, S):
    act = _ffn_in(u, w_in, conv_w, conv_b, layer, ff, sh, fp, S)
    return _mm_norm_res([act], w_out, h, ln_post, next_gain, w_lead=layer)


def _ab_layer(h, u, ln_post, next_gain, w_in, pool_w, pool_scale, cmp_pos, ck_w1, ck_w2, cv_w1, cv_w2, w_out, B, S):
    T, D = h.shape
    G, DH = NSA_KV_GROUPS, NSA_HEAD_DIM
    pw = D // 4
    hw = D - pw
    kvw = G * DH
    w = w_in.astype(BF16)
    c0 = pw + hw
    p_in = _mm(u, w, F32, 0, pw)
    q = _mm(u, w, BF16, pw, hw, scale=DH ** -0.5 * LOG2E)
    kv_cmp = _mm(u, w, F32, c0, 2 * kvw)
    kv = _mm(u, w, BF16, c0 + 2 * kvw, 4 * kvw)
    gates = _mm(u, w, F32, c0 + 6 * kvw, LANES)

    y_a = _pool_mixer(p_in.reshape(B, S, pw), pool_w, pool_scale).reshape(T, pw)
    kvc = _compress(kv_cmp.reshape(B, S, 2 * kvw), cmp_pos, jnp.stack([ck_w1, cv_w1]), jnp.stack([ck_w2, cv_w2]), G)
    y_b = _nsa(q, gates, kvc, kv.reshape(B, S, 4 * kvw), B, S)
    return _mm_norm_res([y_a, y_b], w_out.astype(BF16), h, ln_post, next_gain)


def _c_layer(h, u, ln_post, next_gain, w_in, b_if, head_norm, w_out, B, S):
    T, D = h.shape
    H = MLSTM_HEADS
    dv = D // H
    dk = dv // 2
    qk = H * dk
    w = w_in.astype(BF16)
    q = _mm(u, w, BF16, 0, qk, scale=dk ** -0.5)
    k = _mm(u, w, BF16, qk, qk)
    v = _mm(u, w, BF16, 2 * qk, D)
    o = _mm(u, w, F32, 2 * qk + D, D)
    gates = _mm(u, w, F32, 2 * qk + 2 * D, LANES, bias=_pad_cols(b_if.reshape(1, 2 * H), LANES))
    kt = k.reshape(B, S, qk).transpose(0, 2, 1)
    y = _mlstm(q, kt, v, o, gates, head_norm, B, S)
    return _mm_norm_res([y], w_out.astype(BF16), h, ln_post, next_gain)


def kernel(x, ln_pre, ln_post, w_in_ab, pool_w, pool_scale, cmp_pos, cmp_k_w1, cmp_k_w2, cmp_v_w1, cmp_v_w2,
           w_out_ab, w_in_c, b_if_c, head_norm_c, w_out_c, ffn_ln_pre, ffn_ln_post, ffn_w_in, ffn_conv_w,
           ffn_conv_b, ffn_w_out):
    B, S, D = x.shape
    depth = ln_pre.shape[0]
    h = x.reshape(B * S, D)
    u = _rmsnorm(h, ln_pre[0])
    ffn_wi, ffn_wo, ff, sh, fp = _ffn_weights(ffn_w_in, ffn_w_out)
    ffn_cb = ffn_conv_b.reshape(depth, 1, -1)
    for layer in range(depth):
        i = layer // 2
        if layer % 2 == 0:
            h, u = _ab_layer(h, u, ln_post[layer], ffn_ln_pre[layer], w_in_ab[i], pool_w[i], pool_scale[i],
                             cmp_pos[i], cmp_k_w1[i], cmp_k_w2[i], cmp_v_w1[i], cmp_v_w2[i], w_out_ab[i], B, S)
        else:
            h, u = _c_layer(h, u, ln_post[layer], ffn_ln_pre[layer], w_in_c[i], b_if_c[i], head_norm_c[i],
                            w_out_c[i], B, S)
        next_gain = ln_pre[layer + 1] if layer + 1 < depth else None
        h, u = _conv_ffn(h, u, ffn_ln_post[layer], next_gain, ffn_wi, ffn_wo, ff, sh, fp, ffn_conv_w, ffn_cb, layer,
                         S)
    return h.reshape(B, S, D)
```
